```python
import jax, jax.numpy as jnp
from jax import lax
import numpy as np

D_MODEL = 4096
BATCH = 1
SEQ = 8192
DEPTH = 1

N_MEM = 256
D_MIX = D_MODEL
HEAD_DIM = 128
ATT_WIDTH = D_MIX // 2
ATT_HEADS = ATT_WIDTH // HEAD_DIM
ATT_KV_HEADS = 4
KV_WIDTH = ATT_KV_HEADS * HEAD_DIM
WINDOW = 128
BLOCK = 128
ROPE_THETA = 500000.0
ROT_DIM = HEAD_DIM // 4
LRU_WIDTH = D_MIX // 4
LRU_BLOCKS = 8
LRU_BLOCK_DIM = LRU_WIDTH // LRU_BLOCKS
CONV_WIDTH = 4
CONV_PAD = (2, 1)
LRU_C = 8.0
MEM_HEADS = 4
MEM_WIDTH = D_MIX // 4
MEM_HEAD_DIM = MEM_WIDTH // MEM_HEADS
EPS = 1e-6
SPLIT_WIDTHS = [ATT_WIDTH, KV_WIDTH, KV_WIDTH, LRU_WIDTH, MEM_WIDTH, ATT_WIDTH, LRU_WIDTH, MEM_WIDTH]
IN_WIDTH = sum(SPLIT_WIDTHS)

kernel_name = "hymba_bidir_swa_rglru_memxattn"


def rmsnorm(t, g):
    tf = t.astype(jnp.float32)
    y = tf * lax.rsqrt(jnp.mean(tf * tf, axis=-1, keepdims=True) + EPS)
    return (y * g.astype(jnp.float32)).astype(t.dtype)


def partial_rope(t, pos):
    half = ROT_DIM // 2
    inv_freq = jnp.power(jnp.float32(ROPE_THETA), -jnp.arange(half, dtype=jnp.float32) * 2.0 / ROT_DIM)
    ang = pos.astype(jnp.float32)[:, None] * inv_freq[None, :]
    cos = jnp.cos(ang)[None, :, None, :]
    sin = jnp.sin(ang)[None, :, None, :]
    tf = t.astype(jnp.float32)
    x1 = tf[..., :half]
    x2 = tf[..., half:ROT_DIM]
    out = jnp.concatenate([x1 * cos - x2 * sin, x2 * cos + x1 * sin, tf[..., ROT_DIM:]], axis=-1)
    return out.astype(t.dtype)


def windowed_gqa_sink(q, k, v, sink):
    B, S, H, D = q.shape
    KVH = k.shape[2]
    G = H // KVH
    nb = S // BLOCK
    qb = q.reshape(B, nb, BLOCK, KVH, G, D)

    def band(t):
        tp = jnp.pad(t, ((0, 0), (BLOCK, BLOCK), (0, 0), (0, 0)))
        tb = tp.reshape(B, nb + 2, BLOCK, KVH, D)
        return jnp.concatenate([tb[:, :-2], tb[:, 1:-1], tb[:, 2:]], axis=2)

    kw = band(k)
    vw = band(v)
    s = jnp.einsum('bnqhgd,bnkhd->bnhgqk', qb, kw).astype(jnp.float32) * (D ** -0.5)
    qpos = jnp.arange(nb)[:, None, None] * BLOCK + jnp.arange(BLOCK)[None, :, None]
    kpos = (jnp.arange(nb)[:, None, None] - 1) * BLOCK + jnp.arange(3 * BLOCK)[None, None, :]
    valid = (jnp.abs(kpos - qpos) <= WINDOW) & (kpos >= 0) & (kpos < S)
    s = jnp.where(valid[None, :, None, None, :, :], s, jnp.finfo(jnp.float32).min)
    sk = sink.astype(jnp.float32).reshape(KVH, G)[None, None, :, :, None, None]
    m = jnp.maximum(jnp.max(s, axis=-1, keepdims=True), sk)
    p = jnp.exp(s - m)
    denom = jnp.sum(p, axis=-1, keepdims=True) + jnp.exp(sk - m)
    attn = (p / denom).astype(v.dtype)
    out = jnp.einsum('bnhgqk,bnkhd->bnqhgd', attn, vw)
    return out.reshape(B, S, H * D)


def centred_depthwise_conv(t, w, b):
    C = t.shape[-1]
    y = lax.conv_general_dilated(t, w[:, None, :], window_strides=(1,), padding=[CONV_PAD],
                                 dimension_numbers=('NWC', 'WIO', 'NWC'), feature_group_count=C)
    return y + b


def _linear_combine(e1, e2):
    a1, b1 = e1
    a2, b2 = e2
    return a1 * a2, a2 * b1 + b2


def rglru_direction(xc, w_a, b_a, w_x, b_x, lam, reverse):
    B, S, C = xc.shape
    xb = xc.reshape(B, S, LRU_BLOCKS, LRU_BLOCK_DIM)
    r = jax.nn.sigmoid((jnp.einsum('bsnc,ncd->bsnd', xb, w_a).reshape(B, S, C) + b_a).astype(jnp.float32))
    i = jax.nn.sigmoid((jnp.einsum('bsnc,ncd->bsnd', xb, w_x).reshape(B, S, C) + b_x).astype(jnp.float32))
    log_a = -LRU_C * r * jax.nn.softplus(-lam.astype(jnp.float32))
    a = jnp.exp(log_a)
    mult = jnp.sqrt(-jnp.expm1(2.0 * log_a))
    b_in = mult * i * xc.astype(jnp.float32)
    _, h = lax.associative_scan(_linear_combine, (a, b_in), reverse=reverse, axis=1)
    return h


def memory_cross_attention(q, mem, mem_g, w_mem_kv):
    B, S, _ = q.shape
    M = mem.shape[1]
    kv = rmsnorm(mem, mem_g) @ w_mem_kv
    km, vm = jnp.split(kv, 2, axis=-1)
    qh = q.reshape(B, S, MEM_HEADS, MEM_HEAD_DIM)
    km = km.reshape(B, M, MEM_HEADS, MEM_HEAD_DIM)
    vm = vm.reshape(B, M, MEM_HEADS, MEM_HEAD_DIM)
    s = jnp.einsum('bshd,bmhd->bhsm', qh, km).astype(jnp.float32) * (MEM_HEAD_DIM ** -0.5)
    p = jax.nn.softmax(s, axis=-1).astype(vm.dtype)
    return jnp.einsum('bhsm,bmhd->bshd', p, vm).reshape(B, S, MEM_WIDTH)


def hybrid_layer(h, mem, pos, pre_g, w_in, sink, conv_w, conv_b, w_a, b_a, w_x, b_x, lam,
                 mem_g, w_mem_kv, g_att, g_lru, g_mem, w_out, post_g):
    B, S, _ = h.shape
    u = rmsnorm(h, pre_g)
    proj = u @ w_in
    idx = list(np.cumsum(SPLIT_WIDTHS)[:-1])
    q_att, k_att, v_att, x_lru, q_mem, z_att, z_lru, z_mem = jnp.split(proj, idx, axis=-1)
    q_att = partial_rope(q_att.reshape(B, S, ATT_HEADS, HEAD_DIM), pos)
    k_att = partial_rope(k_att.reshape(B, S, ATT_KV_HEADS, HEAD_DIM), pos)
    v_att = v_att.reshape(B, S, ATT_KV_HEADS, HEAD_DIM)
    y_att = windowed_gqa_sink(q_att, k_att, v_att, sink)
    xc = centred_depthwise_conv(x_lru, conv_w, conv_b)
    h_fwd = rglru_direction(xc, w_a[0], b_a[0], w_x[0], b_x[0], lam[0], reverse=False)
    h_bwd = rglru_direction(xc, w_a[1], b_a[1], w_x[1], b_x[1], lam[1], reverse=True)
    y_lru = (h_fwd + h_bwd).astype(h.dtype)
    y_mem = memory_cross_attention(q_mem, mem, mem_g, w_mem_kv)
    y = jnp.concatenate([rmsnorm(y_att, g_att) * jax.nn.silu(z_att),
                         rmsnorm(y_lru, g_lru) * jax.nn.silu(z_lru),
                         rmsnorm(y_mem, g_mem) * jax.nn.silu(z_mem)], axis=-1)
    out = y @ w_out
    return h + rmsnorm(out, post_g)


def setup_inputs(seed: int = 0) -> dict:
    key = jax.random.key(seed)
    ks = jax.random.split(key, 20)
    f32 = jnp.float32
    nrm = lambda k, shape, scale: jax.random.normal(k, shape, f32) * scale
    gain = lambda k, shape: 1.0 + 0.1 * jax.random.normal(k, shape, f32)
    u = jax.random.uniform(ks[11], (DEPTH, 2, LRU_WIDTH), f32, 0.9, 0.999)
    a0 = jnp.power(u, 1.0 / LRU_C)
    lam = jnp.log(a0) - jnp.log1p(-a0)
    return {
        "x": nrm(ks[0], (BATCH, SEQ, D_MODEL), 1.0),
        "mem": nrm(ks[1], (BATCH, N_MEM, D_MODEL), 1.0),
        "pre_norm_gain": gain(ks[2], (DEPTH, D_MODEL)),
        "w_in": nrm(ks[3], (DEPTH, D_MODEL, IN_WIDTH), D_MODEL ** -0.5),
        "att_sink": nrm(ks[4], (DEPTH, ATT_HEADS), 0.5),
        "conv_w": nrm(ks[5], (DEPTH, CONV_WIDTH, LRU_WIDTH), CONV_WIDTH ** -0.5),
        "conv_b": nrm(ks[6], (DEPTH, LRU_WIDTH), 0.02),
        "lru_w_a": nrm(ks[7], (DEPTH, 2, LRU_BLOCKS, LRU_BLOCK_DIM, LRU_BLOCK_DIM), LRU_BLOCK_DIM ** -0.5),
        "lru_b_a": nrm(ks[8], (DEPTH, 2, LRU_WIDTH), 0.02),
        "lru_w_x": nrm(ks[9], (DEPTH, 2, LRU_BLOCKS, LRU_BLOCK_DIM, LRU_BLOCK_DIM), LRU_BLOCK_DIM ** -0.5),
        "lru_b_x": nrm(ks[10], (DEPTH, 2, LRU_WIDTH), 0.02),
        "lru_lambda": lam,
        "mem_norm_gain": gain(ks[12], (DEPTH, D_MODEL)),
        "w_mem_kv": nrm(ks[13], (DEPTH, D_MODEL, 2 * MEM_WIDTH), D_MODEL ** -0.5),
        "att_out_gain": gain(ks[14], (DEPTH, ATT_WIDTH)),
        "lru_out_gain": gain(ks[15], (DEPTH, LRU_WIDTH)),
        "mem_out_gain": gain(ks[16], (DEPTH, MEM_WIDTH)),
        "w_out": nrm(ks[17], (DEPTH, D_MIX, D_MODEL), D_MIX ** -0.5),
        "post_norm_gain": gain(ks[18], (DEPTH, D_MODEL)),
    }


def reference(x, mem, pre_norm_gain, w_in, att_sink, conv_w, conv_b, lru_w_a, lru_b_a, lru_w_x,
              lru_b_x, lru_lambda, mem_norm_gain, w_mem_kv, att_out_gain, lru_out_gain,
              mem_out_gain, w_out, post_norm_gain):
    S = x.shape[1]
    pos = jnp.arange(S, dtype=jnp.int32)
    h = x
    for l in range(DEPTH):
        h = hybrid_layer(h, mem, pos, pre_norm_gain[l], w_in[l], att_sink[l], conv_w[l], conv_b[l],
                         lru_w_a[l], lru_b_a[l], lru_w_x[l], lru_b_x[l], lru_lambda[l],
                         mem_norm_gain[l], w_mem_kv[l], att_out_gain[l], lru_out_gain[l],
                         mem_out_gain[l], w_out[l], post_norm_gain[l])
    return h
```

```python
import functools
import math

import jax
import jax.numpy as jnp
from jax import lax
from jax.experimental import pallas as pl
from jax.experimental.pallas import tpu as pltpu

F32 = jnp.float32
BF16 = jnp.bfloat16

D_MODEL = 4096
SEQ = 8192
N_MEM = 256
HEAD_DIM = 128
ATT_WIDTH = 2048
ATT_HEADS = 16
ATT_KV_HEADS = 4
ATT_GROUP = ATT_HEADS // ATT_KV_HEADS
KV_WIDTH = ATT_KV_HEADS * HEAD_DIM
BLOCK = 128
ROPE_THETA = 500000.0
ROT_DIM = HEAD_DIM // 4
ROT_HALF = ROT_DIM // 2
LRU_WIDTH = 1024
LRU_BLOCKS = 8
LRU_BLOCK_DIM = LRU_WIDTH // LRU_BLOCKS
CONV_WIDTH = 4
CONV_LEFT = 2
LRU_C = 8.0
MEM_HEADS = 4
MEM_WIDTH = 1024
MEM_HEAD_DIM = MEM_WIDTH // MEM_HEADS
EPS = 1e-6
IN_WIDTH = 9216

COL_XLRU = 3
COL_QMEM = 4
COL_ZATT = 5
COL_ZLRU = 7
COL_ZMEM = 8
COL_K512 = 4
COL_V512 = 5

VMEM_LIMIT_BYTES = 56 * 1024 * 1024
BF16_SUBLANES = 16

NORM_ROWS = 256
MM_TM = 1024
MM_TN = 1024
LRU_CHUNK = 256
LRU_SEG = LRU_CHUNK // 8
MIX_ROWS = 512


def _params(*sem):
    return pltpu.CompilerParams(dimension_semantics=sem, vmem_limit_bytes=VMEM_LIMIT_BYTES)


def _sigmoid(t):
    return 1.0 / (1.0 + jnp.exp(-t))


def _norm_gate(y, g, z):
    ms = jnp.mean(y * y, axis=-1, keepdims=True)
    return y * lax.rsqrt(ms + EPS) * g * (z * _sigmoid(z))


def _rmsnorm_kernel(x_ref, g_ref, o_ref):
    x = x_ref[...]
    ms = jnp.mean(x * x, axis=-1, keepdims=True)
    o_ref[...] = (x * lax.rsqrt(ms + EPS) * g_ref[...]).astype(o_ref.dtype)


def _rmsnorm_cast(x, g):
    m, d = x.shape
    rows = min(NORM_ROWS, m)
    return pl.pallas_call(
        _rmsnorm_kernel,
        grid=(m // rows,),
        in_specs=[pl.BlockSpec((rows, d), lambda i: (i, 0)),
                  pl.BlockSpec((1, d), lambda i: (0, 0))],
        out_specs=pl.BlockSpec((rows, d), lambda i: (i, 0)),
        out_shape=jax.ShapeDtypeStruct((m, d), BF16),
        compiler_params=_params("parallel"),
        name="rmsnorm_cast",
    )(x, g.reshape(1, d))


def _matmul_kernel(a_ref, b_ref, o_ref):
    o_ref[...] = jnp.dot(a_ref[...], b_ref[...], preferred_element_type=F32).astype(o_ref.dtype)


def _matmul(a, b, out_dtype, name):
    m, k = a.shape
    _, n = b.shape
    tm = min(MM_TM, m)
    tn = min(MM_TN, n)
    return pl.pallas_call(
        _matmul_kernel,
        grid=(m // tm, n // tn),
        in_specs=[pl.BlockSpec((tm, k), lambda i, j: (i, 0)),
                  pl.BlockSpec((k, tn), lambda i, j: (0, j))],
        out_specs=pl.BlockSpec((tm, tn), lambda i, j: (i, j)),
        out_shape=jax.ShapeDtypeStruct((m, n), out_dtype),
        compiler_params=_params("parallel", "parallel"),
        name=name,
    )(a, b)


def _rope_tables(block_idx, freq):
    row = lax.broadcasted_iota(jnp.int32, (BLOCK, HEAD_DIM), 0)
    lane = lax.broadcasted_iota(jnp.int32, (BLOCK, HEAD_DIM), 1)
    ang = (block_idx * BLOCK + row).astype(F32) * freq
    c = jnp.cos(ang)
    s = jnp.sin(ang)
    return c, jnp.where(lane < ROT_HALF, -s, 0.0), jnp.where(lane >= ROT_HALF, s, 0.0)


def _rope(t, tables):
    c, sin_lo, sin_hi = tables
    return (t * c + pltpu.roll(t, HEAD_DIM - ROT_HALF, axis=1) * sin_lo
            + pltpu.roll(t, ROT_HALF, axis=1) * sin_hi)


def _win_attn_kernel(sink_ref, q_ref, kp_ref, kc_ref, kn_ref, vp_ref, vc_ref, vn_ref,
                     z0_ref, z1_ref, freq_ref, g_ref, o_ref, y_scr):
    n = pl.program_id(0)
    nb = pl.num_programs(0)
    freq = freq_ref[...]
    tab_p = _rope_tables(n - 1, freq)
    tab_c = _rope_tables(n, freq)
    tab_n = _rope_tables(n + 1, freq)

    rows = ATT_GROUP * BLOCK
    q_local = lax.broadcasted_iota(jnp.int32, (rows, 3 * BLOCK), 0) & (BLOCK - 1)
    col = lax.broadcasted_iota(jnp.int32, (rows, 3 * BLOCK), 1)
    lo = jnp.where(n > 0, q_local, BLOCK)
    hi = jnp.where(n < nb - 1, q_local + 2 * BLOCK, 2 * BLOCK - 1)
    valid = (col >= lo) & (col <= hi)
    row_group = lax.broadcasted_iota(jnp.int32, (rows, 1), 0) // BLOCK
    neg = jnp.finfo(F32).min
    scale = HEAD_DIM ** -0.5

    for h in range(ATT_KV_HEADS):
        ks = slice(h * HEAD_DIM, (h + 1) * HEAD_DIM)
        k_all = jnp.concatenate(
            [_rope(kp_ref[:, ks].astype(F32), tab_p).astype(BF16),
             _rope(kc_ref[:, ks].astype(F32), tab_c).astype(BF16),
             _rope(kn_ref[:, ks].astype(F32), tab_n).astype(BF16)], axis=0)
        v_all = jnp.concatenate([vp_ref[:, ks], vc_ref[:, ks], vn_ref[:, ks]], axis=0)
        q_all = jnp.concatenate(
            [(_rope(q_ref[:, (h * ATT_GROUP + g) * HEAD_DIM:(h * ATT_GROUP + g + 1) * HEAD_DIM]
                    .astype(F32), tab_c) * scale).astype(BF16)
             for g in range(ATT_GROUP)], axis=0)
        sk = jnp.zeros((rows, 1), F32)
        for g in range(ATT_GROUP):
            sk = jnp.where(row_group == g, sink_ref[h * ATT_GROUP + g], sk)
        s = lax.dot_general(q_all, k_all, (((1,), (1,)), ((), ())), preferred_element_type=F32)
        s = jnp.where(valid, s, neg)
        m = jnp.maximum(jnp.max(s, axis=-1, keepdims=True), sk)
        p = jnp.exp(s - m)
        denom = jnp.sum(p, axis=-1, keepdims=True) + jnp.exp(sk - m)
        o = jnp.dot(p.astype(BF16), v_all, preferred_element_type=F32) / denom
        for g in range(ATT_GROUP):
            hq = h * ATT_GROUP + g
            y_scr[:, hq * HEAD_DIM:(hq + 1) * HEAD_DIM] = o[g * BLOCK:(g + 1) * BLOCK]

    z = jnp.concatenate([z0_ref[...], z1_ref[...]], axis=1).astype(F32)
    o_ref[...] = _norm_gate(y_scr[...], g_ref[...], z).astype(o_ref.dtype)


def _win_attn(proj, sink, g_att):
    s_len = proj.shape[0]
    nb = s_len // BLOCK
    inv_freq = jnp.power(jnp.float32(ROPE_THETA),
                         -jnp.arange(ROT_HALF, dtype=F32) * 2.0 / ROT_DIM)
    freq = jnp.concatenate([inv_freq, inv_freq, jnp.zeros((HEAD_DIM - ROT_DIM,), F32)]).reshape(1, HEAD_DIM)

    def kv_spec(col, shift):
        return pl.BlockSpec((BLOCK, KV_WIDTH),
                            lambda n, sink: (jnp.clip(n + shift, 0, nb - 1), col))

    grid_spec = pltpu.PrefetchScalarGridSpec(
        num_scalar_prefetch=1,
        grid=(nb,),
        in_specs=[pl.BlockSpec((BLOCK, ATT_WIDTH), lambda n, sink: (n, 0)),
                  kv_spec(COL_K512, -1), kv_spec(COL_K512, 0), kv_spec(COL_K512, 1),
                  kv_spec(COL_V512, -1), kv_spec(COL_V512, 0), kv_spec(COL_V512, 1),
                  pl.BlockSpec((BLOCK, 1024), lambda n, sink: (n, COL_ZATT)),
                  pl.BlockSpec((BLOCK, 1024), lambda n, sink: (n, COL_ZATT + 1)),
                  pl.BlockSpec((1, HEAD_DIM), lambda n, sink: (0, 0)),
                  pl.BlockSpec((1, ATT_WIDTH), lambda n, sink: (0, 0))],
        out_specs=pl.BlockSpec((BLOCK, ATT_WIDTH), lambda n, sink: (n, 0)),
        scratch_shapes=[pltpu.VMEM((BLOCK, ATT_WIDTH), F32)],
    )
    return pl.pallas_call(
        _win_attn_kernel,
        grid_spec=grid_spec,
        out_shape=jax.ShapeDtypeStruct((s_len, ATT_WIDTH), BF16),
        compiler_params=_params("parallel"),
        name="win_attn",
    )(sink, proj, proj, proj, proj, proj, proj, proj, proj, proj, freq, g_att.reshape(1, ATT_WIDTH))


def _lru_direction(d, reverse, chunk, n_chunks, x_ref, xp_ref, xn_ref, cw_ref, cb_ref, wg_ref,
                   bg_ref, lam_ref, out_ref, xe_scr, a_scr, b_scr, h_scr, p_scr, carry_scr):
    t = LRU_CHUNK
    halo = BF16_SUBLANES
    xe_scr, a_scr, b_scr, h_scr, p_scr = (r.at[d] for r in (xe_scr, a_scr, b_scr, h_scr, p_scr))
    xe_scr[0:halo, :] = jnp.where(chunk > 0, xp_ref[...].astype(F32), 0.0)
    xe_scr[halo:halo + t, :] = x_ref[...].astype(F32)
    xe_scr[halo + t:, :] = jnp.where(chunk < n_chunks - 1, xn_ref[...].astype(F32), 0.0)
    xc = jnp.zeros((t, LRU_WIDTH), F32) + cb_ref[...]
    for j in range(CONV_WIDTH):
        xc = xc + cw_ref[j:j + 1, :] * xe_scr[pl.ds(halo - CONV_LEFT + j, t), :]

    lam = lam_ref[d:d + 1, :]
    neg_lam = -lam
    softplus = jnp.maximum(neg_lam, 0.0) + jnp.log(1.0 + jnp.exp(-jnp.abs(neg_lam)))
    decay = -LRU_C * softplus
    for blk in range(LRU_BLOCKS):
        cs = slice(blk * LRU_BLOCK_DIM, (blk + 1) * LRU_BLOCK_DIM)
        xb = xc[:, cs]
        gates = jnp.dot(xb.astype(BF16), wg_ref[d, blk], preferred_element_type=F32)
        r = _sigmoid(gates[:, :LRU_BLOCK_DIM] + bg_ref[d, 0:1, cs])
        i = _sigmoid(gates[:, LRU_BLOCK_DIM:] + bg_ref[d, 1:2, cs])
        log_a = decay[:, cs] * r
        a = jnp.exp(log_a)
        mult = jnp.sqrt(1.0 - a * a)
        a_scr[blk] = a
        b_scr[blk] = mult * i * xb

    hh = [jnp.zeros((8, LRU_BLOCK_DIM), F32)] * LRU_BLOCKS
    pp = [jnp.ones((8, LRU_BLOCK_DIM), F32)] * LRU_BLOCKS
    steps = range(LRU_SEG - 1, -1, -1) if reverse else range(LRU_SEG)
    for m in steps:
        idx = pl.ds(m, 8, stride=LRU_SEG)
        for blk in range(LRU_BLOCKS):
            aa = a_scr[blk, idx, :]
            hh[blk] = aa * hh[blk] + b_scr[blk, idx, :]
            pp[blk] = aa * pp[blk]
            h_scr[blk, idx, :] = hh[blk]
            p_scr[blk, idx, :] = pp[blk]
    segs = range(7, -1, -1) if reverse else range(8)
    for blk in range(LRU_BLOCKS):
        cs = slice(blk * LRU_BLOCK_DIM, (blk + 1) * LRU_BLOCK_DIM)
        cin = carry_scr[d:d + 1, cs]
        for s in segs:
            rs = slice(s * LRU_SEG, (s + 1) * LRU_SEG)
            out_ref[rs, cs] = h_scr[blk, rs, :] + p_scr[blk, rs, :] * cin
            cin = pp[blk][s:s + 1, :] * cin + hh[blk][s:s + 1, :]
        carry_scr[d:d + 1, cs] = cin


def _lru_kernel(xf_ref, xfp_ref, xfn_ref, xb_ref, xbp_ref, xbn_ref, cw_ref, cb_ref, wg_ref, bg_ref,
                lam_ref, hf_ref, hb_ref, xe_scr, a_scr, b_scr, h_scr, p_scr, carry_scr):
    i = pl.program_id(0)
    nc = pl.num_programs(0)

    @pl.when(i == 0)
    def _():
        carry_scr[...] = jnp.zeros_like(carry_scr)

    shared = (cw_ref, cb_ref, wg_ref, bg_ref, lam_ref)
    scr = (xe_scr, a_scr, b_scr, h_scr, p_scr, carry_scr)
    _lru_direction(0, False, i, nc, xf_ref, xfp_ref, xfn_ref, *shared, hf_ref, *scr)
    _lru_direction(1, True, nc - 1 - i, nc, xb_ref, xbp_ref, xbn_ref, *shared, hb_ref, *scr)


def _lru(proj, conv_w, conv_b, w_a, b_a, w_x, b_x, lam):
    s_len = proj.shape[0]
    nc = s_len // LRU_CHUNK
    halo_per_chunk = LRU_CHUNK // BF16_SUBLANES
    n_halo = s_len // BF16_SUBLANES
    wg = jnp.concatenate([w_a, w_x], axis=-1).astype(BF16)
    bg = jnp.stack([b_a, b_x], axis=1)

    def fwd(i):
        return i

    def bwd(i):
        return nc - 1 - i

    def specs(chunk_of):
        return [
            pl.BlockSpec((LRU_CHUNK, LRU_WIDTH), lambda i: (chunk_of(i), COL_XLRU)),
            pl.BlockSpec((BF16_SUBLANES, LRU_WIDTH),
                         lambda i: (jnp.maximum(chunk_of(i) * halo_per_chunk - 1, 0), COL_XLRU)),
            pl.BlockSpec((BF16_SUBLANES, LRU_WIDTH),
                         lambda i: (jnp.minimum((chunk_of(i) + 1) * halo_per_chunk, n_halo - 1), COL_XLRU)),
        ]

    def whole(shape):
        return pl.BlockSpec(shape, lambda i: (0,) * len(shape))

    return pl.pallas_call(
        _lru_kernel,
        grid=(nc,),
        in_specs=specs(fwd) + specs(bwd) + [
            whole((CONV_WIDTH, LRU_WIDTH)), whole((1, LRU_WIDTH)),
            whole((2, LRU_BLOCKS, LRU_BLOCK_DIM, 2 * LRU_BLOCK_DIM)),
            whole((2, 2, LRU_WIDTH)), whole((2, LRU_WIDTH))],
        out_specs=[pl.BlockSpec((LRU_CHUNK, LRU_WIDTH), lambda i: (i, 0)),
                   pl.BlockSpec((LRU_CHUNK, LRU_WIDTH), lambda i: (nc - 1 - i, 0))],
        out_shape=[jax.ShapeDtypeStruct((s_len, LRU_WIDTH), F32),
                   jax.ShapeDtypeStruct((s_len, LRU_WIDTH), F32)],
        scratch_shapes=[pltpu.VMEM((2, LRU_CHUNK + 2 * BF16_SUBLANES, LRU_WIDTH), F32),
                        pltpu.VMEM((2, LRU_BLOCKS, LRU_CHUNK, LRU_BLOCK_DIM), F32),
                        pltpu.VMEM((2, LRU_BLOCKS, LRU_CHUNK, LRU_BLOCK_DIM), F32),
                        pltpu.VMEM((2, LRU_BLOCKS, LRU_CHUNK, LRU_BLOCK_DIM), F32),
                        pltpu.VMEM((2, LRU_BLOCKS, LRU_CHUNK, LRU_BLOCK_DIM), F32),
                        pltpu.VMEM((2, LRU_WIDTH), F32)],
        compiler_params=_params("arbitrary"),
        name="rglru",
    )(proj, proj, proj, proj, proj, proj, conv_w, conv_b.reshape(1, LRU_WIDTH), wg, bg, lam)


def _mix_kernel(q_ref, kv_ref, zm_ref, gm_ref, hf_ref, hb_ref, zl_ref, gl_ref, ym_ref, yl_ref, y_scr):
    scale = MEM_HEAD_DIM ** -0.5
    for h in range(MEM_HEADS):
        cs = slice(h * MEM_HEAD_DIM, (h + 1) * MEM_HEAD_DIM)
        vs = slice(MEM_WIDTH + h * MEM_HEAD_DIM, MEM_WIDTH + (h + 1) * MEM_HEAD_DIM)
        s = lax.dot_general(q_ref[:, cs], kv_ref[:, cs], (((1,), (1,)), ((), ())),
                            preferred_element_type=F32) * scale
        m = jnp.max(s, axis=-1, keepdims=True)
        p = jnp.exp(s - m)
        denom = jnp.sum(p, axis=-1, keepdims=True)
        y_scr[:, cs] = jnp.dot(p.astype(BF16), kv_ref[:, vs], preferred_element_type=F32) / denom
    ym_ref[...] = _norm_gate(y_scr[...], gm_ref[...], zm_ref[...].astype(F32)).astype(ym_ref.dtype)
    yl_ref[...] = _norm_gate(hf_ref[...] + hb_ref[...], gl_ref[...],
                             zl_ref[...].astype(F32)).astype(yl_ref.dtype)


def _mix(proj, kv, h_fwd, h_bwd, g_mem, g_lru):
    s_len = proj.shape[0]
    rows = MIX_ROWS
    row_block = lambda col: pl.BlockSpec((rows, 1024), lambda i: (i, col))
    const = lambda shape: pl.BlockSpec(shape, lambda i: (0, 0))
    return pl.pallas_call(
        _mix_kernel,
        grid=(s_len // rows,),
        in_specs=[row_block(COL_QMEM), const((N_MEM, 2 * MEM_WIDTH)), row_block(COL_ZMEM),
                  const((1, MEM_WIDTH)), row_block(0), row_block(0), row_block(COL_ZLRU),
                  const((1, LRU_WIDTH))],
        out_specs=[row_block(0), row_block(0)],
        out_shape=[jax.ShapeDtypeStruct((s_len, MEM_WIDTH), BF16),
                   jax.ShapeDtypeStruct((s_len, LRU_WIDTH), BF16)],
        scratch_shapes=[pltpu.VMEM((rows, MEM_WIDTH), F32)],
        compiler_params=_params("parallel"),
        name="mem_attn_lru_finish",
    )(proj, kv, proj, g_mem.reshape(1, MEM_WIDTH), h_fwd, h_bwd, proj, g_lru.reshape(1, LRU_WIDTH))


def _out_proj_kernel(ya_ref, yl_ref, ym_ref, wa_ref, wl_ref, wm_ref, o_ref):
    acc = jnp.dot(ya_ref[...], wa_ref[...], preferred_element_type=F32)
    acc += jnp.dot(yl_ref[...], wl_ref[...], preferred_element_type=F32)
    acc += jnp.dot(ym_ref[...], wm_ref[...], preferred_element_type=F32)
    o_ref[...] = acc.astype(o_ref.dtype)


def _out_proj(y_att, y_lru, y_mem, w_out):
    s_len = y_att.shape[0]
    tm, tn = MM_TM, MM_TN
    lru_row = ATT_WIDTH // LRU_WIDTH
    return pl.pallas_call(
        _out_proj_kernel,
        grid=(s_len // tm, D_MODEL // tn),
        in_specs=[pl.BlockSpec((tm, ATT_WIDTH), lambda i, j: (i, 0)),
                  pl.BlockSpec((tm, LRU_WIDTH), lambda i, j: (i, 0)),
                  pl.BlockSpec((tm, MEM_WIDTH), lambda i, j: (i, 0)),
                  pl.BlockSpec((ATT_WIDTH, tn), lambda i, j: (0, j)),
                  pl.BlockSpec((LRU_WIDTH, tn), lambda i, j: (lru_row, j)),
                  pl.BlockSpec((MEM_WIDTH, tn), lambda i, j: (lru_row + 1, j))],
        out_specs=pl.BlockSpec((tm, tn), lambda i, j: (i, j)),
        out_shape=jax.ShapeDtypeStruct((s_len, D_MODEL), BF16),
        compiler_params=_params("parallel", "parallel"),
        name="out_proj",
    )(y_att, y_lru, y_mem, w_out, w_out, w_out)


def _post_kernel(x_ref, o_ref, g_ref, y_ref):
    o = o_ref[...].astype(F32)
    ms = jnp.mean(o * o, axis=-1, keepdims=True)
    y_ref[...] = x_ref[...] + o * lax.rsqrt(ms + EPS) * g_ref[...]


def _post(x, out, g):
    m, d = x.shape
    rows = NORM_ROWS
    return pl.pallas_call(
        _post_kernel,
        grid=(m // rows,),
        in_specs=[pl.BlockSpec((rows, d), lambda i: (i, 0)),
                  pl.BlockSpec((rows, d), lambda i: (i, 0)),
                  pl.BlockSpec((1, d), lambda i: (0, 0))],
        out_specs=pl.BlockSpec((rows, d), lambda i: (i, 0)),
        out_shape=jax.ShapeDtypeStruct((m, d), F32),
        compiler_params=_params("parallel"),
        name="post_norm_residual",
    )(x, out, g.reshape(1, d))


def _layer(h, mem, pre_g, w_in, sink, conv_w, conv_b, w_a, b_a, w_x, b_x, lam, mem_g, w_mem_kv,
           g_att, g_lru, g_mem, w_out, post_g):
    u = _rmsnorm_cast(h, pre_g)
    proj = _matmul(u, w_in.astype(BF16), BF16, "in_proj")
    kv = _matmul(_rmsnorm_cast(mem, mem_g), w_mem_kv.astype(BF16), BF16, "mem_kv_proj")
    y_att = _win_attn(proj, sink, g_att)
    h_fwd, h_bwd = _lru(proj, conv_w, conv_b, w_a, b_a, w_x, b_x, lam)
    y_mem, y_lru = _mix(proj, kv, h_fwd, h_bwd, g_mem, g_lru)
    out = _out_proj(y_att, y_lru, y_mem, w_out.astype(BF16))
    return _post(h, out, post_g)


def kernel(x, mem, pre_norm_gain, w_in, att_sink, conv_w, conv_b, lru_w_a, lru_b_a, lru_w_x, lru_b_x,
           lru_lambda, mem_norm_gain, w_mem_kv, att_out_gain, lru_out_gain, mem_out_gain, w_out,
           post_norm_gain):
    batch = x.shape[0]
    depth = w_in.shape[0]
    outs = []
    for b in range(batch):
        h = x[b]
        for l in range(depth):
            h = _layer(h, mem[b], pre_norm_gain[l], w_in[l], att_sink[l], conv_w[l], conv_b[l],
                       lru_w_a[l], lru_b_a[l], lru_w_x[l], lru_b_x[l], lru_lambda[l],
                       mem_norm_gain[l], w_mem_kv[l], att_out_gain[l], lru_out_gain[l],
                       mem_out_gain[l], w_out[l], post_norm_gain[l])
        outs.append(h)
    return jnp.stack(outs, axis=0)
```

```python
import functools
import math

import jax
import jax.numpy as jnp
from jax import lax
from jax.experimental import pallas as pl
from jax.experimental.pallas import tpu as pltpu

F32 = jnp.float32
BF16 = jnp.bfloat16

D_MODEL = 4096
SEQ = 8192
N_MEM = 256
HEAD_DIM = 128
ATT_WIDTH = 2048
ATT_HEADS = 16
ATT_KV_HEADS = 4
ATT_GROUP = ATT_HEADS // ATT_KV_HEADS
KV_WIDTH = ATT_KV_HEADS * HEAD_DIM
BLOCK = 128
ROPE_THETA = 500000.0
ROT_DIM = HEAD_DIM // 4
ROT_HALF = ROT_DIM // 2
LRU_WIDTH = 1024
LRU_BLOCKS = 8
LRU_BLOCK_DIM = LRU_WIDTH // LRU_BLOCKS
CONV_WIDTH = 4
CONV_LEFT = 2
LRU_C = 8.0
MEM_HEADS = 4
MEM_WIDTH = 1024
MEM_HEAD_DIM = MEM_WIDTH // MEM_HEADS
EPS = 1e-6
IN_WIDTH = 9216

COL_XLRU = 3
COL_QMEM = 4
COL_ZATT = 5
COL_ZLRU = 7
COL_ZMEM = 8
COL_K512 = 4
COL_V512 = 5

VMEM_LIMIT_BYTES = 56 * 1024 * 1024
BF16_SUBLANES = 16

NORM_ROWS = 256
MM_TM = 1024
MM_TN = 1024
LRU_CHUNK = 512
LRU_SUBCHUNK_SEGS = (36, 28)
assert 8 * sum(LRU_SUBCHUNK_SEGS) == LRU_CHUNK and all(s % 8 == 4 for s in LRU_SUBCHUNK_SEGS)
MIX_ROWS = 512


def _params(*sem):
    return pltpu.CompilerParams(dimension_semantics=sem, vmem_limit_bytes=VMEM_LIMIT_BYTES)


def _sigmoid(t):
    return 1.0 / (1.0 + jnp.exp(-t))


def _norm_gate(y, g, z):
    ms = jnp.mean(y * y, axis=-1, keepdims=True)
    return y * lax.rsqrt(ms + EPS) * g * (z * _sigmoid(z))


def _rmsnorm_kernel(x_ref, g_ref, o_ref):
    x = x_ref[...]
    ms = jnp.mean(x * x, axis=-1, keepdims=True)
    o_ref[...] = (x * lax.rsqrt(ms + EPS) * g_ref[...]).astype(o_ref.dtype)


def _rmsnorm_cast(x, g):
    m, d = x.shape
    rows = min(NORM_ROWS, m)
    return pl.pallas_call(
        _rmsnorm_kernel,
        grid=(m // rows,),
        in_specs=[pl.BlockSpec((rows, d), lambda i: (i, 0)),
                  pl.BlockSpec((1, d), lambda i: (0, 0))],
        out_specs=pl.BlockSpec((rows, d), lambda i: (i, 0)),
        out_shape=jax.ShapeDtypeStruct((m, d), BF16),
        compiler_params=_params("parallel"),
        name="rmsnorm_cast",
    )(x, g.reshape(1, d))


def _matmul_kernel(a_ref, b_ref, o_ref):
    o_ref[...] = jnp.dot(a_ref[...], b_ref[...], preferred_element_type=F32).astype(o_ref.dtype)


def _matmul(a, b, out_dtype, name):
    m, k = a.shape
    _, n = b.shape
    tm = min(MM_TM, m)
    tn = min(MM_TN, n)
    return pl.pallas_call(
        _matmul_kernel,
        grid=(m // tm, n // tn),
        in_specs=[pl.BlockSpec((tm, k), lambda i, j: (i, 0)),
                  pl.BlockSpec((k, tn), lambda i, j: (0, j))],
        out_specs=pl.BlockSpec((tm, tn), lambda i, j: (i, j)),
        out_shape=jax.ShapeDtypeStruct((m, n), out_dtype),
        compiler_params=_params("parallel", "parallel"),
        name=name,
    )(a, b)


def _rope_tables(block_idx, freq):
    row = lax.broadcasted_iota(jnp.int32, (BLOCK, HEAD_DIM), 0)
    lane = lax.broadcasted_iota(jnp.int32, (BLOCK, HEAD_DIM), 1)
    ang = (block_idx * BLOCK + row).astype(F32) * freq
    c = jnp.cos(ang)
    s = jnp.sin(ang)
    return c, jnp.where(lane < ROT_HALF, -s, 0.0), jnp.where(lane >= ROT_HALF, s, 0.0)


def _rope(t, tables):
    c, sin_lo, sin_hi = tables
    return (t * c + pltpu.roll(t, HEAD_DIM - ROT_HALF, axis=1) * sin_lo
            + pltpu.roll(t, ROT_HALF, axis=1) * sin_hi)


def _win_attn_kernel(sink_ref, q_ref, kp_ref, kc_ref, kn_ref, vp_ref, vc_ref, vn_ref,
                     z0_ref, z1_ref, freq_ref, g_ref, o_ref, y_scr):
    n = pl.program_id(0)
    nb = pl.num_programs(0)
    freq = freq_ref[...]
    tab_p = _rope_tables(n - 1, freq)
    tab_c = _rope_tables(n, freq)
    tab_n = _rope_tables(n + 1, freq)

    rows = ATT_GROUP * BLOCK
    q_local = lax.broadcasted_iota(jnp.int32, (rows, 3 * BLOCK), 0) & (BLOCK - 1)
    col = lax.broadcasted_iota(jnp.int32, (rows, 3 * BLOCK), 1)
    lo = jnp.where(n > 0, q_local, BLOCK)
    hi = jnp.where(n < nb - 1, q_local + 2 * BLOCK, 2 * BLOCK - 1)
    valid = (col >= lo) & (col <= hi)
    row_group = lax.broadcasted_iota(jnp.int32, (rows, 1), 0) // BLOCK
    neg = jnp.finfo(F32).min
    scale = HEAD_DIM ** -0.5

    for h in range(ATT_KV_HEADS):
        ks = slice(h * HEAD_DIM, (h + 1) * HEAD_DIM)
        k_all = jnp.concatenate(
            [_rope(kp_ref[:, ks].astype(F32), tab_p).astype(BF16),
             _rope(kc_ref[:, ks].astype(F32), tab_c).astype(BF16),
             _rope(kn_ref[:, ks].astype(F32), tab_n).astype(BF16)], axis=0)
        v_all = jnp.concatenate([vp_ref[:, ks], vc_ref[:, ks], vn_ref[:, ks]], axis=0)
        q_all = jnp.concatenate(
            [(_rope(q_ref[:, (h * ATT_GROUP + g) * HEAD_DIM:(h * ATT_GROUP + g + 1) * HEAD_DIM]
                    .astype(F32), tab_c) * scale).astype(BF16)
             for g in range(ATT_GROUP)], axis=0)
        sk = jnp.zeros((rows, 1), F32)
        for g in range(ATT_GROUP):
            sk = jnp.where(row_group == g, sink_ref[h * ATT_GROUP + g], sk)
        s = lax.dot_general(q_all, k_all, (((1,), (1,)), ((), ())), preferred_element_type=F32)
        s = jnp.where(valid, s, neg)
        m = jnp.maximum(jnp.max(s, axis=-1, keepdims=True), sk)
        p = jnp.exp(s - m)
        denom = jnp.sum(p, axis=-1, keepdims=True) + jnp.exp(sk - m)
        o = jnp.dot(p.astype(BF16), v_all, preferred_element_type=F32) / denom
        for g in range(ATT_GROUP):
            hq = h * ATT_GROUP + g
            y_scr[:, hq * HEAD_DIM:(hq + 1) * HEAD_DIM] = o[g * BLOCK:(g + 1) * BLOCK]

    z = jnp.concatenate([z0_ref[...], z1_ref[...]], axis=1).astype(F32)
    o_ref[...] = _norm_gate(y_scr[...], g_ref[...], z).astype(o_ref.dtype)


def _win_attn(proj, sink, g_att):
    s_len = proj.shape[0]
    nb = s_len // BLOCK
    inv_freq = jnp.power(jnp.float32(ROPE_THETA),
                         -jnp.arange(ROT_HALF, dtype=F32) * 2.0 / ROT_DIM)
    freq = jnp.concatenate([inv_freq, inv_freq, jnp.zeros((HEAD_DIM - ROT_DIM,), F32)]).reshape(1, HEAD_DIM)

    def kv_spec(col, shift):
        return pl.BlockSpec((BLOCK, KV_WIDTH),
                            lambda n, sink: (jnp.clip(n + shift, 0, nb - 1), col))

    grid_spec = pltpu.PrefetchScalarGridSpec(
        num_scalar_prefetch=1,
        grid=(nb,),
        in_specs=[pl.BlockSpec((BLOCK, ATT_WIDTH), lambda n, sink: (n, 0)),
                  kv_spec(COL_K512, -1), kv_spec(COL_K512, 0), kv_spec(COL_K512, 1),
                  kv_spec(COL_V512, -1), kv_spec(COL_V512, 0), kv_spec(COL_V512, 1),
                  pl.BlockSpec((BLOCK, 1024), lambda n, sink: (n, COL_ZATT)),
                  pl.BlockSpec((BLOCK, 1024), lambda n, sink: (n, COL_ZATT + 1)),
                  pl.BlockSpec((1, HEAD_DIM), lambda n, sink: (0, 0)),
                  pl.BlockSpec((1, ATT_WIDTH), lambda n, sink: (0, 0))],
        out_specs=pl.BlockSpec((BLOCK, ATT_WIDTH), lambda n, sink: (n, 0)),
        scratch_shapes=[pltpu.VMEM((BLOCK, ATT_WIDTH), F32)],
    )
    return pl.pallas_call(
        _win_attn_kernel,
        grid_spec=grid_spec,
        out_shape=jax.ShapeDtypeStruct((s_len, ATT_WIDTH), BF16),
        compiler_params=_params("parallel"),
        name="win_attn",
    )(sink, proj, proj, proj, proj, proj, proj, proj, proj, proj, freq, g_att.reshape(1, ATT_WIDTH))


def _lru_direction(d, reverse, chunk, n_chunks, x_ref, xp_ref, xn_ref, cw_ref, cb_ref, wg_ref,
                   bg_ref, lam_ref, out_ref, xe_scr, carry_scr):
    t = LRU_CHUNK
    halo = BF16_SUBLANES
    for blk in range(LRU_BLOCKS):
        cs = slice(blk * LRU_BLOCK_DIM, (blk + 1) * LRU_BLOCK_DIM)
        xe_scr[d, blk, 0:halo, :] = jnp.where(chunk > 0, xp_ref[:, cs].astype(F32), 0.0)
        xe_scr[d, blk, halo:halo + t, :] = x_ref[:, cs].astype(F32)
        xe_scr[d, blk, halo + t:, :] = jnp.where(chunk < n_chunks - 1, xn_ref[:, cs].astype(F32), 0.0)

    neg_lam = -lam_ref[d:d + 1, :]
    softplus = jnp.maximum(neg_lam, 0.0) + jnp.log(1.0 + jnp.exp(-jnp.abs(neg_lam)))
    decay = -LRU_C * softplus
    row0 = [8 * sum(LRU_SUBCHUNK_SEGS[:k]) for k in range(len(LRU_SUBCHUNK_SEGS))]
    subchunks = list(zip(row0, LRU_SUBCHUNK_SEGS))
    for blk in range(LRU_BLOCKS):
        cs = slice(blk * LRU_BLOCK_DIM, (blk + 1) * LRU_BLOCK_DIM)
        taps = [cw_ref[j:j + 1, cs] for j in range(CONV_WIDTH)]
        cin = carry_scr[d:d + 1, cs]
        for r0, seg in (reversed(subchunks) if reverse else subchunks):
            order = range(seg - 1, -1, -1) if reverse else range(seg)
            tiles = []
            for m in range(seg):
                acc = cb_ref[:, cs]
                for j in range(CONV_WIDTH):
                    start = halo + r0 + m + j - CONV_LEFT
                    acc = acc + taps[j] * xe_scr[d, blk, pl.ds(start, 8, stride=seg), :]
                tiles.append(acc)
            xc = jnp.concatenate(tiles, axis=0)
            gates = jnp.dot(xc.astype(BF16), wg_ref[d, blk], preferred_element_type=F32)
            r = _sigmoid(gates[:, :LRU_BLOCK_DIM] + bg_ref[d, 0:1, cs])
            i = _sigmoid(gates[:, LRU_BLOCK_DIM:] + bg_ref[d, 1:2, cs])
            a = jnp.exp(decay[:, cs] * r)
            b = jnp.sqrt(1.0 - a * a) * i * xc
            tile = lambda v, m: v[8 * m:8 * m + 8]
            hh = jnp.zeros((8, LRU_BLOCK_DIM), F32)
            pp = jnp.ones((8, LRU_BLOCK_DIM), F32)
            for m in order:
                hh = tile(a, m) * hh + tile(b, m)
                pp = tile(a, m) * pp
            entering = [None] * 8
            for s in (range(7, -1, -1) if reverse else range(8)):
                entering[s] = cin
                cin = pp[s:s + 1, :] * cin + hh[s:s + 1, :]
            h = jnp.concatenate(entering, axis=0)
            for m in order:
                h = tile(a, m) * h + tile(b, m)
                out_ref[blk, pl.ds(r0 + m, 8, stride=seg), :] = h
        carry_scr[d:d + 1, cs] = cin


def _lru_kernel(xf_ref, xfp_ref, xfn_ref, xb_ref, xbp_ref, xbn_ref, cw_ref, cb_ref, wg_ref, bg_ref,
                lam_ref, hf_ref, hb_ref, xe_scr, carry_scr):
    i = pl.program_id(0)
    nc = pl.num_programs(0)

    @pl.when(i == 0)
    def _():
        carry_scr[...] = jnp.zeros_like(carry_scr)

    shared = (cw_ref, cb_ref, wg_ref, bg_ref, lam_ref)
    scr = (xe_scr, carry_scr)
    _lru_direction(0, False, i, nc, xf_ref, xfp_ref, xfn_ref, *shared, hf_ref, *scr)
    _lru_direction(1, True, nc - 1 - i, nc, xb_ref, xbp_ref, xbn_ref, *shared, hb_ref, *scr)


def _lru(proj, conv_w, conv_b, w_a, b_a, w_x, b_x, lam):
    s_len = proj.shape[0]
    nc = s_len // LRU_CHUNK
    halo_per_chunk = LRU_CHUNK // BF16_SUBLANES
    n_halo = s_len // BF16_SUBLANES
    wg = jnp.concatenate([w_a, w_x], axis=-1).astype(BF16)
    bg = jnp.stack([b_a, b_x], axis=1)

    def fwd(i):
        return i

    def bwd(i):
        return nc - 1 - i

    def specs(chunk_of):
        return [
            pl.BlockSpec((LRU_CHUNK, LRU_WIDTH), lambda i: (chunk_of(i), COL_XLRU)),
            pl.BlockSpec((BF16_SUBLANES, LRU_WIDTH),
                         lambda i: (jnp.maximum(chunk_of(i) * halo_per_chunk - 1, 0), COL_XLRU)),
            pl.BlockSpec((BF16_SUBLANES, LRU_WIDTH),
                         lambda i: (jnp.minimum((chunk_of(i) + 1) * halo_per_chunk, n_halo - 1), COL_XLRU)),
        ]

    def whole(shape):
        return pl.BlockSpec(shape, lambda i: (0,) * len(shape))

    return pl.pallas_call(
        _lru_kernel,
        grid=(nc,),
        in_specs=specs(fwd) + specs(bwd) + [
            whole((CONV_WIDTH, LRU_WIDTH)), whole((1, LRU_WIDTH)),
            whole((2, LRU_BLOCKS, LRU_BLOCK_DIM, 2 * LRU_BLOCK_DIM)),
            whole((2, 2, LRU_WIDTH)), whole((2, LRU_WIDTH))],
        out_specs=[pl.BlockSpec((LRU_BLOCKS, LRU_CHUNK, LRU_BLOCK_DIM), lambda i: (0, i, 0)),
                   pl.BlockSpec((LRU_BLOCKS, LRU_CHUNK, LRU_BLOCK_DIM), lambda i: (0, nc - 1 - i, 0))],
        out_shape=[jax.ShapeDtypeStruct((LRU_BLOCKS, s_len, LRU_BLOCK_DIM), F32),
                   jax.ShapeDtypeStruct((LRU_BLOCKS, s_len, LRU_BLOCK_DIM), F32)],
        scratch_shapes=[pltpu.VMEM((2, LRU_BLOCKS, LRU_CHUNK + 2 * BF16_SUBLANES, LRU_BLOCK_DIM), F32),
                        pltpu.VMEM((2, LRU_WIDTH), F32)],
        compiler_params=_params("arbitrary"),
        name="rglru",
    )(proj, proj, proj, proj, proj, proj, conv_w, conv_b.reshape(1, LRU_WIDTH), wg, bg, lam)


def _mix_kernel(q_ref, kv_ref, zm_ref, gm_ref, hf_ref, hb_ref, zl_ref, gl_ref, ym_ref, yl_ref, y_scr):
    scale = MEM_HEAD_DIM ** -0.5
    for h in range(MEM_HEADS):
        cs = slice(h * MEM_HEAD_DIM, (h + 1) * MEM_HEAD_DIM)
        vs = slice(MEM_WIDTH + h * MEM_HEAD_DIM, MEM_WIDTH + (h + 1) * MEM_HEAD_DIM)
        s = lax.dot_general(q_ref[:, cs], kv_ref[:, cs], (((1,), (1,)), ((), ())),
                            preferred_element_type=F32) * scale
        m = jnp.max(s, axis=-1, keepdims=True)
        p = jnp.exp(s - m)
        denom = jnp.sum(p, axis=-1, keepdims=True)
        y_scr[:, cs] = jnp.dot(p.astype(BF16), kv_ref[:, vs], preferred_element_type=F32) / denom
    ym_ref[...] = _norm_gate(y_scr[...], gm_ref[...], zm_ref[...].astype(F32)).astype(ym_ref.dtype)
    y_lru = jnp.concatenate([hf_ref[blk] + hb_ref[blk] for blk in range(LRU_BLOCKS)], axis=1)
    yl_ref[...] = _norm_gate(y_lru, gl_ref[...],
                             zl_ref[...].astype(F32)).astype(yl_ref.dtype)


def _mix(proj, kv, h_fwd, h_bwd, g_mem, g_lru):
    s_len = proj.shape[0]
    rows = MIX_ROWS
    row_block = lambda col: pl.BlockSpec((rows, 1024), lambda i: (i, col))
    const = lambda shape: pl.BlockSpec(shape, lambda i: (0, 0))
    slab_block = pl.BlockSpec((LRU_BLOCKS, rows, LRU_BLOCK_DIM), lambda i: (0, i, 0))
    return pl.pallas_call(
        _mix_kernel,
        grid=(s_len // rows,),
        in_specs=[row_block(COL_QMEM), const((N_MEM, 2 * MEM_WIDTH)), row_block(COL_ZMEM),
                  const((1, MEM_WIDTH)), slab_block, slab_block, row_block(COL_ZLRU),
                  const((1, LRU_WIDTH))],
        out_specs=[row_block(0), row_block(0)],
        out_shape=[jax.ShapeDtypeStruct((s_len, MEM_WIDTH), BF16),
                   jax.ShapeDtypeStruct((s_len, LRU_WIDTH), BF16)],
        scratch_shapes=[pltpu.VMEM((rows, MEM_WIDTH), F32)],
        compiler_params=_params("parallel"),
        name="mem_attn_lru_finish",
    )(proj, kv, proj, g_mem.reshape(1, MEM_WIDTH), h_fwd, h_bwd, proj, g_lru.reshape(1, LRU_WIDTH))


def _out_proj_kernel(ya_ref, yl_ref, ym_ref, wa_ref, wl_ref, wm_ref, o_ref):
    acc = jnp.dot(ya_ref[...], wa_ref[...], preferred_element_type=F32)
    acc += jnp.dot(yl_ref[...], wl_ref[...], preferred_element_type=F32)
    acc += jnp.dot(ym_ref[...], wm_ref[...], preferred_element_type=F32)
    o_ref[...] = acc.astype(o_ref.dtype)


def _out_proj(y_att, y_lru, y_mem, w_out):
    s_len = y_att.shape[0]
    tm, tn = MM_TM, MM_TN
    lru_row = ATT_WIDTH // LRU_WIDTH
    return pl.pallas_call(
        _out_proj_kernel,
        grid=(s_len // tm, D_MODEL // tn),
        in_specs=[pl.BlockSpec((tm, ATT_WIDTH), lambda i, j: (i, 0)),
                  pl.BlockSpec((tm, LRU_WIDTH), lambda i, j: (i, 0)),
                  pl.BlockSpec((tm, MEM_WIDTH), lambda i, j: (i, 0)),
                  pl.BlockSpec((ATT_WIDTH, tn), lambda i, j: (0, j)),
                  pl.BlockSpec((LRU_WIDTH, tn), lambda i, j: (lru_row, j)),
                  pl.BlockSpec((MEM_WIDTH, tn), lambda i, j: (lru_row + 1, j))],
        out_specs=pl.BlockSpec((tm, tn), lambda i, j: (i, j)),
        out_shape=jax.ShapeDtypeStruct((s_len, D_MODEL), BF16),
        compiler_params=_params("parallel", "parallel"),
        name="out_proj",
    )(y_att, y_lru, y_mem, w_out, w_out, w_out)


def _post_kernel(x_ref, o_ref, g_ref, y_ref):
    o = o_ref[...].astype(F32)
    ms = jnp.mean(o * o, axis=-1, keepdims=True)
    y_ref[...] = x_ref[...] + o * lax.rsqrt(ms + EPS) * g_ref[...]


def _post(x, out, g):
    m, d = x.shape
    rows = NORM_ROWS
    return pl.pallas_call(
        _post_kernel,
        grid=(m // rows,),
        in_specs=[pl.BlockSpec((rows, d), lambda i: (i, 0)),
                  pl.BlockSpec((rows, d), lambda i: (i, 0)),
                  pl.BlockSpec((1, d), lambda i: (0, 0))],
        out_specs=pl.BlockSpec((rows, d), lambda i: (i, 0)),
        out_shape=jax.ShapeDtypeStruct((m, d), F32),
        compiler_params=_params("parallel"),
        name="post_norm_residual",
    )(x, out, g.reshape(1, d))


def _layer(h, mem, pre_g, w_in, sink, conv_w, conv_b, w_a, b_a, w_x, b_x, lam, mem_g, w_mem_kv,
           g_att, g_lru, g_mem, w_out, post_g):
    u = _rmsnorm_cast(h, pre_g)
    proj = _matmul(u, w_in.astype(BF16), BF16, "in_proj")
    kv = _matmul(_rmsnorm_cast(mem, mem_g), w_mem_kv.astype(BF16), BF16, "mem_kv_proj")
    y_att = _win_attn(proj, sink, g_att)
    h_fwd, h_bwd = _lru(proj, conv_w, conv_b, w_a, b_a, w_x, b_x, lam)
    y_mem, y_lru = _mix(proj, kv, h_fwd, h_bwd, g_mem, g_lru)
    out = _out_proj(y_att, y_lru, y_mem, w_out.astype(BF16))
    return _post(h, out, post_g)


def kernel(x, mem, pre_norm_gain, w_in, att_sink, conv_w, conv_b, lru_w_a, lru_b_a, lru_w_x, lru_b_x,
           lru_lambda, mem_norm_gain, w_mem_kv, att_out_gain, lru_out_gain, mem_out_gain, w_out,
           post_norm_gain):
    batch = x.shape[0]
    depth = w_in.shape[0]
    outs = []
    for b in range(batch):
        h = x[b]
        for l in range(depth):
            h = _layer(h, mem[b], pre_norm_gain[l], w_in[l], att_sink[l], conv_w[l], conv_b[l],
                       lru_w_a[l], lru_b_a[l], lru_w_x[l], lru_b_x[l], lru_lambda[l],
                       mem_norm_gain[l], w_mem_kv[l], att_out_gain[l], lru_out_gain[l],
                       mem_out_gain[l], w_out[l], post_norm_gain[l])
        outs.append(h)
    return jnp.stack(outs, axis=0)
```

```python
import functools
import math

import jax
import jax.numpy as jnp
from jax import lax
from jax.experimental import pallas as pl
from jax.experimental.pallas import tpu as pltpu

F32 = jnp.float32
BF16 = jnp.bfloat16

D_MODEL = 4096
SEQ = 8192
N_MEM = 256
HEAD_DIM = 128
ATT_WIDTH = 2048
ATT_HEADS = 16
ATT_KV_HEADS = 4
ATT_GROUP = ATT_HEADS // ATT_KV_HEADS
KV_WIDTH = ATT_KV_HEADS * HEAD_DIM
BLOCK = 128
ROPE_THETA = 500000.0
ROT_DIM = HEAD_DIM // 4
ROT_HALF = ROT_DIM // 2
LRU_WIDTH = 1024
LRU_BLOCKS = 8
LRU_BLOCK_DIM = LRU_WIDTH // LRU_BLOCKS
CONV_WIDTH = 4
CONV_LEFT = 2
LRU_C = 8.0
MEM_HEADS = 4
MEM_WIDTH = 1024
MEM_HEAD_DIM = MEM_WIDTH // MEM_HEADS
EPS = 1e-6
IN_WIDTH = 9216
MASK_BIAS = -1e30
F32_TINY = float(jnp.finfo(jnp.float32).tiny)

COL_XLRU = 3
COL_QMEM = 4
COL_ZATT = 5
COL_ZLRU = 7
COL_ZMEM = 8
COL_K512 = 4
COL_V512 = 5

VMEM_LIMIT_BYTES = 56 * 1024 * 1024
BF16_SUBLANES = 16

NORM_ROWS = 256
MM_TM = 1024
MM_TN = 1024
LRU_CHUNK = 512
LRU_SUBCHUNK_SEGS = (36, 28)
assert 8 * sum(LRU_SUBCHUNK_SEGS) == LRU_CHUNK and all(s % 8 == 4 for s in LRU_SUBCHUNK_SEGS)
MIX_ROWS = 512


def _params(*sem):
    return pltpu.CompilerParams(dimension_semantics=sem, vmem_limit_bytes=VMEM_LIMIT_BYTES)


def _sigmoid(t):
    return 1.0 / (1.0 + jnp.exp(-t))


def _norm_gate(y, g, z):
    ms = jnp.mean(y * y, axis=-1, keepdims=True)
    return y * lax.rsqrt(ms + EPS) * g * (z * _sigmoid(z))


def _rmsnorm_kernel(x_ref, g_ref, o_ref):
    x = x_ref[...]
    ms = jnp.mean(x * x, axis=-1, keepdims=True)
    o_ref[...] = (x * lax.rsqrt(ms + EPS) * g_ref[...]).astype(o_ref.dtype)


def _rmsnorm_cast(x, g):
    m, d = x.shape
    rows = min(NORM_ROWS, m)
    return pl.pallas_call(
        _rmsnorm_kernel,
        grid=(m // rows,),
        in_specs=[pl.BlockSpec((rows, d), lambda i: (i, 0)),
                  pl.BlockSpec((1, d), lambda i: (0, 0))],
        out_specs=pl.BlockSpec((rows, d), lambda i: (i, 0)),
        out_shape=jax.ShapeDtypeStruct((m, d), BF16),
        compiler_params=_params("parallel"),
        name="rmsnorm_cast",
    )(x, g.reshape(1, d))


def _matmul_kernel(a_ref, b_ref, o_ref):
    o_ref[...] = jnp.dot(a_ref[...], b_ref[...], preferred_element_type=F32).astype(o_ref.dtype)


def _matmul(a, b, out_dtype, name):
    m, k = a.shape
    _, n = b.shape
    tm = min(MM_TM, m)
    tn = min(MM_TN, n)
    return pl.pallas_call(
        _matmul_kernel,
        grid=(m // tm, n // tn),
        in_specs=[pl.BlockSpec((tm, k), lambda i, j: (i, 0)),
                  pl.BlockSpec((k, tn), lambda i, j: (0, j))],
        out_specs=pl.BlockSpec((tm, tn), lambda i, j: (i, j)),
        out_shape=jax.ShapeDtypeStruct((m, n), out_dtype),
        compiler_params=_params("parallel", "parallel"),
        name=name,
    )(a, b)


def _rope_tables(block_idx, freq, base_cos, base_sin, mult):
    a = (block_idx * BLOCK).astype(F32) * freq
    ca, sa = jnp.cos(a) * mult, jnp.sin(a) * mult
    c = ca * base_cos - sa * base_sin
    s = sa * base_cos + ca * base_sin
    lane = lax.broadcasted_iota(jnp.int32, (BLOCK, HEAD_DIM), 1)
    return c, jnp.where(lane < ROT_HALF, -s, 0.0), jnp.where(lane >= ROT_HALF, s, 0.0)


def _rope(t, tables):
    c, sin_lo, sin_hi = tables
    return (t * c + pltpu.roll(t, HEAD_DIM - ROT_HALF, axis=1) * sin_lo
            + pltpu.roll(t, ROT_HALF, axis=1) * sin_hi)


def _win_attn_kernel(sink_ref, q_ref, k0_ref, kn_ref, vp_ref, vc_ref, vn_ref, z0_ref, z1_ref, freq_ref,
                     g_ref, o_ref, k_ring, bias_scr, cos_scr, sin_scr, y_scr):
    n = pl.program_id(0)
    nb = pl.num_programs(0)
    freq = freq_ref[...]
    rows = ATT_GROUP * BLOCK
    log2e = math.log2(math.e)

    def rope_keys(k_ref, tables):
        return jnp.concatenate(
            [_rope(k_ref[:, h * HEAD_DIM:(h + 1) * HEAD_DIM].astype(F32), tables).astype(BF16)
             for h in range(ATT_KV_HEADS)], axis=1)

    @pl.when(n == 0)
    def _():
        ang = lax.broadcasted_iota(jnp.int32, (BLOCK, HEAD_DIM), 0).astype(F32) * freq
        cos_scr[...] = jnp.cos(ang)
        sin_scr[...] = jnp.sin(ang)
        key = lax.broadcasted_iota(jnp.int32, (BLOCK, rows), 0)
        q_local = lax.broadcasted_iota(jnp.int32, (BLOCK, rows), 1) & (BLOCK - 1)
        bias_scr[0] = jnp.where(key >= q_local, 0.0, MASK_BIAS)
        bias_scr[1] = jnp.where(key <= q_local, 0.0, MASK_BIAS)
        bias_scr[2] = jnp.full((BLOCK, rows), MASK_BIAS, F32)
        k_ring[0] = rope_keys(k0_ref, _rope_tables(n, freq, cos_scr[...], sin_scr[...], 1.0))
        k_ring[2] = jnp.zeros((BLOCK, KV_WIDTH), BF16)

    base_cos = cos_scr[...]
    base_sin = sin_scr[...]
    slot_p, slot_c, slot_n = lax.rem(n + 2, 3), lax.rem(n, 3), lax.rem(n + 1, 3)
    k_ring[slot_n] = rope_keys(kn_ref, _rope_tables(n + 1, freq, base_cos, base_sin, 1.0))
    tab_q = _rope_tables(n, freq, base_cos, base_sin, HEAD_DIM ** -0.5 * log2e)
    bias_p = bias_scr[jnp.where(n > 0, 0, 2)]
    bias_n = bias_scr[jnp.where(n < nb - 1, 1, 2)]
    lane_group = lax.broadcasted_iota(jnp.int32, (1, rows), 1) // BLOCK
    first_row = lax.broadcasted_iota(jnp.int32, (BF16_SUBLANES, rows), 0) == 0
    ones = jnp.ones((3 * BLOCK + BF16_SUBLANES, HEAD_DIM), BF16)

    for h in range(ATT_KV_HEADS):
        ks = slice(h * HEAD_DIM, (h + 1) * HEAD_DIM)
        k_all = jnp.concatenate([k_ring[slot_p, :, ks], k_ring[slot_c, :, ks], k_ring[slot_n, :, ks]],
                                axis=0)
        q_all = jnp.concatenate(
            [_rope(q_ref[:, (h * ATT_GROUP + g) * HEAD_DIM:(h * ATT_GROUP + g + 1) * HEAD_DIM]
                   .astype(F32), tab_q).astype(BF16) for g in range(ATT_GROUP)], axis=0)
        sink = jnp.zeros((1, rows), F32)
        for g in range(ATT_GROUP):
            sink = jnp.where(lane_group == g, sink_ref[h * ATT_GROUP + g] * log2e, sink)
        s = lax.dot_general(k_all, q_all, (((1,), (1,)), ((), ())), preferred_element_type=F32)
        s_p = s[:BLOCK] + bias_p
        s_c = s[BLOCK:2 * BLOCK]
        s_n = s[2 * BLOCK:] + bias_n
        m = jnp.max(jnp.maximum(jnp.maximum(s_p, s_c), s_n), axis=0, keepdims=True)
        m = jnp.maximum(m, sink)
        p_sink = jnp.where(first_row, jnp.exp2(sink - m), 0.0)
        p = jnp.concatenate([jnp.exp2(s_p - m), jnp.exp2(s_c - m), jnp.exp2(s_n - m), p_sink],
                            axis=0).astype(BF16)
        v_all = jnp.concatenate([vp_ref[:, ks], vc_ref[:, ks], vn_ref[:, ks],
                                 jnp.zeros((BF16_SUBLANES, HEAD_DIM), BF16)], axis=0)
        v_aug = jnp.concatenate([v_all, ones], axis=1)
        o_aug = lax.dot_general(p, v_aug, (((0,), (0,)), ((), ())), preferred_element_type=F32)
        o = o_aug[:, :HEAD_DIM] / o_aug[:, HEAD_DIM:]
        for g in range(ATT_GROUP):
            hq = h * ATT_GROUP + g
            y_scr[:, hq * HEAD_DIM:(hq + 1) * HEAD_DIM] = o[g * BLOCK:(g + 1) * BLOCK]

    z = jnp.concatenate([z0_ref[...], z1_ref[...]], axis=1).astype(F32)
    o_ref[...] = _norm_gate(y_scr[...], g_ref[...], z).astype(o_ref.dtype)


def _win_attn(proj, sink, g_att):
    s_len = proj.shape[0]
    nb = s_len // BLOCK
    inv_freq = jnp.power(jnp.float32(ROPE_THETA),
                         -jnp.arange(ROT_HALF, dtype=F32) * 2.0 / ROT_DIM)
    freq = jnp.concatenate([inv_freq, inv_freq, jnp.zeros((HEAD_DIM - ROT_DIM,), F32)]).reshape(1, HEAD_DIM)

    def kv_spec(col, shift):
        return pl.BlockSpec((BLOCK, KV_WIDTH),
                            lambda n, sink: (jnp.clip(n + shift, 0, nb - 1), col))

    grid_spec = pltpu.PrefetchScalarGridSpec(
        num_scalar_prefetch=1,
        grid=(nb,),
        in_specs=[pl.BlockSpec((BLOCK, ATT_WIDTH), lambda n, sink: (n, 0)),
                  pl.BlockSpec((BLOCK, KV_WIDTH), lambda n, sink: (0, COL_K512)), kv_spec(COL_K512, 1),
                  kv_spec(COL_V512, -1), kv_spec(COL_V512, 0), kv_spec(COL_V512, 1),
                  pl.BlockSpec((BLOCK, 1024), lambda n, sink: (n, COL_ZATT)),
                  pl.BlockSpec((BLOCK, 1024), lambda n, sink: (n, COL_ZATT + 1)),
                  pl.BlockSpec((1, HEAD_DIM), lambda n, sink: (0, 0)),
                  pl.BlockSpec((1, ATT_WIDTH), lambda n, sink: (0, 0))],
        out_specs=pl.BlockSpec((BLOCK, ATT_WIDTH), lambda n, sink: (n, 0)),
        scratch_shapes=[pltpu.VMEM((3, BLOCK, KV_WIDTH), BF16),
                        pltpu.VMEM((3, BLOCK, ATT_GROUP * BLOCK), F32),
                        pltpu.VMEM((BLOCK, HEAD_DIM), F32),
                        pltpu.VMEM((BLOCK, HEAD_DIM), F32),
                        pltpu.VMEM((BLOCK, ATT_WIDTH), F32)],
    )
    return pl.pallas_call(
        _win_attn_kernel,
        grid_spec=grid_spec,
        out_shape=jax.ShapeDtypeStruct((s_len, ATT_WIDTH), BF16),
        compiler_params=_params("arbitrary"),
        name="win_attn",
    )(sink, proj, proj, proj, proj, proj, proj, proj, proj, freq, g_att.reshape(1, ATT_WIDTH))


def _lru_direction(d, reverse, chunk, n_chunks, x_ref, xp_ref, xn_ref, cw_ref, cb_ref, wg_ref,
                   bg_ref, lam_ref, out_ref, xe_scr, carry_scr):
    t = LRU_CHUNK
    halo = BF16_SUBLANES
    for blk in range(LRU_BLOCKS):
        cs = slice(blk * LRU_BLOCK_DIM, (blk + 1) * LRU_BLOCK_DIM)
        xe_scr[d, blk, 0:halo, :] = jnp.where(chunk > 0, xp_ref[:, cs].astype(F32), 0.0)
        xe_scr[d, blk, halo:halo + t, :] = x_ref[:, cs].astype(F32)
        xe_scr[d, blk, halo + t:, :] = jnp.where(chunk < n_chunks - 1, xn_ref[:, cs].astype(F32), 0.0)

    neg_lam = -lam_ref[d:d + 1, :]
    softplus = jnp.maximum(neg_lam, 0.0) + jnp.log(1.0 + jnp.exp(-jnp.abs(neg_lam)))
    decay = (-LRU_C * math.log2(math.e)) * softplus
    row0 = [8 * sum(LRU_SUBCHUNK_SEGS[:k]) for k in range(len(LRU_SUBCHUNK_SEGS))]
    subchunks = list(zip(row0, LRU_SUBCHUNK_SEGS))
    for blk in range(LRU_BLOCKS):
        cs = slice(blk * LRU_BLOCK_DIM, (blk + 1) * LRU_BLOCK_DIM)
        taps = [cw_ref[j:j + 1, cs] for j in range(CONV_WIDTH)]
        cin = carry_scr[d:d + 1, cs]
        for r0, seg in (reversed(subchunks) if reverse else subchunks):
            order = range(seg - 1, -1, -1) if reverse else range(seg)
            tiles = []
            for m in range(seg):
                acc = cb_ref[:, cs]
                for j in range(CONV_WIDTH):
                    start = halo + r0 + m + j - CONV_LEFT
                    acc = acc + taps[j] * xe_scr[d, blk, pl.ds(start, 8, stride=seg), :]
                tiles.append(acc)
            xc = jnp.concatenate(tiles, axis=0)
            gates = jnp.dot(xc.astype(BF16), wg_ref[d, blk], preferred_element_type=F32)
            r = 1.0 / (1.0 + jnp.exp(gates[:, :LRU_BLOCK_DIM] + bg_ref[d, 0:1, cs]))
            i = 1.0 / (1.0 + jnp.exp(gates[:, LRU_BLOCK_DIM:] + bg_ref[d, 1:2, cs]))
            a = jnp.exp2(decay[:, cs] * r)
            w = 1.0 - a * a
            b = w * lax.rsqrt(jnp.maximum(w, F32_TINY)) * i * xc
            tile = lambda v, m: v[8 * m:8 * m + 8]
            hh = jnp.zeros((8, LRU_BLOCK_DIM), F32)
            pp = jnp.ones((8, LRU_BLOCK_DIM), F32)
            for m in order:
                hh = tile(a, m) * hh + tile(b, m)
                pp = tile(a, m) * pp
            entering = [None] * 8
            for s in (range(7, -1, -1) if reverse else range(8)):
                entering[s] = cin
                cin = pp[s:s + 1, :] * cin + hh[s:s + 1, :]
            h = jnp.concatenate(entering, axis=0)
            for m in order:
                h = tile(a, m) * h + tile(b, m)
                out_ref[blk, pl.ds(r0 + m, 8, stride=seg), :] = h
        carry_scr[d:d + 1, cs] = cin


def _lru_kernel(xf_ref, xfp_ref, xfn_ref, xb_ref, xbp_ref, xbn_ref, cw_ref, cb_ref, wg_ref, bg_ref,
                lam_ref, hf_ref, hb_ref, xe_scr, carry_scr):
    i = pl.program_id(0)
    nc = pl.num_programs(0)

    @pl.when(i == 0)
    def _():
        carry_scr[...] = jnp.zeros_like(carry_scr)

    shared = (cw_ref, cb_ref, wg_ref, bg_ref, lam_ref)
    scr = (xe_scr, carry_scr)
    _lru_direction(0, False, i, nc, xf_ref, xfp_ref, xfn_ref, *shared, hf_ref, *scr)
    _lru_direction(1, True, nc - 1 - i, nc, xb_ref, xbp_ref, xbn_ref, *shared, hb_ref, *scr)


def _lru(proj, conv_w, conv_b, w_a, b_a, w_x, b_x, lam):
    s_len = proj.shape[0]
    nc = s_len // LRU_CHUNK
    halo_per_chunk = LRU_CHUNK // BF16_SUBLANES
    n_halo = s_len // BF16_SUBLANES
    wg = (-jnp.concatenate([w_a, w_x], axis=-1)).astype(BF16)
    bg = -jnp.stack([b_a, b_x], axis=1)

    def fwd(i):
        return i

    def bwd(i):
        return nc - 1 - i

    def specs(chunk_of):
        return [
            pl.BlockSpec((LRU_CHUNK, LRU_WIDTH), lambda i: (chunk_of(i), COL_XLRU)),
            pl.BlockSpec((BF16_SUBLANES, LRU_WIDTH),
                         lambda i: (jnp.maximum(chunk_of(i) * halo_per_chunk - 1, 0), COL_XLRU)),
            pl.BlockSpec((BF16_SUBLANES, LRU_WIDTH),
                         lambda i: (jnp.minimum((chunk_of(i) + 1) * halo_per_chunk, n_halo - 1), COL_XLRU)),
        ]

    def whole(shape):
        return pl.BlockSpec(shape, lambda i: (0,) * len(shape))

    return pl.pallas_call(
        _lru_kernel,
        grid=(nc,),
        in_specs=specs(fwd) + specs(bwd) + [
            whole((CONV_WIDTH, LRU_WIDTH)), whole((1, LRU_WIDTH)),
            whole((2, LRU_BLOCKS, LRU_BLOCK_DIM, 2 * LRU_BLOCK_DIM)),
            whole((2, 2, LRU_WIDTH)), whole((2, LRU_WIDTH))],
        out_specs=[pl.BlockSpec((LRU_BLOCKS, LRU_CHUNK, LRU_BLOCK_DIM), lambda i: (0, i, 0)),
                   pl.BlockSpec((LRU_BLOCKS, LRU_CHUNK, LRU_BLOCK_DIM), lambda i: (0, nc - 1 - i, 0))],
        out_shape=[jax.ShapeDtypeStruct((LRU_BLOCKS, s_len, LRU_BLOCK_DIM), F32),
                   jax.ShapeDtypeStruct((LRU_BLOCKS, s_len, LRU_BLOCK_DIM), F32)],
        scratch_shapes=[pltpu.VMEM((2, LRU_BLOCKS, LRU_CHUNK + 2 * BF16_SUBLANES, LRU_BLOCK_DIM), F32),
                        pltpu.VMEM((2, LRU_WIDTH), F32)],
        compiler_params=_params("arbitrary"),
        name="rglru",
    )(proj, proj, proj, proj, proj, proj, conv_w, conv_b.reshape(1, LRU_WIDTH), wg, bg, lam)


def _mix_kernel(q_ref, kv_ref, zm_ref, gm_ref, hf_ref, hb_ref, zl_ref, gl_ref, ym_ref, yl_ref, y_scr):
    scale = MEM_HEAD_DIM ** -0.5
    for h in range(MEM_HEADS):
        cs = slice(h * MEM_HEAD_DIM, (h + 1) * MEM_HEAD_DIM)
        vs = slice(MEM_WIDTH + h * MEM_HEAD_DIM, MEM_WIDTH + (h + 1) * MEM_HEAD_DIM)
        s = lax.dot_general(q_ref[:, cs], kv_ref[:, cs], (((1,), (1,)), ((), ())),
                            preferred_element_type=F32) * scale
        m = jnp.max(s, axis=-1, keepdims=True)
        p = jnp.exp(s - m)
        denom = jnp.sum(p, axis=-1, keepdims=True)
        y_scr[:, cs] = jnp.dot(p.astype(BF16), kv_ref[:, vs], preferred_element_type=F32) / denom
    ym_ref[...] = _norm_gate(y_scr[...], gm_ref[...], zm_ref[...].astype(F32)).astype(ym_ref.dtype)
    y_lru = jnp.concatenate([hf_ref[blk] + hb_ref[blk] for blk in range(LRU_BLOCKS)], axis=1)
    yl_ref[...] = _norm_gate(y_lru, gl_ref[...],
                             zl_ref[...].astype(F32)).astype(yl_ref.dtype)


def _mix(proj, kv, h_fwd, h_bwd, g_mem, g_lru):
    s_len = proj.shape[0]
    rows = MIX_ROWS
    row_block = lambda col: pl.BlockSpec((rows, 1024), lambda i: (i, col))
    const = lambda shape: pl.BlockSpec(shape, lambda i: (0, 0))
    slab_block = pl.BlockSpec((LRU_BLOCKS, rows, LRU_BLOCK_DIM), lambda i: (0, i, 0))
    return pl.pallas_call(
        _mix_kernel,
        grid=(s_len // rows,),
        in_specs=[row_block(COL_QMEM), const((N_MEM, 2 * MEM_WIDTH)), row_block(COL_ZMEM),
                  const((1, MEM_WIDTH)), slab_block, slab_block, row_block(COL_ZLRU),
                  const((1, LRU_WIDTH))],
        out_specs=[row_block(0), row_block(0)],
        out_shape=[jax.ShapeDtypeStruct((s_len, MEM_WIDTH), BF16),
                   jax.ShapeDtypeStruct((s_len, LRU_WIDTH), BF16)],
        scratch_shapes=[pltpu.VMEM((rows, MEM_WIDTH), F32)],
        compiler_params=_params("parallel"),
        name="mem_attn_lru_finish",
    )(proj, kv, proj, g_mem.reshape(1, MEM_WIDTH), h_fwd, h_bwd, proj, g_lru.reshape(1, LRU_WIDTH))


def _out_proj_kernel(ya_ref, yl_ref, ym_ref, wa_ref, wl_ref, wm_ref, o_ref):
    acc = jnp.dot(ya_ref[...], wa_ref[...], preferred_element_type=F32)
    acc += jnp.dot(yl_ref[...], wl_ref[...], preferred_element_type=F32)
    acc += jnp.dot(ym_ref[...], wm_ref[...], preferred_element_type=F32)
    o_ref[...] = acc.astype(o_ref.dtype)


def _out_proj(y_att, y_lru, y_mem, w_out):
    s_len = y_att.shape[0]
    tm, tn = MM_TM, MM_TN
    lru_row = ATT_WIDTH // LRU_WIDTH
    return pl.pallas_call(
        _out_proj_kernel,
        grid=(s_len // tm, D_MODEL // tn),
        in_specs=[pl.BlockSpec((tm, ATT_WIDTH), lambda i, j: (i, 0)),
                  pl.BlockSpec((tm, LRU_WIDTH), lambda i, j: (i, 0)),
                  pl.BlockSpec((tm, MEM_WIDTH), lambda i, j: (i, 0)),
                  pl.BlockSpec((ATT_WIDTH, tn), lambda i, j: (0, j)),
                  pl.BlockSpec((LRU_WIDTH, tn), lambda i, j: (lru_row, j)),
                  pl.BlockSpec((MEM_WIDTH, tn), lambda i, j: (lru_row + 1, j))],
        out_specs=pl.BlockSpec((tm, tn), lambda i, j: (i, j)),
        out_shape=jax.ShapeDtypeStruct((s_len, D_MODEL), BF16),
        compiler_params=_params("parallel", "parallel"),
        name="out_proj",
    )(y_att, y_lru, y_mem, w_out, w_out, w_out)


def _post_kernel(x_ref, o_ref, g_ref, y_ref):
    o = o_ref[...].astype(F32)
    ms = jnp.mean(o * o, axis=-1, keepdims=True)
    y_ref[...] = x_ref[...] + o * lax.rsqrt(ms + EPS) * g_ref[...]


def _post(x, out, g):
    m, d = x.shape
    rows = NORM_ROWS
    return pl.pallas_call(
        _post_kernel,
        grid=(m // rows,),
        in_specs=[pl.BlockSpec((rows, d), lambda i: (i, 0)),
                  pl.BlockSpec((rows, d), lambda i: (i, 0)),
                  pl.BlockSpec((1, d), lambda i: (0, 0))],
        out_specs=pl.BlockSpec((rows, d), lambda i: (i, 0)),
        out_shape=jax.ShapeDtypeStruct((m, d), F32),
        compiler_params=_params("parallel"),
        name="post_norm_residual",
    )(x, out, g.reshape(1, d))


def _layer(h, mem, pre_g, w_in, sink, conv_w, conv_b, w_a, b_a, w_x, b_x, lam, mem_g, w_mem_kv,
           g_att, g_lru, g_mem, w_out, post_g):
    u = _rmsnorm_cast(h, pre_g)
    proj = _matmul(u, w_in.astype(BF16), BF16, "in_proj")
    kv = _matmul(_rmsnorm_cast(mem, mem_g), w_mem_kv.astype(BF16), BF16, "mem_kv_proj")
    y_att = _win_attn(proj, sink, g_att)
    h_fwd, h_bwd = _lru(proj, conv_w, conv_b, w_a, b_a, w_x, b_x, lam)
    y_mem, y_lru = _mix(proj, kv, h_fwd, h_bwd, g_mem, g_lru)
    out = _out_proj(y_att, y_lru, y_mem, w_out.astype(BF16))
    return _post(h, out, post_g)


def kernel(x, mem, pre_norm_gain, w_in, att_sink, conv_w, conv_b, lru_w_a, lru_b_a, lru_w_x, lru_b_x,
           lru_lambda, mem_norm_gain, w_mem_kv, att_out_gain, lru_out_gain, mem_out_gain, w_out,
           post_norm_gain):
    batch = x.shape[0]
    depth = w_in.shape[0]
    outs = []
    for b in range(batch):
        h = x[b]
        for l in range(depth):
            h = _layer(h, mem[b], pre_norm_gain[l], w_in[l], att_sink[l], conv_w[l], conv_b[l],
                       lru_w_a[l], lru_b_a[l], lru_w_x[l], lru_b_x[l], lru_lambda[l],
                       mem_norm_gain[l], w_mem_kv[l], att_out_gain[l], lru_out_gain[l],
                       mem_out_gain[l], w_out[l], post_norm_gain[l])
        outs.append(h)
    return jnp.stack(outs, axis=0)
```

```python
import functools
import math

import jax
import jax.numpy as jnp
from jax import lax
from jax.experimental import pallas as pl
from jax.experimental.pallas import tpu as pltpu

F32 = jnp.float32
BF16 = jnp.bfloat16

D_MODEL = 4096
SEQ = 8192
N_MEM = 256
HEAD_DIM = 128
ATT_WIDTH = 2048
ATT_HEADS = 16
ATT_KV_HEADS = 4
ATT_GROUP = ATT_HEADS // ATT_KV_HEADS
KV_WIDTH = ATT_KV_HEADS * HEAD_DIM
BLOCK = 128
ROPE_THETA = 500000.0
ROT_DIM = HEAD_DIM // 4
ROT_HALF = ROT_DIM // 2
LRU_WIDTH = 1024
LRU_BLOCKS = 8
LRU_BLOCK_DIM = LRU_WIDTH // LRU_BLOCKS
CONV_WIDTH = 4
CONV_LEFT = 2
LRU_C = 8.0
MEM_HEADS = 4
MEM_WIDTH = 1024
MEM_HEAD_DIM = MEM_WIDTH // MEM_HEADS
EPS = 1e-6
IN_WIDTH = 9216
MASK_BIAS = -1e30
F32_TINY = float(jnp.finfo(jnp.float32).tiny)

COL_XLRU = 3
COL_QMEM = 4
COL_ZATT = 5
COL_ZLRU = 7
COL_ZMEM = 8
COL_K512 = 4
COL_V512 = 5

VMEM_LIMIT_BYTES = 56 * 1024 * 1024
BF16_SUBLANES = 16

NORM_ROWS = 256
MM_TM = 1024
MM_TN = 1024
OUT_TM = 512
OUT_TN = 1024
LANES = 128
LRU_CHUNK = 512
LRU_SUBCHUNK_SEGS = (36, 28)
assert 8 * sum(LRU_SUBCHUNK_SEGS) == LRU_CHUNK and all(s % 8 == 4 for s in LRU_SUBCHUNK_SEGS)
MIX_ROWS = 512


def _params(*sem):
    return pltpu.CompilerParams(dimension_semantics=sem, vmem_limit_bytes=VMEM_LIMIT_BYTES)


def _sigmoid(t):
    return 1.0 / (1.0 + jnp.exp(-t))


def _norm_gate(y, g, z):
    ms = jnp.mean(y * y, axis=-1, keepdims=True)
    return y * lax.rsqrt(ms + EPS) * g * (z * _sigmoid(z))


def _rmsnorm_kernel(x_ref, g_ref, o_ref):
    x = x_ref[...]
    ms = jnp.mean(x * x, axis=-1, keepdims=True)
    o_ref[...] = (x * lax.rsqrt(ms + EPS) * g_ref[...]).astype(o_ref.dtype)


def _rmsnorm_cast(x, g):
    m, d = x.shape
    rows = min(NORM_ROWS, m)
    return pl.pallas_call(
        _rmsnorm_kernel,
        grid=(m // rows,),
        in_specs=[pl.BlockSpec((rows, d), lambda i: (i, 0)),
                  pl.BlockSpec((1, d), lambda i: (0, 0))],
        out_specs=pl.BlockSpec((rows, d), lambda i: (i, 0)),
        out_shape=jax.ShapeDtypeStruct((m, d), BF16),
        compiler_params=_params("parallel"),
        name="rmsnorm_cast",
    )(x, g.reshape(1, d))


def _matmul_kernel(a_ref, b_ref, o_ref):
    o_ref[...] = jnp.dot(a_ref[...], b_ref[...], preferred_element_type=F32).astype(o_ref.dtype)


def _matmul(a, b, out_dtype, name):
    m, k = a.shape
    _, n = b.shape
    tm = min(MM_TM, m)
    tn = min(MM_TN, n)
    return pl.pallas_call(
        _matmul_kernel,
        grid=(m // tm, n // tn),
        in_specs=[pl.BlockSpec((tm, k), lambda i, j: (i, 0)),
                  pl.BlockSpec((k, tn), lambda i, j: (0, j))],
        out_specs=pl.BlockSpec((tm, tn), lambda i, j: (i, j)),
        out_shape=jax.ShapeDtypeStruct((m, n), out_dtype),
        compiler_params=_params("parallel", "parallel"),
        name=name,
    )(a, b)


def _rope_tables(block_idx, freq, base_cos, base_sin, mult):
    a = (block_idx * BLOCK).astype(F32) * freq
    ca, sa = jnp.cos(a) * mult, jnp.sin(a) * mult
    c = ca * base_cos - sa * base_sin
    s = sa * base_cos + ca * base_sin
    lane = lax.broadcasted_iota(jnp.int32, (BLOCK, HEAD_DIM), 1)
    return c, jnp.where(lane < ROT_HALF, -s, 0.0), jnp.where(lane >= ROT_HALF, s, 0.0)


def _rope(t, tables):
    c, sin_lo, sin_hi = tables
    return (t * c + pltpu.roll(t, HEAD_DIM - ROT_HALF, axis=1) * sin_lo
            + pltpu.roll(t, ROT_HALF, axis=1) * sin_hi)


def _win_attn_kernel(sink_ref, q_ref, k0_ref, kn_ref, vp_ref, vc_ref, vn_ref, z0_ref, z1_ref, freq_ref,
                     g_ref, o_ref, k_ring, bias_scr, cos_scr, sin_scr, y_scr):
    n = pl.program_id(0)
    nb = pl.num_programs(0)
    freq = freq_ref[...]
    rows = ATT_GROUP * BLOCK
    log2e = math.log2(math.e)

    def rope_keys(k_ref, tables):
        return jnp.concatenate(
            [_rope(k_ref[:, h * HEAD_DIM:(h + 1) * HEAD_DIM].astype(F32), tables).astype(BF16)
             for h in range(ATT_KV_HEADS)], axis=1)

    @pl.when(n == 0)
    def _():
        ang = lax.broadcasted_iota(jnp.int32, (BLOCK, HEAD_DIM), 0).astype(F32) * freq
        cos_scr[...] = jnp.cos(ang)
        sin_scr[...] = jnp.sin(ang)
        key = lax.broadcasted_iota(jnp.int32, (BLOCK, rows), 0)
        q_local = lax.broadcasted_iota(jnp.int32, (BLOCK, rows), 1) & (BLOCK - 1)
        bias_scr[0] = jnp.where(key >= q_local, 0.0, MASK_BIAS)
        bias_scr[1] = jnp.where(key <= q_local, 0.0, MASK_BIAS)
        bias_scr[2] = jnp.full((BLOCK, rows), MASK_BIAS, F32)
        k_ring[0] = rope_keys(k0_ref, _rope_tables(n, freq, cos_scr[...], sin_scr[...], 1.0))
        k_ring[2] = jnp.zeros((BLOCK, KV_WIDTH), BF16)

    base_cos = cos_scr[...]
    base_sin = sin_scr[...]
    slot_p, slot_c, slot_n = lax.rem(n + 2, 3), lax.rem(n, 3), lax.rem(n + 1, 3)
    k_ring[slot_n] = rope_keys(kn_ref, _rope_tables(n + 1, freq, base_cos, base_sin, 1.0))
    tab_q = _rope_tables(n, freq, base_cos, base_sin, HEAD_DIM ** -0.5 * log2e)
    bias_p = bias_scr[jnp.where(n > 0, 0, 2)]
    bias_n = bias_scr[jnp.where(n < nb - 1, 1, 2)]
    lane_group = lax.broadcasted_iota(jnp.int32, (1, rows), 1) // BLOCK
    first_row = lax.broadcasted_iota(jnp.int32, (BF16_SUBLANES, rows), 0) == 0
    ones = jnp.ones((3 * BLOCK + BF16_SUBLANES, HEAD_DIM), BF16)

    for h in range(ATT_KV_HEADS):
        ks = slice(h * HEAD_DIM, (h + 1) * HEAD_DIM)
        k_all = jnp.concatenate([k_ring[slot_p, :, ks], k_ring[slot_c, :, ks], k_ring[slot_n, :, ks]],
                                axis=0)
        q_all = jnp.concatenate(
            [_rope(q_ref[:, (h * ATT_GROUP + g) * HEAD_DIM:(h * ATT_GROUP + g + 1) * HEAD_DIM]
                   .astype(F32), tab_q).astype(BF16) for g in range(ATT_GROUP)], axis=0)
        sink = jnp.zeros((1, rows), F32)
        for g in range(ATT_GROUP):
            sink = jnp.where(lane_group == g, sink_ref[h * ATT_GROUP + g] * log2e, sink)
        s = lax.dot_general(k_all, q_all, (((1,), (1,)), ((), ())), preferred_element_type=F32)
        s_p = s[:BLOCK] + bias_p
        s_c = s[BLOCK:2 * BLOCK]
        s_n = s[2 * BLOCK:] + bias_n
        m = jnp.max(jnp.maximum(jnp.maximum(s_p, s_c), s_n), axis=0, keepdims=True)
        m = jnp.maximum(m, sink)
        p_sink = jnp.where(first_row, jnp.exp2(sink - m), 0.0)
        p = jnp.concatenate([jnp.exp2(s_p - m), jnp.exp2(s_c - m), jnp.exp2(s_n - m), p_sink],
                            axis=0).astype(BF16)
        v_all = jnp.concatenate([vp_ref[:, ks], vc_ref[:, ks], vn_ref[:, ks],
                                 jnp.zeros((BF16_SUBLANES, HEAD_DIM), BF16)], axis=0)
        v_aug = jnp.concatenate([v_all, ones], axis=1)
        o_aug = lax.dot_general(p, v_aug, (((0,), (0,)), ((), ())), preferred_element_type=F32)
        o = o_aug[:, :HEAD_DIM] / o_aug[:, HEAD_DIM:]
        for g in range(ATT_GROUP):
            hq = h * ATT_GROUP + g
            y_scr[:, hq * HEAD_DIM:(hq + 1) * HEAD_DIM] = o[g * BLOCK:(g + 1) * BLOCK]

    z = jnp.concatenate([z0_ref[...], z1_ref[...]], axis=1).astype(F32)
    o_ref[...] = _norm_gate(y_scr[...], g_ref[...], z).astype(o_ref.dtype)


def _win_attn(proj, sink, g_att):
    s_len = proj.shape[0]
    nb = s_len // BLOCK
    inv_freq = jnp.power(jnp.float32(ROPE_THETA),
                         -jnp.arange(ROT_HALF, dtype=F32) * 2.0 / ROT_DIM)
    freq = jnp.concatenate([inv_freq, inv_freq, jnp.zeros((HEAD_DIM - ROT_DIM,), F32)]).reshape(1, HEAD_DIM)

    def kv_spec(col, shift):
        return pl.BlockSpec((BLOCK, KV_WIDTH),
                            lambda n, sink: (jnp.clip(n + shift, 0, nb - 1), col))

    grid_spec = pltpu.PrefetchScalarGridSpec(
        num_scalar_prefetch=1,
        grid=(nb,),
        in_specs=[pl.BlockSpec((BLOCK, ATT_WIDTH), lambda n, sink: (n, 0)),
                  pl.BlockSpec((BLOCK, KV_WIDTH), lambda n, sink: (0, COL_K512)), kv_spec(COL_K512, 1),
                  kv_spec(COL_V512, -1), kv_spec(COL_V512, 0), kv_spec(COL_V512, 1),
                  pl.BlockSpec((BLOCK, 1024), lambda n, sink: (n, COL_ZATT)),
                  pl.BlockSpec((BLOCK, 1024), lambda n, sink: (n, COL_ZATT + 1)),
                  pl.BlockSpec((1, HEAD_DIM), lambda n, sink: (0, 0)),
                  pl.BlockSpec((1, ATT_WIDTH), lambda n, sink: (0, 0))],
        out_specs=pl.BlockSpec((BLOCK, ATT_WIDTH), lambda n, sink: (n, 0)),
        scratch_shapes=[pltpu.VMEM((3, BLOCK, KV_WIDTH), BF16),
                        pltpu.VMEM((3, BLOCK, ATT_GROUP * BLOCK), F32),
                        pltpu.VMEM((BLOCK, HEAD_DIM), F32),
                        pltpu.VMEM((BLOCK, HEAD_DIM), F32),
                        pltpu.VMEM((BLOCK, ATT_WIDTH), F32)],
    )
    return pl.pallas_call(
        _win_attn_kernel,
        grid_spec=grid_spec,
        out_shape=jax.ShapeDtypeStruct((s_len, ATT_WIDTH), BF16),
        compiler_params=_params("arbitrary"),
        name="win_attn",
    )(sink, proj, proj, proj, proj, proj, proj, proj, proj, freq, g_att.reshape(1, ATT_WIDTH))


def _lru_direction(d, reverse, chunk, n_chunks, x_ref, xp_ref, xn_ref, cw_ref, cb_ref, wg_ref,
                   bg_ref, lam_ref, out_ref, xe_scr, carry_scr):
    t = LRU_CHUNK
    halo = BF16_SUBLANES
    for blk in range(LRU_BLOCKS):
        cs = slice(blk * LRU_BLOCK_DIM, (blk + 1) * LRU_BLOCK_DIM)
        xe_scr[d, blk, 0:halo, :] = jnp.where(chunk > 0, xp_ref[:, cs].astype(F32), 0.0)
        xe_scr[d, blk, halo:halo + t, :] = x_ref[:, cs].astype(F32)
        xe_scr[d, blk, halo + t:, :] = jnp.where(chunk < n_chunks - 1, xn_ref[:, cs].astype(F32), 0.0)

    neg_lam = -lam_ref[d:d + 1, :]
    softplus = jnp.maximum(neg_lam, 0.0) + jnp.log(1.0 + jnp.exp(-jnp.abs(neg_lam)))
    decay = (-LRU_C * math.log2(math.e)) * softplus
    row0 = [8 * sum(LRU_SUBCHUNK_SEGS[:k]) for k in range(len(LRU_SUBCHUNK_SEGS))]
    subchunks = list(zip(row0, LRU_SUBCHUNK_SEGS))
    for blk in range(LRU_BLOCKS):
        cs = slice(blk * LRU_BLOCK_DIM, (blk + 1) * LRU_BLOCK_DIM)
        taps = [cw_ref[j:j + 1, cs] for j in range(CONV_WIDTH)]
        cin = carry_scr[d:d + 1, cs]
        for r0, seg in (reversed(subchunks) if reverse else subchunks):
            order = range(seg - 1, -1, -1) if reverse else range(seg)
            tiles = []
            for m in range(seg):
                acc = cb_ref[:, cs]
                for j in range(CONV_WIDTH):
                    start = halo + r0 + m + j - CONV_LEFT
                    acc = acc + taps[j] * xe_scr[d, blk, pl.ds(start, 8, stride=seg), :]
                tiles.append(acc)
            xc = jnp.concatenate(tiles, axis=0)
            gates = jnp.dot(xc.astype(BF16), wg_ref[d, blk], preferred_element_type=F32)
            r = 1.0 / (1.0 + jnp.exp(gates[:, :LRU_BLOCK_DIM] + bg_ref[d, 0:1, cs]))
            i = 1.0 / (1.0 + jnp.exp(gates[:, LRU_BLOCK_DIM:] + bg_ref[d, 1:2, cs]))
            a = jnp.exp2(decay[:, cs] * r)
            w = 1.0 - a * a
            b = w * lax.rsqrt(jnp.maximum(w, F32_TINY)) * i * xc
            tile = lambda v, m: v[8 * m:8 * m + 8]
            hh = jnp.zeros((8, LRU_BLOCK_DIM), F32)
            pp = jnp.ones((8, LRU_BLOCK_DIM), F32)
            for m in order:
                hh = tile(a, m) * hh + tile(b, m)
                pp = tile(a, m) * pp
            entering = [None] * 8
            for s in (range(7, -1, -1) if reverse else range(8)):
                entering[s] = cin
                cin = pp[s:s + 1, :] * cin + hh[s:s + 1, :]
            h = jnp.concatenate(entering, axis=0)
            for m in order:
                h = tile(a, m) * h + tile(b, m)
                out_ref[blk, pl.ds(r0 + m, 8, stride=seg), :] = h
        carry_scr[d:d + 1, cs] = cin


def _lru_kernel(xf_ref, xfp_ref, xfn_ref, xb_ref, xbp_ref, xbn_ref, cw_ref, cb_ref, wg_ref, bg_ref,
                lam_ref, hf_ref, hb_ref, xe_scr, carry_scr):
    i = pl.program_id(0)
    nc = pl.num_programs(0)

    @pl.when(i == 0)
    def _():
        carry_scr[...] = jnp.zeros_like(carry_scr)

    shared = (cw_ref, cb_ref, wg_ref, bg_ref, lam_ref)
    scr = (xe_scr, carry_scr)
    _lru_direction(0, False, i, nc, xf_ref, xfp_ref, xfn_ref, *shared, hf_ref, *scr)
    _lru_direction(1, True, nc - 1 - i, nc, xb_ref, xbp_ref, xbn_ref, *shared, hb_ref, *scr)


def _lru(proj, conv_w, conv_b, w_a, b_a, w_x, b_x, lam):
    s_len = proj.shape[0]
    nc = s_len // LRU_CHUNK
    halo_per_chunk = LRU_CHUNK // BF16_SUBLANES
    n_halo = s_len // BF16_SUBLANES
    wg = (-jnp.concatenate([w_a, w_x], axis=-1)).astype(BF16)
    bg = -jnp.stack([b_a, b_x], axis=1)

    def fwd(i):
        return i

    def bwd(i):
        return nc - 1 - i

    def specs(chunk_of):
        return [
            pl.BlockSpec((LRU_CHUNK, LRU_WIDTH), lambda i: (chunk_of(i), COL_XLRU)),
            pl.BlockSpec((BF16_SUBLANES, LRU_WIDTH),
                         lambda i: (jnp.maximum(chunk_of(i) * halo_per_chunk - 1, 0), COL_XLRU)),
            pl.BlockSpec((BF16_SUBLANES, LRU_WIDTH),
                         lambda i: (jnp.minimum((chunk_of(i) + 1) * halo_per_chunk, n_halo - 1), COL_XLRU)),
        ]

    def whole(shape):
        return pl.BlockSpec(shape, lambda i: (0,) * len(shape))

    return pl.pallas_call(
        _lru_kernel,
        grid=(nc,),
        in_specs=specs(fwd) + specs(bwd) + [
            whole((CONV_WIDTH, LRU_WIDTH)), whole((1, LRU_WIDTH)),
            whole((2, LRU_BLOCKS, LRU_BLOCK_DIM, 2 * LRU_BLOCK_DIM)),
            whole((2, 2, LRU_WIDTH)), whole((2, LRU_WIDTH))],
        out_specs=[pl.BlockSpec((LRU_BLOCKS, LRU_CHUNK, LRU_BLOCK_DIM), lambda i: (0, i, 0)),
                   pl.BlockSpec((LRU_BLOCKS, LRU_CHUNK, LRU_BLOCK_DIM), lambda i: (0, nc - 1 - i, 0))],
        out_shape=[jax.ShapeDtypeStruct((LRU_BLOCKS, s_len, LRU_BLOCK_DIM), F32),
                   jax.ShapeDtypeStruct((LRU_BLOCKS, s_len, LRU_BLOCK_DIM), F32)],
        scratch_shapes=[pltpu.VMEM((2, LRU_BLOCKS, LRU_CHUNK + 2 * BF16_SUBLANES, LRU_BLOCK_DIM), F32),
                        pltpu.VMEM((2, LRU_WIDTH), F32)],
        compiler_params=_params("arbitrary"),
        name="rglru",
    )(proj, proj, proj, proj, proj, proj, conv_w, conv_b.reshape(1, LRU_WIDTH), wg, bg, lam)


def _mix_kernel(q_ref, kv_ref, zm_ref, gm_ref, hf_ref, hb_ref, zl_ref, gl_ref, ym_ref, yl_ref, y_scr):
    scale = MEM_HEAD_DIM ** -0.5
    for h in range(MEM_HEADS):
        cs = slice(h * MEM_HEAD_DIM, (h + 1) * MEM_HEAD_DIM)
        vs = slice(MEM_WIDTH + h * MEM_HEAD_DIM, MEM_WIDTH + (h + 1) * MEM_HEAD_DIM)
        s = lax.dot_general(q_ref[:, cs], kv_ref[:, cs], (((1,), (1,)), ((), ())),
                            preferred_element_type=F32) * scale
        m = jnp.max(s, axis=-1, keepdims=True)
        p = jnp.exp(s - m)
        denom = jnp.sum(p, axis=-1, keepdims=True)
        y_scr[:, cs] = jnp.dot(p.astype(BF16), kv_ref[:, vs], preferred_element_type=F32) / denom
    ym_ref[...] = _norm_gate(y_scr[...], gm_ref[...], zm_ref[...].astype(F32)).astype(ym_ref.dtype)
    y_lru = jnp.concatenate([hf_ref[blk] + hb_ref[blk] for blk in range(LRU_BLOCKS)], axis=1)
    yl_ref[...] = _norm_gate(y_lru, gl_ref[...],
                             zl_ref[...].astype(F32)).astype(yl_ref.dtype)


def _mix(proj, kv, h_fwd, h_bwd, g_mem, g_lru):
    s_len = proj.shape[0]
    rows = MIX_ROWS
    row_block = lambda col: pl.BlockSpec((rows, 1024), lambda i: (i, col))
    const = lambda shape: pl.BlockSpec(shape, lambda i: (0, 0))
    slab_block = pl.BlockSpec((LRU_BLOCKS, rows, LRU_BLOCK_DIM), lambda i: (0, i, 0))
    return pl.pallas_call(
        _mix_kernel,
        grid=(s_len // rows,),
        in_specs=[row_block(COL_QMEM), const((N_MEM, 2 * MEM_WIDTH)), row_block(COL_ZMEM),
                  const((1, MEM_WIDTH)), slab_block, slab_block, row_block(COL_ZLRU),
                  const((1, LRU_WIDTH))],
        out_specs=[row_block(0), row_block(0)],
        out_shape=[jax.ShapeDtypeStruct((s_len, MEM_WIDTH), BF16),
                   jax.ShapeDtypeStruct((s_len, LRU_WIDTH), BF16)],
        scratch_shapes=[pltpu.VMEM((rows, MEM_WIDTH), F32)],
        compiler_params=_params("parallel"),
        name="mem_attn_lru_finish",
    )(proj, kv, proj, g_mem.reshape(1, MEM_WIDTH), h_fwd, h_bwd, proj, g_lru.reshape(1, LRU_WIDTH))


def _out_proj_kernel(ya_ref, yl_ref, ym_ref, wa_ref, wl_ref, wm_ref, x_ref, g_ref, o_ref, acc_scr, ss_scr):
    i = pl.program_id(0)
    j = pl.program_id(1)
    n_row_tiles = pl.num_programs(0) - 1

    @pl.when(i < n_row_tiles)
    def _():
        slot = lax.rem(i, 2)
        acc = jnp.dot(ya_ref[...], wa_ref[...], preferred_element_type=F32)
        acc += jnp.dot(yl_ref[...], wl_ref[...], preferred_element_type=F32)
        acc += jnp.dot(ym_ref[...], wm_ref[...], preferred_element_type=F32)
        acc_scr[slot, j] = acc
        sq = acc * acc
        part = sq[:, 0:LANES]
        for c in range(1, OUT_TN // LANES):
            part = part + sq[:, c * LANES:(c + 1) * LANES]

        @pl.when(j == 0)
        def _():
            ss_scr[slot] = part

        @pl.when(j > 0)
        def _():
            ss_scr[slot] += part

    @pl.when(i > 0)
    def _():
        slot = lax.rem(i + 1, 2)
        ms = jnp.sum(ss_scr[slot], axis=-1, keepdims=True) * (1.0 / D_MODEL)
        o_ref[...] = x_ref[...] + acc_scr[slot, j] * lax.rsqrt(ms + EPS) * g_ref[...]


def _out_proj_post(y_att, y_lru, y_mem, w_out, x, post_g):
    s_len = y_att.shape[0]
    tm, tn = OUT_TM, OUT_TN
    ni, nj = s_len // tm, D_MODEL // tn
    lru_row = ATT_WIDTH // LRU_WIDTH

    def y_spec(width):
        return pl.BlockSpec((tm, width), lambda i, j: (jnp.minimum(i, ni - 1), 0))

    def w_spec(rows, row_block):
        return pl.BlockSpec((rows, tn), lambda i, j: (row_block, jnp.where(i < ni, j, nj - 1)))

    def finished_tile(i, j):
        return (jnp.maximum(i - 1, 0), jnp.where(i > 0, j, 0))

    return pl.pallas_call(
        _out_proj_kernel,
        grid=(ni + 1, nj),
        in_specs=[y_spec(ATT_WIDTH), y_spec(LRU_WIDTH), y_spec(MEM_WIDTH),
                  w_spec(ATT_WIDTH, 0), w_spec(LRU_WIDTH, lru_row), w_spec(MEM_WIDTH, lru_row + 1),
                  pl.BlockSpec((tm, tn), finished_tile),
                  pl.BlockSpec((1, tn), lambda i, j: (0, j))],
        out_specs=pl.BlockSpec((tm, tn), finished_tile),
        out_shape=jax.ShapeDtypeStruct((s_len, D_MODEL), F32),
        scratch_shapes=[pltpu.VMEM((2, nj, tm, tn), F32),
                        pltpu.VMEM((2, tm, LANES), F32)],
        compiler_params=_params("arbitrary", "arbitrary"),
        name="out_proj_post_norm",
    )(y_att, y_lru, y_mem, w_out, w_out, w_out, x, post_g.reshape(1, D_MODEL))


def _layer(h, mem, pre_g, w_in, sink, conv_w, conv_b, w_a, b_a, w_x, b_x, lam, mem_g, w_mem_kv,
           g_att, g_lru, g_mem, w_out, post_g):
    u = _rmsnorm_cast(h, pre_g)
    proj = _matmul(u, w_in.astype(BF16), BF16, "in_proj")
    kv = _matmul(_rmsnorm_cast(mem, mem_g), w_mem_kv.astype(BF16), BF16, "mem_kv_proj")
    y_att = _win_attn(proj, sink, g_att)
    h_fwd, h_bwd = _lru(proj, conv_w, conv_b, w_a, b_a, w_x, b_x, lam)
    y_mem, y_lru = _mix(proj, kv, h_fwd, h_bwd, g_mem, g_lru)
    return _out_proj_post(y_att, y_lru, y_mem, w_out.astype(BF16), h, post_g)


def kernel(x, mem, pre_norm_gain, w_in, att_sink, conv_w, conv_b, lru_w_a, lru_b_a, lru_w_x, lru_b_x,
           lru_lambda, mem_norm_gain, w_mem_kv, att_out_gain, lru_out_gain, mem_out_gain, w_out,
           post_norm_gain):
    batch = x.shape[0]
    depth = w_in.shape[0]
    outs = []
    for b in range(batch):
        h = x[b]
        for l in range(depth):
            h = _layer(h, mem[b], pre_norm_gain[l], w_in[l], att_sink[l], conv_w[l], conv_b[l],
                       lru_w_a[l], lru_b_a[l], lru_w_x[l], lru_b_x[l], lru_lambda[l],
                       mem_norm_gain[l], w_mem_kv[l], att_out_gain[l], lru_out_gain[l],
                       mem_out_gain[l], w_out[l], post_norm_gain[l])
        outs.append(h)
    return jnp.stack(outs, axis=0)
```

```python
import functools
import math

import jax
import jax.numpy as jnp
from jax import lax
from jax.experimental import pallas as pl
from jax.experimental.pallas import tpu as pltpu

F32 = jnp.float32
BF16 = jnp.bfloat16

D_MODEL = 4096
SEQ = 8192
N_MEM = 256
HEAD_DIM = 128
ATT_WIDTH = 2048
ATT_HEADS = 16
ATT_KV_HEADS = 4
ATT_GROUP = ATT_HEADS // ATT_KV_HEADS
KV_WIDTH = ATT_KV_HEADS * HEAD_DIM
BLOCK = 128
ROPE_THETA = 500000.0
ROT_DIM = HEAD_DIM // 4
ROT_HALF = ROT_DIM // 2
LRU_WIDTH = 1024
LRU_BLOCKS = 8
LRU_BLOCK_DIM = LRU_WIDTH // LRU_BLOCKS
CONV_WIDTH = 4
CONV_LEFT = 2
LRU_C = 8.0
MEM_HEADS = 4
MEM_WIDTH = 1024
MEM_HEAD_DIM = MEM_WIDTH // MEM_HEADS
EPS = 1e-6
IN_WIDTH = 9216
MASK_BIAS = -1e30
F32_TINY = float(jnp.finfo(jnp.float32).tiny)

COL_XLRU = 3
COL_QMEM = 4
COL_ZATT = 5
COL_ZLRU = 7
COL_ZMEM = 8
COL_K512 = 4
COL_V512 = 5

VMEM_LIMIT_BYTES = 56 * 1024 * 1024
BF16_SUBLANES = 16

NORM_ROWS = 256
MM_TM = 1024
MM_TN = 1024
OUT_TM = 1024
OUT_TN = 512
LANES = 128
LRU_CHUNK = 512
LRU_SUBCHUNK_SEGS = (36, 28)
assert 8 * sum(LRU_SUBCHUNK_SEGS) == LRU_CHUNK and all(s % 8 == 4 for s in LRU_SUBCHUNK_SEGS)
MIX_ROWS = 512


def _params(*sem):
    return pltpu.CompilerParams(dimension_semantics=sem, vmem_limit_bytes=VMEM_LIMIT_BYTES)


def _sigmoid(t):
    return 1.0 / (1.0 + jnp.exp(-t))


def _norm_gate(y, g, z):
    ms = jnp.mean(y * y, axis=-1, keepdims=True)
    return y * lax.rsqrt(ms + EPS) * g * (z * _sigmoid(z))


def _rmsnorm_kernel(x_ref, g_ref, o_ref):
    x = x_ref[...]
    ms = jnp.mean(x * x, axis=-1, keepdims=True)
    o_ref[...] = (x * lax.rsqrt(ms + EPS) * g_ref[...]).astype(o_ref.dtype)


def _rmsnorm_cast(x, g):
    m, d = x.shape
    rows = min(NORM_ROWS, m)
    return pl.pallas_call(
        _rmsnorm_kernel,
        grid=(m // rows,),
        in_specs=[pl.BlockSpec((rows, d), lambda i: (i, 0)),
                  pl.BlockSpec((1, d), lambda i: (0, 0))],
        out_specs=pl.BlockSpec((rows, d), lambda i: (i, 0)),
        out_shape=jax.ShapeDtypeStruct((m, d), BF16),
        compiler_params=_params("parallel"),
        name="rmsnorm_cast",
    )(x, g.reshape(1, d))


def _matmul_kernel(a_ref, b_ref, o_ref):
    o_ref[...] = jnp.dot(a_ref[...], b_ref[...], preferred_element_type=F32).astype(o_ref.dtype)


def _matmul(a, b, out_dtype, name):
    m, k = a.shape
    _, n = b.shape
    tm = min(MM_TM, m)
    tn = min(MM_TN, n)
    return pl.pallas_call(
        _matmul_kernel,
        grid=(m // tm, n // tn),
        in_specs=[pl.BlockSpec((tm, k), lambda i, j: (i, 0)),
                  pl.BlockSpec((k, tn), lambda i, j: (0, j))],
        out_specs=pl.BlockSpec((tm, tn), lambda i, j: (i, j)),
        out_shape=jax.ShapeDtypeStruct((m, n), out_dtype),
        compiler_params=_params("parallel", "parallel"),
        name=name,
    )(a, b)


def _rope_tables(block_idx, freq, base_cos, base_sin, mult):
    a = (block_idx * BLOCK).astype(F32) * freq
    ca, sa = jnp.cos(a) * mult, jnp.sin(a) * mult
    c = ca * base_cos - sa * base_sin
    s = sa * base_cos + ca * base_sin
    lane = lax.broadcasted_iota(jnp.int32, (BLOCK, HEAD_DIM), 1)
    return c, jnp.where(lane < ROT_HALF, -s, 0.0), jnp.where(lane >= ROT_HALF, s, 0.0)


def _rope(t, tables):
    c, sin_lo, sin_hi = tables
    return (t * c + pltpu.roll(t, HEAD_DIM - ROT_HALF, axis=1) * sin_lo
            + pltpu.roll(t, ROT_HALF, axis=1) * sin_hi)


def _win_attn_kernel(sink_ref, q_ref, k0_ref, kn_ref, vp_ref, vc_ref, vn_ref, z0_ref, z1_ref, freq_ref,
                     g_ref, o_ref, k_ring, bias_scr, cos_scr, sin_scr, y_scr):
    n = pl.program_id(0)
    nb = pl.num_programs(0)
    freq = freq_ref[...]
    rows = ATT_GROUP * BLOCK
    log2e = math.log2(math.e)

    def rope_keys(k_ref, tables):
        return jnp.concatenate(
            [_rope(k_ref[:, h * HEAD_DIM:(h + 1) * HEAD_DIM].astype(F32), tables).astype(BF16)
             for h in range(ATT_KV_HEADS)], axis=1)

    @pl.when(n == 0)
    def _():
        ang = lax.broadcasted_iota(jnp.int32, (BLOCK, HEAD_DIM), 0).astype(F32) * freq
        cos_scr[...] = jnp.cos(ang)
        sin_scr[...] = jnp.sin(ang)
        key = lax.broadcasted_iota(jnp.int32, (BLOCK, rows), 0)
        q_local = lax.broadcasted_iota(jnp.int32, (BLOCK, rows), 1) & (BLOCK - 1)
        bias_scr[0] = jnp.where(key >= q_local, 0.0, MASK_BIAS)
        bias_scr[1] = jnp.where(key <= q_local, 0.0, MASK_BIAS)
        bias_scr[2] = jnp.full((BLOCK, rows), MASK_BIAS, F32)
        k_ring[0] = rope_keys(k0_ref, _rope_tables(n, freq, cos_scr[...], sin_scr[...], 1.0))
        k_ring[2] = jnp.zeros((BLOCK, KV_WIDTH), BF16)

    base_cos = cos_scr[...]
    base_sin = sin_scr[...]
    slot_p, slot_c, slot_n = lax.rem(n + 2, 3), lax.rem(n, 3), lax.rem(n + 1, 3)
    k_ring[slot_n] = rope_keys(kn_ref, _rope_tables(n + 1, freq, base_cos, base_sin, 1.0))
    tab_q = _rope_tables(n, freq, base_cos, base_sin, HEAD_DIM ** -0.5 * log2e)
    bias_p = bias_scr[jnp.where(n > 0, 0, 2)]
    bias_n = bias_scr[jnp.where(n < nb - 1, 1, 2)]
    lane_group = lax.broadcasted_iota(jnp.int32, (1, rows), 1) // BLOCK
    first_row = lax.broadcasted_iota(jnp.int32, (BF16_SUBLANES, rows), 0) == 0
    ones = jnp.ones((3 * BLOCK + BF16_SUBLANES, HEAD_DIM), BF16)

    for h in range(ATT_KV_HEADS):
        ks = slice(h * HEAD_DIM, (h + 1) * HEAD_DIM)
        k_all = jnp.concatenate([k_ring[slot_p, :, ks], k_ring[slot_c, :, ks], k_ring[slot_n, :, ks]],
                                axis=0)
        q_all = jnp.concatenate(
            [_rope(q_ref[:, (h * ATT_GROUP + g) * HEAD_DIM:(h * ATT_GROUP + g + 1) * HEAD_DIM]
                   .astype(F32), tab_q).astype(BF16) for g in range(ATT_GROUP)], axis=0)
        sink = jnp.zeros((1, rows), F32)
        for g in range(ATT_GROUP):
            sink = jnp.where(lane_group == g, sink_ref[h * ATT_GROUP + g] * log2e, sink)
        s = lax.dot_general(k_all, q_all, (((1,), (1,)), ((), ())), preferred_element_type=F32)
        s_p = s[:BLOCK] + bias_p
        s_c = s[BLOCK:2 * BLOCK]
        s_n = s[2 * BLOCK:] + bias_n
        m = jnp.max(jnp.maximum(jnp.maximum(s_p, s_c), s_n), axis=0, keepdims=True)
        m = jnp.maximum(m, sink)
        p_sink = jnp.where(first_row, jnp.exp2(sink - m), 0.0)
        p = jnp.concatenate([jnp.exp2(s_p - m), jnp.exp2(s_c - m), jnp.exp2(s_n - m), p_sink],
                            axis=0).astype(BF16)
        v_all = jnp.concatenate([vp_ref[:, ks], vc_ref[:, ks], vn_ref[:, ks],
                                 jnp.zeros((BF16_SUBLANES, HEAD_DIM), BF16)], axis=0)
        v_aug = jnp.concatenate([v_all, ones], axis=1)
        o_aug = lax.dot_general(p, v_aug, (((0,), (0,)), ((), ())), preferred_element_type=F32)
        o = o_aug[:, :HEAD_DIM] / o_aug[:, HEAD_DIM:]
        for g in range(ATT_GROUP):
            hq = h * ATT_GROUP + g
            y_scr[:, hq * HEAD_DIM:(hq + 1) * HEAD_DIM] = o[g * BLOCK:(g + 1) * BLOCK]

    z = jnp.concatenate([z0_ref[...], z1_ref[...]], axis=1).astype(F32)
    o_ref[...] = _norm_gate(y_scr[...], g_ref[...], z).astype(o_ref.dtype)


def _win_attn(proj, sink, g_att):
    s_len = proj.shape[0]
    nb = s_len // BLOCK
    inv_freq = jnp.power(jnp.float32(ROPE_THETA),
                         -jnp.arange(ROT_HALF, dtype=F32) * 2.0 / ROT_DIM)
    freq = jnp.concatenate([inv_freq, inv_freq, jnp.zeros((HEAD_DIM - ROT_DIM,), F32)]).reshape(1, HEAD_DIM)

    def kv_spec(col, shift):
        return pl.BlockSpec((BLOCK, KV_WIDTH),
                            lambda n, sink: (jnp.clip(n + shift, 0, nb - 1), col))

    grid_spec = pltpu.PrefetchScalarGridSpec(
        num_scalar_prefetch=1,
        grid=(nb,),
        in_specs=[pl.BlockSpec((BLOCK, ATT_WIDTH), lambda n, sink: (n, 0)),
                  pl.BlockSpec((BLOCK, KV_WIDTH), lambda n, sink: (0, COL_K512)), kv_spec(COL_K512, 1),
                  kv_spec(COL_V512, -1), kv_spec(COL_V512, 0), kv_spec(COL_V512, 1),
                  pl.BlockSpec((BLOCK, 1024), lambda n, sink: (n, COL_ZATT)),
                  pl.BlockSpec((BLOCK, 1024), lambda n, sink: (n, COL_ZATT + 1)),
                  pl.BlockSpec((1, HEAD_DIM), lambda n, sink: (0, 0)),
                  pl.BlockSpec((1, ATT_WIDTH), lambda n, sink: (0, 0))],
        out_specs=pl.BlockSpec((BLOCK, ATT_WIDTH), lambda n, sink: (n, 0)),
        scratch_shapes=[pltpu.VMEM((3, BLOCK, KV_WIDTH), BF16),
                        pltpu.VMEM((3, BLOCK, ATT_GROUP * BLOCK), F32),
                        pltpu.VMEM((BLOCK, HEAD_DIM), F32),
                        pltpu.VMEM((BLOCK, HEAD_DIM), F32),
                        pltpu.VMEM((BLOCK, ATT_WIDTH), F32)],
    )
    return pl.pallas_call(
        _win_attn_kernel,
        grid_spec=grid_spec,
        out_shape=jax.ShapeDtypeStruct((s_len, ATT_WIDTH), BF16),
        compiler_params=_params("arbitrary"),
        name="win_attn",
    )(sink, proj, proj, proj, proj, proj, proj, proj, proj, freq, g_att.reshape(1, ATT_WIDTH))


def _lru_direction(d, reverse, chunk, n_chunks, x_ref, xp_ref, xn_ref, cw_ref, cb_ref, wg_ref,
                   bg_ref, lam_ref, out_ref, xe_scr, carry_scr):
    t = LRU_CHUNK
    halo = BF16_SUBLANES
    for blk in range(LRU_BLOCKS):
        cs = slice(blk * LRU_BLOCK_DIM, (blk + 1) * LRU_BLOCK_DIM)
        xe_scr[d, blk, 0:halo, :] = jnp.where(chunk > 0, xp_ref[:, cs].astype(F32), 0.0)
        xe_scr[d, blk, halo:halo + t, :] = x_ref[:, cs].astype(F32)
        xe_scr[d, blk, halo + t:, :] = jnp.where(chunk < n_chunks - 1, xn_ref[:, cs].astype(F32), 0.0)

    neg_lam = -lam_ref[d:d + 1, :]
    softplus = jnp.maximum(neg_lam, 0.0) + jnp.log(1.0 + jnp.exp(-jnp.abs(neg_lam)))
    decay = (-LRU_C * math.log2(math.e)) * softplus
    row0 = [8 * sum(LRU_SUBCHUNK_SEGS[:k]) for k in range(len(LRU_SUBCHUNK_SEGS))]
    subchunks = list(zip(row0, LRU_SUBCHUNK_SEGS))
    for blk in range(LRU_BLOCKS):
        cs = slice(blk * LRU_BLOCK_DIM, (blk + 1) * LRU_BLOCK_DIM)
        taps = [cw_ref[j:j + 1, cs] for j in range(CONV_WIDTH)]
        cin = carry_scr[d:d + 1, cs]
        for r0, seg in (reversed(subchunks) if reverse else subchunks):
            order = range(seg - 1, -1, -1) if reverse else range(seg)
            tiles = []
            for m in range(seg):
                acc = cb_ref[:, cs]
                for j in range(CONV_WIDTH):
                    start = halo + r0 + m + j - CONV_LEFT
                    acc = acc + taps[j] * xe_scr[d, blk, pl.ds(start, 8, stride=seg), :]
                tiles.append(acc)
            xc = jnp.concatenate(tiles, axis=0)
            gates = jnp.dot(xc.astype(BF16), wg_ref[d, blk], preferred_element_type=F32)
            r = 1.0 / (1.0 + jnp.exp(gates[:, :LRU_BLOCK_DIM] + bg_ref[d, 0:1, cs]))
            i = 1.0 / (1.0 + jnp.exp(gates[:, LRU_BLOCK_DIM:] + bg_ref[d, 1:2, cs]))
            a = jnp.exp2(decay[:, cs] * r)
            w = 1.0 - a * a
            b = w * lax.rsqrt(jnp.maximum(w, F32_TINY)) * i * xc
            tile = lambda v, m: v[8 * m:8 * m + 8]
            hh = jnp.zeros((8, LRU_BLOCK_DIM), F32)
            pp = jnp.ones((8, LRU_BLOCK_DIM), F32)
            for m in order:
                hh = tile(a, m) * hh + tile(b, m)
                pp = tile(a, m) * pp
            entering = [None] * 8
            for s in (range(7, -1, -1) if reverse else range(8)):
                entering[s] = cin
                cin = pp[s:s + 1, :] * cin + hh[s:s + 1, :]
            h = jnp.concatenate(entering, axis=0)
            for m in order:
                h = tile(a, m) * h + tile(b, m)
                out_ref[blk, pl.ds(r0 + m, 8, stride=seg), :] = h
        carry_scr[d:d + 1, cs] = cin


def _lru_kernel(xf_ref, xfp_ref, xfn_ref, xb_ref, xbp_ref, xbn_ref, cw_ref, cb_ref, wg_ref, bg_ref,
                lam_ref, hf_ref, hb_ref, xe_scr, carry_scr):
    i = pl.program_id(0)
    nc = pl.num_programs(0)

    @pl.when(i == 0)
    def _():
        carry_scr[...] = jnp.zeros_like(carry_scr)

    shared = (cw_ref, cb_ref, wg_ref, bg_ref, lam_ref)
    scr = (xe_scr, carry_scr)
    _lru_direction(0, False, i, nc, xf_ref, xfp_ref, xfn_ref, *shared, hf_ref, *scr)
    _lru_direction(1, True, nc - 1 - i, nc, xb_ref, xbp_ref, xbn_ref, *shared, hb_ref, *scr)


def _lru(proj, conv_w, conv_b, w_a, b_a, w_x, b_x, lam):
    s_len = proj.shape[0]
    nc = s_len // LRU_CHUNK
    halo_per_chunk = LRU_CHUNK // BF16_SUBLANES
    n_halo = s_len // BF16_SUBLANES
    wg = (-jnp.concatenate([w_a, w_x], axis=-1)).astype(BF16)
    bg = -jnp.stack([b_a, b_x], axis=1)

    def fwd(i):
        return i

    def bwd(i):
        return nc - 1 - i

    def specs(chunk_of):
        return [
            pl.BlockSpec((LRU_CHUNK, LRU_WIDTH), lambda i: (chunk_of(i), COL_XLRU)),
            pl.BlockSpec((BF16_SUBLANES, LRU_WIDTH),
                         lambda i: (jnp.maximum(chunk_of(i) * halo_per_chunk - 1, 0), COL_XLRU)),
            pl.BlockSpec((BF16_SUBLANES, LRU_WIDTH),
                         lambda i: (jnp.minimum((chunk_of(i) + 1) * halo_per_chunk, n_halo - 1), COL_XLRU)),
        ]

    def whole(shape):
        return pl.BlockSpec(shape, lambda i: (0,) * len(shape))

    return pl.pallas_call(
        _lru_kernel,
        grid=(nc,),
        in_specs=specs(fwd) + specs(bwd) + [
            whole((CONV_WIDTH, LRU_WIDTH)), whole((1, LRU_WIDTH)),
            whole((2, LRU_BLOCKS, LRU_BLOCK_DIM, 2 * LRU_BLOCK_DIM)),
            whole((2, 2, LRU_WIDTH)), whole((2, LRU_WIDTH))],
        out_specs=[pl.BlockSpec((LRU_BLOCKS, LRU_CHUNK, LRU_BLOCK_DIM), lambda i: (0, i, 0)),
                   pl.BlockSpec((LRU_BLOCKS, LRU_CHUNK, LRU_BLOCK_DIM), lambda i: (0, nc - 1 - i, 0))],
        out_shape=[jax.ShapeDtypeStruct((LRU_BLOCKS, s_len, LRU_BLOCK_DIM), F32),
                   jax.ShapeDtypeStruct((LRU_BLOCKS, s_len, LRU_BLOCK_DIM), F32)],
        scratch_shapes=[pltpu.VMEM((2, LRU_BLOCKS, LRU_CHUNK + 2 * BF16_SUBLANES, LRU_BLOCK_DIM), F32),
                        pltpu.VMEM((2, LRU_WIDTH), F32)],
        compiler_params=_params("arbitrary"),
        name="rglru",
    )(proj, proj, proj, proj, proj, proj, conv_w, conv_b.reshape(1, LRU_WIDTH), wg, bg, lam)


def _mix_kernel(q_ref, kv_ref, zm_ref, gm_ref, hf_ref, hb_ref, zl_ref, gl_ref, ym_ref, yl_ref, y_scr):
    scale = MEM_HEAD_DIM ** -0.5
    for h in range(MEM_HEADS):
        cs = slice(h * MEM_HEAD_DIM, (h + 1) * MEM_HEAD_DIM)
        vs = slice(MEM_WIDTH + h * MEM_HEAD_DIM, MEM_WIDTH + (h + 1) * MEM_HEAD_DIM)
        s = lax.dot_general(q_ref[:, cs], kv_ref[:, cs], (((1,), (1,)), ((), ())),
                            preferred_element_type=F32) * scale
        m = jnp.max(s, axis=-1, keepdims=True)
        p = jnp.exp(s - m)
        denom = jnp.sum(p, axis=-1, keepdims=True)
        y_scr[:, cs] = jnp.dot(p.astype(BF16), kv_ref[:, vs], preferred_element_type=F32) / denom
    ym_ref[...] = _norm_gate(y_scr[...], gm_ref[...], zm_ref[...].astype(F32)).astype(ym_ref.dtype)
    y_lru = jnp.concatenate([hf_ref[blk] + hb_ref[blk] for blk in range(LRU_BLOCKS)], axis=1)
    yl_ref[...] = _norm_gate(y_lru, gl_ref[...],
                             zl_ref[...].astype(F32)).astype(yl_ref.dtype)


def _mix(proj, kv, h_fwd, h_bwd, g_mem, g_lru):
    s_len = proj.shape[0]
    rows = MIX_ROWS
    row_block = lambda col: pl.BlockSpec((rows, 1024), lambda i: (i, col))
    const = lambda shape: pl.BlockSpec(shape, lambda i: (0, 0))
    slab_block = pl.BlockSpec((LRU_BLOCKS, rows, LRU_BLOCK_DIM), lambda i: (0, i, 0))
    return pl.pallas_call(
        _mix_kernel,
        grid=(s_len // rows,),
        in_specs=[row_block(COL_QMEM), const((N_MEM, 2 * MEM_WIDTH)), row_block(COL_ZMEM),
                  const((1, MEM_WIDTH)), slab_block, slab_block, row_block(COL_ZLRU),
                  const((1, LRU_WIDTH))],
        out_specs=[row_block(0), row_block(0)],
        out_shape=[jax.ShapeDtypeStruct((s_len, MEM_WIDTH), BF16),
                   jax.ShapeDtypeStruct((s_len, LRU_WIDTH), BF16)],
        scratch_shapes=[pltpu.VMEM((rows, MEM_WIDTH), F32)],
        compiler_params=_params("parallel"),
        name="mem_attn_lru_finish",
    )(proj, kv, proj, g_mem.reshape(1, MEM_WIDTH), h_fwd, h_bwd, proj, g_lru.reshape(1, LRU_WIDTH))


def _out_proj_kernel(ya_ref, yl_ref, ym_ref, wa_ref, wl_ref, wm_ref, x_ref, g_ref, o_ref, acc_scr, ss_scr):
    i = pl.program_id(0)
    j = pl.program_id(1)
    n_row_tiles = pl.num_programs(0) - 1

    @pl.when(i < n_row_tiles)
    def _():
        slot = lax.rem(i, 2)
        acc = jnp.dot(ya_ref[...], wa_ref[...], preferred_element_type=F32)
        acc += jnp.dot(yl_ref[...], wl_ref[...], preferred_element_type=F32)
        acc += jnp.dot(ym_ref[...], wm_ref[...], preferred_element_type=F32)
        acc_scr[slot, j] = acc.astype(acc_scr.dtype)
        sq = acc * acc
        part = sq[:, 0:LANES]
        for c in range(1, OUT_TN // LANES):
            part = part + sq[:, c * LANES:(c + 1) * LANES]

        @pl.when(j == 0)
        def _():
            ss_scr[slot] = part

        @pl.when(j > 0)
        def _():
            ss_scr[slot] += part

    @pl.when(i > 0)
    def _():
        slot = lax.rem(i + 1, 2)
        ms = jnp.sum(ss_scr[slot], axis=-1, keepdims=True) * (1.0 / D_MODEL)
        o_ref[...] = x_ref[...] + acc_scr[slot, j].astype(F32) * lax.rsqrt(ms + EPS) * g_ref[...]


def _out_proj_post(y_att, y_lru, y_mem, w_out, x, post_g):
    s_len = y_att.shape[0]
    tm, tn = OUT_TM, OUT_TN
    ni, nj = s_len // tm, D_MODEL // tn
    lru_row = ATT_WIDTH // LRU_WIDTH

    def y_spec(width):
        return pl.BlockSpec((tm, width), lambda i, j: (jnp.minimum(i, ni - 1), 0))

    def w_spec(rows, row_block):
        return pl.BlockSpec((rows, tn), lambda i, j: (row_block, jnp.where(i < ni, j, nj - 1)))

    def finished_tile(i, j):
        return (jnp.maximum(i - 1, 0), jnp.where(i > 0, j, 0))

    return pl.pallas_call(
        _out_proj_kernel,
        grid=(ni + 1, nj),
        in_specs=[y_spec(ATT_WIDTH), y_spec(LRU_WIDTH), y_spec(MEM_WIDTH),
                  w_spec(ATT_WIDTH, 0), w_spec(LRU_WIDTH, lru_row), w_spec(MEM_WIDTH, lru_row + 1),
                  pl.BlockSpec((tm, tn), finished_tile),
                  pl.BlockSpec((1, tn), lambda i, j: (0, j))],
        out_specs=pl.BlockSpec((tm, tn), finished_tile),
        out_shape=jax.ShapeDtypeStruct((s_len, D_MODEL), F32),
        scratch_shapes=[pltpu.VMEM((2, nj, tm, tn), BF16),
                        pltpu.VMEM((2, tm, LANES), F32)],
        compiler_params=_params("arbitrary", "arbitrary"),
        name="out_proj_post_norm",
    )(y_att, y_lru, y_mem, w_out, w_out, w_out, x, post_g.reshape(1, D_MODEL))


def _layer(h, mem, pre_g, w_in, sink, conv_w, conv_b, w_a, b_a, w_x, b_x, lam, mem_g, w_mem_kv,
           g_att, g_lru, g_mem, w_out, post_g):
    u = _rmsnorm_cast(h, pre_g)
    proj = _matmul(u, w_in.astype(BF16), BF16, "in_proj")
    kv = _matmul(_rmsnorm_cast(mem, mem_g), w_mem_kv.astype(BF16), BF16, "mem_kv_proj")
    y_att = _win_attn(proj, sink, g_att)
    h_fwd, h_bwd = _lru(proj, conv_w, conv_b, w_a, b_a, w_x, b_x, lam)
    y_mem, y_lru = _mix(proj, kv, h_fwd, h_bwd, g_mem, g_lru)
    return _out_proj_post(y_att, y_lru, y_mem, w_out.astype(BF16), h, post_g)


def kernel(x, mem, pre_norm_gain, w_in, att_sink, conv_w, conv_b, lru_w_a, lru_b_a, lru_w_x, lru_b_x,
           lru_lambda, mem_norm_gain, w_mem_kv, att_out_gain, lru_out_gain, mem_out_gain, w_out,
           post_norm_gain):
    batch = x.shape[0]
    depth = w_in.shape[0]
    outs = []
    for b in range(batch):
        h = x[b]
        for l in range(depth):
            h = _layer(h, mem[b], pre_norm_gain[l], w_in[l], att_sink[l], conv_w[l], conv_b[l],
                       lru_w_a[l], lru_b_a[l], lru_w_x[l], lru_b_x[l], lru_lambda[l],
                       mem_norm_gain[l], w_mem_kv[l], att_out_gain[l], lru_out_gain[l],
                       mem_out_gain[l], w_out[l], post_norm_gain[l])
        outs.append(h)
    return jnp.stack(outs, axis=0)
```

```python
import functools
import math

import jax
import jax.numpy as jnp
from jax import lax
from jax.experimental import pallas as pl
from jax.experimental.pallas import tpu as pltpu

F32 = jnp.float32
BF16 = jnp.bfloat16

D_MODEL = 4096
SEQ = 8192
N_MEM = 256
HEAD_DIM = 128
ATT_WIDTH = 2048
ATT_HEADS = 16
ATT_KV_HEADS = 4
ATT_GROUP = ATT_HEADS // ATT_KV_HEADS
KV_WIDTH = ATT_KV_HEADS * HEAD_DIM
BLOCK = 128
ROPE_THETA = 500000.0
ROT_DIM = HEAD_DIM // 4
ROT_HALF = ROT_DIM // 2
LRU_WIDTH = 1024
LRU_BLOCKS = 8
LRU_BLOCK_DIM = LRU_WIDTH // LRU_BLOCKS
CONV_WIDTH = 4
CONV_LEFT = 2
LRU_C = 8.0
MEM_HEADS = 4
MEM_WIDTH = 1024
MEM_HEAD_DIM = MEM_WIDTH // MEM_HEADS
EPS = 1e-6
IN_WIDTH = 9216
MASK_BIAS = -1e30
F32_TINY = float(jnp.finfo(jnp.float32).tiny)

COL_XLRU = 3
COL_QMEM = 4
COL_ZATT = 5
COL_ZLRU = 7
COL_ZMEM = 8
COL_K512 = 4
COL_V512 = 5

VMEM_LIMIT_BYTES = 56 * 1024 * 1024
BF16_SUBLANES = 16

NORM_ROWS = 256
MM_TM = 1024
MM_TN = 1024
IN_TM = 1024
IN_TN = 512
IN_KC = 512
IN_KCHUNKS = D_MODEL // IN_KC
OUT_TM = 1024
OUT_TN = 512
LANES = 128
LRU_CHUNK = 512
LRU_SUBCHUNK_SEGS = (36, 28)
assert 8 * sum(LRU_SUBCHUNK_SEGS) == LRU_CHUNK and all(s % 8 == 4 for s in LRU_SUBCHUNK_SEGS)
MIX_ROWS = 512


def _params(*sem):
    return pltpu.CompilerParams(dimension_semantics=sem, vmem_limit_bytes=VMEM_LIMIT_BYTES)


def _sigmoid(t):
    return 1.0 / (1.0 + jnp.exp(-t))


def _norm_gate(y, g, z):
    ms = jnp.mean(y * y, axis=-1, keepdims=True)
    return y * lax.rsqrt(ms + EPS) * g * (z * _sigmoid(z))


def _rmsnorm_kernel(x_ref, g_ref, o_ref):
    x = x_ref[...]
    ms = jnp.mean(x * x, axis=-1, keepdims=True)
    o_ref[...] = (x * lax.rsqrt(ms + EPS) * g_ref[...]).astype(o_ref.dtype)


def _rmsnorm_cast(x, g):
    m, d = x.shape
    rows = min(NORM_ROWS, m)
    return pl.pallas_call(
        _rmsnorm_kernel,
        grid=(m // rows,),
        in_specs=[pl.BlockSpec((rows, d), lambda i: (i, 0)),
                  pl.BlockSpec((1, d), lambda i: (0, 0))],
        out_specs=pl.BlockSpec((rows, d), lambda i: (i, 0)),
        out_shape=jax.ShapeDtypeStruct((m, d), BF16),
        compiler_params=_params("parallel"),
        name="rmsnorm_cast",
    )(x, g.reshape(1, d))


def _matmul_kernel(a_ref, b_ref, o_ref):
    o_ref[...] = jnp.dot(a_ref[...], b_ref[...], preferred_element_type=F32).astype(o_ref.dtype)


def _matmul(a, b, out_dtype, name):
    m, k = a.shape
    _, n = b.shape
    tm = min(MM_TM, m)
    tn = min(MM_TN, n)
    return pl.pallas_call(
        _matmul_kernel,
        grid=(m // tm, n // tn),
        in_specs=[pl.BlockSpec((tm, k), lambda i, j: (i, 0)),
                  pl.BlockSpec((k, tn), lambda i, j: (0, j))],
        out_specs=pl.BlockSpec((tm, tn), lambda i, j: (i, j)),
        out_shape=jax.ShapeDtypeStruct((m, n), out_dtype),
        compiler_params=_params("parallel", "parallel"),
        name=name,
    )(a, b)


def _in_proj_kernel(x_ref, g_ref, w_ref, wo_ref, wk_ref, proj_ref, wo_bf_ref, wk_bf_ref, u_scr, ss_scr):
    o = pl.program_id(0)
    j = pl.program_id(1)
    wo_bf_ref[...] = wo_ref[...].astype(BF16)
    wk_bf_ref[...] = wk_ref[...].astype(BF16)

    def normalise_next_tile():
        x = x_ref[...]
        sq = x * x
        part = sq[:, 0:LANES]
        for c in range(1, IN_KC // LANES):
            part = part + sq[:, c * LANES:(c + 1) * LANES]
        ss_old = ss_scr[...]
        ss = jnp.where(j == 0, part, jnp.where(j < IN_KCHUNKS, ss_old + part, ss_old))
        ss_scr[...] = ss
        r = lax.rsqrt(jnp.sum(ss, axis=-1, keepdims=True) * (1.0 / D_MODEL) + EPS)
        chunk = jnp.clip(j - IN_KCHUNKS, 0, IN_KCHUNKS - 1)
        u_scr[lax.rem(o, 2), chunk] = (x * r * g_ref[...]).astype(BF16)

    @pl.when(o == 0)
    def _():
        @pl.when(j == 0)
        def _():
            ss_scr[...] = jnp.zeros_like(ss_scr)

        normalise_next_tile()

    @pl.when(o > 0)
    def _():
        slot = lax.rem(o + 1, 2)
        acc = None
        for c in range(IN_KCHUNKS):
            d = jnp.dot(u_scr[slot, c], w_ref[c * IN_KC:(c + 1) * IN_KC, :].astype(BF16),
                        preferred_element_type=F32)
            acc = d if acc is None else acc + d
        proj_ref[...] = acc.astype(proj_ref.dtype)
        normalise_next_tile()


def _in_proj(x, pre_g, w_in, w_out, w_mem_kv):
    s_len = x.shape[0]
    ni, nj = s_len // IN_TM, IN_WIDTH // IN_TN
    n_steps = (ni + 1) * nj
    side_rows = 32
    n_side = D_MODEL // side_rows
    assert n_side <= n_steps and nj >= 2 * IN_KCHUNKS

    def x_tile(o, j):
        return (jnp.minimum(o, ni - 1), jnp.where(j < 2 * IN_KCHUNKS, lax.rem(j, IN_KCHUNKS), IN_KCHUNKS - 1))

    def side(o, j):
        return (jnp.minimum(o * nj + j, n_side - 1), 0)

    return pl.pallas_call(
        _in_proj_kernel,
        grid=(ni + 1, nj),
        in_specs=[pl.BlockSpec((IN_TM, IN_KC), x_tile),
                  pl.BlockSpec((1, IN_KC), lambda o, j: (0, x_tile(o, j)[1])),
                  pl.BlockSpec((D_MODEL, IN_TN), lambda o, j: (0, jnp.where(o > 0, j, 0))),
                  pl.BlockSpec((side_rows, D_MODEL), side),
                  pl.BlockSpec((side_rows, 2 * MEM_WIDTH), side)],
        out_specs=[pl.BlockSpec((IN_TM, IN_TN), lambda o, j: (jnp.maximum(o - 1, 0), jnp.where(o > 0, j, 0))),
                   pl.BlockSpec((side_rows, D_MODEL), side),
                   pl.BlockSpec((side_rows, 2 * MEM_WIDTH), side)],
        out_shape=[jax.ShapeDtypeStruct((s_len, IN_WIDTH), BF16),
                   jax.ShapeDtypeStruct((D_MODEL, D_MODEL), BF16),
                   jax.ShapeDtypeStruct((D_MODEL, 2 * MEM_WIDTH), BF16)],
        scratch_shapes=[pltpu.VMEM((2, IN_KCHUNKS, IN_TM, IN_KC), BF16),
                        pltpu.VMEM((IN_TM, LANES), F32)],
        compiler_params=_params("arbitrary", "arbitrary"),
        name="in_proj_pre_norm",
    )(x, pre_g.reshape(1, D_MODEL), w_in, w_out, w_mem_kv)


def _rope_tables(block_idx, freq, base_cos, base_sin, mult):
    a = (block_idx * BLOCK).astype(F32) * freq
    ca, sa = jnp.cos(a) * mult, jnp.sin(a) * mult
    c = ca * base_cos - sa * base_sin
    s = sa * base_cos + ca * base_sin
    lane = lax.broadcasted_iota(jnp.int32, (BLOCK, HEAD_DIM), 1)
    return c, jnp.where(lane < ROT_HALF, -s, 0.0), jnp.where(lane >= ROT_HALF, s, 0.0)


def _rope(t, tables):
    c, sin_lo, sin_hi = tables
    return (t * c + pltpu.roll(t, HEAD_DIM - ROT_HALF, axis=1) * sin_lo
            + pltpu.roll(t, ROT_HALF, axis=1) * sin_hi)


def _win_attn_kernel(sink_ref, q_ref, k0_ref, kn_ref, vp_ref, vc_ref, vn_ref, z0_ref, z1_ref, freq_ref,
                     g_ref, o_ref, k_ring, bias_scr, cos_scr, sin_scr, y_scr):
    n = pl.program_id(0)
    nb = pl.num_programs(0)
    freq = freq_ref[...]
    rows = ATT_GROUP * BLOCK
    log2e = math.log2(math.e)

    def rope_keys(k_ref, tables):
        return jnp.concatenate(
            [_rope(k_ref[:, h * HEAD_DIM:(h + 1) * HEAD_DIM].astype(F32), tables).astype(BF16)
             for h in range(ATT_KV_HEADS)], axis=1)

    @pl.when(n == 0)
    def _():
        ang = lax.broadcasted_iota(jnp.int32, (BLOCK, HEAD_DIM), 0).astype(F32) * freq
        cos_scr[...] = jnp.cos(ang)
        sin_scr[...] = jnp.sin(ang)
        key = lax.broadcasted_iota(jnp.int32, (BLOCK, rows), 0)
        q_local = lax.broadcasted_iota(jnp.int32, (BLOCK, rows), 1) & (BLOCK - 1)
        bias_scr[0] = jnp.where(key >= q_local, 0.0, MASK_BIAS)
        bias_scr[1] = jnp.where(key <= q_local, 0.0, MASK_BIAS)
        bias_scr[2] = jnp.full((BLOCK, rows), MASK_BIAS, F32)
        k_ring[0] = rope_keys(k0_ref, _rope_tables(n, freq, cos_scr[...], sin_scr[...], 1.0))
        k_ring[2] = jnp.zeros((BLOCK, KV_WIDTH), BF16)

    base_cos = cos_scr[...]
    base_sin = sin_scr[...]
    slot_p, slot_c, slot_n = lax.rem(n + 2, 3), lax.rem(n, 3), lax.rem(n + 1, 3)
    k_ring[slot_n] = rope_keys(kn_ref, _rope_tables(n + 1, freq, base_cos, base_sin, 1.0))
    tab_q = _rope_tables(n, freq, base_cos, base_sin, HEAD_DIM ** -0.5 * log2e)
    bias_p = bias_scr[jnp.where(n > 0, 0, 2)]
    bias_n = bias_scr[jnp.where(n < nb - 1, 1, 2)]
    lane_group = lax.broadcasted_iota(jnp.int32, (1, rows), 1) // BLOCK
    first_row = lax.broadcasted_iota(jnp.int32, (BF16_SUBLANES, rows), 0) == 0
    ones = jnp.ones((3 * BLOCK + BF16_SUBLANES, HEAD_DIM), BF16)

    for h in range(ATT_KV_HEADS):
        ks = slice(h * HEAD_DIM, (h + 1) * HEAD_DIM)
        k_all = jnp.concatenate([k_ring[slot_p, :, ks], k_ring[slot_c, :, ks], k_ring[slot_n, :, ks]],
                                axis=0)
        q_all = jnp.concatenate(
            [_rope(q_ref[:, (h * ATT_GROUP + g) * HEAD_DIM:(h * ATT_GROUP + g + 1) * HEAD_DIM]
                   .astype(F32), tab_q).astype(BF16) for g in range(ATT_GROUP)], axis=0)
        sink = jnp.zeros((1, rows), F32)
        for g in range(ATT_GROUP):
            sink = jnp.where(lane_group == g, sink_ref[h * ATT_GROUP + g] * log2e, sink)
        s = lax.dot_general(k_all, q_all, (((1,), (1,)), ((), ())), preferred_element_type=F32)
        s_p = s[:BLOCK] + bias_p
        s_c = s[BLOCK:2 * BLOCK]
        s_n = s[2 * BLOCK:] + bias_n
        m = jnp.max(jnp.maximum(jnp.maximum(s_p, s_c), s_n), axis=0, keepdims=True)
        m = jnp.maximum(m, sink)
        p_sink = jnp.where(first_row, jnp.exp2(sink - m), 0.0)
        p = jnp.concatenate([jnp.exp2(s_p - m), jnp.exp2(s_c - m), jnp.exp2(s_n - m), p_sink],
                            axis=0).astype(BF16)
        v_all = jnp.concatenate([vp_ref[:, ks], vc_ref[:, ks], vn_ref[:, ks],
                                 jnp.zeros((BF16_SUBLANES, HEAD_DIM), BF16)], axis=0)
        v_aug = jnp.concatenate([v_all, ones], axis=1)
        o_aug = lax.dot_general(p, v_aug, (((0,), (0,)), ((), ())), preferred_element_type=F32)
        o = o_aug[:, :HEAD_DIM] / o_aug[:, HEAD_DIM:]
        for g in range(ATT_GROUP):
            hq = h * ATT_GROUP + g
            y_scr[:, hq * HEAD_DIM:(hq + 1) * HEAD_DIM] = o[g * BLOCK:(g + 1) * BLOCK]

    z = jnp.concatenate([z0_ref[...], z1_ref[...]], axis=1).astype(F32)
    o_ref[...] = _norm_gate(y_scr[...], g_ref[...], z).astype(o_ref.dtype)


def _win_attn(proj, sink, g_att):
    s_len = proj.shape[0]
    nb = s_len // BLOCK
    inv_freq = jnp.power(jnp.float32(ROPE_THETA),
                         -jnp.arange(ROT_HALF, dtype=F32) * 2.0 / ROT_DIM)
    freq = jnp.concatenate([inv_freq, inv_freq, jnp.zeros((HEAD_DIM - ROT_DIM,), F32)]).reshape(1, HEAD_DIM)

    def kv_spec(col, shift):
        return pl.BlockSpec((BLOCK, KV_WIDTH),
                            lambda n, sink: (jnp.clip(n + shift, 0, nb - 1), col))

    grid_spec = pltpu.PrefetchScalarGridSpec(
        num_scalar_prefetch=1,
        grid=(nb,),
        in_specs=[pl.BlockSpec((BLOCK, ATT_WIDTH), lambda n, sink: (n, 0)),
                  pl.BlockSpec((BLOCK, KV_WIDTH), lambda n, sink: (0, COL_K512)), kv_spec(COL_K512, 1),
                  kv_spec(COL_V512, -1), kv_spec(COL_V512, 0), kv_spec(COL_V512, 1),
                  pl.BlockSpec((BLOCK, 1024), lambda n, sink: (n, COL_ZATT)),
                  pl.BlockSpec((BLOCK, 1024), lambda n, sink: (n, COL_ZATT + 1)),
                  pl.BlockSpec((1, HEAD_DIM), lambda n, sink: (0, 0)),
                  pl.BlockSpec((1, ATT_WIDTH), lambda n, sink: (0, 0))],
        out_specs=pl.BlockSpec((BLOCK, ATT_WIDTH), lambda n, sink: (n, 0)),
        scratch_shapes=[pltpu.VMEM((3, BLOCK, KV_WIDTH), BF16),
                        pltpu.VMEM((3, BLOCK, ATT_GROUP * BLOCK), F32),
                        pltpu.VMEM((BLOCK, HEAD_DIM), F32),
                        pltpu.VMEM((BLOCK, HEAD_DIM), F32),
                        pltpu.VMEM((BLOCK, ATT_WIDTH), F32)],
    )
    return pl.pallas_call(
        _win_attn_kernel,
        grid_spec=grid_spec,
        out_shape=jax.ShapeDtypeStruct((s_len, ATT_WIDTH), BF16),
        compiler_params=_params("arbitrary"),
        name="win_attn",
    )(sink, proj, proj, proj, proj, proj, proj, proj, proj, freq, g_att.reshape(1, ATT_WIDTH))


def _lru_direction(d, reverse, chunk, n_chunks, x_ref, xp_ref, xn_ref, cw_ref, cb_ref, wg_ref,
                   bg_ref, lam_ref, out_ref, xe_scr, carry_scr):
    t = LRU_CHUNK
    halo = BF16_SUBLANES
    for blk in range(LRU_BLOCKS):
        cs = slice(blk * LRU_BLOCK_DIM, (blk + 1) * LRU_BLOCK_DIM)
        xe_scr[d, blk, 0:halo, :] = jnp.where(chunk > 0, xp_ref[:, cs].astype(F32), 0.0)
        xe_scr[d, blk, halo:halo + t, :] = x_ref[:, cs].astype(F32)
        xe_scr[d, blk, halo + t:, :] = jnp.where(chunk < n_chunks - 1, xn_ref[:, cs].astype(F32), 0.0)

    neg_lam = -lam_ref[d:d + 1, :]
    softplus = jnp.maximum(neg_lam, 0.0) + jnp.log(1.0 + jnp.exp(-jnp.abs(neg_lam)))
    decay = (-LRU_C * math.log2(math.e)) * softplus
    row0 = [8 * sum(LRU_SUBCHUNK_SEGS[:k]) for k in range(len(LRU_SUBCHUNK_SEGS))]
    subchunks = list(zip(row0, LRU_SUBCHUNK_SEGS))
    for blk in range(LRU_BLOCKS):
        cs = slice(blk * LRU_BLOCK_DIM, (blk + 1) * LRU_BLOCK_DIM)
        taps = [cw_ref[j:j + 1, cs] for j in range(CONV_WIDTH)]
        cin = carry_scr[d:d + 1, cs]
        for r0, seg in (reversed(subchunks) if reverse else subchunks):
            order = range(seg - 1, -1, -1) if reverse else range(seg)
            tiles = []
            for m in range(seg):
                acc = cb_ref[:, cs]
                for j in range(CONV_WIDTH):
                    start = halo + r0 + m + j - CONV_LEFT
                    acc = acc + taps[j] * xe_scr[d, blk, pl.ds(start, 8, stride=seg), :]
                tiles.append(acc)
            xc = jnp.concatenate(tiles, axis=0)
            gates = jnp.dot(xc.astype(BF16), wg_ref[d, blk], preferred_element_type=F32)
            r = 1.0 / (1.0 + jnp.exp(gates[:, :LRU_BLOCK_DIM] + bg_ref[d, 0:1, cs]))
            i = 1.0 / (1.0 + jnp.exp(gates[:, LRU_BLOCK_DIM:] + bg_ref[d, 1:2, cs]))
            a = jnp.exp2(decay[:, cs] * r)
            w = 1.0 - a * a
            b = w * lax.rsqrt(jnp.maximum(w, F32_TINY)) * i * xc
            tile = lambda v, m: v[8 * m:8 * m + 8]
            hh = jnp.zeros((8, LRU_BLOCK_DIM), F32)
            pp = jnp.ones((8, LRU_BLOCK_DIM), F32)
            for m in order:
                hh = tile(a, m) * hh + tile(b, m)
                pp = tile(a, m) * pp
            entering = [None] * 8
            for s in (range(7, -1, -1) if reverse else range(8)):
                entering[s] = cin
                cin = pp[s:s + 1, :] * cin + hh[s:s + 1, :]
            h = jnp.concatenate(entering, axis=0)
            for m in order:
                h = tile(a, m) * h + tile(b, m)
                out_ref[blk, pl.ds(r0 + m, 8, stride=seg), :] = h
        carry_scr[d:d + 1, cs] = cin


def _lru_kernel(xf_ref, xfp_ref, xfn_ref, xb_ref, xbp_ref, xbn_ref, cw_ref, cb_ref, wg_ref, bg_ref,
                lam_ref, hf_ref, hb_ref, xe_scr, carry_scr):
    i = pl.program_id(0)
    nc = pl.num_programs(0)

    @pl.when(i == 0)
    def _():
        carry_scr[...] = jnp.zeros_like(carry_scr)

    shared = (cw_ref, cb_ref, wg_ref, bg_ref, lam_ref)
    scr = (xe_scr, carry_scr)
    _lru_direction(0, False, i, nc, xf_ref, xfp_ref, xfn_ref, *shared, hf_ref, *scr)
    _lru_direction(1, True, nc - 1 - i, nc, xb_ref, xbp_ref, xbn_ref, *shared, hb_ref, *scr)


def _lru(proj, conv_w, conv_b, w_a, b_a, w_x, b_x, lam):
    s_len = proj.shape[0]
    nc = s_len // LRU_CHUNK
    halo_per_chunk = LRU_CHUNK // BF16_SUBLANES
    n_halo = s_len // BF16_SUBLANES
    wg = (-jnp.concatenate([w_a, w_x], axis=-1)).astype(BF16)
    bg = -jnp.stack([b_a, b_x], axis=1)

    def fwd(i):
        return i

    def bwd(i):
        return nc - 1 - i

    def specs(chunk_of):
        return [
            pl.BlockSpec((LRU_CHUNK, LRU_WIDTH), lambda i: (chunk_of(i), COL_XLRU)),
            pl.BlockSpec((BF16_SUBLANES, LRU_WIDTH),
                         lambda i: (jnp.maximum(chunk_of(i) * halo_per_chunk - 1, 0), COL_XLRU)),
            pl.BlockSpec((BF16_SUBLANES, LRU_WIDTH),
                         lambda i: (jnp.minimum((chunk_of(i) + 1) * halo_per_chunk, n_halo - 1), COL_XLRU)),
        ]

    def whole(shape):
        return pl.BlockSpec(shape, lambda i: (0,) * len(shape))

    return pl.pallas_call(
        _lru_kernel,
        grid=(nc,),
        in_specs=specs(fwd) + specs(bwd) + [
            whole((CONV_WIDTH, LRU_WIDTH)), whole((1, LRU_WIDTH)),
            whole((2, LRU_BLOCKS, LRU_BLOCK_DIM, 2 * LRU_BLOCK_DIM)),
            whole((2, 2, LRU_WIDTH)), whole((2, LRU_WIDTH))],
        out_specs=[pl.BlockSpec((LRU_BLOCKS, LRU_CHUNK, LRU_BLOCK_DIM), lambda i: (0, i, 0)),
                   pl.BlockSpec((LRU_BLOCKS, LRU_CHUNK, LRU_BLOCK_DIM), lambda i: (0, nc - 1 - i, 0))],
        out_shape=[jax.ShapeDtypeStruct((LRU_BLOCKS, s_len, LRU_BLOCK_DIM), F32),
                   jax.ShapeDtypeStruct((LRU_BLOCKS, s_len, LRU_BLOCK_DIM), F32)],
        scratch_shapes=[pltpu.VMEM((2, LRU_BLOCKS, LRU_CHUNK + 2 * BF16_SUBLANES, LRU_BLOCK_DIM), F32),
                        pltpu.VMEM((2, LRU_WIDTH), F32)],
        compiler_params=_params("arbitrary"),
        name="rglru",
    )(proj, proj, proj, proj, proj, proj, conv_w, conv_b.reshape(1, LRU_WIDTH), wg, bg, lam)


def _mix_kernel(q_ref, kv_ref, zm_ref, gm_ref, hf_ref, hb_ref, zl_ref, gl_ref, ym_ref, yl_ref, y_scr):
    scale = MEM_HEAD_DIM ** -0.5
    for h in range(MEM_HEADS):
        cs = slice(h * MEM_HEAD_DIM, (h + 1) * MEM_HEAD_DIM)
        vs = slice(MEM_WIDTH + h * MEM_HEAD_DIM, MEM_WIDTH + (h + 1) * MEM_HEAD_DIM)
        s = lax.dot_general(q_ref[:, cs], kv_ref[:, cs], (((1,), (1,)), ((), ())),
                            preferred_element_type=F32) * scale
        m = jnp.max(s, axis=-1, keepdims=True)
        p = jnp.exp(s - m)
        denom = jnp.sum(p, axis=-1, keepdims=True)
        y_scr[:, cs] = jnp.dot(p.astype(BF16), kv_ref[:, vs], preferred_element_type=F32) / denom
    ym_ref[...] = _norm_gate(y_scr[...], gm_ref[...], zm_ref[...].astype(F32)).astype(ym_ref.dtype)
    y_lru = jnp.concatenate([hf_ref[blk] + hb_ref[blk] for blk in range(LRU_BLOCKS)], axis=1)
    yl_ref[...] = _norm_gate(y_lru, gl_ref[...],
                             zl_ref[...].astype(F32)).astype(yl_ref.dtype)


def _mix(proj, kv, h_fwd, h_bwd, g_mem, g_lru):
    s_len = proj.shape[0]
    rows = MIX_ROWS
    row_block = lambda col: pl.BlockSpec((rows, 1024), lambda i: (i, col))
    const = lambda shape: pl.BlockSpec(shape, lambda i: (0, 0))
    slab_block = pl.BlockSpec((LRU_BLOCKS, rows, LRU_BLOCK_DIM), lambda i: (0, i, 0))
    return pl.pallas_call(
        _mix_kernel,
        grid=(s_len // rows,),
        in_specs=[row_block(COL_QMEM), const((N_MEM, 2 * MEM_WIDTH)), row_block(COL_ZMEM),
                  const((1, MEM_WIDTH)), slab_block, slab_block, row_block(COL_ZLRU),
                  const((1, LRU_WIDTH))],
        out_specs=[row_block(0), row_block(0)],
        out_shape=[jax.ShapeDtypeStruct((s_len, MEM_WIDTH), BF16),
                   jax.ShapeDtypeStruct((s_len, LRU_WIDTH), BF16)],
        scratch_shapes=[pltpu.VMEM((rows, MEM_WIDTH), F32)],
        compiler_params=_params("parallel"),
        name="mem_attn_lru_finish",
    )(proj, kv, proj, g_mem.reshape(1, MEM_WIDTH), h_fwd, h_bwd, proj, g_lru.reshape(1, LRU_WIDTH))


def _out_proj_kernel(ya_ref, yl_ref, ym_ref, wa_ref, wl_ref, wm_ref, x_ref, g_ref, o_ref, acc_scr, ss_scr):
    i = pl.program_id(0)
    j = pl.program_id(1)
    n_row_tiles = pl.num_programs(0) - 1

    @pl.when(i < n_row_tiles)
    def _():
        slot = lax.rem(i, 2)
        acc = jnp.dot(ya_ref[...], wa_ref[...], preferred_element_type=F32)
        acc += jnp.dot(yl_ref[...], wl_ref[...], preferred_element_type=F32)
        acc += jnp.dot(ym_ref[...], wm_ref[...], preferred_element_type=F32)
        acc_scr[slot, j] = acc.astype(acc_scr.dtype)
        sq = acc * acc
        part = sq[:, 0:LANES]
        for c in range(1, OUT_TN // LANES):
            part = part + sq[:, c * LANES:(c + 1) * LANES]

        @pl.when(j == 0)
        def _():
            ss_scr[slot] = part

        @pl.when(j > 0)
        def _():
            ss_scr[slot] += part

    @pl.when(i > 0)
    def _():
        slot = lax.rem(i + 1, 2)
        ms = jnp.sum(ss_scr[slot], axis=-1, keepdims=True) * (1.0 / D_MODEL)
        o_ref[...] = x_ref[...] + acc_scr[slot, j].astype(F32) * lax.rsqrt(ms + EPS) * g_ref[...]


def _out_proj_post(y_att, y_lru, y_mem, w_out, x, post_g):
    s_len = y_att.shape[0]
    tm, tn = OUT_TM, OUT_TN
    ni, nj = s_len // tm, D_MODEL // tn
    lru_row = ATT_WIDTH // LRU_WIDTH

    def y_spec(width):
        return pl.BlockSpec((tm, width), lambda i, j: (jnp.minimum(i, ni - 1), 0))

    def w_spec(rows, row_block):
        return pl.BlockSpec((rows, tn), lambda i, j: (row_block, jnp.where(i < ni, j, nj - 1)))

    def finished_tile(i, j):
        return (jnp.maximum(i - 1, 0), jnp.where(i > 0, j, 0))

    return pl.pallas_call(
        _out_proj_kernel,
        grid=(ni + 1, nj),
        in_specs=[y_spec(ATT_WIDTH), y_spec(LRU_WIDTH), y_spec(MEM_WIDTH),
                  w_spec(ATT_WIDTH, 0), w_spec(LRU_WIDTH, lru_row), w_spec(MEM_WIDTH, lru_row + 1),
                  pl.BlockSpec((tm, tn), finished_tile),
                  pl.BlockSpec((1, tn), lambda i, j: (0, j))],
        out_specs=pl.BlockSpec((tm, tn), finished_tile),
        out_shape=jax.ShapeDtypeStruct((s_len, D_MODEL), F32),
        scratch_shapes=[pltpu.VMEM((2, nj, tm, tn), BF16),
                        pltpu.VMEM((2, tm, LANES), F32)],
        compiler_params=_params("arbitrary", "arbitrary"),
        name="out_proj_post_norm",
    )(y_att, y_lru, y_mem, w_out, w_out, w_out, x, post_g.reshape(1, D_MODEL))


def _layer(h, mem, pre_g, w_in, sink, conv_w, conv_b, w_a, b_a, w_x, b_x, lam, mem_g, w_mem_kv,
           g_att, g_lru, g_mem, w_out, post_g):
    proj, w_out_bf, w_mem_kv_bf = _in_proj(h, pre_g, w_in, w_out, w_mem_kv)
    kv = _matmul(_rmsnorm_cast(mem, mem_g), w_mem_kv_bf, BF16, "mem_kv_proj")
    y_att = _win_attn(proj, sink, g_att)
    h_fwd, h_bwd = _lru(proj, conv_w, conv_b, w_a, b_a, w_x, b_x, lam)
    y_mem, y_lru = _mix(proj, kv, h_fwd, h_bwd, g_mem, g_lru)
    return _out_proj_post(y_att, y_lru, y_mem, w_out_bf, h, post_g)


def kernel(x, mem, pre_norm_gain, w_in, att_sink, conv_w, conv_b, lru_w_a, lru_b_a, lru_w_x, lru_b_x,
           lru_lambda, mem_norm_gain, w_mem_kv, att_out_gain, lru_out_gain, mem_out_gain, w_out,
           post_norm_gain):
    batch = x.shape[0]
    depth = w_in.shape[0]
    outs = []
    for b in range(batch):
        h = x[b]
        for l in range(depth):
            h = _layer(h, mem[b], pre_norm_gain[l], w_in[l], att_sink[l], conv_w[l], conv_b[l],
                       lru_w_a[l], lru_b_a[l], lru_w_x[l], lru_b_x[l], lru_lambda[l],
                       mem_norm_gain[l], w_mem_kv[l], att_out_gain[l], lru_out_gain[l],
                       mem_out_gain[l], w_out[l], post_norm_gain[l])
        outs.append(h)
    return jnp.stack(outs, axis=0)
```

```python
import functools
import math

import jax
import jax.numpy as jnp
from jax import lax
from jax.experimental import pallas as pl
from jax.experimental.pallas import tpu as pltpu

F32 = jnp.float32
BF16 = jnp.bfloat16

D_MODEL = 4096
SEQ = 8192
N_MEM = 256
HEAD_DIM = 128
ATT_WIDTH = 2048
ATT_HEADS = 16
ATT_KV_HEADS = 4
ATT_GROUP = ATT_HEADS // ATT_KV_HEADS
KV_WIDTH = ATT_KV_HEADS * HEAD_DIM
BLOCK = 128
ROPE_THETA = 500000.0
ROT_DIM = HEAD_DIM // 4
ROT_HALF = ROT_DIM // 2
LRU_WIDTH = 1024
LRU_BLOCKS = 8
LRU_BLOCK_DIM = LRU_WIDTH // LRU_BLOCKS
CONV_WIDTH = 4
CONV_LEFT = 2
LRU_C = 8.0
MEM_HEADS = 4
MEM_WIDTH = 1024
MEM_HEAD_DIM = MEM_WIDTH // MEM_HEADS
EPS = 1e-6
IN_WIDTH = 9216
MASK_BIAS = -1e30
F32_TINY = float(jnp.finfo(jnp.float32).tiny)

COL_XLRU = 3
COL_QMEM = 4
COL_ZATT = 5
COL_ZLRU = 7
COL_ZMEM = 8
COL_K512 = 4
COL_V512 = 5

VMEM_LIMIT_BYTES = 60000 * 1024
BF16_SUBLANES = 16

NORM_ROWS = 256
MM_TM = 1024
MM_TN = 1024
IN_TM = 1024
IN_TN = 512
IN_KC = 512
IN_KCHUNKS = D_MODEL // IN_KC
TAIL_TM = 512
TAIL_TN = 1024
LANES = 128
LRU_CHUNK = 512
LRU_SUBCHUNK_SEGS = (36, 28)
assert 8 * sum(LRU_SUBCHUNK_SEGS) == LRU_CHUNK and all(s % 8 == 4 for s in LRU_SUBCHUNK_SEGS)
MIX_ROWS = 512


def _params(*sem):
    return pltpu.CompilerParams(dimension_semantics=sem, vmem_limit_bytes=VMEM_LIMIT_BYTES)


def _sigmoid(t):
    return 1.0 / (1.0 + jnp.exp(-t))


def _norm_gate(y, g, z):
    ms = jnp.mean(y * y, axis=-1, keepdims=True)
    return y * lax.rsqrt(ms + EPS) * g * (z * _sigmoid(z))


def _rmsnorm_kernel(x_ref, g_ref, o_ref):
    x = x_ref[...]
    ms = jnp.mean(x * x, axis=-1, keepdims=True)
    o_ref[...] = (x * lax.rsqrt(ms + EPS) * g_ref[...]).astype(o_ref.dtype)


def _rmsnorm_cast(x, g):
    m, d = x.shape
    rows = min(NORM_ROWS, m)
    return pl.pallas_call(
        _rmsnorm_kernel,
        grid=(m // rows,),
        in_specs=[pl.BlockSpec((rows, d), lambda i: (i, 0)),
                  pl.BlockSpec((1, d), lambda i: (0, 0))],
        out_specs=pl.BlockSpec((rows, d), lambda i: (i, 0)),
        out_shape=jax.ShapeDtypeStruct((m, d), BF16),
        compiler_params=_params("parallel"),
        name="rmsnorm_cast",
    )(x, g.reshape(1, d))


def _matmul_kernel(a_ref, b_ref, o_ref):
    o_ref[...] = jnp.dot(a_ref[...], b_ref[...], preferred_element_type=F32).astype(o_ref.dtype)


def _matmul(a, b, out_dtype, name):
    m, k = a.shape
    _, n = b.shape
    tm = min(MM_TM, m)
    tn = min(MM_TN, n)
    return pl.pallas_call(
        _matmul_kernel,
        grid=(m // tm, n // tn),
        in_specs=[pl.BlockSpec((tm, k), lambda i, j: (i, 0)),
                  pl.BlockSpec((k, tn), lambda i, j: (0, j))],
        out_specs=pl.BlockSpec((tm, tn), lambda i, j: (i, j)),
        out_shape=jax.ShapeDtypeStruct((m, n), out_dtype),
        compiler_params=_params("parallel", "parallel"),
        name=name,
    )(a, b)


def _in_proj_kernel(x_ref, g_ref, w_ref, wo_ref, wk_ref, proj_ref, wo_bf_ref, wk_bf_ref, u_scr, ss_scr):
    o = pl.program_id(0)
    j = pl.program_id(1)

    def normalise_next_tile():
        wo_bf_ref[...] = wo_ref[...].astype(BF16)
        wk_bf_ref[...] = wk_ref[...].astype(BF16)
        x = x_ref[...]
        sq = x * x
        part = sq[:, 0:LANES]
        for c in range(1, IN_KC // LANES):
            part = part + sq[:, c * LANES:(c + 1) * LANES]
        ss_old = ss_scr[...]
        ss = jnp.where(j == 0, part, jnp.where(j < IN_KCHUNKS, ss_old + part, ss_old))
        ss_scr[...] = ss
        r = lax.rsqrt(jnp.sum(ss, axis=-1, keepdims=True) * (1.0 / D_MODEL) + EPS)
        chunk = jnp.clip(j - IN_KCHUNKS, 0, IN_KCHUNKS - 1)
        u_scr[lax.rem(o, 2), chunk] = (x * r * g_ref[...]).astype(BF16)

    @pl.when(o == 0)
    def _():
        @pl.when(j == 0)
        def _():
            ss_scr[...] = jnp.zeros_like(ss_scr)

        normalise_next_tile()

    @pl.when(o > 0)
    def _():
        slot = lax.rem(o + 1, 2)
        acc = None
        for c in range(IN_KCHUNKS):
            d = jnp.dot(u_scr[slot, c], w_ref[c * IN_KC:(c + 1) * IN_KC, :].astype(BF16),
                        preferred_element_type=F32)
            acc = d if acc is None else acc + d
        proj_ref[...] = acc.astype(proj_ref.dtype)
        normalise_next_tile()


def _in_proj(x, pre_g, w_in, w_out, w_mem_kv):
    s_len = x.shape[0]
    ni, nj = s_len // IN_TM, IN_WIDTH // IN_TN
    n_steps = (ni + 1) * nj
    side_rows = 32
    n_side = D_MODEL // side_rows
    assert n_side <= n_steps and nj >= 2 * IN_KCHUNKS

    def x_tile(o, j):
        return (jnp.minimum(o, ni - 1), jnp.where(j < 2 * IN_KCHUNKS, lax.rem(j, IN_KCHUNKS), IN_KCHUNKS - 1))

    def side(o, j):
        return (jnp.minimum(o * nj + j, n_side - 1), 0)

    return pl.pallas_call(
        _in_proj_kernel,
        grid=(ni + 1, nj),
        in_specs=[pl.BlockSpec((IN_TM, IN_KC), x_tile),
                  pl.BlockSpec((1, IN_KC), lambda o, j: (0, x_tile(o, j)[1])),
                  pl.BlockSpec((D_MODEL, IN_TN), lambda o, j: (0, jnp.where(o > 0, j, 0))),
                  pl.BlockSpec((side_rows, D_MODEL), side),
                  pl.BlockSpec((side_rows, 2 * MEM_WIDTH), side)],
        out_specs=[pl.BlockSpec((IN_TM, IN_TN), lambda o, j: (jnp.maximum(o - 1, 0), jnp.where(o > 0, j, 0))),
                   pl.BlockSpec((side_rows, D_MODEL), side),
                   pl.BlockSpec((side_rows, 2 * MEM_WIDTH), side)],
        out_shape=[jax.ShapeDtypeStruct((s_len, IN_WIDTH), BF16),
                   jax.ShapeDtypeStruct((D_MODEL, D_MODEL), BF16),
                   jax.ShapeDtypeStruct((D_MODEL, 2 * MEM_WIDTH), BF16)],
        scratch_shapes=[pltpu.VMEM((2, IN_KCHUNKS, IN_TM, IN_KC), BF16),
                        pltpu.VMEM((IN_TM, LANES), F32)],
        compiler_params=_params("arbitrary", "arbitrary"),
        name="in_proj_pre_norm",
    )(x, pre_g.reshape(1, D_MODEL), w_in, w_out, w_mem_kv)


def _rope_tables(block_idx, freq, base_cos, base_sin, mult):
    a = (block_idx * BLOCK).astype(F32) * freq
    ca, sa = jnp.cos(a) * mult, jnp.sin(a) * mult
    c = ca * base_cos - sa * base_sin
    s = sa * base_cos + ca * base_sin
    lane = lax.broadcasted_iota(jnp.int32, (BLOCK, HEAD_DIM), 1)
    return c, jnp.where(lane < ROT_HALF, -s, 0.0), jnp.where(lane >= ROT_HALF, s, 0.0)


def _rope(t, tables):
    c, sin_lo, sin_hi = tables
    return (t * c + pltpu.roll(t, HEAD_DIM - ROT_HALF, axis=1) * sin_lo
            + pltpu.roll(t, ROT_HALF, axis=1) * sin_hi)


def _rope_keys(k_ref, tables):
    return jnp.concatenate(
        [_rope(k_ref[:, h * HEAD_DIM:(h + 1) * HEAD_DIM].astype(F32), tables).astype(BF16)
         for h in range(ATT_KV_HEADS)], axis=1)


def _attn_init(k0_ref, freq_ref, k_ring, bias_scr, cos_scr, sin_scr):
    freq = freq_ref[...]
    rows = ATT_GROUP * BLOCK
    ang = lax.broadcasted_iota(jnp.int32, (BLOCK, HEAD_DIM), 0).astype(F32) * freq
    cos_scr[...] = jnp.cos(ang)
    sin_scr[...] = jnp.sin(ang)
    key = lax.broadcasted_iota(jnp.int32, (BLOCK, rows), 0)
    q_local = lax.broadcasted_iota(jnp.int32, (BLOCK, rows), 1) & (BLOCK - 1)
    bias_scr[0] = jnp.where(key >= q_local, 0.0, MASK_BIAS)
    bias_scr[1] = jnp.where(key <= q_local, 0.0, MASK_BIAS)
    bias_scr[2] = jnp.full((BLOCK, rows), MASK_BIAS, F32)
    k_ring[0] = _rope_keys(k0_ref, _rope_tables(jnp.int32(0), freq, cos_scr[...], sin_scr[...], 1.0))
    k_ring[2] = jnp.zeros((BLOCK, KV_WIDTH), BF16)


class _AttnBlock:
    def __init__(self, n, nb, sink_ref, q_ref, kn_ref, vp_ref, vc_ref, vn_ref, freq_ref,
                 k_ring, bias_scr, cos_scr, sin_scr, y_scr):
        self.sink_ref, self.q_ref, self.v_refs, self.k_ring, self.y_scr = (
            sink_ref, q_ref, (vp_ref, vc_ref, vn_ref), k_ring, y_scr)
        freq = freq_ref[...]
        rows = ATT_GROUP * BLOCK
        self.log2e = math.log2(math.e)
        base_cos = cos_scr[...]
        base_sin = sin_scr[...]
        self.slots = (lax.rem(n + 2, 3), lax.rem(n, 3), lax.rem(n + 1, 3))
        k_ring[self.slots[2]] = _rope_keys(kn_ref, _rope_tables(n + 1, freq, base_cos, base_sin, 1.0))
        self.tab_q = _rope_tables(n, freq, base_cos, base_sin, HEAD_DIM ** -0.5 * self.log2e)
        self.bias_p = bias_scr[jnp.where(n > 0, 0, 2)]
        self.bias_n = bias_scr[jnp.where(n < nb - 1, 1, 2)]
        self.lane_group = lax.broadcasted_iota(jnp.int32, (1, rows), 1) // BLOCK
        self.first_row = lax.broadcasted_iota(jnp.int32, (BF16_SUBLANES, rows), 0) == 0
        self.ones = jnp.ones((3 * BLOCK + BF16_SUBLANES, HEAD_DIM), BF16)

    def scores(self, h):
        ks = slice(h * HEAD_DIM, (h + 1) * HEAD_DIM)
        k_all = jnp.concatenate([self.k_ring[s, :, ks] for s in self.slots], axis=0)
        q_all = jnp.concatenate(
            [_rope(self.q_ref[:, (h * ATT_GROUP + g) * HEAD_DIM:(h * ATT_GROUP + g + 1) * HEAD_DIM]
                   .astype(F32), self.tab_q).astype(BF16) for g in range(ATT_GROUP)], axis=0)
        return lax.dot_general(k_all, q_all, (((1,), (1,)), ((), ())), preferred_element_type=F32)

    def values(self, h, s):
        ks = slice(h * HEAD_DIM, (h + 1) * HEAD_DIM)
        sink = jnp.zeros((1, ATT_GROUP * BLOCK), F32)
        for g in range(ATT_GROUP):
            sink = jnp.where(self.lane_group == g, self.sink_ref[h * ATT_GROUP + g] * self.log2e, sink)
        s_p = s[:BLOCK] + self.bias_p
        s_c = s[BLOCK:2 * BLOCK]
        s_n = s[2 * BLOCK:] + self.bias_n
        m = jnp.max(jnp.maximum(jnp.maximum(s_p, s_c), s_n), axis=0, keepdims=True)
        m = jnp.maximum(m, sink)
        p_sink = jnp.where(self.first_row, jnp.exp2(sink - m), 0.0)
        p = jnp.concatenate([jnp.exp2(s_p - m), jnp.exp2(s_c - m), jnp.exp2(s_n - m), p_sink],
                            axis=0).astype(BF16)
        v_all = jnp.concatenate([r[:, ks] for r in self.v_refs]
                                + [jnp.zeros((BF16_SUBLANES, HEAD_DIM), BF16)], axis=0)
        v_aug = jnp.concatenate([v_all, self.ones], axis=1)
        o_aug = lax.dot_general(p, v_aug, (((0,), (0,)), ((), ())), preferred_element_type=F32)
        o = o_aug[:, :HEAD_DIM] / o_aug[:, HEAD_DIM:]
        for g in range(ATT_GROUP):
            hq = h * ATT_GROUP + g
            self.y_scr[:, hq * HEAD_DIM:(hq + 1) * HEAD_DIM] = o[g * BLOCK:(g + 1) * BLOCK]

    def result(self, z0_ref, z1_ref, g_ref):
        z = jnp.concatenate([z0_ref[...], z1_ref[...]], axis=1).astype(F32)
        return _norm_gate(self.y_scr[...], g_ref[...], z)


def _lru_direction(d, reverse, chunk, n_chunks, x_ref, xp_ref, xn_ref, cw_ref, cb_ref, wg_ref,
                   bg_ref, lam_ref, out_ref, xe_scr, carry_scr):
    t = LRU_CHUNK
    halo = BF16_SUBLANES
    for blk in range(LRU_BLOCKS):
        cs = slice(blk * LRU_BLOCK_DIM, (blk + 1) * LRU_BLOCK_DIM)
        xe_scr[d, blk, 0:halo, :] = jnp.where(chunk > 0, xp_ref[:, cs].astype(F32), 0.0)
        xe_scr[d, blk, halo:halo + t, :] = x_ref[:, cs].astype(F32)
        xe_scr[d, blk, halo + t:, :] = jnp.where(chunk < n_chunks - 1, xn_ref[:, cs].astype(F32), 0.0)

    neg_lam = -lam_ref[d:d + 1, :]
    softplus = jnp.maximum(neg_lam, 0.0) + jnp.log(1.0 + jnp.exp(-jnp.abs(neg_lam)))
    decay = (-LRU_C * math.log2(math.e)) * softplus
    row0 = [8 * sum(LRU_SUBCHUNK_SEGS[:k]) for k in range(len(LRU_SUBCHUNK_SEGS))]
    subchunks = list(zip(row0, LRU_SUBCHUNK_SEGS))
    for blk in range(LRU_BLOCKS):
        cs = slice(blk * LRU_BLOCK_DIM, (blk + 1) * LRU_BLOCK_DIM)
        taps = [cw_ref[j:j + 1, cs] for j in range(CONV_WIDTH)]
        cin = carry_scr[d:d + 1, cs]
        for r0, seg in (reversed(subchunks) if reverse else subchunks):
            order = range(seg - 1, -1, -1) if reverse else range(seg)
            tiles = []
            for m in range(seg):
                acc = cb_ref[:, cs]
                for j in range(CONV_WIDTH):
                    start = halo + r0 + m + j - CONV_LEFT
                    acc = acc + taps[j] * xe_scr[d, blk, pl.ds(start, 8, stride=seg), :]
                tiles.append(acc)
            xc = jnp.concatenate(tiles, axis=0)
            gates = jnp.dot(xc.astype(BF16), wg_ref[d, blk], preferred_element_type=F32)
            r = 1.0 / (1.0 + jnp.exp(gates[:, :LRU_BLOCK_DIM] + bg_ref[d, 0:1, cs]))
            i = 1.0 / (1.0 + jnp.exp(gates[:, LRU_BLOCK_DIM:] + bg_ref[d, 1:2, cs]))
            a = jnp.exp2(decay[:, cs] * r)
            w = 1.0 - a * a
            b = w * lax.rsqrt(jnp.maximum(w, F32_TINY)) * i * xc
            tile = lambda v, m: v[8 * m:8 * m + 8]
            hh = jnp.zeros((8, LRU_BLOCK_DIM), F32)
            pp = jnp.ones((8, LRU_BLOCK_DIM), F32)
            for m in order:
                hh = tile(a, m) * hh + tile(b, m)
                pp = tile(a, m) * pp
            entering = [None] * 8
            for s in (range(7, -1, -1) if reverse else range(8)):
                entering[s] = cin
                cin = pp[s:s + 1, :] * cin + hh[s:s + 1, :]
            h = jnp.concatenate(entering, axis=0)
            for m in order:
                h = tile(a, m) * h + tile(b, m)
                out_ref[blk, pl.ds(r0 + m, 8, stride=seg), :] = h
        carry_scr[d:d + 1, cs] = cin


def _lru_kernel(xf_ref, xfp_ref, xfn_ref, xb_ref, xbp_ref, xbn_ref, cw_ref, cb_ref, wg_ref, bg_ref,
                lam_ref, hf_ref, hb_ref, xe_scr, carry_scr):
    i = pl.program_id(0)
    nc = pl.num_programs(0)

    @pl.when(i == 0)
    def _():
        carry_scr[...] = jnp.zeros_like(carry_scr)

    shared = (cw_ref, cb_ref, wg_ref, bg_ref, lam_ref)
    scr = (xe_scr, carry_scr)
    _lru_direction(0, False, i, nc, xf_ref, xfp_ref, xfn_ref, *shared, hf_ref, *scr)
    _lru_direction(1, True, nc - 1 - i, nc, xb_ref, xbp_ref, xbn_ref, *shared, hb_ref, *scr)


def _lru(proj, conv_w, conv_b, w_a, b_a, w_x, b_x, lam):
    s_len = proj.shape[0]
    nc = s_len // LRU_CHUNK
    halo_per_chunk = LRU_CHUNK // BF16_SUBLANES
    n_halo = s_len // BF16_SUBLANES
    wg = (-jnp.concatenate([w_a, w_x], axis=-1)).astype(BF16)
    bg = -jnp.stack([b_a, b_x], axis=1)

    def fwd(i):
        return i

    def bwd(i):
        return nc - 1 - i

    def specs(chunk_of):
        return [
            pl.BlockSpec((LRU_CHUNK, LRU_WIDTH), lambda i: (chunk_of(i), COL_XLRU)),
            pl.BlockSpec((BF16_SUBLANES, LRU_WIDTH),
                         lambda i: (jnp.maximum(chunk_of(i) * halo_per_chunk - 1, 0), COL_XLRU)),
            pl.BlockSpec((BF16_SUBLANES, LRU_WIDTH),
                         lambda i: (jnp.minimum((chunk_of(i) + 1) * halo_per_chunk, n_halo - 1), COL_XLRU)),
        ]

    def whole(shape):
        return pl.BlockSpec(shape, lambda i: (0,) * len(shape))

    return pl.pallas_call(
        _lru_kernel,
        grid=(nc,),
        in_specs=specs(fwd) + specs(bwd) + [
            whole((CONV_WIDTH, LRU_WIDTH)), whole((1, LRU_WIDTH)),
            whole((2, LRU_BLOCKS, LRU_BLOCK_DIM, 2 * LRU_BLOCK_DIM)),
            whole((2, 2, LRU_WIDTH)), whole((2, LRU_WIDTH))],
        out_specs=[pl.BlockSpec((LRU_BLOCKS, LRU_CHUNK, LRU_BLOCK_DIM), lambda i: (0, i, 0)),
                   pl.BlockSpec((LRU_BLOCKS, LRU_CHUNK, LRU_BLOCK_DIM), lambda i: (0, nc - 1 - i, 0))],
        out_shape=[jax.ShapeDtypeStruct((LRU_BLOCKS, s_len, LRU_BLOCK_DIM), F32),
                   jax.ShapeDtypeStruct((LRU_BLOCKS, s_len, LRU_BLOCK_DIM), F32)],
        scratch_shapes=[pltpu.VMEM((2, LRU_BLOCKS, LRU_CHUNK + 2 * BF16_SUBLANES, LRU_BLOCK_DIM), F32),
                        pltpu.VMEM((2, LRU_WIDTH), F32)],
        compiler_params=_params("arbitrary"),
        name="rglru",
    )(proj, proj, proj, proj, proj, proj, conv_w, conv_b.reshape(1, LRU_WIDTH), wg, bg, lam)


def _mem_scores(h, q_ref, kv_ref):
    cs = slice(h * MEM_HEAD_DIM, (h + 1) * MEM_HEAD_DIM)
    return lax.dot_general(q_ref[:, cs], kv_ref[:, cs], (((1,), (1,)), ((), ())),
                           preferred_element_type=F32)


def _mem_values(h, s, kv_ref, y_scr):
    cs = slice(h * MEM_HEAD_DIM, (h + 1) * MEM_HEAD_DIM)
    vs = slice(MEM_WIDTH + h * MEM_HEAD_DIM, MEM_WIDTH + (h + 1) * MEM_HEAD_DIM)
    s = s * MEM_HEAD_DIM ** -0.5
    m = jnp.max(s, axis=-1, keepdims=True)
    p = jnp.exp(s - m)
    denom = jnp.sum(p, axis=-1, keepdims=True)
    y_scr[:, cs] = jnp.dot(p.astype(BF16), kv_ref[:, vs], preferred_element_type=F32) / denom


def _lru_finish(hf_ref, hb_ref, zl_ref, gl_ref):
    y_lru = jnp.concatenate([hf_ref[blk] + hb_ref[blk] for blk in range(LRU_BLOCKS)], axis=1)
    return _norm_gate(y_lru, gl_ref[...], zl_ref[...].astype(F32))


def _tail_kernel(sink_ref,
                 q_ref, k0_ref, kn_ref, vp_ref, vc_ref, vn_ref, z0_ref, z1_ref, freq_ref, ga_ref,
                 qm_ref, kv_ref, zm_ref, gm_ref, hf_ref, hb_ref, zl_ref, gl_ref,
                 w_ref, x_ref, gp_ref, o_ref,
                 lhs_scr, acc_scr, ss_scr, k_ring, bias_scr, cos_scr, sin_scr, ya_scr, ym_scr):
    o = pl.program_id(0)
    j = pl.program_id(1)
    nj = pl.num_programs(1)
    n_row_tiles = pl.num_programs(0) - 2
    nb = n_row_tiles * nj

    n_chunks = ATT_KV_HEADS
    chunk = TAIL_TN // n_chunks

    def stage_b_chunk(c, part):
        slot = lax.rem(o + 1, 2)
        cols = slice(c * chunk, (c + 1) * chunk)
        acc = jnp.dot(lhs_scr[slot], w_ref[:, cols], preferred_element_type=F32)
        acc_scr[slot, j, :, cols] = acc.astype(acc_scr.dtype)
        sq = acc * acc
        for k in range(chunk // LANES):
            piece = sq[:, k * LANES:(k + 1) * LANES]
            part = piece if part is None else part + piece
        return part

    def stages_ab(with_a, with_b):
        part = None
        if with_b:
            part = stage_b_chunk(0, part)
        if with_a:
            n = o * nj + j
            attn = _AttnBlock(n, nb, sink_ref, q_ref, kn_ref, vp_ref, vc_ref, vn_ref, freq_ref,
                              k_ring, bias_scr, cos_scr, sin_scr, ya_scr)
            s_att = [attn.scores(h) for h in range(ATT_KV_HEADS)]
            s_mem = [_mem_scores(h, qm_ref, kv_ref) for h in range(MEM_HEADS)]
        for pair in range(2):
            if with_b:
                part = stage_b_chunk(1 + pair, part)
            if with_a:
                for h in (2 * pair, 2 * pair + 1):
                    attn.values(h, s_att[h])
                    _mem_values(h, s_mem[h], kv_ref, ym_scr)
        if with_b:
            part = stage_b_chunk(3, part)
            slot = lax.rem(o + 1, 2)
            ss_scr[slot] = jnp.where(j == 0, part, ss_scr[slot] + part)
        if with_a:
            rows = pl.ds(pl.multiple_of(j * BLOCK, BLOCK), BLOCK)
            slot = lax.rem(o, 2)
            lhs_scr[slot, rows, 0:ATT_WIDTH] = attn.result(z0_ref, z1_ref, ga_ref).astype(BF16)
            lhs_scr[slot, rows, ATT_WIDTH:ATT_WIDTH + LRU_WIDTH] = _lru_finish(
                hf_ref, hb_ref, zl_ref, gl_ref).astype(BF16)
            lhs_scr[slot, rows, ATT_WIDTH + LRU_WIDTH:] = _norm_gate(
                ym_scr[...], gm_ref[...], zm_ref[...].astype(F32)).astype(BF16)

    def stage_c():
        slot = lax.rem(o, 2)
        ms = jnp.sum(ss_scr[slot], axis=-1, keepdims=True) * (1.0 / D_MODEL)
        o_ref[...] = x_ref[...] + acc_scr[slot, j].astype(F32) * lax.rsqrt(ms + EPS) * gp_ref[...]

    @pl.when(o == 0)
    def _():
        @pl.when(j == 0)
        def _():
            ss_scr[...] = jnp.zeros_like(ss_scr)
            _attn_init(k0_ref, freq_ref, k_ring, bias_scr, cos_scr, sin_scr)

        stages_ab(True, False)

    @pl.when(o == 1)
    def _():
        stages_ab(True, True)

    @pl.when((o >= 2) & (o < n_row_tiles))
    def _():
        stage_c()
        stages_ab(True, True)

    @pl.when(o == n_row_tiles)
    def _():
        stage_c()
        stages_ab(False, True)

    @pl.when(o == n_row_tiles + 1)
    def _():
        stage_c()


def _tail(proj, kv, h_fwd, h_bwd, sink, g_att, g_mem, g_lru, w_out, x, post_g):
    s_len = proj.shape[0]
    tm, tn = TAIL_TM, TAIL_TN
    ni, nj = s_len // tm, D_MODEL // tn
    nb = s_len // BLOCK
    assert tm == nj * BLOCK
    inv_freq = jnp.power(jnp.float32(ROPE_THETA),
                         -jnp.arange(ROT_HALF, dtype=F32) * 2.0 / ROT_DIM)
    freq = jnp.concatenate([inv_freq, inv_freq, jnp.zeros((HEAD_DIM - ROT_DIM,), F32)]).reshape(1, HEAD_DIM)

    def block_of(o, j):
        return jnp.where(o < ni, o * nj + j, nb - 1)

    def rows_spec(width, col, shift=0):
        return pl.BlockSpec((BLOCK, width),
                            lambda o, j, sink: (jnp.clip(block_of(o, j) + shift, 0, nb - 1), col))

    def const_spec(shape):
        return pl.BlockSpec(shape, lambda o, j, sink: (0,) * len(shape))

    slab_spec = pl.BlockSpec((LRU_BLOCKS, BLOCK, LRU_BLOCK_DIM), lambda o, j, sink: (0, block_of(o, j), 0))

    def w_col(o, j, sink):
        return (0, jnp.where(o == 0, 0, jnp.where(o <= ni, j, nj - 1)))

    def finished_tile(o, j, sink):
        return (jnp.maximum(o - 2, 0), jnp.where(o >= 2, j, 0))

    grid_spec = pltpu.PrefetchScalarGridSpec(
        num_scalar_prefetch=1,
        grid=(ni + 2, nj),
        in_specs=[rows_spec(ATT_WIDTH, 0),
                  pl.BlockSpec((BLOCK, KV_WIDTH), lambda o, j, sink: (0, COL_K512)),
                  rows_spec(KV_WIDTH, COL_K512, 1),
                  rows_spec(KV_WIDTH, COL_V512, -1), rows_spec(KV_WIDTH, COL_V512), rows_spec(KV_WIDTH, COL_V512, 1),
                  rows_spec(1024, COL_ZATT), rows_spec(1024, COL_ZATT + 1),
                  const_spec((1, HEAD_DIM)), const_spec((1, ATT_WIDTH)),
                  rows_spec(MEM_WIDTH, COL_QMEM), const_spec((N_MEM, 2 * MEM_WIDTH)),
                  rows_spec(MEM_WIDTH, COL_ZMEM), const_spec((1, MEM_WIDTH)),
                  slab_spec, slab_spec, rows_spec(LRU_WIDTH, COL_ZLRU), const_spec((1, LRU_WIDTH)),
                  pl.BlockSpec((D_MODEL, tn), w_col),
                  pl.BlockSpec((tm, tn), finished_tile),
                  pl.BlockSpec((1, tn), lambda o, j, sink: (0, j))],
        out_specs=pl.BlockSpec((tm, tn), finished_tile),
        scratch_shapes=[pltpu.VMEM((2, tm, D_MODEL), BF16),
                        pltpu.VMEM((2, nj, tm, tn), BF16),
                        pltpu.VMEM((2, tm, LANES), F32),
                        pltpu.VMEM((3, BLOCK, KV_WIDTH), BF16),
                        pltpu.VMEM((3, BLOCK, ATT_GROUP * BLOCK), F32),
                        pltpu.VMEM((BLOCK, HEAD_DIM), F32),
                        pltpu.VMEM((BLOCK, HEAD_DIM), F32),
                        pltpu.VMEM((BLOCK, ATT_WIDTH), F32),
                        pltpu.VMEM((BLOCK, MEM_WIDTH), F32)],
    )
    return pl.pallas_call(
        _tail_kernel,
        grid_spec=grid_spec,
        out_shape=jax.ShapeDtypeStruct((s_len, D_MODEL), F32),
        compiler_params=_params("arbitrary", "arbitrary"),
        name="mixers_out_proj_post_norm",
    )(sink, proj, proj, proj, proj, proj, proj, proj, proj, freq, g_att.reshape(1, ATT_WIDTH),
      proj, kv, proj, g_mem.reshape(1, MEM_WIDTH), h_fwd, h_bwd, proj, g_lru.reshape(1, LRU_WIDTH),
      w_out, x, post_g.reshape(1, D_MODEL))


def _layer(h, mem, pre_g, w_in, sink, conv_w, conv_b, w_a, b_a, w_x, b_x, lam, mem_g, w_mem_kv,
           g_att, g_lru, g_mem, w_out, post_g):
    proj, w_out_bf, w_mem_kv_bf = _in_proj(h, pre_g, w_in, w_out, w_mem_kv)
    kv = _matmul(_rmsnorm_cast(mem, mem_g), w_mem_kv_bf, BF16, "mem_kv_proj")
    h_fwd, h_bwd = _lru(proj, conv_w, conv_b, w_a, b_a, w_x, b_x, lam)
    return _tail(proj, kv, h_fwd, h_bwd, sink, g_att, g_mem, g_lru, w_out_bf, h, post_g)


def kernel(x, mem, pre_norm_gain, w_in, att_sink, conv_w, conv_b, lru_w_a, lru_b_a, lru_w_x, lru_b_x,
           lru_lambda, mem_norm_gain, w_mem_kv, att_out_gain, lru_out_gain, mem_out_gain, w_out,
           post_norm_gain):
    batch = x.shape[0]
    depth = w_in.shape[0]
    outs = []
    for b in range(batch):
        h = x[b]
        for l in range(depth):
            h = _layer(h, mem[b], pre_norm_gain[l], w_in[l], att_sink[l], conv_w[l], conv_b[l],
                       lru_w_a[l], lru_b_a[l], lru_w_x[l], lru_b_x[l], lru_lambda[l],
                       mem_norm_gain[l], w_mem_kv[l], att_out_gain[l], lru_out_gain[l],
                       mem_out_gain[l], w_out[l], post_norm_gain[l])
        outs.append(h)
    return jnp.stack(outs, axis=0)
```

```python
import functools
import math

import jax
import jax.numpy as jnp
from jax import lax
from jax.experimental import pallas as pl
from jax.experimental.pallas import tpu as pltpu

F32 = jnp.float32
BF16 = jnp.bfloat16

D_MODEL = 4096
SEQ = 8192
N_MEM = 256
HEAD_DIM = 128
ATT_WIDTH = 2048
ATT_HEADS = 16
ATT_KV_HEADS = 4
ATT_GROUP = ATT_HEADS // ATT_KV_HEADS
KV_WIDTH = ATT_KV_HEADS * HEAD_DIM
BLOCK = 128
ROPE_THETA = 500000.0
ROT_DIM = HEAD_DIM // 4
ROT_HALF = ROT_DIM // 2
LRU_WIDTH = 1024
LRU_BLOCKS = 8
LRU_BLOCK_DIM = LRU_WIDTH // LRU_BLOCKS
CONV_WIDTH = 4
CONV_LEFT = 2
LRU_C = 8.0
MEM_HEADS = 4
MEM_WIDTH = 1024
MEM_HEAD_DIM = MEM_WIDTH // MEM_HEADS
EPS = 1e-6
IN_WIDTH = 9216
MASK_BIAS = -1e30
F32_TINY = float(jnp.finfo(jnp.float32).tiny)

COL_XLRU = 3
COL_QMEM = 4
COL_ZATT = 5
COL_ZLRU = 7
COL_ZMEM = 8
COL_K512 = 4
COL_V512 = 5

VMEM_LIMIT_BYTES = 60000 * 1024
BF16_SUBLANES = 16

NORM_ROWS = 256
MM_TM = 1024
MM_TN = 1024
IN_TM = 1024
IN_TN = 512
IN_KC = 512
IN_KCHUNKS = D_MODEL // IN_KC
TAIL_TM = 512
TAIL_TN = 1024
LANES = 128
LRU_CHUNK = 512
LRU_SUBCHUNK_SEGS = (36, 28)
assert 8 * sum(LRU_SUBCHUNK_SEGS) == LRU_CHUNK and all(s % 8 == 4 for s in LRU_SUBCHUNK_SEGS)
MIX_ROWS = 512


def _params(*sem):
    return pltpu.CompilerParams(dimension_semantics=sem, vmem_limit_bytes=VMEM_LIMIT_BYTES)


def _sigmoid(t):
    return 1.0 / (1.0 + jnp.exp(-t))


def _norm_gate(y, g, z):
    ms = jnp.mean(y * y, axis=-1, keepdims=True)
    return y * lax.rsqrt(ms + EPS) * g * (z * _sigmoid(z))


def _rmsnorm_kernel(x_ref, g_ref, o_ref):
    x = x_ref[...]
    ms = jnp.mean(x * x, axis=-1, keepdims=True)
    o_ref[...] = (x * lax.rsqrt(ms + EPS) * g_ref[...]).astype(o_ref.dtype)


def _rmsnorm_cast(x, g):
    m, d = x.shape
    rows = min(NORM_ROWS, m)
    return pl.pallas_call(
        _rmsnorm_kernel,
        grid=(m // rows,),
        in_specs=[pl.BlockSpec((rows, d), lambda i: (i, 0)),
                  pl.BlockSpec((1, d), lambda i: (0, 0))],
        out_specs=pl.BlockSpec((rows, d), lambda i: (i, 0)),
        out_shape=jax.ShapeDtypeStruct((m, d), BF16),
        compiler_params=_params("parallel"),
        name="rmsnorm_cast",
    )(x, g.reshape(1, d))


def _matmul_kernel(a_ref, b_ref, o_ref):
    o_ref[...] = jnp.dot(a_ref[...], b_ref[...], preferred_element_type=F32).astype(o_ref.dtype)


def _matmul(a, b, out_dtype, name):
    m, k = a.shape
    _, n = b.shape
    tm = min(MM_TM, m)
    tn = min(MM_TN, n)
    return pl.pallas_call(
        _matmul_kernel,
        grid=(m // tm, n // tn),
        in_specs=[pl.BlockSpec((tm, k), lambda i, j: (i, 0)),
                  pl.BlockSpec((k, tn), lambda i, j: (0, j))],
        out_specs=pl.BlockSpec((tm, tn), lambda i, j: (i, j)),
        out_shape=jax.ShapeDtypeStruct((m, n), out_dtype),
        compiler_params=_params("parallel", "parallel"),
        name=name,
    )(a, b)


def _in_proj_kernel(x_ref, g_ref, w_ref, wo_ref, wk_ref, proj_ref, wo_bf_ref, wk_bf_ref, u_scr, ss_scr):
    o = pl.program_id(0)
    j = pl.program_id(1)

    def normalise_next_tile():
        wo_bf_ref[...] = wo_ref[...].astype(BF16)
        wk_bf_ref[...] = wk_ref[...].astype(BF16)
        x = x_ref[...]
        sq = x * x
        part = sq[:, 0:LANES]
        for c in range(1, IN_KC // LANES):
            part = part + sq[:, c * LANES:(c + 1) * LANES]
        ss_old = ss_scr[...]
        ss = jnp.where(j == 0, part, jnp.where(j < IN_KCHUNKS, ss_old + part, ss_old))
        ss_scr[...] = ss
        r = lax.rsqrt(jnp.sum(ss, axis=-1, keepdims=True) * (1.0 / D_MODEL) + EPS)
        chunk = jnp.clip(j - IN_KCHUNKS, 0, IN_KCHUNKS - 1)
        u_scr[lax.rem(o, 2), chunk] = (x * r * g_ref[...]).astype(BF16)

    @pl.when(o == 0)
    def _():
        @pl.when(j == 0)
        def _():
            ss_scr[...] = jnp.zeros_like(ss_scr)

        normalise_next_tile()

    @pl.when(o > 0)
    def _():
        slot = lax.rem(o + 1, 2)
        acc = None
        for c in range(IN_KCHUNKS):
            d = jnp.dot(u_scr[slot, c], w_ref[c * IN_KC:(c + 1) * IN_KC, :].astype(BF16),
                        preferred_element_type=F32)
            acc = d if acc is None else acc + d
        proj_ref[...] = acc.astype(proj_ref.dtype)
        normalise_next_tile()


def _in_proj(x, pre_g, w_in, w_out, w_mem_kv):
    s_len = x.shape[0]
    ni, nj = s_len // IN_TM, IN_WIDTH // IN_TN
    n_steps = (ni + 1) * nj
    side_rows = 32
    n_side = D_MODEL // side_rows
    assert n_side <= n_steps and nj >= 2 * IN_KCHUNKS

    def x_tile(o, j):
        return (jnp.minimum(o, ni - 1), jnp.where(j < 2 * IN_KCHUNKS, lax.rem(j, IN_KCHUNKS), IN_KCHUNKS - 1))

    def side(o, j):
        return (jnp.minimum(o * nj + j, n_side - 1), 0)

    return pl.pallas_call(
        _in_proj_kernel,
        grid=(ni + 1, nj),
        in_specs=[pl.BlockSpec((IN_TM, IN_KC), x_tile),
                  pl.BlockSpec((1, IN_KC), lambda o, j: (0, x_tile(o, j)[1])),
                  pl.BlockSpec((D_MODEL, IN_TN), lambda o, j: (0, jnp.where(o > 0, j, 0))),
                  pl.BlockSpec((side_rows, D_MODEL), side),
                  pl.BlockSpec((side_rows, 2 * MEM_WIDTH), side)],
        out_specs=[pl.BlockSpec((IN_TM, IN_TN), lambda o, j: (jnp.maximum(o - 1, 0), jnp.where(o > 0, j, 0))),
                   pl.BlockSpec((side_rows, D_MODEL), side),
                   pl.BlockSpec((side_rows, 2 * MEM_WIDTH), side)],
        out_shape=[jax.ShapeDtypeStruct((s_len, IN_WIDTH), BF16),
                   jax.ShapeDtypeStruct((D_MODEL, D_MODEL), BF16),
                   jax.ShapeDtypeStruct((D_MODEL, 2 * MEM_WIDTH), BF16)],
        scratch_shapes=[pltpu.VMEM((2, IN_KCHUNKS, IN_TM, IN_KC), BF16),
                        pltpu.VMEM((IN_TM, LANES), F32)],
        compiler_params=_params("arbitrary", "arbitrary"),
        name="in_proj_pre_norm",
    )(x, pre_g.reshape(1, D_MODEL), w_in, w_out, w_mem_kv)


def _rope_tables(block_idx, freq, base_cos, base_sin, mult):
    a = (block_idx * BLOCK).astype(F32) * freq
    ca, sa = jnp.cos(a) * mult, jnp.sin(a) * mult
    c = ca * base_cos - sa * base_sin
    s = sa * base_cos + ca * base_sin
    lane = lax.broadcasted_iota(jnp.int32, (BLOCK, HEAD_DIM), 1)
    return c, jnp.where(lane < ROT_HALF, -s, 0.0), jnp.where(lane >= ROT_HALF, s, 0.0)


def _rope(t, tables):
    c, sin_lo, sin_hi = tables
    return (t * c + pltpu.roll(t, HEAD_DIM - ROT_HALF, axis=1) * sin_lo
            + pltpu.roll(t, ROT_HALF, axis=1) * sin_hi)


def _rope_keys(k_ref, tables):
    return jnp.concatenate(
        [_rope(k_ref[:, h * HEAD_DIM:(h + 1) * HEAD_DIM].astype(F32), tables).astype(BF16)
         for h in range(ATT_KV_HEADS)], axis=1)


def _attn_init(k0_ref, v0_ref, freq_ref, k_ring, v_ring, bias_scr, cos_scr, sin_scr):
    freq = freq_ref[...]
    rows = ATT_GROUP * BLOCK
    ang = lax.broadcasted_iota(jnp.int32, (BLOCK, HEAD_DIM), 0).astype(F32) * freq
    cos_scr[...] = jnp.cos(ang)
    sin_scr[...] = jnp.sin(ang)
    key = lax.broadcasted_iota(jnp.int32, (BLOCK, rows), 0)
    q_local = lax.broadcasted_iota(jnp.int32, (BLOCK, rows), 1) & (BLOCK - 1)
    bias_scr[0] = jnp.where(key >= q_local, 0.0, MASK_BIAS)
    bias_scr[1] = jnp.where(key <= q_local, 0.0, MASK_BIAS)
    bias_scr[2] = jnp.full((BLOCK, rows), MASK_BIAS, F32)
    k_ring[0] = _rope_keys(k0_ref, _rope_tables(jnp.int32(0), freq, cos_scr[...], sin_scr[...], 1.0))
    k_ring[2] = jnp.zeros((BLOCK, KV_WIDTH), BF16)
    v_ring[0] = v0_ref[...]
    v_ring[2] = jnp.zeros((BLOCK, KV_WIDTH), BF16)


class _AttnBlock:
    def __init__(self, n, nb, sink_ref, q_ref, kn_ref, vn_ref, freq_ref,
                 k_ring, v_ring, bias_scr, cos_scr, sin_scr, y_scr):
        self.sink_ref, self.q_ref, self.k_ring, self.v_ring, self.y_scr = (
            sink_ref, q_ref, k_ring, v_ring, y_scr)
        freq = freq_ref[...]
        rows = ATT_GROUP * BLOCK
        self.log2e = math.log2(math.e)
        base_cos = cos_scr[...]
        base_sin = sin_scr[...]
        self.slots = (lax.rem(n + 2, 3), lax.rem(n, 3), lax.rem(n + 1, 3))
        k_ring[self.slots[2]] = _rope_keys(kn_ref, _rope_tables(n + 1, freq, base_cos, base_sin, 1.0))
        v_ring[self.slots[2]] = vn_ref[...]
        self.tab_q = _rope_tables(n, freq, base_cos, base_sin, HEAD_DIM ** -0.5 * self.log2e)
        self.bias_p = bias_scr[jnp.where(n > 0, 0, 2)]
        self.bias_n = bias_scr[jnp.where(n < nb - 1, 1, 2)]
        self.lane_group = lax.broadcasted_iota(jnp.int32, (1, rows), 1) // BLOCK
        self.first_row = lax.broadcasted_iota(jnp.int32, (BF16_SUBLANES, rows), 0) == 0
        self.ones = jnp.ones((3 * BLOCK + BF16_SUBLANES, HEAD_DIM), BF16)

    def scores(self, h):
        ks = slice(h * HEAD_DIM, (h + 1) * HEAD_DIM)
        k_all = jnp.concatenate([self.k_ring[s, :, ks] for s in self.slots], axis=0)
        q_all = jnp.concatenate(
            [_rope(self.q_ref[:, (h * ATT_GROUP + g) * HEAD_DIM:(h * ATT_GROUP + g + 1) * HEAD_DIM]
                   .astype(F32), self.tab_q).astype(BF16) for g in range(ATT_GROUP)], axis=0)
        return lax.dot_general(k_all, q_all, (((1,), (1,)), ((), ())), preferred_element_type=F32)

    def values(self, h, s):
        ks = slice(h * HEAD_DIM, (h + 1) * HEAD_DIM)
        sink = jnp.zeros((1, ATT_GROUP * BLOCK), F32)
        for g in range(ATT_GROUP):
            sink = jnp.where(self.lane_group == g, self.sink_ref[h * ATT_GROUP + g] * self.log2e, sink)
        s_p = s[:BLOCK] + self.bias_p
        s_c = s[BLOCK:2 * BLOCK]
        s_n = s[2 * BLOCK:] + self.bias_n
        m = jnp.max(jnp.maximum(jnp.maximum(s_p, s_c), s_n), axis=0, keepdims=True)
        m = jnp.maximum(m, sink)
        p_sink = jnp.where(self.first_row, jnp.exp2(sink - m), 0.0)
        p = jnp.concatenate([jnp.exp2(s_p - m), jnp.exp2(s_c - m), jnp.exp2(s_n - m), p_sink],
                            axis=0).astype(BF16)
        v_all = jnp.concatenate([self.v_ring[s, :, ks] for s in self.slots]
                                + [jnp.zeros((BF16_SUBLANES, HEAD_DIM), BF16)], axis=0)
        v_aug = jnp.concatenate([v_all, self.ones], axis=1)
        o_aug = lax.dot_general(p, v_aug, (((0,), (0,)), ((), ())), preferred_element_type=F32)
        o = o_aug[:, :HEAD_DIM] / o_aug[:, HEAD_DIM:]
        for g in range(ATT_GROUP):
            hq = h * ATT_GROUP + g
            self.y_scr[:, hq * HEAD_DIM:(hq + 1) * HEAD_DIM] = o[g * BLOCK:(g + 1) * BLOCK]

    def result(self, z0_ref, z1_ref, g_ref):
        z = jnp.concatenate([z0_ref[...], z1_ref[...]], axis=1).astype(F32)
        return _norm_gate(self.y_scr[...], g_ref[...], z)


def _lru_direction(d, reverse, chunk, n_chunks, x_ref, xp_ref, xn_ref, cw_ref, cb_ref, wg_ref,
                   bg_ref, lam_ref, out_ref, xe_scr, carry_scr):
    t = LRU_CHUNK
    halo = BF16_SUBLANES
    for blk in range(LRU_BLOCKS):
        cs = slice(blk * LRU_BLOCK_DIM, (blk + 1) * LRU_BLOCK_DIM)
        xe_scr[d, blk, 0:halo, :] = jnp.where(chunk > 0, xp_ref[:, cs].astype(F32), 0.0)
        xe_scr[d, blk, halo:halo + t, :] = x_ref[:, cs].astype(F32)
        xe_scr[d, blk, halo + t:, :] = jnp.where(chunk < n_chunks - 1, xn_ref[:, cs].astype(F32), 0.0)

    neg_lam = -lam_ref[d:d + 1, :]
    softplus = jnp.maximum(neg_lam, 0.0) + jnp.log(1.0 + jnp.exp(-jnp.abs(neg_lam)))
    decay = (-LRU_C * math.log2(math.e)) * softplus
    row0 = [8 * sum(LRU_SUBCHUNK_SEGS[:k]) for k in range(len(LRU_SUBCHUNK_SEGS))]
    subchunks = list(zip(row0, LRU_SUBCHUNK_SEGS))
    for blk in range(LRU_BLOCKS):
        cs = slice(blk * LRU_BLOCK_DIM, (blk + 1) * LRU_BLOCK_DIM)
        taps = [cw_ref[j:j + 1, cs] for j in range(CONV_WIDTH)]
        cin = carry_scr[d:d + 1, cs]
        for r0, seg in (reversed(subchunks) if reverse else subchunks):
            order = range(seg - 1, -1, -1) if reverse else range(seg)
            tiles = []
            for m in range(seg):
                acc = cb_ref[:, cs]
                for j in range(CONV_WIDTH):
                    start = halo + r0 + m + j - CONV_LEFT
                    acc = acc + taps[j] * xe_scr[d, blk, pl.ds(start, 8, stride=seg), :]
                tiles.append(acc)
            xc = jnp.concatenate(tiles, axis=0)
            gates = jnp.dot(xc.astype(BF16), wg_ref[d, blk], preferred_element_type=F32)
            r = 1.0 / (1.0 + jnp.exp(gates[:, :LRU_BLOCK_DIM] + bg_ref[d, 0:1, cs]))
            i = 1.0 / (1.0 + jnp.exp(gates[:, LRU_BLOCK_DIM:] + bg_ref[d, 1:2, cs]))
            a = jnp.exp2(decay[:, cs] * r)
            w = 1.0 - a * a
            b = w * lax.rsqrt(jnp.maximum(w, F32_TINY)) * i * xc
            tile = lambda v, m: v[8 * m:8 * m + 8]
            hh = jnp.zeros((8, LRU_BLOCK_DIM), F32)
            pp = jnp.ones((8, LRU_BLOCK_DIM), F32)
            for m in order:
                hh = tile(a, m) * hh + tile(b, m)
                pp = tile(a, m) * pp
            entering = [None] * 8
            for s in (range(7, -1, -1) if reverse else range(8)):
                entering[s] = cin
                cin = pp[s:s + 1, :] * cin + hh[s:s + 1, :]
            h = jnp.concatenate(entering, axis=0)
            for m in order:
                h = tile(a, m) * h + tile(b, m)
                out_ref[blk, pl.ds(r0 + m, 8, stride=seg), :] = h
        carry_scr[d:d + 1, cs] = cin


def _lru_kernel(xf_ref, xfp_ref, xfn_ref, xb_ref, xbp_ref, xbn_ref, cw_ref, cb_ref, wg_ref, bg_ref,
                lam_ref, hf_ref, hb_ref, xe_scr, carry_scr):
    i = pl.program_id(0)
    nc = pl.num_programs(0)

    @pl.when(i == 0)
    def _():
        carry_scr[...] = jnp.zeros_like(carry_scr)

    shared = (cw_ref, cb_ref, wg_ref, bg_ref, lam_ref)
    scr = (xe_scr, carry_scr)
    _lru_direction(0, False, i, nc, xf_ref, xfp_ref, xfn_ref, *shared, hf_ref, *scr)
    _lru_direction(1, True, nc - 1 - i, nc, xb_ref, xbp_ref, xbn_ref, *shared, hb_ref, *scr)


def _lru(proj, conv_w, conv_b, w_a, b_a, w_x, b_x, lam):
    s_len = proj.shape[0]
    nc = s_len // LRU_CHUNK
    halo_per_chunk = LRU_CHUNK // BF16_SUBLANES
    n_halo = s_len // BF16_SUBLANES
    wg = (-jnp.concatenate([w_a, w_x], axis=-1)).astype(BF16)
    bg = -jnp.stack([b_a, b_x], axis=1)

    def fwd(i):
        return i

    def bwd(i):
        return nc - 1 - i

    def specs(chunk_of):
        return [
            pl.BlockSpec((LRU_CHUNK, LRU_WIDTH), lambda i: (chunk_of(i), COL_XLRU)),
            pl.BlockSpec((BF16_SUBLANES, LRU_WIDTH),
                         lambda i: (jnp.maximum(chunk_of(i) * halo_per_chunk - 1, 0), COL_XLRU)),
            pl.BlockSpec((BF16_SUBLANES, LRU_WIDTH),
                         lambda i: (jnp.minimum((chunk_of(i) + 1) * halo_per_chunk, n_halo - 1), COL_XLRU)),
        ]

    def whole(shape):
        return pl.BlockSpec(shape, lambda i: (0,) * len(shape))

    return pl.pallas_call(
        _lru_kernel,
        grid=(nc,),
        in_specs=specs(fwd) + specs(bwd) + [
            whole((CONV_WIDTH, LRU_WIDTH)), whole((1, LRU_WIDTH)),
            whole((2, LRU_BLOCKS, LRU_BLOCK_DIM, 2 * LRU_BLOCK_DIM)),
            whole((2, 2, LRU_WIDTH)), whole((2, LRU_WIDTH))],
        out_specs=[pl.BlockSpec((LRU_BLOCKS, LRU_CHUNK, LRU_BLOCK_DIM), lambda i: (0, i, 0)),
                   pl.BlockSpec((LRU_BLOCKS, LRU_CHUNK, LRU_BLOCK_DIM), lambda i: (0, nc - 1 - i, 0))],
        out_shape=[jax.ShapeDtypeStruct((LRU_BLOCKS, s_len, LRU_BLOCK_DIM), F32),
                   jax.ShapeDtypeStruct((LRU_BLOCKS, s_len, LRU_BLOCK_DIM), F32)],
        scratch_shapes=[pltpu.VMEM((2, LRU_BLOCKS, LRU_CHUNK + 2 * BF16_SUBLANES, LRU_BLOCK_DIM), F32),
                        pltpu.VMEM((2, LRU_WIDTH), F32)],
        compiler_params=_params("arbitrary"),
        name="rglru",
    )(proj, proj, proj, proj, proj, proj, conv_w, conv_b.reshape(1, LRU_WIDTH), wg, bg, lam)


def _mem_scores(h, q_ref, kv_ref):
    cs = slice(h * MEM_HEAD_DIM, (h + 1) * MEM_HEAD_DIM)
    return lax.dot_general(q_ref[:, cs], kv_ref[:, cs], (((1,), (1,)), ((), ())),
                           preferred_element_type=F32)


def _mem_values(h, s, kv_ref, y_scr):
    cs = slice(h * MEM_HEAD_DIM, (h + 1) * MEM_HEAD_DIM)
    vs = slice(MEM_WIDTH + h * MEM_HEAD_DIM, MEM_WIDTH + (h + 1) * MEM_HEAD_DIM)
    s = s * MEM_HEAD_DIM ** -0.5
    m = jnp.max(s, axis=-1, keepdims=True)
    p = jnp.exp(s - m)
    denom = jnp.sum(p, axis=-1, keepdims=True)
    y_scr[:, cs] = jnp.dot(p.astype(BF16), kv_ref[:, vs], preferred_element_type=F32) / denom


def _lru_finish(hf_ref, hb_ref, zl_ref, gl_ref):
    y_lru = jnp.concatenate([hf_ref[blk] + hb_ref[blk] for blk in range(LRU_BLOCKS)], axis=1)
    return _norm_gate(y_lru, gl_ref[...], zl_ref[...].astype(F32))


def _tail_kernel(sink_ref,
                 q_ref, kv0_ref, kvn_ref, pa_ref, pz_ref, c_ref, kv_ref, hf_ref, hb_ref, w_ref, x_ref, o_ref,
                 lhs_scr, acc_scr, ss_scr, k_ring, v_ring, bias_scr, cos_scr, sin_scr, ya_scr, ym_scr):
    o = pl.program_id(0)
    j = pl.program_id(1)
    nj = pl.num_programs(1)
    n_row_tiles = pl.num_programs(0) - 2
    nb = n_row_tiles * nj
    k0_ref, v0_ref = kv0_ref.at[:, 0:KV_WIDTH], kv0_ref.at[:, KV_WIDTH:2 * KV_WIDTH]
    kn_ref, vn_ref = kvn_ref.at[:, 0:KV_WIDTH], kvn_ref.at[:, KV_WIDTH:2 * KV_WIDTH]
    qm_ref, z0_ref = pa_ref.at[:, 0:MEM_WIDTH], pa_ref.at[:, MEM_WIDTH:2 * MEM_WIDTH]
    z1_ref, zl_ref, zm_ref = pz_ref.at[:, 0:1024], pz_ref.at[:, 1024:2048], pz_ref.at[:, 2048:3072]
    ga_ref = c_ref.at[0:1, 0:ATT_WIDTH]
    gl_ref = c_ref.at[0:1, ATT_WIDTH:ATT_WIDTH + LRU_WIDTH]
    gm_ref = c_ref.at[0:1, ATT_WIDTH + LRU_WIDTH:D_MODEL]
    freq_ref = c_ref.at[0:1, D_MODEL:D_MODEL + HEAD_DIM]

    n_chunks = ATT_KV_HEADS
    chunk = TAIL_TN // n_chunks

    def stage_b_chunk(c, part):
        slot = lax.rem(o + 1, 2)
        cols = slice(c * chunk, (c + 1) * chunk)
        acc = jnp.dot(lhs_scr[slot], w_ref[:, cols], preferred_element_type=F32)
        acc_scr[slot, j, :, cols] = acc.astype(acc_scr.dtype)
        sq = acc * acc
        for k in range(chunk // LANES):
            piece = sq[:, k * LANES:(k + 1) * LANES]
            part = piece if part is None else part + piece
        return part

    def stages_ab(with_a, with_b):
        part = None
        if with_b:
            part = stage_b_chunk(0, part)
        if with_a:
            n = o * nj + j
            attn = _AttnBlock(n, nb, sink_ref, q_ref, kn_ref, vn_ref, freq_ref,
                              k_ring, v_ring, bias_scr, cos_scr, sin_scr, ya_scr)
            s_att = [attn.scores(h) for h in range(ATT_KV_HEADS)]
            s_mem = [_mem_scores(h, qm_ref, kv_ref) for h in range(MEM_HEADS)]
        for pair in range(2):
            if with_b:
                part = stage_b_chunk(1 + pair, part)
            if with_a:
                for h in (2 * pair, 2 * pair + 1):
                    attn.values(h, s_att[h])
                    _mem_values(h, s_mem[h], kv_ref, ym_scr)
        if with_b:
            part = stage_b_chunk(3, part)
            slot = lax.rem(o + 1, 2)
            ss_scr[slot] = jnp.where(j == 0, part, ss_scr[slot] + part)
        if with_a:
            rows = pl.ds(pl.multiple_of(j * BLOCK, BLOCK), BLOCK)
            slot = lax.rem(o, 2)
            lhs_scr[slot, rows, 0:ATT_WIDTH] = attn.result(z0_ref, z1_ref, ga_ref).astype(BF16)
            lhs_scr[slot, rows, ATT_WIDTH:ATT_WIDTH + LRU_WIDTH] = _lru_finish(
                hf_ref, hb_ref, zl_ref, gl_ref).astype(BF16)
            lhs_scr[slot, rows, ATT_WIDTH + LRU_WIDTH:] = _norm_gate(
                ym_scr[...], gm_ref[...], zm_ref[...].astype(F32)).astype(BF16)

    def stage_c():
        slot = lax.rem(o, 2)
        ms = jnp.sum(ss_scr[slot], axis=-1, keepdims=True) * (1.0 / D_MODEL)
        post_g = c_ref[pl.ds(1 + j, 1), 0:TAIL_TN]
        o_ref[...] = x_ref[...] + acc_scr[slot, j].astype(F32) * lax.rsqrt(ms + EPS) * post_g

    @pl.when(o == 0)
    def _():
        @pl.when(j == 0)
        def _():
            ss_scr[...] = jnp.zeros_like(ss_scr)
            _attn_init(k0_ref, v0_ref, freq_ref, k_ring, v_ring, bias_scr, cos_scr, sin_scr)

        stages_ab(True, False)

    @pl.when(o == 1)
    def _():
        stages_ab(True, True)

    @pl.when((o >= 2) & (o < n_row_tiles))
    def _():
        stage_c()
        stages_ab(True, True)

    @pl.when(o == n_row_tiles)
    def _():
        stage_c()
        stages_ab(False, True)

    @pl.when(o == n_row_tiles + 1)
    def _():
        stage_c()


def _tail(proj, kv, h_fwd, h_bwd, sink, g_att, g_mem, g_lru, w_out, x, post_g):
    s_len = proj.shape[0]
    tm, tn = TAIL_TM, TAIL_TN
    ni, nj = s_len // tm, D_MODEL // tn
    nb = s_len // BLOCK
    assert tm == nj * BLOCK
    inv_freq = jnp.power(jnp.float32(ROPE_THETA),
                         -jnp.arange(ROT_HALF, dtype=F32) * 2.0 / ROT_DIM)
    freq = jnp.concatenate([inv_freq, inv_freq, jnp.zeros((HEAD_DIM - ROT_DIM,), F32)])
    row0 = jnp.concatenate([g_att, g_lru, g_mem, freq])
    post_rows = jnp.pad(post_g.reshape(nj, tn), ((0, 0), (0, row0.shape[0] - tn)))
    consts = jnp.concatenate([row0[None], post_rows, jnp.zeros((8 - 1 - nj, row0.shape[0]), F32)], axis=0)

    def block_of(o, j):
        return jnp.where(o < ni, o * nj + j, nb - 1)

    def rows_spec(width, col, shift=0):
        return pl.BlockSpec((BLOCK, width),
                            lambda o, j, sink: (jnp.clip(block_of(o, j) + shift, 0, nb - 1), col))

    def const_spec(shape):
        return pl.BlockSpec(shape, lambda o, j, sink: (0,) * len(shape))

    slab_spec = pl.BlockSpec((LRU_BLOCKS, BLOCK, LRU_BLOCK_DIM), lambda o, j, sink: (0, block_of(o, j), 0))

    def w_col(o, j, sink):
        return (0, jnp.where(o == 0, 0, jnp.where(o <= ni, j, nj - 1)))

    def finished_tile(o, j, sink):
        return (jnp.maximum(o - 2, 0), jnp.where(o >= 2, j, 0))

    grid_spec = pltpu.PrefetchScalarGridSpec(
        num_scalar_prefetch=1,
        grid=(ni + 2, nj),
        in_specs=[rows_spec(ATT_WIDTH, 0),
                  pl.BlockSpec((BLOCK, 2 * KV_WIDTH), lambda o, j, sink: (0, 2)),
                  rows_spec(2 * KV_WIDTH, 2, 1),
                  rows_spec(2 * MEM_WIDTH, 2),
                  rows_spec(3 * 1024, 2),
                  const_spec(consts.shape),
                  const_spec((N_MEM, 2 * MEM_WIDTH)),
                  slab_spec, slab_spec,
                  pl.BlockSpec((D_MODEL, tn), w_col),
                  pl.BlockSpec((tm, tn), finished_tile)],
        out_specs=pl.BlockSpec((tm, tn), finished_tile),
        scratch_shapes=[pltpu.VMEM((2, tm, D_MODEL), BF16),
                        pltpu.VMEM((2, nj, tm, tn), BF16),
                        pltpu.VMEM((2, tm, LANES), F32),
                        pltpu.VMEM((3, BLOCK, KV_WIDTH), BF16),
                        pltpu.VMEM((3, BLOCK, KV_WIDTH), BF16),
                        pltpu.VMEM((3, BLOCK, ATT_GROUP * BLOCK), F32),
                        pltpu.VMEM((BLOCK, HEAD_DIM), F32),
                        pltpu.VMEM((BLOCK, HEAD_DIM), F32),
                        pltpu.VMEM((BLOCK, ATT_WIDTH), F32),
                        pltpu.VMEM((BLOCK, MEM_WIDTH), F32)],
    )
    return pl.pallas_call(
        _tail_kernel,
        grid_spec=grid_spec,
        out_shape=jax.ShapeDtypeStruct((s_len, D_MODEL), F32),
        compiler_params=_params("arbitrary", "arbitrary"),
        name="mixers_out_proj_post_norm",
    )(sink, proj, proj, proj, proj, proj, consts, kv, h_fwd, h_bwd, w_out, x)


def _layer(h, mem, pre_g, w_in, sink, conv_w, conv_b, w_a, b_a, w_x, b_x, lam, mem_g, w_mem_kv,
           g_att, g_lru, g_mem, w_out, post_g):
    proj, w_out_bf, w_mem_kv_bf = _in_proj(h, pre_g, w_in, w_out, w_mem_kv)
    kv = _matmul(_rmsnorm_cast(mem, mem_g), w_mem_kv_bf, BF16, "mem_kv_proj")
    h_fwd, h_bwd = _lru(proj, conv_w, conv_b, w_a, b_a, w_x, b_x, lam)
    return _tail(proj, kv, h_fwd, h_bwd, sink, g_att, g_mem, g_lru, w_out_bf, h, post_g)


def kernel(x, mem, pre_norm_gain, w_in, att_sink, conv_w, conv_b, lru_w_a, lru_b_a, lru_w_x, lru_b_x,
           lru_lambda, mem_norm_gain, w_mem_kv, att_out_gain, lru_out_gain, mem_out_gain, w_out,
           post_norm_gain):
    batch = x.shape[0]
    depth = w_in.shape[0]
    outs = []
    for b in range(batch):
        h = x[b]
        for l in range(depth):
            h = _layer(h, mem[b], pre_norm_gain[l], w_in[l], att_sink[l], conv_w[l], conv_b[l],
                       lru_w_a[l], lru_b_a[l], lru_w_x[l], lru_b_x[l], lru_lambda[l],
                       mem_norm_gain[l], w_mem_kv[l], att_out_gain[l], lru_out_gain[l],
                       mem_out_gain[l], w_out[l], post_norm_gain[l])
        outs.append(h)
    return jnp.stack(outs, axis=0)
```

```python
import functools
import math

import jax
import jax.numpy as jnp
from jax import lax
from jax.experimental import pallas as pl
from jax.experimental.pallas import tpu as pltpu

F32 = jnp.float32
BF16 = jnp.bfloat16

D_MODEL = 4096
SEQ = 8192
N_MEM = 256
HEAD_DIM = 128
ATT_WIDTH = 2048
ATT_HEADS = 16
ATT_KV_HEADS = 4
ATT_GROUP = ATT_HEADS // ATT_KV_HEADS
KV_WIDTH = ATT_KV_HEADS * HEAD_DIM
BLOCK = 128
ROPE_THETA = 500000.0
ROT_DIM = HEAD_DIM // 4
ROT_HALF = ROT_DIM // 2
LRU_WIDTH = 1024
LRU_BLOCKS = 8
LRU_BLOCK_DIM = LRU_WIDTH // LRU_BLOCKS
CONV_WIDTH = 4
CONV_LEFT = 2
LRU_C = 8.0
MEM_HEADS = 4
MEM_WIDTH = 1024
MEM_HEAD_DIM = MEM_WIDTH // MEM_HEADS
EPS = 1e-6
IN_WIDTH = 9216
MASK_BIAS = -1e30
F32_TINY = float(jnp.finfo(jnp.float32).tiny)

COL_XLRU = 3
COL_QMEM = 4
COL_ZATT = 5
COL_ZLRU = 7
COL_ZMEM = 8
COL_K512 = 4
COL_V512 = 5

VMEM_LIMIT_BYTES = 60000 * 1024
BF16_SUBLANES = 16

NORM_ROWS = 256
MM_TM = 1024
MM_TN = 1024
IN_TM = 1024
IN_TN = 512
IN_KC = 512
IN_KCHUNKS = D_MODEL // IN_KC
TAIL_TM = 512
TAIL_TN = 1024
LANES = 128
LRU_CHUNK = 512
LRU_SUBCHUNK_SEGS = (36, 28)
assert 8 * sum(LRU_SUBCHUNK_SEGS) == LRU_CHUNK and all(s % 8 == 4 for s in LRU_SUBCHUNK_SEGS)
MIX_ROWS = 512


def _params(*sem):
    return pltpu.CompilerParams(dimension_semantics=sem, vmem_limit_bytes=VMEM_LIMIT_BYTES)


def _sigmoid(t):
    return 1.0 / (1.0 + jnp.exp(-t))


def _norm_gate(y, g, z):
    ms = jnp.mean(y * y, axis=-1, keepdims=True)
    return y * lax.rsqrt(ms + EPS) * g * (z * _sigmoid(z))


def _rmsnorm_kernel(x_ref, g_ref, o_ref):
    x = x_ref[...]
    ms = jnp.mean(x * x, axis=-1, keepdims=True)
    o_ref[...] = (x * lax.rsqrt(ms + EPS) * g_ref[...]).astype(o_ref.dtype)


def _rmsnorm_cast(x, g):
    m, d = x.shape
    rows = min(NORM_ROWS, m)
    return pl.pallas_call(
        _rmsnorm_kernel,
        grid=(m // rows,),
        in_specs=[pl.BlockSpec((rows, d), lambda i: (i, 0)),
                  pl.BlockSpec((1, d), lambda i: (0, 0))],
        out_specs=pl.BlockSpec((rows, d), lambda i: (i, 0)),
        out_shape=jax.ShapeDtypeStruct((m, d), BF16),
        compiler_params=_params("parallel"),
        name="rmsnorm_cast",
    )(x, g.reshape(1, d))


def _matmul_kernel(a_ref, b_ref, o_ref):
    o_ref[...] = jnp.dot(a_ref[...], b_ref[...], preferred_element_type=F32).astype(o_ref.dtype)


def _matmul(a, b, out_dtype, name):
    m, k = a.shape
    _, n = b.shape
    tm = min(MM_TM, m)
    tn = min(MM_TN, n)
    return pl.pallas_call(
        _matmul_kernel,
        grid=(m // tm, n // tn),
        in_specs=[pl.BlockSpec((tm, k), lambda i, j: (i, 0)),
                  pl.BlockSpec((k, tn), lambda i, j: (0, j))],
        out_specs=pl.BlockSpec((tm, tn), lambda i, j: (i, j)),
        out_shape=jax.ShapeDtypeStruct((m, n), out_dtype),
        compiler_params=_params("parallel", "parallel"),
        name=name,
    )(a, b)


def _in_proj_kernel(x_ref, g_ref, w_ref, wo_ref, wk_ref, proj_ref, wo_bf_ref, wk_bf_ref, u_scr, ss_scr):
    o = pl.program_id(0)
    j = pl.program_id(1)

    def normalise_next_tile():
        for c in range(D_MODEL // TAIL_TN):
            wo_bf_ref[c] = wo_ref[:, c * TAIL_TN:(c + 1) * TAIL_TN].astype(BF16)
        wk_bf_ref[...] = wk_ref[...].astype(BF16)
        x = x_ref[...]
        sq = x * x
        part = sq[:, 0:LANES]
        for c in range(1, IN_KC // LANES):
            part = part + sq[:, c * LANES:(c + 1) * LANES]
        ss_old = ss_scr[...]
        ss = jnp.where(j == 0, part, jnp.where(j < IN_KCHUNKS, ss_old + part, ss_old))
        ss_scr[...] = ss
        r = lax.rsqrt(jnp.sum(ss, axis=-1, keepdims=True) * (1.0 / D_MODEL) + EPS)
        chunk = jnp.clip(j - IN_KCHUNKS, 0, IN_KCHUNKS - 1)
        u_scr[lax.rem(o, 2), chunk] = (x * r * g_ref[...]).astype(BF16)

    @pl.when(o == 0)
    def _():
        @pl.when(j == 0)
        def _():
            ss_scr[...] = jnp.zeros_like(ss_scr)

        normalise_next_tile()

    @pl.when(o > 0)
    def _():
        slot = lax.rem(o + 1, 2)
        acc = None
        for c in range(IN_KCHUNKS):
            d = jnp.dot(u_scr[slot, c], w_ref[c * IN_KC:(c + 1) * IN_KC, :].astype(BF16),
                        preferred_element_type=F32)
            acc = d if acc is None else acc + d
        proj_ref[...] = acc.astype(proj_ref.dtype)
        normalise_next_tile()


def _in_proj(x, pre_g, w_in, w_out, w_mem_kv):
    s_len = x.shape[0]
    ni, nj = s_len // IN_TM, IN_WIDTH // IN_TN
    n_steps = (ni + 1) * nj
    side_rows = 32
    n_side = D_MODEL // side_rows
    assert n_side <= n_steps and nj >= 2 * IN_KCHUNKS

    def x_tile(o, j):
        return (jnp.minimum(o, ni - 1), jnp.where(j < 2 * IN_KCHUNKS, lax.rem(j, IN_KCHUNKS), IN_KCHUNKS - 1))

    def side(o, j):
        return (jnp.minimum(o * nj + j, n_side - 1), 0)

    return pl.pallas_call(
        _in_proj_kernel,
        grid=(ni + 1, nj),
        in_specs=[pl.BlockSpec((IN_TM, IN_KC), x_tile),
                  pl.BlockSpec((1, IN_KC), lambda o, j: (0, x_tile(o, j)[1])),
                  pl.BlockSpec((D_MODEL, IN_TN), lambda o, j: (0, jnp.where(o > 0, j, 0))),
                  pl.BlockSpec((side_rows, D_MODEL), side),
                  pl.BlockSpec((side_rows, 2 * MEM_WIDTH), side)],
        out_specs=[pl.BlockSpec((IN_TM, IN_TN), lambda o, j: (jnp.maximum(o - 1, 0), jnp.where(o > 0, j, 0))),
                   pl.BlockSpec((D_MODEL // TAIL_TN, side_rows, TAIL_TN), lambda o, j: (0, side(o, j)[0], 0)),
                   pl.BlockSpec((side_rows, 2 * MEM_WIDTH), side)],
        out_shape=[jax.ShapeDtypeStruct((s_len, IN_WIDTH), BF16),
                   jax.ShapeDtypeStruct((D_MODEL // TAIL_TN, D_MODEL, TAIL_TN), BF16),
                   jax.ShapeDtypeStruct((D_MODEL, 2 * MEM_WIDTH), BF16)],
        scratch_shapes=[pltpu.VMEM((2, IN_KCHUNKS, IN_TM, IN_KC), BF16),
                        pltpu.VMEM((IN_TM, LANES), F32)],
        compiler_params=_params("arbitrary", "arbitrary"),
        name="in_proj_pre_norm",
    )(x, pre_g.reshape(1, D_MODEL), w_in, w_out, w_mem_kv)


def _rope_tables(block_idx, freq, base_cos, base_sin, mult):
    a = (block_idx * BLOCK).astype(F32) * freq
    ca, sa = jnp.cos(a) * mult, jnp.sin(a) * mult
    c = ca * base_cos - sa * base_sin
    s = sa * base_cos + ca * base_sin
    lane = lax.broadcasted_iota(jnp.int32, (BLOCK, HEAD_DIM), 1)
    return c, jnp.where(lane < ROT_HALF, -s, 0.0), jnp.where(lane >= ROT_HALF, s, 0.0)


def _rope(t, tables):
    c, sin_lo, sin_hi = tables
    return (t * c + pltpu.roll(t, HEAD_DIM - ROT_HALF, axis=1) * sin_lo
            + pltpu.roll(t, ROT_HALF, axis=1) * sin_hi)


def _rope_keys(k_ref, tables):
    return jnp.concatenate(
        [_rope(k_ref[:, h * HEAD_DIM:(h + 1) * HEAD_DIM].astype(F32), tables).astype(BF16)
         for h in range(ATT_KV_HEADS)], axis=1)


def _attn_init(k0_ref, freq_ref, k_ring, bias_scr, cos_scr, sin_scr):
    freq = freq_ref[...]
    rows = ATT_GROUP * BLOCK
    ang = lax.broadcasted_iota(jnp.int32, (BLOCK, HEAD_DIM), 0).astype(F32) * freq
    cos_scr[...] = jnp.cos(ang)
    sin_scr[...] = jnp.sin(ang)
    key = lax.broadcasted_iota(jnp.int32, (BLOCK, rows), 0)
    q_local = lax.broadcasted_iota(jnp.int32, (BLOCK, rows), 1) & (BLOCK - 1)
    bias_scr[0] = jnp.where(key >= q_local, 0.0, MASK_BIAS)
    bias_scr[1] = jnp.where(key <= q_local, 0.0, MASK_BIAS)
    bias_scr[2] = jnp.full((BLOCK, rows), MASK_BIAS, F32)
    k_ring[0] = _rope_keys(k0_ref, _rope_tables(jnp.int32(0), freq, cos_scr[...], sin_scr[...], 1.0))
    k_ring[2] = jnp.zeros((BLOCK, KV_WIDTH), BF16)


class _AttnBlock:
    def __init__(self, n, nb, sink_ref, q_ref, kn_ref, vp_ref, vc_ref, vn_ref, freq_ref,
                 k_ring, bias_scr, cos_scr, sin_scr, y_scr):
        self.sink_ref, self.q_ref, self.v_refs, self.k_ring, self.y_scr = (
            sink_ref, q_ref, (vp_ref, vc_ref, vn_ref), k_ring, y_scr)
        freq = freq_ref[...]
        rows = ATT_GROUP * BLOCK
        self.log2e = math.log2(math.e)
        base_cos = cos_scr[...]
        base_sin = sin_scr[...]
        self.slots = (lax.rem(n + 2, 3), lax.rem(n, 3), lax.rem(n + 1, 3))
        k_ring[self.slots[2]] = _rope_keys(kn_ref, _rope_tables(n + 1, freq, base_cos, base_sin, 1.0))
        self.tab_q = _rope_tables(n, freq, base_cos, base_sin, HEAD_DIM ** -0.5 * self.log2e)
        self.bias_p = bias_scr[jnp.where(n > 0, 0, 2)]
        self.bias_n = bias_scr[jnp.where(n < nb - 1, 1, 2)]
        self.lane_group = lax.broadcasted_iota(jnp.int32, (1, rows), 1) // BLOCK
        self.first_row = lax.broadcasted_iota(jnp.int32, (BF16_SUBLANES, rows), 0) == 0
        self.ones = jnp.ones((3 * BLOCK + BF16_SUBLANES, HEAD_DIM), BF16)

    def scores(self, h):
        ks = slice(h * HEAD_DIM, (h + 1) * HEAD_DIM)
        k_all = jnp.concatenate([self.k_ring[s, :, ks] for s in self.slots], axis=0)
        q_all = jnp.concatenate(
            [_rope(self.q_ref[:, (h * ATT_GROUP + g) * HEAD_DIM:(h * ATT_GROUP + g + 1) * HEAD_DIM]
                   .astype(F32), self.tab_q).astype(BF16) for g in range(ATT_GROUP)], axis=0)
        return lax.dot_general(k_all, q_all, (((1,), (1,)), ((), ())), preferred_element_type=F32)

    def values(self, h, s):
        ks = slice(h * HEAD_DIM, (h + 1) * HEAD_DIM)
        sink = jnp.zeros((1, ATT_GROUP * BLOCK), F32)
        for g in range(ATT_GROUP):
            sink = jnp.where(self.lane_group == g, self.sink_ref[h * ATT_GROUP + g] * self.log2e, sink)
        s_p = s[:BLOCK] + self.bias_p
        s_c = s[BLOCK:2 * BLOCK]
        s_n = s[2 * BLOCK:] + self.bias_n
        m = jnp.max(jnp.maximum(jnp.maximum(s_p, s_c), s_n), axis=0, keepdims=True)
        m = jnp.maximum(m, sink)
        p_sink = jnp.where(self.first_row, jnp.exp2(sink - m), 0.0)
        p = jnp.concatenate([jnp.exp2(s_p - m), jnp.exp2(s_c - m), jnp.exp2(s_n - m), p_sink],
                            axis=0).astype(BF16)
        v_all = jnp.concatenate([r[:, ks] for r in self.v_refs]
                                + [jnp.zeros((BF16_SUBLANES, HEAD_DIM), BF16)], axis=0)
        v_aug = jnp.concatenate([v_all, self.ones], axis=1)
        o_aug = lax.dot_general(p, v_aug, (((0,), (0,)), ((), ())), preferred_element_type=F32)
        o = o_aug[:, :HEAD_DIM] / o_aug[:, HEAD_DIM:]
        for g in range(ATT_GROUP):
            hq = h * ATT_GROUP + g
            self.y_scr[:, hq * HEAD_DIM:(hq + 1) * HEAD_DIM] = o[g * BLOCK:(g + 1) * BLOCK]

    def result(self, z0_ref, z1_ref, g_ref):
        z = jnp.concatenate([z0_ref[...], z1_ref[...]], axis=1).astype(F32)
        return _norm_gate(self.y_scr[...], g_ref[...], z)


def _lru_direction(d, reverse, chunk, n_chunks, x_ref, xp_ref, xn_ref, cw_ref, cb_ref, wg_ref,
                   bg_ref, lam_ref, out_ref, xe_scr, carry_scr):
    t = LRU_CHUNK
    halo = BF16_SUBLANES
    for blk in range(LRU_BLOCKS):
        cs = slice(blk * LRU_BLOCK_DIM, (blk + 1) * LRU_BLOCK_DIM)
        xe_scr[d, blk, 0:halo, :] = jnp.where(chunk > 0, xp_ref[:, cs].astype(F32), 0.0)
        xe_scr[d, blk, halo:halo + t, :] = x_ref[:, cs].astype(F32)
        xe_scr[d, blk, halo + t:, :] = jnp.where(chunk < n_chunks - 1, xn_ref[:, cs].astype(F32), 0.0)

    neg_lam = -lam_ref[d:d + 1, :]
    softplus = jnp.maximum(neg_lam, 0.0) + jnp.log(1.0 + jnp.exp(-jnp.abs(neg_lam)))
    decay = (-LRU_C * math.log2(math.e)) * softplus
    row0 = [8 * sum(LRU_SUBCHUNK_SEGS[:k]) for k in range(len(LRU_SUBCHUNK_SEGS))]
    subchunks = list(zip(row0, LRU_SUBCHUNK_SEGS))
    for blk in range(LRU_BLOCKS):
        cs = slice(blk * LRU_BLOCK_DIM, (blk + 1) * LRU_BLOCK_DIM)
        taps = [cw_ref[j:j + 1, cs] for j in range(CONV_WIDTH)]
        cin = carry_scr[d:d + 1, cs]
        for r0, seg in (reversed(subchunks) if reverse else subchunks):
            order = range(seg - 1, -1, -1) if reverse else range(seg)
            tiles = []
            for m in range(seg):
                acc = cb_ref[:, cs]
                for j in range(CONV_WIDTH):
                    start = halo + r0 + m + j - CONV_LEFT
                    acc = acc + taps[j] * xe_scr[d, blk, pl.ds(start, 8, stride=seg), :]
                tiles.append(acc)
            xc = jnp.concatenate(tiles, axis=0)
            gates = jnp.dot(xc.astype(BF16), wg_ref[d, blk], preferred_element_type=F32)
            r = 1.0 / (1.0 + jnp.exp(gates[:, :LRU_BLOCK_DIM] + bg_ref[d, 0:1, cs]))
            i = 1.0 / (1.0 + jnp.exp(gates[:, LRU_BLOCK_DIM:] + bg_ref[d, 1:2, cs]))
            a = jnp.exp2(decay[:, cs] * r)
            w = 1.0 - a * a
            b = w * lax.rsqrt(jnp.maximum(w, F32_TINY)) * i * xc
            tile = lambda v, m: v[8 * m:8 * m + 8]
            hh = jnp.zeros((8, LRU_BLOCK_DIM), F32)
            pp = jnp.ones((8, LRU_BLOCK_DIM), F32)
            for m in order:
                hh = tile(a, m) * hh + tile(b, m)
                pp = tile(a, m) * pp
            entering = [None] * 8
            for s in (range(7, -1, -1) if reverse else range(8)):
                entering[s] = cin
                cin = pp[s:s + 1, :] * cin + hh[s:s + 1, :]
            h = jnp.concatenate(entering, axis=0)
            for m in order:
                h = tile(a, m) * h + tile(b, m)
                out_ref[blk, pl.ds(r0 + m, 8, stride=seg), :] = h
        carry_scr[d:d + 1, cs] = cin


def _lru_kernel(xf_ref, xfp_ref, xfn_ref, xb_ref, xbp_ref, xbn_ref, cw_ref, cb_ref, wg_ref, bg_ref,
                lam_ref, hf_ref, hb_ref, xe_scr, carry_scr):
    i = pl.program_id(0)
    nc = pl.num_programs(0)

    @pl.when(i == 0)
    def _():
        carry_scr[...] = jnp.zeros_like(carry_scr)

    shared = (cw_ref, cb_ref, wg_ref, bg_ref, lam_ref)
    scr = (xe_scr, carry_scr)
    _lru_direction(0, False, i, nc, xf_ref, xfp_ref, xfn_ref, *shared, hf_ref, *scr)
    _lru_direction(1, True, nc - 1 - i, nc, xb_ref, xbp_ref, xbn_ref, *shared, hb_ref, *scr)


def _lru(proj, conv_w, conv_b, w_a, b_a, w_x, b_x, lam):
    s_len = proj.shape[0]
    nc = s_len // LRU_CHUNK
    halo_per_chunk = LRU_CHUNK // BF16_SUBLANES
    n_halo = s_len // BF16_SUBLANES
    wg = (-jnp.concatenate([w_a, w_x], axis=-1)).astype(BF16)
    bg = -jnp.stack([b_a, b_x], axis=1)

    def fwd(i):
        return i

    def bwd(i):
        return nc - 1 - i

    def specs(chunk_of):
        return [
            pl.BlockSpec((LRU_CHUNK, LRU_WIDTH), lambda i: (chunk_of(i), COL_XLRU)),
            pl.BlockSpec((BF16_SUBLANES, LRU_WIDTH),
                         lambda i: (jnp.maximum(chunk_of(i) * halo_per_chunk - 1, 0), COL_XLRU)),
            pl.BlockSpec((BF16_SUBLANES, LRU_WIDTH),
                         lambda i: (jnp.minimum((chunk_of(i) + 1) * halo_per_chunk, n_halo - 1), COL_XLRU)),
        ]

    def whole(shape):
        return pl.BlockSpec(shape, lambda i: (0,) * len(shape))

    return pl.pallas_call(
        _lru_kernel,
        grid=(nc,),
        in_specs=specs(fwd) + specs(bwd) + [
            whole((CONV_WIDTH, LRU_WIDTH)), whole((1, LRU_WIDTH)),
            whole((2, LRU_BLOCKS, LRU_BLOCK_DIM, 2 * LRU_BLOCK_DIM)),
            whole((2, 2, LRU_WIDTH)), whole((2, LRU_WIDTH))],
        out_specs=[pl.BlockSpec((LRU_BLOCKS, LRU_CHUNK, LRU_BLOCK_DIM), lambda i: (0, i, 0)),
                   pl.BlockSpec((LRU_BLOCKS, LRU_CHUNK, LRU_BLOCK_DIM), lambda i: (0, nc - 1 - i, 0))],
        out_shape=[jax.ShapeDtypeStruct((LRU_BLOCKS, s_len, LRU_BLOCK_DIM), F32),
                   jax.ShapeDtypeStruct((LRU_BLOCKS, s_len, LRU_BLOCK_DIM), F32)],
        scratch_shapes=[pltpu.VMEM((2, LRU_BLOCKS, LRU_CHUNK + 2 * BF16_SUBLANES, LRU_BLOCK_DIM), F32),
                        pltpu.VMEM((2, LRU_WIDTH), F32)],
        compiler_params=_params("arbitrary"),
        name="rglru",
    )(proj, proj, proj, proj, proj, proj, conv_w, conv_b.reshape(1, LRU_WIDTH), wg, bg, lam)


def _mem_scores(h, q_ref, kv_ref):
    cs = slice(h * MEM_HEAD_DIM, (h + 1) * MEM_HEAD_DIM)
    return lax.dot_general(q_ref[:, cs], kv_ref[:, cs], (((1,), (1,)), ((), ())),
                           preferred_element_type=F32)


def _mem_values(h, s, kv_ref, y_scr):
    cs = slice(h * MEM_HEAD_DIM, (h + 1) * MEM_HEAD_DIM)
    vs = slice(MEM_WIDTH + h * MEM_HEAD_DIM, MEM_WIDTH + (h + 1) * MEM_HEAD_DIM)
    s = s * MEM_HEAD_DIM ** -0.5
    m = jnp.max(s, axis=-1, keepdims=True)
    p = jnp.exp(s - m)
    denom = jnp.sum(p, axis=-1, keepdims=True)
    y_scr[:, cs] = jnp.dot(p.astype(BF16), kv_ref[:, vs], preferred_element_type=F32) / denom


def _lru_finish(hf_ref, hb_ref, zl_ref, gl_ref):
    y_lru = jnp.concatenate([hf_ref[blk] + hb_ref[blk] for blk in range(LRU_BLOCKS)], axis=1)
    return _norm_gate(y_lru, gl_ref[...], zl_ref[...].astype(F32))


def _tail_kernel(sink_ref,
                 q_ref, k0_ref, kn_ref, vp_ref, vc_ref, vn_ref, z0_ref, z1_ref, freq_ref, ga_ref,
                 qm_ref, kv_ref, zm_ref, gm_ref, hf_ref, hb_ref, zl_ref, gl_ref,
                 w_ref, x_ref, gp_ref, o_ref,
                 lhs_scr, acc_scr, ss_scr, k_ring, bias_scr, cos_scr, sin_scr, ya_scr, ym_scr):
    o = pl.program_id(0)
    j = pl.program_id(1)
    nj = pl.num_programs(1)
    n_row_tiles = pl.num_programs(0) - 2
    nb = n_row_tiles * nj

    n_chunks = ATT_KV_HEADS
    chunk = TAIL_TN // n_chunks

    def stage_b_chunk(c, part):
        slot = lax.rem(o + 1, 2)
        cols = slice(c * chunk, (c + 1) * chunk)
        acc = jnp.dot(lhs_scr[slot], w_ref[:, cols], preferred_element_type=F32)
        acc_scr[slot, j, :, cols] = acc.astype(acc_scr.dtype)
        sq = acc * acc
        for k in range(chunk // LANES):
            piece = sq[:, k * LANES:(k + 1) * LANES]
            part = piece if part is None else part + piece
        return part

    def stages_ab(with_a, with_b):
        part = None
        if with_b:
            part = stage_b_chunk(0, part)
        if with_a:
            n = o * nj + j
            attn = _AttnBlock(n, nb, sink_ref, q_ref, kn_ref, vp_ref, vc_ref, vn_ref, freq_ref,
                              k_ring, bias_scr, cos_scr, sin_scr, ya_scr)
            s_att = [attn.scores(h) for h in range(ATT_KV_HEADS)]
            s_mem = [_mem_scores(h, qm_ref, kv_ref) for h in range(MEM_HEADS)]
        for pair in range(2):
            if with_b:
                part = stage_b_chunk(1 + pair, part)
            if with_a:
                for h in (2 * pair, 2 * pair + 1):
                    attn.values(h, s_att[h])
                    _mem_values(h, s_mem[h], kv_ref, ym_scr)
        if with_b:
            part = stage_b_chunk(3, part)
            slot = lax.rem(o + 1, 2)
            ss_scr[slot] = jnp.where(j == 0, part, ss_scr[slot] + part)
        if with_a:
            rows = pl.ds(pl.multiple_of(j * BLOCK, BLOCK), BLOCK)
            slot = lax.rem(o, 2)
            lhs_scr[slot, rows, 0:ATT_WIDTH] = attn.result(z0_ref, z1_ref, ga_ref).astype(BF16)
            lhs_scr[slot, rows, ATT_WIDTH:ATT_WIDTH + LRU_WIDTH] = _lru_finish(
                hf_ref, hb_ref, zl_ref, gl_ref).astype(BF16)
            lhs_scr[slot, rows, ATT_WIDTH + LRU_WIDTH:] = _norm_gate(
                ym_scr[...], gm_ref[...], zm_ref[...].astype(F32)).astype(BF16)

    def stage_c():
        slot = lax.rem(o, 2)
        ms = jnp.sum(ss_scr[slot], axis=-1, keepdims=True) * (1.0 / D_MODEL)
        o_ref[...] = x_ref[...] + acc_scr[slot, j].astype(F32) * lax.rsqrt(ms + EPS) * gp_ref[...]

    @pl.when(o == 0)
    def _():
        @pl.when(j == 0)
        def _():
            ss_scr[...] = jnp.zeros_like(ss_scr)
            _attn_init(k0_ref, freq_ref, k_ring, bias_scr, cos_scr, sin_scr)

        stages_ab(True, False)

    @pl.when(o == 1)
    def _():
        stages_ab(True, True)

    @pl.when((o >= 2) & (o < n_row_tiles))
    def _():
        stage_c()
        stages_ab(True, True)

    @pl.when(o == n_row_tiles)
    def _():
        stage_c()
        stages_ab(False, True)

    @pl.when(o == n_row_tiles + 1)
    def _():
        stage_c()


def _tail(proj, kv, h_fwd, h_bwd, sink, g_att, g_mem, g_lru, w_out, x, post_g):
    s_len = proj.shape[0]
    tm, tn = TAIL_TM, TAIL_TN
    ni, nj = s_len // tm, D_MODEL // tn
    nb = s_len // BLOCK
    assert tm == nj * BLOCK
    inv_freq = jnp.power(jnp.float32(ROPE_THETA),
                         -jnp.arange(ROT_HALF, dtype=F32) * 2.0 / ROT_DIM)
    freq = jnp.concatenate([inv_freq, inv_freq, jnp.zeros((HEAD_DIM - ROT_DIM,), F32)]).reshape(1, HEAD_DIM)

    def block_of(o, j):
        return jnp.where(o < ni, o * nj + j, nb - 1)

    def rows_spec(width, col, shift=0):
        return pl.BlockSpec((BLOCK, width),
                            lambda o, j, sink: (jnp.clip(block_of(o, j) + shift, 0, nb - 1), col))

    def const_spec(shape):
        return pl.BlockSpec(shape, lambda o, j, sink: (0,) * len(shape))

    slab_spec = pl.BlockSpec((LRU_BLOCKS, BLOCK, LRU_BLOCK_DIM), lambda o, j, sink: (0, block_of(o, j), 0))

    def w_col(o, j, sink):
        return (jnp.where(o == 0, 0, jnp.where(o <= ni, j, nj - 1)), 0, 0)

    def finished_tile(o, j, sink):
        return (jnp.maximum(o - 2, 0), jnp.where(o >= 2, j, 0))

    grid_spec = pltpu.PrefetchScalarGridSpec(
        num_scalar_prefetch=1,
        grid=(ni + 2, nj),
        in_specs=[rows_spec(ATT_WIDTH, 0),
                  pl.BlockSpec((BLOCK, KV_WIDTH), lambda o, j, sink: (0, COL_K512)),
                  rows_spec(KV_WIDTH, COL_K512, 1),
                  rows_spec(KV_WIDTH, COL_V512, -1), rows_spec(KV_WIDTH, COL_V512), rows_spec(KV_WIDTH, COL_V512, 1),
                  rows_spec(1024, COL_ZATT), rows_spec(1024, COL_ZATT + 1),
                  const_spec((1, HEAD_DIM)), const_spec((1, ATT_WIDTH)),
                  rows_spec(MEM_WIDTH, COL_QMEM), const_spec((N_MEM, 2 * MEM_WIDTH)),
                  rows_spec(MEM_WIDTH, COL_ZMEM), const_spec((1, MEM_WIDTH)),
                  slab_spec, slab_spec, rows_spec(LRU_WIDTH, COL_ZLRU), const_spec((1, LRU_WIDTH)),
                  pl.BlockSpec((None, D_MODEL, tn), w_col),
                  pl.BlockSpec((tm, tn), finished_tile),
                  pl.BlockSpec((1, tn), lambda o, j, sink: (0, j))],
        out_specs=pl.BlockSpec((tm, tn), finished_tile),
        scratch_shapes=[pltpu.VMEM((2, tm, D_MODEL), BF16),
                        pltpu.VMEM((2, nj, tm, tn), BF16),
                        pltpu.VMEM((2, tm, LANES), F32),
                        pltpu.VMEM((3, BLOCK, KV_WIDTH), BF16),
                        pltpu.VMEM((3, BLOCK, ATT_GROUP * BLOCK), F32),
                        pltpu.VMEM((BLOCK, HEAD_DIM), F32),
                        pltpu.VMEM((BLOCK, HEAD_DIM), F32),
                        pltpu.VMEM((BLOCK, ATT_WIDTH), F32),
                        pltpu.VMEM((BLOCK, MEM_WIDTH), F32)],
    )
    return pl.pallas_call(
        _tail_kernel,
        grid_spec=grid_spec,
        out_shape=jax.ShapeDtypeStruct((s_len, D_MODEL), F32),
        compiler_params=_params("arbitrary", "arbitrary"),
        name="mixers_out_proj_post_norm",
    )(sink, proj, proj, proj, proj, proj, proj, proj, proj, freq, g_att.reshape(1, ATT_WIDTH),
      proj, kv, proj, g_mem.reshape(1, MEM_WIDTH), h_fwd, h_bwd, proj, g_lru.reshape(1, LRU_WIDTH),
      w_out, x, post_g.reshape(1, D_MODEL))


def _layer(h, mem, pre_g, w_in, sink, conv_w, conv_b, w_a, b_a, w_x, b_x, lam, mem_g, w_mem_kv,
           g_att, g_lru, g_mem, w_out, post_g):
    proj, w_out_bf, w_mem_kv_bf = _in_proj(h, pre_g, w_in, w_out, w_mem_kv)
    kv = _matmul(_rmsnorm_cast(mem, mem_g), w_mem_kv_bf, BF16, "mem_kv_proj")
    h_fwd, h_bwd = _lru(proj, conv_w, conv_b, w_a, b_a, w_x, b_x, lam)
    return _tail(proj, kv, h_fwd, h_bwd, sink, g_att, g_mem, g_lru, w_out_bf, h, post_g)


def kernel(x, mem, pre_norm_gain, w_in, att_sink, conv_w, conv_b, lru_w_a, lru_b_a, lru_w_x, lru_b_x,
           lru_lambda, mem_norm_gain, w_mem_kv, att_out_gain, lru_out_gain, mem_out_gain, w_out,
           post_norm_gain):
    batch = x.shape[0]
    depth = w_in.shape[0]
    outs = []
    for b in range(batch):
        h = x[b]
        for l in range(depth):
            h = _layer(h, mem[b], pre_norm_gain[l], w_in[l], att_sink[l], conv_w[l], conv_b[l],
                       lru_w_a[l], lru_b_a[l], lru_w_x[l], lru_b_x[l], lru_lambda[l],
                       mem_norm_gain[l], w_mem_kv[l], att_out_gain[l], lru_out_gain[l],
                       mem_out_gain[l], w_out[l], post_norm_gain[l])
        outs.append(h)
    return jnp.stack(outs, axis=0)
```

```python
import functools
import math

import jax
import jax.numpy as jnp
from jax import lax
from jax.experimental import pallas as pl
from jax.experimental.pallas import tpu as pltpu

F32 = jnp.float32
BF16 = jnp.bfloat16

D_MODEL = 4096
SEQ = 8192
N_MEM = 256
HEAD_DIM = 128
ATT_WIDTH = 2048
ATT_HEADS = 16
ATT_KV_HEADS = 4
ATT_GROUP = ATT_HEADS // ATT_KV_HEADS
KV_WIDTH = ATT_KV_HEADS * HEAD_DIM
BLOCK = 128
ROPE_THETA = 500000.0
ROT_DIM = HEAD_DIM // 4
ROT_HALF = ROT_DIM // 2
LRU_WIDTH = 1024
LRU_BLOCKS = 8
LRU_BLOCK_DIM = LRU_WIDTH // LRU_BLOCKS
CONV_WIDTH = 4
CONV_LEFT = 2
LRU_C = 8.0
MEM_HEADS = 4
MEM_WIDTH = 1024
MEM_HEAD_DIM = MEM_WIDTH // MEM_HEADS
EPS = 1e-6
IN_WIDTH = 9216
MASK_BIAS = -1e30
F32_TINY = float(jnp.finfo(jnp.float32).tiny)

COL_XLRU = 3
COL_QMEM = 4
COL_ZATT = 5
COL_ZLRU = 7
COL_ZMEM = 8
COL_K512 = 4
COL_V512 = 5

VMEM_LIMIT_BYTES = 60000 * 1024
BF16_SUBLANES = 16

NORM_ROWS = 256
MM_TM = 1024
MM_TN = 1024
IN_TM = 1024
IN_TN = 512
IN_KC = 512
IN_KCHUNKS = D_MODEL // IN_KC
TAIL_TM = 512
TAIL_TN = 1024
LANES = 128
LRU_CHUNK = 512
LRU_SUBCHUNK_SEGS = (36, 28)
assert 8 * sum(LRU_SUBCHUNK_SEGS) == LRU_CHUNK and all(s % 8 == 4 for s in LRU_SUBCHUNK_SEGS)
MIX_ROWS = 512


def _params(*sem):
    return pltpu.CompilerParams(dimension_semantics=sem, vmem_limit_bytes=VMEM_LIMIT_BYTES)


def _sigmoid(t):
    return 1.0 / (1.0 + jnp.exp(-t))


def _norm_gate(y, g, z):
    ms = jnp.mean(y * y, axis=-1, keepdims=True)
    return y * lax.rsqrt(ms + EPS) * g * (z * _sigmoid(z))


def _rmsnorm_kernel(x_ref, g_ref, o_ref):
    x = x_ref[...]
    ms = jnp.mean(x * x, axis=-1, keepdims=True)
    o_ref[...] = (x * lax.rsqrt(ms + EPS) * g_ref[...]).astype(o_ref.dtype)


def _rmsnorm_cast(x, g):
    m, d = x.shape
    rows = min(NORM_ROWS, m)
    return pl.pallas_call(
        _rmsnorm_kernel,
        grid=(m // rows,),
        in_specs=[pl.BlockSpec((rows, d), lambda i: (i, 0)),
                  pl.BlockSpec((1, d), lambda i: (0, 0))],
        out_specs=pl.BlockSpec((rows, d), lambda i: (i, 0)),
        out_shape=jax.ShapeDtypeStruct((m, d), BF16),
        compiler_params=_params("parallel"),
        name="rmsnorm_cast",
    )(x, g.reshape(1, d))


def _matmul_kernel(a_ref, b_ref, o_ref):
    o_ref[...] = jnp.dot(a_ref[...], b_ref[...], preferred_element_type=F32).astype(o_ref.dtype)


def _matmul(a, b, out_dtype, name):
    m, k = a.shape
    _, n = b.shape
    tm = min(MM_TM, m)
    tn = min(MM_TN, n)
    return pl.pallas_call(
        _matmul_kernel,
        grid=(m // tm, n // tn),
        in_specs=[pl.BlockSpec((tm, k), lambda i, j: (i, 0)),
                  pl.BlockSpec((k, tn), lambda i, j: (0, j))],
        out_specs=pl.BlockSpec((tm, tn), lambda i, j: (i, j)),
        out_shape=jax.ShapeDtypeStruct((m, n), out_dtype),
        compiler_params=_params("parallel", "parallel"),
        name=name,
    )(a, b)


def _in_proj_kernel(x_ref, g_ref, w_ref, wo_ref, wk_ref, proj_ref, wo_bf_ref, wk_bf_ref, u_scr, ss_scr):
    o = pl.program_id(0)
    j = pl.program_id(1)

    def normalise_next_tile():
        for c in range(D_MODEL // TAIL_TN):
            wo_bf_ref[c] = wo_ref[:, c * TAIL_TN:(c + 1) * TAIL_TN].astype(BF16)
        wk_bf_ref[...] = wk_ref[...].astype(BF16)
        x = x_ref[...]
        sq = x * x
        part = sq[:, 0:LANES]
        for c in range(1, IN_KC // LANES):
            part = part + sq[:, c * LANES:(c + 1) * LANES]
        ss_old = ss_scr[...]
        ss = jnp.where(j == 0, part, jnp.where(j < IN_KCHUNKS, ss_old + part, ss_old))
        ss_scr[...] = ss
        r = lax.rsqrt(jnp.sum(ss, axis=-1, keepdims=True) * (1.0 / D_MODEL) + EPS)
        chunk = jnp.clip(j - IN_KCHUNKS, 0, IN_KCHUNKS - 1)
        u_scr[lax.rem(o, 2), chunk] = (x * r * g_ref[...]).astype(BF16)

    @pl.when(o == 0)
    def _():
        @pl.when(j == 0)
        def _():
            ss_scr[...] = jnp.zeros_like(ss_scr)

        normalise_next_tile()

    @pl.when(o > 0)
    def _():
        slot = lax.rem(o + 1, 2)
        acc = None
        for c in range(IN_KCHUNKS):
            d = jnp.dot(u_scr[slot, c], w_ref[c * IN_KC:(c + 1) * IN_KC, :].astype(BF16),
                        preferred_element_type=F32)
            acc = d if acc is None else acc + d
        proj_ref[...] = acc.astype(proj_ref.dtype)
        normalise_next_tile()


def _in_proj(x, pre_g, w_in, w_out, w_mem_kv):
    s_len = x.shape[0]
    ni, nj = s_len // IN_TM, IN_WIDTH // IN_TN
    n_steps = (ni + 1) * nj
    side_rows = 32
    n_side = D_MODEL // side_rows
    assert n_side <= n_steps and nj >= 2 * IN_KCHUNKS

    def x_tile(o, j):
        return (jnp.minimum(o, ni - 1), jnp.where(j < 2 * IN_KCHUNKS, lax.rem(j, IN_KCHUNKS), IN_KCHUNKS - 1))

    def side(o, j):
        return (jnp.minimum(o * nj + j, n_side - 1), 0)

    return pl.pallas_call(
        _in_proj_kernel,
        grid=(ni + 1, nj),
        in_specs=[pl.BlockSpec((IN_TM, IN_KC), x_tile),
                  pl.BlockSpec((1, IN_KC), lambda o, j: (0, x_tile(o, j)[1])),
                  pl.BlockSpec((D_MODEL, IN_TN), lambda o, j: (0, jnp.where(o > 0, j, 0))),
                  pl.BlockSpec((side_rows, D_MODEL), side),
                  pl.BlockSpec((side_rows, 2 * MEM_WIDTH), side)],
        out_specs=[pl.BlockSpec((IN_TM, IN_TN), lambda o, j: (jnp.maximum(o - 1, 0), jnp.where(o > 0, j, 0))),
                   pl.BlockSpec((D_MODEL // TAIL_TN, side_rows, TAIL_TN), lambda o, j: (0, side(o, j)[0], 0)),
                   pl.BlockSpec((side_rows, 2 * MEM_WIDTH), side)],
        out_shape=[jax.ShapeDtypeStruct((s_len, IN_WIDTH), BF16),
                   jax.ShapeDtypeStruct((D_MODEL // TAIL_TN, D_MODEL, TAIL_TN), BF16),
                   jax.ShapeDtypeStruct((D_MODEL, 2 * MEM_WIDTH), BF16)],
        scratch_shapes=[pltpu.VMEM((2, IN_KCHUNKS, IN_TM, IN_KC), BF16),
                        pltpu.VMEM((IN_TM, LANES), F32)],
        compiler_params=_params("arbitrary", "arbitrary"),
        name="in_proj_pre_norm",
    )(x, pre_g.reshape(1, D_MODEL), w_in, w_out, w_mem_kv)


def _rope_tables(block_idx, freq, base_cos, base_sin, mult):
    a = (block_idx * BLOCK).astype(F32) * freq
    ca, sa = jnp.cos(a) * mult, jnp.sin(a) * mult
    c = ca * base_cos - sa * base_sin
    s = sa * base_cos + ca * base_sin
    lane = lax.broadcasted_iota(jnp.int32, (BLOCK, HEAD_DIM), 1)
    return c, jnp.where(lane < ROT_HALF, -s, 0.0), jnp.where(lane >= ROT_HALF, s, 0.0)


def _rope(t, tables):
    c, sin_lo, sin_hi = tables
    return (t * c + pltpu.roll(t, HEAD_DIM - ROT_HALF, axis=1) * sin_lo
            + pltpu.roll(t, ROT_HALF, axis=1) * sin_hi)


def _rope_keys(k_ref, tables):
    return jnp.concatenate(
        [_rope(k_ref[:, h * HEAD_DIM:(h + 1) * HEAD_DIM].astype(F32), tables).astype(BF16)
         for h in range(ATT_KV_HEADS)], axis=1)


def _attn_init(k0_ref, freq_ref, k_ring, bias_scr, cos_scr, sin_scr):
    freq = freq_ref[...]
    rows = ATT_GROUP * BLOCK
    ang = lax.broadcasted_iota(jnp.int32, (BLOCK, HEAD_DIM), 0).astype(F32) * freq
    cos_scr[...] = jnp.cos(ang)
    sin_scr[...] = jnp.sin(ang)
    key = lax.broadcasted_iota(jnp.int32, (BLOCK, rows), 0)
    q_local = lax.broadcasted_iota(jnp.int32, (BLOCK, rows), 1) & (BLOCK - 1)
    bias_scr[0] = jnp.where(key >= q_local, 0.0, MASK_BIAS)
    bias_scr[1] = jnp.where(key <= q_local, 0.0, MASK_BIAS)
    bias_scr[2] = jnp.full((BLOCK, rows), MASK_BIAS, F32)
    k_ring[0] = _rope_keys(k0_ref, _rope_tables(jnp.int32(0), freq, cos_scr[...], sin_scr[...], 1.0))
    k_ring[2] = jnp.zeros((BLOCK, KV_WIDTH), BF16)


class _AttnBlock:
    def __init__(self, n, nb, sink_ref, q_ref, kn_ref, vp_ref, vc_ref, vn_ref, freq_ref,
                 k_ring, bias_scr, cos_scr, sin_scr, y_scr):
        self.sink_ref, self.q_ref, self.v_refs, self.k_ring, self.y_scr = (
            sink_ref, q_ref, (vp_ref, vc_ref, vn_ref), k_ring, y_scr)
        freq = freq_ref[...]
        rows = ATT_GROUP * BLOCK
        self.log2e = math.log2(math.e)
        base_cos = cos_scr[...]
        base_sin = sin_scr[...]
        self.slots = (lax.rem(n + 2, 3), lax.rem(n, 3), lax.rem(n + 1, 3))
        k_ring[self.slots[2]] = _rope_keys(kn_ref, _rope_tables(n + 1, freq, base_cos, base_sin, 1.0))
        self.tab_q = _rope_tables(n, freq, base_cos, base_sin, HEAD_DIM ** -0.5 * self.log2e)
        self.bias_p = bias_scr[jnp.where(n > 0, 0, 2)]
        self.bias_n = bias_scr[jnp.where(n < nb - 1, 1, 2)]
        self.lane_group = lax.broadcasted_iota(jnp.int32, (1, rows), 1) // BLOCK
        self.first_row = lax.broadcasted_iota(jnp.int32, (BF16_SUBLANES, rows), 0) == 0
        self.ones = jnp.ones((3 * BLOCK + BF16_SUBLANES, HEAD_DIM), BF16)

    def scores(self, h):
        ks = slice(h * HEAD_DIM, (h + 1) * HEAD_DIM)
        k_all = jnp.concatenate([self.k_ring[s, :, ks] for s in self.slots], axis=0)
        q_all = jnp.concatenate(
            [_rope(self.q_ref[:, (h * ATT_GROUP + g) * HEAD_DIM:(h * ATT_GROUP + g + 1) * HEAD_DIM]
                   .astype(F32), self.tab_q).astype(BF16) for g in range(ATT_GROUP)], axis=0)
        return lax.dot_general(k_all, q_all, (((1,), (1,)), ((), ())), preferred_element_type=F32)

    def values(self, h, s):
        ks = slice(h * HEAD_DIM, (h + 1) * HEAD_DIM)
        sink = jnp.zeros((1, ATT_GROUP * BLOCK), F32)
        for g in range(ATT_GROUP):
            sink = jnp.where(self.lane_group == g, self.sink_ref[h * ATT_GROUP + g] * self.log2e, sink)
        s_p = s[:BLOCK] + self.bias_p
        s_c = s[BLOCK:2 * BLOCK]
        s_n = s[2 * BLOCK:] + self.bias_n
        m = jnp.max(jnp.maximum(jnp.maximum(s_p, s_c), s_n), axis=0, keepdims=True)
        m = jnp.maximum(m, sink)
        p_sink = jnp.where(self.first_row, jnp.exp2(sink - m), 0.0)
        p = jnp.concatenate([jnp.exp2(s_p - m), jnp.exp2(s_c - m), jnp.exp2(s_n - m), p_sink],
                            axis=0).astype(BF16)
        v_all = jnp.concatenate([r[:, ks] for r in self.v_refs]
                                + [jnp.zeros((BF16_SUBLANES, HEAD_DIM), BF16)], axis=0)
        v_aug = jnp.concatenate([v_all, self.ones], axis=1)
        o_aug = lax.dot_general(p, v_aug, (((0,), (0,)), ((), ())), preferred_element_type=F32)
        o = o_aug[:, :HEAD_DIM] / o_aug[:, HEAD_DIM:]
        for g in range(ATT_GROUP):
            hq = h * ATT_GROUP + g
            self.y_scr[:, hq * HEAD_DIM:(hq + 1) * HEAD_DIM] = o[g * BLOCK:(g + 1) * BLOCK]

    def result(self, z0_ref, z1_ref, g_ref):
        z = jnp.concatenate([z0_ref[...], z1_ref[...]], axis=1).astype(F32)
        return _norm_gate(self.y_scr[...], g_ref[...], z)


def _lru_direction(d, reverse, chunk, n_chunks, x_ref, xp_ref, xn_ref, cw_ref, cb_ref, wg_ref,
                   bg_ref, lam_ref, out_ref, xe_scr, carry_scr):
    t = LRU_CHUNK
    halo = BF16_SUBLANES
    for blk in range(LRU_BLOCKS):
        cs = slice(blk * LRU_BLOCK_DIM, (blk + 1) * LRU_BLOCK_DIM)
        xe_scr[d, blk, 0:halo, :] = jnp.where(chunk > 0, xp_ref[:, cs].astype(F32), 0.0)
        xe_scr[d, blk, halo:halo + t, :] = x_ref[:, cs].astype(F32)
        xe_scr[d, blk, halo + t:, :] = jnp.where(chunk < n_chunks - 1, xn_ref[:, cs].astype(F32), 0.0)

    neg_lam = -lam_ref[d:d + 1, :]
    softplus = jnp.maximum(neg_lam, 0.0) + jnp.log(1.0 + jnp.exp(-jnp.abs(neg_lam)))
    half_decay = (-0.5 * LRU_C * math.log2(math.e)) * softplus
    row0 = [8 * sum(LRU_SUBCHUNK_SEGS[:k]) for k in range(len(LRU_SUBCHUNK_SEGS))]
    subchunks = list(zip(row0, LRU_SUBCHUNK_SEGS))
    for blk in range(LRU_BLOCKS):
        cs = slice(blk * LRU_BLOCK_DIM, (blk + 1) * LRU_BLOCK_DIM)
        taps = [jnp.broadcast_to(cw_ref[j:j + 1, cs], (8, LRU_BLOCK_DIM)) for j in range(CONV_WIDTH)]
        conv_bias = jnp.broadcast_to(cb_ref[:, cs], (8, LRU_BLOCK_DIM))
        hd = half_decay[:, cs]
        cin = carry_scr[d:d + 1, cs]
        for r0, seg in (reversed(subchunks) if reverse else subchunks):
            order = range(seg - 1, -1, -1) if reverse else range(seg)
            tiles = []
            for m in range(seg):
                acc = conv_bias
                for j in range(CONV_WIDTH):
                    start = halo + r0 + m + j - CONV_LEFT
                    acc = acc + taps[j] * xe_scr[d, blk, pl.ds(start, 8, stride=seg), :]
                tiles.append(acc)
            xc = jnp.concatenate(tiles, axis=0)
            gates = jnp.dot(xc.astype(BF16), wg_ref[d, blk], preferred_element_type=F32)
            t_r = jnp.tanh(gates[:, :LRU_BLOCK_DIM] + bg_ref[d, 0:1, cs])
            t_i = jnp.tanh(gates[:, LRU_BLOCK_DIM:] + bg_ref[d, 1:2, cs])
            a = jnp.exp2(hd * t_r + hd)
            w = 1.0 - a * a
            b = w * lax.rsqrt(jnp.maximum(w, F32_TINY)) * ((t_i + 1.0) * 0.5) * xc
            tile = lambda v, m: v[8 * m:8 * m + 8]
            hh = jnp.zeros((8, LRU_BLOCK_DIM), F32)
            pp = jnp.ones((8, LRU_BLOCK_DIM), F32)
            for m in order:
                hh = tile(a, m) * hh + tile(b, m)
                pp = tile(a, m) * pp
            entering = [None] * 8
            for s in (range(7, -1, -1) if reverse else range(8)):
                entering[s] = cin
                cin = pp[s:s + 1, :] * cin + hh[s:s + 1, :]
            h = jnp.concatenate(entering, axis=0)
            for m in order:
                h = tile(a, m) * h + tile(b, m)
                out_ref[blk, pl.ds(r0 + m, 8, stride=seg), :] = h
        carry_scr[d:d + 1, cs] = cin


def _lru_kernel(xf_ref, xfp_ref, xfn_ref, xb_ref, xbp_ref, xbn_ref, cw_ref, cb_ref, wg_ref, bg_ref,
                lam_ref, hf_ref, hb_ref, xe_scr, carry_scr):
    i = pl.program_id(0)
    nc = pl.num_programs(0)

    @pl.when(i == 0)
    def _():
        carry_scr[...] = jnp.zeros_like(carry_scr)

    shared = (cw_ref, cb_ref, wg_ref, bg_ref, lam_ref)
    scr = (xe_scr, carry_scr)
    _lru_direction(0, False, i, nc, xf_ref, xfp_ref, xfn_ref, *shared, hf_ref, *scr)
    _lru_direction(1, True, nc - 1 - i, nc, xb_ref, xbp_ref, xbn_ref, *shared, hb_ref, *scr)


def _lru(proj, conv_w, conv_b, w_a, b_a, w_x, b_x, lam):
    s_len = proj.shape[0]
    nc = s_len // LRU_CHUNK
    halo_per_chunk = LRU_CHUNK // BF16_SUBLANES
    n_halo = s_len // BF16_SUBLANES
    wg = (0.5 * jnp.concatenate([w_a, w_x], axis=-1)).astype(BF16)
    bg = 0.5 * jnp.stack([b_a, b_x], axis=1)

    def fwd(i):
        return i

    def bwd(i):
        return nc - 1 - i

    def specs(chunk_of):
        return [
            pl.BlockSpec((LRU_CHUNK, LRU_WIDTH), lambda i: (chunk_of(i), COL_XLRU)),
            pl.BlockSpec((BF16_SUBLANES, LRU_WIDTH),
                         lambda i: (jnp.maximum(chunk_of(i) * halo_per_chunk - 1, 0), COL_XLRU)),
            pl.BlockSpec((BF16_SUBLANES, LRU_WIDTH),
                         lambda i: (jnp.minimum((chunk_of(i) + 1) * halo_per_chunk, n_halo - 1), COL_XLRU)),
        ]

    def whole(shape):
        return pl.BlockSpec(shape, lambda i: (0,) * len(shape))

    return pl.pallas_call(
        _lru_kernel,
        grid=(nc,),
        in_specs=specs(fwd) + specs(bwd) + [
            whole((CONV_WIDTH, LRU_WIDTH)), whole((1, LRU_WIDTH)),
            whole((2, LRU_BLOCKS, LRU_BLOCK_DIM, 2 * LRU_BLOCK_DIM)),
            whole((2, 2, LRU_WIDTH)), whole((2, LRU_WIDTH))],
        out_specs=[pl.BlockSpec((LRU_BLOCKS, LRU_CHUNK, LRU_BLOCK_DIM), lambda i: (0, i, 0)),
                   pl.BlockSpec((LRU_BLOCKS, LRU_CHUNK, LRU_BLOCK_DIM), lambda i: (0, nc - 1 - i, 0))],
        out_shape=[jax.ShapeDtypeStruct((LRU_BLOCKS, s_len, LRU_BLOCK_DIM), F32),
                   jax.ShapeDtypeStruct((LRU_BLOCKS, s_len, LRU_BLOCK_DIM), F32)],
        scratch_shapes=[pltpu.VMEM((2, LRU_BLOCKS, LRU_CHUNK + 2 * BF16_SUBLANES, LRU_BLOCK_DIM), F32),
                        pltpu.VMEM((2, LRU_WIDTH), F32)],
        compiler_params=_params("arbitrary"),
        name="rglru",
    )(proj, proj, proj, proj, proj, proj, conv_w, conv_b.reshape(1, LRU_WIDTH), wg, bg, lam)


def _mem_scores(h, q_ref, kv_ref):
    cs = slice(h * MEM_HEAD_DIM, (h + 1) * MEM_HEAD_DIM)
    return lax.dot_general(q_ref[:, cs], kv_ref[:, cs], (((1,), (1,)), ((), ())),
                           preferred_element_type=F32)


def _mem_values(h, s, kv_ref, y_scr):
    cs = slice(h * MEM_HEAD_DIM, (h + 1) * MEM_HEAD_DIM)
    vs = slice(MEM_WIDTH + h * MEM_HEAD_DIM, MEM_WIDTH + (h + 1) * MEM_HEAD_DIM)
    s = s * MEM_HEAD_DIM ** -0.5
    m = jnp.max(s, axis=-1, keepdims=True)
    p = jnp.exp(s - m)
    denom = jnp.sum(p, axis=-1, keepdims=True)
    y_scr[:, cs] = jnp.dot(p.astype(BF16), kv_ref[:, vs], preferred_element_type=F32) / denom


def _lru_finish(hf_ref, hb_ref, zl_ref, gl_ref):
    y_lru = jnp.concatenate([hf_ref[blk] + hb_ref[blk] for blk in range(LRU_BLOCKS)], axis=1)
    return _norm_gate(y_lru, gl_ref[...], zl_ref[...].astype(F32))


def _tail_kernel(sink_ref,
                 q_ref, k0_ref, kn_ref, vp_ref, vc_ref, vn_ref, z0_ref, z1_ref, freq_ref, ga_ref,
                 qm_ref, kv_ref, zm_ref, gm_ref, hf_ref, hb_ref, zl_ref, gl_ref,
                 w_ref, x_ref, gp_ref, o_ref,
                 lhs_scr, acc_scr, ss_scr, k_ring, bias_scr, cos_scr, sin_scr, ya_scr, ym_scr):
    o = pl.program_id(0)
    j = pl.program_id(1)
    nj = pl.num_programs(1)
    n_row_tiles = pl.num_programs(0) - 2
    nb = n_row_tiles * nj

    n_chunks = ATT_KV_HEADS
    chunk = TAIL_TN // n_chunks

    def stage_b_chunk(c, part):
        slot = lax.rem(o + 1, 2)
        cols = slice(c * chunk, (c + 1) * chunk)
        acc = jnp.dot(lhs_scr[slot], w_ref[:, cols], preferred_element_type=F32)
        acc_scr[slot, j, :, cols] = acc.astype(acc_scr.dtype)
        sq = acc * acc
        for k in range(chunk // LANES):
            piece = sq[:, k * LANES:(k + 1) * LANES]
            part = piece if part is None else part + piece
        return part

    def stages_ab(with_a, with_b):
        part = None
        if with_b:
            part = stage_b_chunk(0, part)
        if with_a:
            n = o * nj + j
            attn = _AttnBlock(n, nb, sink_ref, q_ref, kn_ref, vp_ref, vc_ref, vn_ref, freq_ref,
                              k_ring, bias_scr, cos_scr, sin_scr, ya_scr)
            s_att = [attn.scores(h) for h in range(ATT_KV_HEADS)]
            s_mem = [_mem_scores(h, qm_ref, kv_ref) for h in range(MEM_HEADS)]
        for pair in range(2):
            if with_b:
                part = stage_b_chunk(1 + pair, part)
            if with_a:
                for h in (2 * pair, 2 * pair + 1):
                    attn.values(h, s_att[h])
                    _mem_values(h, s_mem[h], kv_ref, ym_scr)
        if with_b:
            part = stage_b_chunk(3, part)
            slot = lax.rem(o + 1, 2)
            ss_scr[slot] = jnp.where(j == 0, part, ss_scr[slot] + part)
        if with_a:
            rows = pl.ds(pl.multiple_of(j * BLOCK, BLOCK), BLOCK)
            slot = lax.rem(o, 2)
            lhs_scr[slot, rows, 0:ATT_WIDTH] = attn.result(z0_ref, z1_ref, ga_ref).astype(BF16)
            lhs_scr[slot, rows, ATT_WIDTH:ATT_WIDTH + LRU_WIDTH] = _lru_finish(
                hf_ref, hb_ref, zl_ref, gl_ref).astype(BF16)
            lhs_scr[slot, rows, ATT_WIDTH + LRU_WIDTH:] = _norm_gate(
                ym_scr[...], gm_ref[...], zm_ref[...].astype(F32)).astype(BF16)

    def stage_c():
        slot = lax.rem(o, 2)
        ms = jnp.sum(ss_scr[slot], axis=-1, keepdims=True) * (1.0 / D_MODEL)
        o_ref[...] = x_ref[...] + acc_scr[slot, j].astype(F32) * lax.rsqrt(ms + EPS) * gp_ref[...]

    @pl.when(o == 0)
    def _():
        @pl.when(j == 0)
        def _():
            ss_scr[...] = jnp.zeros_like(ss_scr)
            _attn_init(k0_ref, freq_ref, k_ring, bias_scr, cos_scr, sin_scr)

        stages_ab(True, False)

    @pl.when(o == 1)
    def _():
        stages_ab(True, True)

    @pl.when((o >= 2) & (o < n_row_tiles))
    def _():
        stage_c()
        stages_ab(True, True)

    @pl.when(o == n_row_tiles)
    def _():
        stage_c()
        stages_ab(False, True)

    @pl.when(o == n_row_tiles + 1)
    def _():
        stage_c()


def _tail(proj, kv, h_fwd, h_bwd, sink, g_att, g_mem, g_lru, w_out, x, post_g):
    s_len = proj.shape[0]
    tm, tn = TAIL_TM, TAIL_TN
    ni, nj = s_len // tm, D_MODEL // tn
    nb = s_len // BLOCK
    assert tm == nj * BLOCK
    inv_freq = jnp.power(jnp.float32(ROPE_THETA),
                         -jnp.arange(ROT_HALF, dtype=F32) * 2.0 / ROT_DIM)
    freq = jnp.concatenate([inv_freq, inv_freq, jnp.zeros((HEAD_DIM - ROT_DIM,), F32)]).reshape(1, HEAD_DIM)

    def block_of(o, j):
        return jnp.where(o < ni, o * nj + j, nb - 1)

    def rows_spec(width, col, shift=0):
        return pl.BlockSpec((BLOCK, width),
                            lambda o, j, sink: (jnp.clip(block_of(o, j) + shift, 0, nb - 1), col))

    def const_spec(shape):
        return pl.BlockSpec(shape, lambda o, j, sink: (0,) * len(shape))

    slab_spec = pl.BlockSpec((LRU_BLOCKS, BLOCK, LRU_BLOCK_DIM), lambda o, j, sink: (0, block_of(o, j), 0))

    def w_col(o, j, sink):
        return (jnp.where(o == 0, 0, jnp.where(o <= ni, j, nj - 1)), 0, 0)

    def finished_tile(o, j, sink):
        return (jnp.maximum(o - 2, 0), jnp.where(o >= 2, j, 0))

    grid_spec = pltpu.PrefetchScalarGridSpec(
        num_scalar_prefetch=1,
        grid=(ni + 2, nj),
        in_specs=[rows_spec(ATT_WIDTH, 0),
                  pl.BlockSpec((BLOCK, KV_WIDTH), lambda o, j, sink: (0, COL_K512)),
                  rows_spec(KV_WIDTH, COL_K512, 1),
                  rows_spec(KV_WIDTH, COL_V512, -1), rows_spec(KV_WIDTH, COL_V512), rows_spec(KV_WIDTH, COL_V512, 1),
                  rows_spec(1024, COL_ZATT), rows_spec(1024, COL_ZATT + 1),
                  const_spec((1, HEAD_DIM)), const_spec((1, ATT_WIDTH)),
                  rows_spec(MEM_WIDTH, COL_QMEM), const_spec((N_MEM, 2 * MEM_WIDTH)),
                  rows_spec(MEM_WIDTH, COL_ZMEM), const_spec((1, MEM_WIDTH)),
                  slab_spec, slab_spec, rows_spec(LRU_WIDTH, COL_ZLRU), const_spec((1, LRU_WIDTH)),
                  pl.BlockSpec((None, D_MODEL, tn), w_col),
                  pl.BlockSpec((tm, tn), finished_tile),
                  pl.BlockSpec((1, tn), lambda o, j, sink: (0, j))],
        out_specs=pl.BlockSpec((tm, tn), finished_tile),
        scratch_shapes=[pltpu.VMEM((2, tm, D_MODEL), BF16),
                        pltpu.VMEM((2, nj, tm, tn), BF16),
                        pltpu.VMEM((2, tm, LANES), F32),
                        pltpu.VMEM((3, BLOCK, KV_WIDTH), BF16),
                        pltpu.VMEM((3, BLOCK, ATT_GROUP * BLOCK), F32),
                        pltpu.VMEM((BLOCK, HEAD_DIM), F32),
                        pltpu.VMEM((BLOCK, HEAD_DIM), F32),
                        pltpu.VMEM((BLOCK, ATT_WIDTH), F32),
                        pltpu.VMEM((BLOCK, MEM_WIDTH), F32)],
    )
    return pl.pallas_call(
        _tail_kernel,
        grid_spec=grid_spec,
        out_shape=jax.ShapeDtypeStruct((s_len, D_MODEL), F32),
        compiler_params=_params("arbitrary", "arbitrary"),
        name="mixers_out_proj_post_norm",
    )(sink, proj, proj, proj, proj, proj, proj, proj, proj, freq, g_att.reshape(1, ATT_WIDTH),
      proj, kv, proj, g_mem.reshape(1, MEM_WIDTH), h_fwd, h_bwd, proj, g_lru.reshape(1, LRU_WIDTH),
      w_out, x, post_g.reshape(1, D_MODEL))


def _layer(h, mem, pre_g, w_in, sink, conv_w, conv_b, w_a, b_a, w_x, b_x, lam, mem_g, w_mem_kv,
           g_att, g_lru, g_mem, w_out, post_g):
    proj, w_out_bf, w_mem_kv_bf = _in_proj(h, pre_g, w_in, w_out, w_mem_kv)
    kv = _matmul(_rmsnorm_cast(mem, mem_g), w_mem_kv_bf, BF16, "mem_kv_proj")
    h_fwd, h_bwd = _lru(proj, conv_w, conv_b, w_a, b_a, w_x, b_x, lam)
    return _tail(proj, kv, h_fwd, h_bwd, sink, g_att, g_mem, g_lru, w_out_bf, h, post_g)


def kernel(x, mem, pre_norm_gain, w_in, att_sink, conv_w, conv_b, lru_w_a, lru_b_a, lru_w_x, lru_b_x,
           lru_lambda, mem_norm_gain, w_mem_kv, att_out_gain, lru_out_gain, mem_out_gain, w_out,
           post_norm_gain):
    batch = x.shape[0]
    depth = w_in.shape[0]
    outs = []
    for b in range(batch):
        h = x[b]
        for l in range(depth):
            h = _layer(h, mem[b], pre_norm_gain[l], w_in[l], att_sink[l], conv_w[l], conv_b[l],
                       lru_w_a[l], lru_b_a[l], lru_w_x[l], lru_b_x[l], lru_lambda[l],
                       mem_norm_gain[l], w_mem_kv[l], att_out_gain[l], lru_out_gain[l],
                       mem_out_gain[l], w_out[l], post_norm_gain[l])
        outs.append(h)
    return jnp.stack(outs, axis=0)
```

```python
import math

import jax
import jax.numpy as jnp
from jax import lax
from jax.experimental import pallas as pl
from jax.experimental.pallas import tpu as pltpu

F32 = jnp.float32
BF16 = jnp.bfloat16

D_MODEL = 4096
N_MEM = 256
HEAD_DIM = 128
ATT_WIDTH = 2048
ATT_HEADS = 16
ATT_KV_HEADS = 4
ATT_GROUP = ATT_HEADS // ATT_KV_HEADS
KV_WIDTH = ATT_KV_HEADS * HEAD_DIM
BLOCK = 128
ROPE_THETA = 500000.0
ROT_DIM = HEAD_DIM // 4
ROT_HALF = ROT_DIM // 2
LRU_WIDTH = 1024
LRU_BLOCKS = 8
LRU_BLOCK_DIM = LRU_WIDTH // LRU_BLOCKS
CONV_WIDTH = 4
CONV_LEFT = 2
LRU_C = 8.0
MEM_HEADS = 4
MEM_WIDTH = 1024
MEM_HEAD_DIM = MEM_WIDTH // MEM_HEADS
EPS = 1e-6
IN_WIDTH = 9216
MASK_BIAS = -1e30
F32_TINY = float(jnp.finfo(jnp.float32).tiny)

COL_XLRU = 3
COL_QMEM = 4
COL_ZATT = 5
COL_ZLRU = 7
COL_ZMEM = 8
COL_K512 = 4
COL_V512 = 5

VMEM_LIMIT_BYTES = 60000 * 1024
BF16_SUBLANES = 16

MEM_KV_TN = 1024
IN_TM = 1024
IN_TN = 512
IN_KC = 512
IN_KCHUNKS = D_MODEL // IN_KC
TAIL_TM = 512
TAIL_TN = 1024
LANES = 128
LRU_CHUNK = 1024
LRU_SUBCHUNK_SEGS = (36, 28, 36, 28)
assert 8 * sum(LRU_SUBCHUNK_SEGS) == LRU_CHUNK and all(s % 8 == 4 for s in LRU_SUBCHUNK_SEGS)


def _params(*sem):
    return pltpu.CompilerParams(dimension_semantics=sem, vmem_limit_bytes=VMEM_LIMIT_BYTES)


def _sigmoid(t):
    return 1.0 / (1.0 + jnp.exp(-t))


def _norm_gate(y, g, z):
    ms = jnp.mean(y * y, axis=-1, keepdims=True)
    return y * lax.rsqrt(ms + EPS) * g * (z * _sigmoid(z))


def _mem_kv_kernel(mem_ref, g_ref, w_ref, o_ref):
    x = mem_ref[...]
    ms = jnp.mean(x * x, axis=-1, keepdims=True)
    u = (x * lax.rsqrt(ms + EPS) * g_ref[...]).astype(BF16)
    o_ref[...] = jnp.dot(u, w_ref[...], preferred_element_type=F32).astype(o_ref.dtype)


def _mem_kv(mem, g, w_mem_kv_bf):
    m, d = mem.shape
    n = w_mem_kv_bf.shape[1]
    return pl.pallas_call(
        _mem_kv_kernel,
        grid=(n // MEM_KV_TN,),
        in_specs=[pl.BlockSpec((m, d), lambda j: (0, 0)),
                  pl.BlockSpec((1, d), lambda j: (0, 0)),
                  pl.BlockSpec((d, MEM_KV_TN), lambda j: (0, j))],
        out_specs=pl.BlockSpec((m, MEM_KV_TN), lambda j: (0, j)),
        out_shape=jax.ShapeDtypeStruct((m, n), BF16),
        compiler_params=_params("parallel"),
        name="mem_kv_proj",
    )(mem, g.reshape(1, d), w_mem_kv_bf)


def _in_proj_kernel(x_ref, g_ref, w_ref, wo_ref, wk_ref, proj_ref, wo_bf_ref, wk_bf_ref, u_scr, ss_scr):
    o = pl.program_id(0)
    j = pl.program_id(1)

    def normalise_next_tile():
        for c in range(D_MODEL // TAIL_TN):
            wo_bf_ref[c] = wo_ref[:, c * TAIL_TN:(c + 1) * TAIL_TN].astype(BF16)
        wk_bf_ref[...] = wk_ref[...].astype(BF16)
        x = x_ref[...]
        sq = x * x
        part = sq[:, 0:LANES]
        for c in range(1, IN_KC // LANES):
            part = part + sq[:, c * LANES:(c + 1) * LANES]
        ss_old = ss_scr[...]
        ss = jnp.where(j == 0, part, jnp.where(j < IN_KCHUNKS, ss_old + part, ss_old))
        ss_scr[...] = ss
        r = lax.rsqrt(jnp.sum(ss, axis=-1, keepdims=True) * (1.0 / D_MODEL) + EPS)
        chunk = jnp.clip(j - IN_KCHUNKS, 0, IN_KCHUNKS - 1)
        u_scr[lax.rem(o, 2), chunk] = (x * r * g_ref[...]).astype(BF16)

    @pl.when(o == 0)
    def _():
        @pl.when(j == 0)
        def _():
            ss_scr[...] = jnp.zeros_like(ss_scr)

        normalise_next_tile()

    @pl.when(o > 0)
    def _():
        slot = lax.rem(o + 1, 2)
        acc = None
        for c in range(IN_KCHUNKS):
            d = jnp.dot(u_scr[slot, c], w_ref[c * IN_KC:(c + 1) * IN_KC, :].astype(BF16),
                        preferred_element_type=F32)
            acc = d if acc is None else acc + d
        proj_ref[...] = acc.astype(proj_ref.dtype)
        normalise_next_tile()


def _in_proj(x, pre_g, w_in, w_out, w_mem_kv):
    s_len = x.shape[0]
    ni, nj = s_len // IN_TM, IN_WIDTH // IN_TN
    n_steps = (ni + 1) * nj
    side_rows = 32
    n_side = D_MODEL // side_rows
    assert n_side <= n_steps and nj >= 2 * IN_KCHUNKS

    def x_tile(o, j):
        return (jnp.minimum(o, ni - 1), jnp.where(j < 2 * IN_KCHUNKS, lax.rem(j, IN_KCHUNKS), IN_KCHUNKS - 1))

    def side(o, j):
        return (jnp.minimum(o * nj + j, n_side - 1), 0)

    return pl.pallas_call(
        _in_proj_kernel,
        grid=(ni + 1, nj),
        in_specs=[pl.BlockSpec((IN_TM, IN_KC), x_tile),
                  pl.BlockSpec((1, IN_KC), lambda o, j: (0, x_tile(o, j)[1])),
                  pl.BlockSpec((D_MODEL, IN_TN), lambda o, j: (0, jnp.where(o > 0, j, 0))),
                  pl.BlockSpec((side_rows, D_MODEL), side),
                  pl.BlockSpec((side_rows, 2 * MEM_WIDTH), side)],
        out_specs=[pl.BlockSpec((IN_TM, IN_TN), lambda o, j: (jnp.maximum(o - 1, 0), jnp.where(o > 0, j, 0))),
                   pl.BlockSpec((D_MODEL // TAIL_TN, side_rows, TAIL_TN), lambda o, j: (0, side(o, j)[0], 0)),
                   pl.BlockSpec((side_rows, 2 * MEM_WIDTH), side)],
        out_shape=[jax.ShapeDtypeStruct((s_len, IN_WIDTH), BF16),
                   jax.ShapeDtypeStruct((D_MODEL // TAIL_TN, D_MODEL, TAIL_TN), BF16),
                   jax.ShapeDtypeStruct((D_MODEL, 2 * MEM_WIDTH), BF16)],
        scratch_shapes=[pltpu.VMEM((2, IN_KCHUNKS, IN_TM, IN_KC), BF16),
                        pltpu.VMEM((IN_TM, LANES), F32)],
        compiler_params=_params("arbitrary", "arbitrary"),
        name="in_proj_pre_norm",
    )(x, pre_g.reshape(1, D_MODEL), w_in, w_out, w_mem_kv)


def _rope_tables(block_idx, freq, base_cos, base_sin, mult):
    a = (block_idx * BLOCK).astype(F32) * freq
    ca, sa = jnp.cos(a) * mult, jnp.sin(a) * mult
    c = ca * base_cos - sa * base_sin
    s = sa * base_cos + ca * base_sin
    lane = lax.broadcasted_iota(jnp.int32, (BLOCK, HEAD_DIM), 1)
    return c, jnp.where(lane < ROT_HALF, -s, 0.0), jnp.where(lane >= ROT_HALF, s, 0.0)


def _rope(t, tables):
    c, sin_lo, sin_hi = tables
    return (t * c + pltpu.roll(t, HEAD_DIM - ROT_HALF, axis=1) * sin_lo
            + pltpu.roll(t, ROT_HALF, axis=1) * sin_hi)


def _rope_keys(k_ref, tables):
    return jnp.concatenate(
        [_rope(k_ref[:, h * HEAD_DIM:(h + 1) * HEAD_DIM].astype(F32), tables).astype(BF16)
         for h in range(ATT_KV_HEADS)], axis=1)


def _attn_init(k0_ref, freq_ref, k_ring, bias_scr, cos_scr, sin_scr):
    freq = freq_ref[...]
    rows = ATT_GROUP * BLOCK
    ang = lax.broadcasted_iota(jnp.int32, (BLOCK, HEAD_DIM), 0).astype(F32) * freq
    cos_scr[...] = jnp.cos(ang)
    sin_scr[...] = jnp.sin(ang)
    key = lax.broadcasted_iota(jnp.int32, (BLOCK, rows), 0)
    q_local = lax.broadcasted_iota(jnp.int32, (BLOCK, rows), 1) & (BLOCK - 1)
    bias_scr[0] = jnp.where(key >= q_local, 0.0, MASK_BIAS)
    bias_scr[1] = jnp.where(key <= q_local, 0.0, MASK_BIAS)
    bias_scr[2] = jnp.full((BLOCK, rows), MASK_BIAS, F32)
    k_ring[0] = _rope_keys(k0_ref, _rope_tables(jnp.int32(0), freq, cos_scr[...], sin_scr[...], 1.0))
    k_ring[2] = jnp.zeros((BLOCK, KV_WIDTH), BF16)


class _AttnBlock:
    def __init__(self, n, nb, sink_ref, q_ref, kn_ref, vp_ref, vc_ref, vn_ref, freq_ref,
                 k_ring, bias_scr, cos_scr, sin_scr, y_scr):
        self.sink_ref, self.q_ref, self.v_refs, self.k_ring, self.y_scr = (
            sink_ref, q_ref, (vp_ref, vc_ref, vn_ref), k_ring, y_scr)
        freq = freq_ref[...]
        rows = ATT_GROUP * BLOCK
        self.log2e = math.log2(math.e)
        base_cos = cos_scr[...]
        base_sin = sin_scr[...]
        self.slots = (lax.rem(n + 2, 3), lax.rem(n, 3), lax.rem(n + 1, 3))
        k_ring[self.slots[2]] = _rope_keys(kn_ref, _rope_tables(n + 1, freq, base_cos, base_sin, 1.0))
        self.tab_q = _rope_tables(n, freq, base_cos, base_sin, HEAD_DIM ** -0.5 * self.log2e)
        self.bias_p = bias_scr[jnp.where(n > 0, 0, 2)]
        self.bias_n = bias_scr[jnp.where(n < nb - 1, 1, 2)]
        self.lane_group = lax.broadcasted_iota(jnp.int32, (1, rows), 1) // BLOCK
        self.first_row = lax.broadcasted_iota(jnp.int32, (BF16_SUBLANES, rows), 0) == 0
        self.ones = jnp.ones((3 * BLOCK + BF16_SUBLANES, HEAD_DIM), BF16)

    def scores(self, h):
        ks = slice(h * HEAD_DIM, (h + 1) * HEAD_DIM)
        k_all = jnp.concatenate([self.k_ring[s, :, ks] for s in self.slots], axis=0)
        q_all = jnp.concatenate(
            [_rope(self.q_ref[:, (h * ATT_GROUP + g) * HEAD_DIM:(h * ATT_GROUP + g + 1) * HEAD_DIM]
                   .astype(F32), self.tab_q).astype(BF16) for g in range(ATT_GROUP)], axis=0)
        return lax.dot_general(k_all, q_all, (((1,), (1,)), ((), ())), preferred_element_type=F32)

    def values(self, h, s):
        ks = slice(h * HEAD_DIM, (h + 1) * HEAD_DIM)
        sink = jnp.zeros((1, ATT_GROUP * BLOCK), F32)
        for g in range(ATT_GROUP):
            sink = jnp.where(self.lane_group == g, self.sink_ref[h * ATT_GROUP + g] * self.log2e, sink)
        s_p = s[:BLOCK] + self.bias_p
        s_c = s[BLOCK:2 * BLOCK]
        s_n = s[2 * BLOCK:] + self.bias_n
        m = jnp.max(jnp.maximum(jnp.maximum(s_p, s_c), s_n), axis=0, keepdims=True)
        m = jnp.maximum(m, sink)
        p_sink = jnp.where(self.first_row, jnp.exp2(sink - m), 0.0)
        p = jnp.concatenate([jnp.exp2(s_p - m), jnp.exp2(s_c - m), jnp.exp2(s_n - m), p_sink],
                            axis=0).astype(BF16)
        v_all = jnp.concatenate([r[:, ks] for r in self.v_refs]
                                + [jnp.zeros((BF16_SUBLANES, HEAD_DIM), BF16)], axis=0)
        v_aug = jnp.concatenate([v_all, self.ones], axis=1)
        o_aug = lax.dot_general(p, v_aug, (((0,), (0,)), ((), ())), preferred_element_type=F32)
        o = o_aug[:, :HEAD_DIM] / o_aug[:, HEAD_DIM:]
        for g in range(ATT_GROUP):
            hq = h * ATT_GROUP + g
            self.y_scr[:, hq * HEAD_DIM:(hq + 1) * HEAD_DIM] = o[g * BLOCK:(g + 1) * BLOCK]

    def result(self, z0_ref, z1_ref, g_ref):
        z = jnp.concatenate([z0_ref[...], z1_ref[...]], axis=1).astype(F32)
        return _norm_gate(self.y_scr[...], g_ref[...], z)


def _lru_direction(d, reverse, chunk, n_chunks, x_ref, xp_ref, xn_ref, cw_ref, cb_ref, wg_ref,
                   bg_ref, lam_ref, out_ref, xe_scr, carry_scr):
    t = LRU_CHUNK
    halo = BF16_SUBLANES
    for blk in range(LRU_BLOCKS):
        cs = slice(blk * LRU_BLOCK_DIM, (blk + 1) * LRU_BLOCK_DIM)
        xe_scr[d, blk, 0:halo, :] = jnp.where(chunk > 0, xp_ref[:, cs].astype(F32), 0.0)
        xe_scr[d, blk, halo:halo + t, :] = x_ref[:, cs].astype(F32)
        xe_scr[d, blk, halo + t:, :] = jnp.where(chunk < n_chunks - 1, xn_ref[:, cs].astype(F32), 0.0)

    neg_lam = -lam_ref[d:d + 1, :]
    softplus = jnp.maximum(neg_lam, 0.0) + jnp.log(1.0 + jnp.exp(-jnp.abs(neg_lam)))
    half_decay = (-0.5 * LRU_C * math.log2(math.e)) * softplus
    row0 = [8 * sum(LRU_SUBCHUNK_SEGS[:k]) for k in range(len(LRU_SUBCHUNK_SEGS))]
    subchunks = list(zip(row0, LRU_SUBCHUNK_SEGS))
    for blk in range(LRU_BLOCKS):
        cs = slice(blk * LRU_BLOCK_DIM, (blk + 1) * LRU_BLOCK_DIM)
        taps = [jnp.broadcast_to(cw_ref[j:j + 1, cs], (8, LRU_BLOCK_DIM)) for j in range(CONV_WIDTH)]
        conv_bias = jnp.broadcast_to(cb_ref[:, cs], (8, LRU_BLOCK_DIM))
        hd = half_decay[:, cs]
        cin = carry_scr[d:d + 1, cs]
        for r0, seg in (reversed(subchunks) if reverse else subchunks):
            order = range(seg - 1, -1, -1) if reverse else range(seg)
            tiles = []
            for m in range(seg):
                acc = conv_bias
                for j in range(CONV_WIDTH):
                    start = halo + r0 + m + j - CONV_LEFT
                    acc = acc + taps[j] * xe_scr[d, blk, pl.ds(start, 8, stride=seg), :]
                tiles.append(acc)
            xc = jnp.concatenate(tiles, axis=0)
            gates = jnp.dot(xc.astype(BF16), wg_ref[d, blk], preferred_element_type=F32)
            t_r = jnp.tanh(gates[:, :LRU_BLOCK_DIM] + bg_ref[d, 0:1, cs])
            t_i = jnp.tanh(gates[:, LRU_BLOCK_DIM:] + bg_ref[d, 1:2, cs])
            a = jnp.exp2(hd * t_r + hd)
            w = 1.0 - a * a
            b = w * lax.rsqrt(jnp.maximum(w, F32_TINY)) * ((t_i + 1.0) * 0.5) * xc
            tile = lambda v, m: v[8 * m:8 * m + 8]
            hh = jnp.zeros((8, LRU_BLOCK_DIM), F32)
            pp = jnp.ones((8, LRU_BLOCK_DIM), F32)
            for m in order:
                hh = tile(a, m) * hh + tile(b, m)
                pp = tile(a, m) * pp
            entering = [None] * 8
            for s in (range(7, -1, -1) if reverse else range(8)):
                entering[s] = cin
                cin = pp[s:s + 1, :] * cin + hh[s:s + 1, :]
            h = jnp.concatenate(entering, axis=0)
            for m in order:
                h = tile(a, m) * h + tile(b, m)
                out_ref[blk, pl.ds(r0 + m, 8, stride=seg), :] = h
        carry_scr[d:d + 1, cs] = cin


def _lru_kernel(xf_ref, xfp_ref, xfn_ref, xb_ref, xbp_ref, xbn_ref, cw_ref, cb_ref, wg_ref, bg_ref,
                lam_ref, hf_ref, hb_ref, xe_scr, carry_scr):
    i = pl.program_id(0)
    nc = pl.num_programs(0)

    @pl.when(i == 0)
    def _():
        carry_scr[...] = jnp.zeros_like(carry_scr)

    shared = (cw_ref, cb_ref, wg_ref, bg_ref, lam_ref)
    scr = (xe_scr, carry_scr)
    _lru_direction(0, False, i, nc, xf_ref, xfp_ref, xfn_ref, *shared, hf_ref, *scr)
    _lru_direction(1, True, nc - 1 - i, nc, xb_ref, xbp_ref, xbn_ref, *shared, hb_ref, *scr)


def _lru(proj, conv_w, conv_b, w_a, b_a, w_x, b_x, lam):
    s_len = proj.shape[0]
    nc = s_len // LRU_CHUNK
    halo_per_chunk = LRU_CHUNK // BF16_SUBLANES
    n_halo = s_len // BF16_SUBLANES
    wg = (0.5 * jnp.concatenate([w_a, w_x], axis=-1)).astype(BF16)
    bg = 0.5 * jnp.stack([b_a, b_x], axis=1)

    def fwd(i):
        return i

    def bwd(i):
        return nc - 1 - i

    def specs(chunk_of):
        return [
            pl.BlockSpec((LRU_CHUNK, LRU_WIDTH), lambda i: (chunk_of(i), COL_XLRU)),
            pl.BlockSpec((BF16_SUBLANES, LRU_WIDTH),
                         lambda i: (jnp.maximum(chunk_of(i) * halo_per_chunk - 1, 0), COL_XLRU)),
            pl.BlockSpec((BF16_SUBLANES, LRU_WIDTH),
                         lambda i: (jnp.minimum((chunk_of(i) + 1) * halo_per_chunk, n_halo - 1), COL_XLRU)),
        ]

    def whole(shape):
        return pl.BlockSpec(shape, lambda i: (0,) * len(shape))

    return pl.pallas_call(
        _lru_kernel,
        grid=(nc,),
        in_specs=specs(fwd) + specs(bwd) + [
            whole((CONV_WIDTH, LRU_WIDTH)), whole((1, LRU_WIDTH)),
            whole((2, LRU_BLOCKS, LRU_BLOCK_DIM, 2 * LRU_BLOCK_DIM)),
            whole((2, 2, LRU_WIDTH)), whole((2, LRU_WIDTH))],
        out_specs=[pl.BlockSpec((LRU_BLOCKS, LRU_CHUNK, LRU_BLOCK_DIM), lambda i: (0, i, 0)),
                   pl.BlockSpec((LRU_BLOCKS, LRU_CHUNK, LRU_BLOCK_DIM), lambda i: (0, nc - 1 - i, 0))],
        out_shape=[jax.ShapeDtypeStruct((LRU_BLOCKS, s_len, LRU_BLOCK_DIM), F32),
                   jax.ShapeDtypeStruct((LRU_BLOCKS, s_len, LRU_BLOCK_DIM), F32)],
        scratch_shapes=[pltpu.VMEM((2, LRU_BLOCKS, LRU_CHUNK + 2 * BF16_SUBLANES, LRU_BLOCK_DIM), F32),
                        pltpu.VMEM((2, LRU_WIDTH), F32)],
        compiler_params=_params("arbitrary"),
        name="rglru",
    )(proj, proj, proj, proj, proj, proj, conv_w, conv_b.reshape(1, LRU_WIDTH), wg, bg, lam)


def _mem_scores(h, q_ref, kv_ref):
    cs = slice(h * MEM_HEAD_DIM, (h + 1) * MEM_HEAD_DIM)
    return lax.dot_general(q_ref[:, cs], kv_ref[:, cs], (((1,), (1,)), ((), ())),
                           preferred_element_type=F32)


def _mem_values(h, s, kv_ref, y_scr):
    cs = slice(h * MEM_HEAD_DIM, (h + 1) * MEM_HEAD_DIM)
    vs = slice(MEM_WIDTH + h * MEM_HEAD_DIM, MEM_WIDTH + (h + 1) * MEM_HEAD_DIM)
    s = s * MEM_HEAD_DIM ** -0.5
    m = jnp.max(s, axis=-1, keepdims=True)
    p = jnp.exp(s - m)
    denom = jnp.sum(p, axis=-1, keepdims=True)
    y_scr[:, cs] = jnp.dot(p.astype(BF16), kv_ref[:, vs], preferred_element_type=F32) / denom


def _lru_finish(hf_ref, hb_ref, zl_ref, gl_ref):
    y_lru = jnp.concatenate([hf_ref[blk] + hb_ref[blk] for blk in range(LRU_BLOCKS)], axis=1)
    return _norm_gate(y_lru, gl_ref[...], zl_ref[...].astype(F32))


def _tail_kernel(sink_ref,
                 q_ref, k0_ref, kn_ref, vp_ref, vc_ref, vn_ref, z0_ref, z1_ref, freq_ref, ga_ref,
                 qm_ref, kv_ref, zm_ref, gm_ref, hf_ref, hb_ref, zl_ref, gl_ref,
                 w_ref, x_ref, gp_ref, o_ref,
                 lhs_scr, acc_scr, ss_scr, k_ring, bias_scr, cos_scr, sin_scr, ya_scr, ym_scr):
    o = pl.program_id(0)
    j = pl.program_id(1)
    nj = pl.num_programs(1)
    n_row_tiles = pl.num_programs(0) - 2
    nb = n_row_tiles * nj

    n_chunks = ATT_KV_HEADS
    chunk = TAIL_TN // n_chunks

    def stage_b_chunk(c, part):
        slot = lax.rem(o + 1, 2)
        cols = slice(c * chunk, (c + 1) * chunk)
        acc = jnp.dot(lhs_scr[slot], w_ref[:, cols], preferred_element_type=F32)
        acc_scr[slot, j, :, cols] = acc.astype(acc_scr.dtype)
        sq = acc * acc
        for k in range(chunk // LANES):
            piece = sq[:, k * LANES:(k + 1) * LANES]
            part = piece if part is None else part + piece
        return part

    def stages_ab(with_a, with_b):
        part = None
        if with_b:
            part = stage_b_chunk(0, part)
        if with_a:
            n = o * nj + j
            attn = _AttnBlock(n, nb, sink_ref, q_ref, kn_ref, vp_ref, vc_ref, vn_ref, freq_ref,
                              k_ring, bias_scr, cos_scr, sin_scr, ya_scr)
            s_att = [attn.scores(h) for h in range(ATT_KV_HEADS)]
            s_mem = [_mem_scores(h, qm_ref, kv_ref) for h in range(MEM_HEADS)]
        for pair in range(2):
            if with_b:
                part = stage_b_chunk(1 + pair, part)
            if with_a:
                for h in (2 * pair, 2 * pair + 1):
                    attn.values(h, s_att[h])
                    _mem_values(h, s_mem[h], kv_ref, ym_scr)
        if with_b:
            part = stage_b_chunk(3, part)
            slot = lax.rem(o + 1, 2)
            ss_scr[slot] = jnp.where(j == 0, part, ss_scr[slot] + part)
        if with_a:
            rows = pl.ds(pl.multiple_of(j * BLOCK, BLOCK), BLOCK)
            slot = lax.rem(o, 2)
            lhs_scr[slot, rows, 0:ATT_WIDTH] = attn.result(z0_ref, z1_ref, ga_ref).astype(BF16)
            lhs_scr[slot, rows, ATT_WIDTH:ATT_WIDTH + LRU_WIDTH] = _lru_finish(
                hf_ref, hb_ref, zl_ref, gl_ref).astype(BF16)
            lhs_scr[slot, rows, ATT_WIDTH + LRU_WIDTH:] = _norm_gate(
                ym_scr[...], gm_ref[...], zm_ref[...].astype(F32)).astype(BF16)

    def stage_c():
        slot = lax.rem(o, 2)
        ms = jnp.sum(ss_scr[slot], axis=-1, keepdims=True) * (1.0 / D_MODEL)
        o_ref[...] = x_ref[...] + acc_scr[slot, j].astype(F32) * lax.rsqrt(ms + EPS) * gp_ref[...]

    @pl.when(o == 0)
    def _():
        @pl.when(j == 0)
        def _():
            ss_scr[...] = jnp.zeros_like(ss_scr)
            _attn_init(k0_ref, freq_ref, k_ring, bias_scr, cos_scr, sin_scr)

        stages_ab(True, False)

    @pl.when(o == 1)
    def _():
        stages_ab(True, True)

    @pl.when((o >= 2) & (o < n_row_tiles))
    def _():
        stage_c()
        stages_ab(True, True)

    @pl.when(o == n_row_tiles)
    def _():
        stage_c()
        stages_ab(False, True)

    @pl.when(o == n_row_tiles + 1)
    def _():
        stage_c()


def _tail(proj, kv, h_fwd, h_bwd, sink, g_att, g_mem, g_lru, w_out, x, post_g):
    s_len = proj.shape[0]
    tm, tn = TAIL_TM, TAIL_TN
    ni, nj = s_len // tm, D_MODEL // tn
    nb = s_len // BLOCK
    assert tm == nj * BLOCK
    inv_freq = jnp.power(jnp.float32(ROPE_THETA),
                         -jnp.arange(ROT_HALF, dtype=F32) * 2.0 / ROT_DIM)
    freq = jnp.concatenate([inv_freq, inv_freq, jnp.zeros((HEAD_DIM - ROT_DIM,), F32)]).reshape(1, HEAD_DIM)

    def block_of(o, j):
        return jnp.where(o < ni, o * nj + j, nb - 1)

    def rows_spec(width, col, shift=0):
        return pl.BlockSpec((BLOCK, width),
                            lambda o, j, sink: (jnp.clip(block_of(o, j) + shift, 0, nb - 1), col))

    def const_spec(shape):
        return pl.BlockSpec(shape, lambda o, j, sink: (0,) * len(shape))

    slab_spec = pl.BlockSpec((LRU_BLOCKS, BLOCK, LRU_BLOCK_DIM), lambda o, j, sink: (0, block_of(o, j), 0))

    def w_col(o, j, sink):
        return (jnp.where(o == 0, 0, jnp.where(o <= ni, j, nj - 1)), 0, 0)

    def finished_tile(o, j, sink):
        return (jnp.maximum(o - 2, 0), jnp.where(o >= 2, j, 0))

    grid_spec = pltpu.PrefetchScalarGridSpec(
        num_scalar_prefetch=1,
        grid=(ni + 2, nj),
        in_specs=[rows_spec(ATT_WIDTH, 0),
                  pl.BlockSpec((BLOCK, KV_WIDTH), lambda o, j, sink: (0, COL_K512)),
                  rows_spec(KV_WIDTH, COL_K512, 1),
                  rows_spec(KV_WIDTH, COL_V512, -1), rows_spec(KV_WIDTH, COL_V512), rows_spec(KV_WIDTH, COL_V512, 1),
                  rows_spec(1024, COL_ZATT), rows_spec(1024, COL_ZATT + 1),
                  const_spec((1, HEAD_DIM)), const_spec((1, ATT_WIDTH)),
                  rows_spec(MEM_WIDTH, COL_QMEM), const_spec((N_MEM, 2 * MEM_WIDTH)),
                  rows_spec(MEM_WIDTH, COL_ZMEM), const_spec((1, MEM_WIDTH)),
                  slab_spec, slab_spec, rows_spec(LRU_WIDTH, COL_ZLRU), const_spec((1, LRU_WIDTH)),
                  pl.BlockSpec((None, D_MODEL, tn), w_col),
                  pl.BlockSpec((tm, tn), finished_tile),
                  pl.BlockSpec((1, tn), lambda o, j, sink: (0, j))],
        out_specs=pl.BlockSpec((tm, tn), finished_tile),
        scratch_shapes=[pltpu.VMEM((2, tm, D_MODEL), BF16),
                        pltpu.VMEM((2, nj, tm, tn), BF16),
                        pltpu.VMEM((2, tm, LANES), F32),
                        pltpu.VMEM((3, BLOCK, KV_WIDTH), BF16),
                        pltpu.VMEM((3, BLOCK, ATT_GROUP * BLOCK), F32),
                        pltpu.VMEM((BLOCK, HEAD_DIM), F32),
                        pltpu.VMEM((BLOCK, HEAD_DIM), F32),
                        pltpu.VMEM((BLOCK, ATT_WIDTH), F32),
                        pltpu.VMEM((BLOCK, MEM_WIDTH), F32)],
    )
    return pl.pallas_call(
        _tail_kernel,
        grid_spec=grid_spec,
        out_shape=jax.ShapeDtypeStruct((s_len, D_MODEL), F32),
        compiler_params=_params("arbitrary", "arbitrary"),
        name="mixers_out_proj_post_norm",
    )(sink, proj, proj, proj, proj, proj, proj, proj, proj, freq, g_att.reshape(1, ATT_WIDTH),
      proj, kv, proj, g_mem.reshape(1, MEM_WIDTH), h_fwd, h_bwd, proj, g_lru.reshape(1, LRU_WIDTH),
      w_out, x, post_g.reshape(1, D_MODEL))


def _layer(h, mem, pre_g, w_in, sink, conv_w, conv_b, w_a, b_a, w_x, b_x, lam, mem_g, w_mem_kv,
           g_att, g_lru, g_mem, w_out, post_g):
    proj, w_out_bf, w_mem_kv_bf = _in_proj(h, pre_g, w_in, w_out, w_mem_kv)
    kv = _mem_kv(mem, mem_g, w_mem_kv_bf)
    h_fwd, h_bwd = _lru(proj, conv_w, conv_b, w_a, b_a, w_x, b_x, lam)
    return _tail(proj, kv, h_fwd, h_bwd, sink, g_att, g_mem, g_lru, w_out_bf, h, post_g)


def kernel(x, mem, pre_norm_gain, w_in, att_sink, conv_w, conv_b, lru_w_a, lru_b_a, lru_w_x, lru_b_x,
           lru_lambda, mem_norm_gain, w_mem_kv, att_out_gain, lru_out_gain, mem_out_gain, w_out,
           post_norm_gain):
    batch, s_len, d_model = x.shape
    depth = w_in.shape[0]
    outs = []
    for b in range(batch):
        h = x.reshape(s_len, d_model) if batch == 1 else x[b]
        m = mem.reshape(mem.shape[1], d_model) if batch == 1 else mem[b]
        for l in range(depth):
            h = _layer(h, m, pre_norm_gain[l], w_in[l], att_sink[l], conv_w[l], conv_b[l],
                       lru_w_a[l], lru_b_a[l], lru_w_x[l], lru_b_x[l], lru_lambda[l],
                       mem_norm_gain[l], w_mem_kv[l], att_out_gain[l], lru_out_gain[l],
                       mem_out_gain[l], w_out[l], post_norm_gain[l])
        outs.append(h)
    return outs[0].reshape(x.shape) if batch == 1 else jnp.stack(outs, axis=0)
```

```python
import math

import jax
import jax.numpy as jnp
from jax import lax
from jax.experimental import pallas as pl
from jax.experimental.pallas import tpu as pltpu

F32 = jnp.float32
BF16 = jnp.bfloat16

D_MODEL = 4096
N_MEM = 256
HEAD_DIM = 128
ATT_WIDTH = 2048
ATT_HEADS = 16
ATT_KV_HEADS = 4
ATT_GROUP = ATT_HEADS // ATT_KV_HEADS
KV_WIDTH = ATT_KV_HEADS * HEAD_DIM
BLOCK = 128
ROPE_THETA = 500000.0
ROT_DIM = HEAD_DIM // 4
ROT_HALF = ROT_DIM // 2
LRU_WIDTH = 1024
LRU_BLOCKS = 8
LRU_BLOCK_DIM = LRU_WIDTH // LRU_BLOCKS
CONV_WIDTH = 4
CONV_LEFT = 2
LRU_C = 8.0
MEM_HEADS = 4
MEM_WIDTH = 1024
MEM_HEAD_DIM = MEM_WIDTH // MEM_HEADS
EPS = 1e-6
IN_WIDTH = 9216
F32_MIN = float(jnp.finfo(jnp.float32).min)
F32_TINY = float(jnp.finfo(jnp.float32).tiny)

COL_XLRU = 3
COL_QMEM = 4
COL_ZATT = 5
COL_ZLRU = 7
COL_ZMEM = 8
COL_K512 = 4
COL_V512 = 5

VMEM_LIMIT_BYTES = 60000 * 1024
BF16_SUBLANES = 16

MEM_KV_TN = 1024
IN_TM = 1024
IN_TN = 512
IN_KC = 512
IN_KCHUNKS = D_MODEL // IN_KC
TAIL_TM = 512
TAIL_TN = 1024
LANES = 128
LRU_CHUNK = 512
LRU_SUBCHUNK_SEGS = (36, 28)
assert 8 * sum(LRU_SUBCHUNK_SEGS) == LRU_CHUNK and all(s % 8 == 4 for s in LRU_SUBCHUNK_SEGS)


def _params(*sem):
    return pltpu.CompilerParams(dimension_semantics=sem, vmem_limit_bytes=VMEM_LIMIT_BYTES)


def _sigmoid(t):
    return 1.0 / (1.0 + jnp.exp(-t))


def _norm_gate(y, g, z):
    ms = jnp.mean(y * y, axis=-1, keepdims=True)
    return y * lax.rsqrt(ms + EPS) * g * (z * _sigmoid(z))


def _mem_kv_kernel(mem_ref, g_ref, w_ref, o_ref):
    x = mem_ref[...]
    ms = jnp.mean(x * x, axis=-1, keepdims=True)
    u = (x * lax.rsqrt(ms + EPS) * g_ref[...]).astype(BF16)
    o_ref[...] = jnp.dot(u, w_ref[...], preferred_element_type=F32).astype(o_ref.dtype)


def _mem_kv(mem, g, w_mem_kv_bf):
    m, d = mem.shape
    n = w_mem_kv_bf.shape[1]
    return pl.pallas_call(
        _mem_kv_kernel,
        grid=(n // MEM_KV_TN,),
        in_specs=[pl.BlockSpec((m, d), lambda j: (0, 0)),
                  pl.BlockSpec((1, d), lambda j: (0, 0)),
                  pl.BlockSpec((d, MEM_KV_TN), lambda j: (0, j))],
        out_specs=pl.BlockSpec((m, MEM_KV_TN), lambda j: (0, j)),
        out_shape=jax.ShapeDtypeStruct((m, n), BF16),
        compiler_params=_params("parallel"),
        name="mem_kv_proj",
    )(mem, g.reshape(1, d), w_mem_kv_bf)


def _in_proj_kernel(x_ref, g_ref, w_ref, wo_ref, wk_ref, proj_ref, wo_bf_ref, wk_bf_ref, u_scr, ss_scr):
    o = pl.program_id(0)
    j = pl.program_id(1)

    def normalise_next_tile():
        for c in range(D_MODEL // TAIL_TN):
            wo_bf_ref[c] = wo_ref[:, c * TAIL_TN:(c + 1) * TAIL_TN].astype(BF16)
        wk_bf_ref[...] = wk_ref[...].astype(BF16)
        x = x_ref[...]
        sq = x * x
        part = sq[:, 0:LANES]
        for c in range(1, IN_KC // LANES):
            part = part + sq[:, c * LANES:(c + 1) * LANES]
        ss_old = ss_scr[...]
        ss = jnp.where(j == 0, part, jnp.where(j < IN_KCHUNKS, ss_old + part, ss_old))
        ss_scr[...] = ss
        r = lax.rsqrt(jnp.sum(ss, axis=-1, keepdims=True) * (1.0 / D_MODEL) + EPS)
        chunk = jnp.clip(j - IN_KCHUNKS, 0, IN_KCHUNKS - 1)
        u_scr[lax.rem(o, 2), chunk] = (x * r * g_ref[...]).astype(BF16)

    @pl.when(o == 0)
    def _():
        @pl.when(j == 0)
        def _():
            ss_scr[...] = jnp.zeros_like(ss_scr)

        normalise_next_tile()

    @pl.when(o > 0)
    def _():
        slot = lax.rem(o + 1, 2)
        acc = None
        for c in range(IN_KCHUNKS):
            d = jnp.dot(u_scr[slot, c], w_ref[c * IN_KC:(c + 1) * IN_KC, :].astype(BF16),
                        preferred_element_type=F32)
            acc = d if acc is None else acc + d
        proj_ref[...] = acc.astype(proj_ref.dtype)
        normalise_next_tile()


def _in_proj(x, pre_g, w_in, w_out, w_mem_kv):
    s_len = x.shape[0]
    ni, nj = s_len // IN_TM, IN_WIDTH // IN_TN
    n_steps = (ni + 1) * nj
    side_rows = 32
    n_side = D_MODEL // side_rows
    assert n_side <= n_steps and nj >= 2 * IN_KCHUNKS

    def x_tile(o, j):
        return (jnp.minimum(o, ni - 1), jnp.where(j < 2 * IN_KCHUNKS, lax.rem(j, IN_KCHUNKS), IN_KCHUNKS - 1))

    def side(o, j):
        return (jnp.minimum(o * nj + j, n_side - 1), 0)

    return pl.pallas_call(
        _in_proj_kernel,
        grid=(ni + 1, nj),
        in_specs=[pl.BlockSpec((IN_TM, IN_KC), x_tile),
                  pl.BlockSpec((1, IN_KC), lambda o, j: (0, x_tile(o, j)[1])),
                  pl.BlockSpec((D_MODEL, IN_TN), lambda o, j: (0, jnp.where(o > 0, j, 0))),
                  pl.BlockSpec((side_rows, D_MODEL), side),
                  pl.BlockSpec((side_rows, 2 * MEM_WIDTH), side)],
        out_specs=[pl.BlockSpec((IN_TM, IN_TN), lambda o, j: (jnp.maximum(o - 1, 0), jnp.where(o > 0, j, 0))),
                   pl.BlockSpec((D_MODEL // TAIL_TN, side_rows, TAIL_TN), lambda o, j: (0, side(o, j)[0], 0)),
                   pl.BlockSpec((side_rows, 2 * MEM_WIDTH), side)],
        out_shape=[jax.ShapeDtypeStruct((s_len, IN_WIDTH), BF16),
                   jax.ShapeDtypeStruct((D_MODEL // TAIL_TN, D_MODEL, TAIL_TN), BF16),
                   jax.ShapeDtypeStruct((D_MODEL, 2 * MEM_WIDTH), BF16)],
        scratch_shapes=[pltpu.VMEM((2, IN_KCHUNKS, IN_TM, IN_KC), BF16),
                        pltpu.VMEM((IN_TM, LANES), F32)],
        compiler_params=_params("arbitrary", "arbitrary"),
        name="in_proj_pre_norm",
    )(x, pre_g.reshape(1, D_MODEL), w_in, w_out, w_mem_kv)


def _rope_tables(block_idx, freq, base_cos, base_sin, mult):
    a = (block_idx * BLOCK).astype(F32) * freq
    ca, sa = jnp.cos(a) * mult, jnp.sin(a) * mult
    c = ca * base_cos - sa * base_sin
    s = sa * base_cos + ca * base_sin
    lane = lax.broadcasted_iota(jnp.int32, (BLOCK, HEAD_DIM), 1)
    return c, jnp.where(lane < ROT_HALF, -s, 0.0), jnp.where(lane >= ROT_HALF, s, 0.0)


def _rope(t, tables):
    c, sin_lo, sin_hi = tables
    return (t * c + pltpu.roll(t, HEAD_DIM - ROT_HALF, axis=1) * sin_lo
            + pltpu.roll(t, ROT_HALF, axis=1) * sin_hi)


def _rope_keys(k_ref, tables):
    return jnp.concatenate(
        [_rope(k_ref[:, h * HEAD_DIM:(h + 1) * HEAD_DIM].astype(F32), tables).astype(BF16)
         for h in range(ATT_KV_HEADS)], axis=1)


def _attn_init(k0_ref, freq_ref, k_ring, bias_scr, cos_scr, sin_scr):
    freq = freq_ref[...]
    rows = ATT_GROUP * BLOCK
    ang = lax.broadcasted_iota(jnp.int32, (BLOCK, HEAD_DIM), 0).astype(F32) * freq
    cos_scr[...] = jnp.cos(ang)
    sin_scr[...] = jnp.sin(ang)
    key = lax.broadcasted_iota(jnp.int32, (BLOCK, rows), 0)
    q_local = lax.broadcasted_iota(jnp.int32, (BLOCK, rows), 1) & (BLOCK - 1)
    bias_scr[0] = jnp.where(key >= q_local, jnp.inf, F32_MIN)
    bias_scr[1] = jnp.where(key <= q_local, jnp.inf, F32_MIN)
    bias_scr[2] = jnp.full((BLOCK, rows), F32_MIN, F32)
    k_ring[0] = _rope_keys(k0_ref, _rope_tables(jnp.int32(0), freq, cos_scr[...], sin_scr[...], 1.0))
    k_ring[2] = jnp.zeros((BLOCK, KV_WIDTH), BF16)


class _AttnBlock:
    def __init__(self, n, nb, sink_ref, q_ref, kn_ref, vp_ref, vc_ref, vn_ref, freq_ref,
                 k_ring, bias_scr, cos_scr, sin_scr, y_scr):
        self.sink_ref, self.q_ref, self.v_refs, self.k_ring, self.y_scr = (
            sink_ref, q_ref, (vp_ref, vc_ref, vn_ref), k_ring, y_scr)
        freq = freq_ref[...]
        rows = ATT_GROUP * BLOCK
        self.log2e = math.log2(math.e)
        base_cos = cos_scr[...]
        base_sin = sin_scr[...]
        self.slots = (lax.rem(n + 2, 3), lax.rem(n, 3), lax.rem(n + 1, 3))
        k_ring[self.slots[2]] = _rope_keys(kn_ref, _rope_tables(n + 1, freq, base_cos, base_sin, 1.0))
        self.tab_q = _rope_tables(n, freq, base_cos, base_sin, HEAD_DIM ** -0.5 * self.log2e)
        self.bias_p = bias_scr[jnp.where(n > 0, 0, 2)]
        self.bias_n = bias_scr[jnp.where(n < nb - 1, 1, 2)]
        self.lane_group = lax.broadcasted_iota(jnp.int32, (1, rows), 1) // BLOCK
        self.first_row = lax.broadcasted_iota(jnp.int32, (BF16_SUBLANES, rows), 0) == 0
        self.ones = jnp.ones((3 * BLOCK + BF16_SUBLANES, HEAD_DIM), BF16)

    def scores(self, h):
        ks = slice(h * HEAD_DIM, (h + 1) * HEAD_DIM)
        k_all = jnp.concatenate([self.k_ring[s, :, ks] for s in self.slots], axis=0)
        q_all = jnp.concatenate(
            [_rope(self.q_ref[:, (h * ATT_GROUP + g) * HEAD_DIM:(h * ATT_GROUP + g + 1) * HEAD_DIM]
                   .astype(F32), self.tab_q).astype(BF16) for g in range(ATT_GROUP)], axis=0)
        return lax.dot_general(k_all, q_all, (((1,), (1,)), ((), ())), preferred_element_type=F32)

    def values(self, h, s):
        ks = slice(h * HEAD_DIM, (h + 1) * HEAD_DIM)
        sink = jnp.zeros((1, ATT_GROUP * BLOCK), F32)
        for g in range(ATT_GROUP):
            sink = jnp.where(self.lane_group == g, self.sink_ref[h * ATT_GROUP + g] * self.log2e, sink)
        s_p = jnp.minimum(s[:BLOCK], self.bias_p)
        s_c = s[BLOCK:2 * BLOCK]
        s_n = jnp.minimum(s[2 * BLOCK:], self.bias_n)
        m = jnp.max(jnp.maximum(jnp.maximum(s_p, s_c), s_n), axis=0, keepdims=True)
        m = jnp.maximum(m, sink)
        p_sink = jnp.where(self.first_row, jnp.exp2(sink - m), 0.0)
        p = jnp.concatenate([jnp.exp2(s_p - m), jnp.exp2(s_c - m), jnp.exp2(s_n - m), p_sink],
                            axis=0).astype(BF16)
        v_all = jnp.concatenate([r[:, ks] for r in self.v_refs]
                                + [jnp.zeros((BF16_SUBLANES, HEAD_DIM), BF16)], axis=0)
        v_aug = jnp.concatenate([v_all, self.ones], axis=1)
        o_aug = lax.dot_general(p, v_aug, (((0,), (0,)), ((), ())), preferred_element_type=F32)
        o = o_aug[:, :HEAD_DIM] / o_aug[:, HEAD_DIM:]
        for g in range(ATT_GROUP):
            hq = h * ATT_GROUP + g
            self.y_scr[:, hq * HEAD_DIM:(hq + 1) * HEAD_DIM] = o[g * BLOCK:(g + 1) * BLOCK]

    def result(self, z0_ref, z1_ref, g_ref):
        z = jnp.concatenate([z0_ref[...], z1_ref[...]], axis=1).astype(F32)
        return _norm_gate(self.y_scr[...], g_ref[...], z)


def _lru_direction(d, reverse, chunk, n_chunks, x_ref, xp_ref, xn_ref, cw_ref, cb_ref, wg_ref,
                   bg_ref, lam_ref, out_ref, xe_scr, carry_scr):
    t = LRU_CHUNK
    halo = BF16_SUBLANES
    for blk in range(LRU_BLOCKS):
        cs = slice(blk * LRU_BLOCK_DIM, (blk + 1) * LRU_BLOCK_DIM)
        xe_scr[d, blk, 0:halo, :] = jnp.where(chunk > 0, xp_ref[:, cs].astype(F32), 0.0)
        xe_scr[d, blk, halo:halo + t, :] = x_ref[:, cs].astype(F32)
        xe_scr[d, blk, halo + t:, :] = jnp.where(chunk < n_chunks - 1, xn_ref[:, cs].astype(F32), 0.0)

    neg_lam = -lam_ref[d:d + 1, :]
    softplus = jnp.maximum(neg_lam, 0.0) + jnp.log(1.0 + jnp.exp(-jnp.abs(neg_lam)))
    half_decay = (-0.5 * LRU_C * math.log2(math.e)) * softplus
    row0 = [8 * sum(LRU_SUBCHUNK_SEGS[:k]) for k in range(len(LRU_SUBCHUNK_SEGS))]
    subchunks = list(zip(row0, LRU_SUBCHUNK_SEGS))
    for blk in range(LRU_BLOCKS):
        cs = slice(blk * LRU_BLOCK_DIM, (blk + 1) * LRU_BLOCK_DIM)
        taps = [jnp.broadcast_to(cw_ref[j:j + 1, cs], (8, LRU_BLOCK_DIM)) for j in range(CONV_WIDTH)]
        conv_bias = jnp.broadcast_to(cb_ref[:, cs], (8, LRU_BLOCK_DIM))
        hd = half_decay[:, cs]
        cin = carry_scr[d:d + 1, cs]
        for r0, seg in (reversed(subchunks) if reverse else subchunks):
            order = range(seg - 1, -1, -1) if reverse else range(seg)
            tiles = []
            for m in range(seg):
                acc = conv_bias
                for j in range(CONV_WIDTH):
                    start = halo + r0 + m + j - CONV_LEFT
                    acc = acc + taps[j] * xe_scr[d, blk, pl.ds(start, 8, stride=seg), :]
                tiles.append(acc)
            xc = jnp.concatenate(tiles, axis=0)
            gates = jnp.dot(xc.astype(BF16), wg_ref[d, blk], preferred_element_type=F32)
            t_r = jnp.tanh(gates[:, :LRU_BLOCK_DIM] + bg_ref[d, 0:1, cs])
            t_i = jnp.tanh(gates[:, LRU_BLOCK_DIM:] + bg_ref[d, 1:2, cs])
            a = jnp.exp2(hd * t_r + hd)
            w = 1.0 - a * a
            b = w * lax.rsqrt(jnp.maximum(w, F32_TINY)) * ((t_i + 1.0) * 0.5) * xc
            tile = lambda v, m: v[8 * m:8 * m + 8]
            hh = jnp.zeros((8, LRU_BLOCK_DIM), F32)
            pp = jnp.ones((8, LRU_BLOCK_DIM), F32)
            for m in order:
                hh = tile(a, m) * hh + tile(b, m)
                pp = tile(a, m) * pp
            entering = [None] * 8
            for s in (range(7, -1, -1) if reverse else range(8)):
                entering[s] = cin
                cin = pp[s:s + 1, :] * cin + hh[s:s + 1, :]
            h = jnp.concatenate(entering, axis=0)
            for m in order:
                h = tile(a, m) * h + tile(b, m)
                out_ref[blk, pl.ds(r0 + m, 8, stride=seg), :] = h
        carry_scr[d:d + 1, cs] = cin


def _lru_kernel(xf_ref, xfp_ref, xfn_ref, xb_ref, xbp_ref, xbn_ref, cw_ref, cb_ref, wg_ref, bg_ref,
                lam_ref, hf_ref, hb_ref, xe_scr, carry_scr):
    i = pl.program_id(0)
    nc = pl.num_programs(0)

    @pl.when(i == 0)
    def _():
        carry_scr[...] = jnp.zeros_like(carry_scr)

    shared = (cw_ref, cb_ref, wg_ref, bg_ref, lam_ref)
    scr = (xe_scr, carry_scr)
    _lru_direction(0, False, i, nc, xf_ref, xfp_ref, xfn_ref, *shared, hf_ref, *scr)
    _lru_direction(1, True, nc - 1 - i, nc, xb_ref, xbp_ref, xbn_ref, *shared, hb_ref, *scr)


def _lru(proj, conv_w, conv_b, w_a, b_a, w_x, b_x, lam):
    s_len = proj.shape[0]
    nc = s_len // LRU_CHUNK
    halo_per_chunk = LRU_CHUNK // BF16_SUBLANES
    n_halo = s_len // BF16_SUBLANES
    wg = (0.5 * jnp.concatenate([w_a, w_x], axis=-1)).astype(BF16)
    bg = 0.5 * jnp.stack([b_a, b_x], axis=1)

    def fwd(i):
        return i

    def bwd(i):
        return nc - 1 - i

    def specs(chunk_of):
        return [
            pl.BlockSpec((LRU_CHUNK, LRU_WIDTH), lambda i: (chunk_of(i), COL_XLRU)),
            pl.BlockSpec((BF16_SUBLANES, LRU_WIDTH),
                         lambda i: (jnp.maximum(chunk_of(i) * halo_per_chunk - 1, 0), COL_XLRU)),
            pl.BlockSpec((BF16_SUBLANES, LRU_WIDTH),
                         lambda i: (jnp.minimum((chunk_of(i) + 1) * halo_per_chunk, n_halo - 1), COL_XLRU)),
        ]

    def whole(shape):
        return pl.BlockSpec(shape, lambda i: (0,) * len(shape))

    return pl.pallas_call(
        _lru_kernel,
        grid=(nc,),
        in_specs=specs(fwd) + specs(bwd) + [
            whole((CONV_WIDTH, LRU_WIDTH)), whole((1, LRU_WIDTH)),
            whole((2, LRU_BLOCKS, LRU_BLOCK_DIM, 2 * LRU_BLOCK_DIM)),
            whole((2, 2, LRU_WIDTH)), whole((2, LRU_WIDTH))],
        out_specs=[pl.BlockSpec((LRU_BLOCKS, LRU_CHUNK, LRU_BLOCK_DIM), lambda i: (0, i, 0)),
                   pl.BlockSpec((LRU_BLOCKS, LRU_CHUNK, LRU_BLOCK_DIM), lambda i: (0, nc - 1 - i, 0))],
        out_shape=[jax.ShapeDtypeStruct((LRU_BLOCKS, s_len, LRU_BLOCK_DIM), F32),
                   jax.ShapeDtypeStruct((LRU_BLOCKS, s_len, LRU_BLOCK_DIM), F32)],
        scratch_shapes=[pltpu.VMEM((2, LRU_BLOCKS, LRU_CHUNK + 2 * BF16_SUBLANES, LRU_BLOCK_DIM), F32),
                        pltpu.VMEM((2, LRU_WIDTH), F32)],
        compiler_params=_params("arbitrary"),
        name="rglru",
    )(proj, proj, proj, proj, proj, proj, conv_w, conv_b.reshape(1, LRU_WIDTH), wg, bg, lam)


def _mem_scores(h, q_ref, kv_ref):
    cs = slice(h * MEM_HEAD_DIM, (h + 1) * MEM_HEAD_DIM)
    return lax.dot_general(q_ref[:, cs], kv_ref[:, cs], (((1,), (1,)), ((), ())),
                           preferred_element_type=F32)


def _mem_values(h, s, kv_ref, y_scr):
    cs = slice(h * MEM_HEAD_DIM, (h + 1) * MEM_HEAD_DIM)
    vs = slice(MEM_WIDTH + h * MEM_HEAD_DIM, MEM_WIDTH + (h + 1) * MEM_HEAD_DIM)
    s = s * MEM_HEAD_DIM ** -0.5
    m = jnp.max(s, axis=-1, keepdims=True)
    p = jnp.exp(s - m)
    denom = jnp.sum(p, axis=-1, keepdims=True)
    y_scr[:, cs] = jnp.dot(p.astype(BF16), kv_ref[:, vs], preferred_element_type=F32) / denom


def _lru_finish(hf_ref, hb_ref, zl_ref, gl_ref):
    y_lru = jnp.concatenate([hf_ref[blk] + hb_ref[blk] for blk in range(LRU_BLOCKS)], axis=1)
    return _norm_gate(y_lru, gl_ref[...], zl_ref[...].astype(F32))


def _tail_kernel(sink_ref,
                 q_ref, k0_ref, kn_ref, vp_ref, vc_ref, vn_ref, z0_ref, z1_ref, freq_ref, ga_ref,
                 qm_ref, kv_ref, zm_ref, gm_ref, hf_ref, hb_ref, zl_ref, gl_ref,
                 w_ref, x_ref, gp_ref, o_ref,
                 lhs_scr, acc_scr, ss_scr, k_ring, bias_scr, cos_scr, sin_scr, ya_scr, ym_scr):
    o = pl.program_id(0)
    j = pl.program_id(1)
    nj = pl.num_programs(1)
    n_row_tiles = pl.num_programs(0) - 2
    nb = n_row_tiles * nj

    n_chunks = ATT_KV_HEADS
    chunk = TAIL_TN // n_chunks

    def stage_b_chunk(c, part):
        slot = lax.rem(o + 1, 2)
        cols = slice(c * chunk, (c + 1) * chunk)
        acc = jnp.dot(lhs_scr[slot], w_ref[:, cols], preferred_element_type=F32)
        acc_scr[slot, j, :, cols] = acc.astype(acc_scr.dtype)
        sq = acc * acc
        for k in range(chunk // LANES):
            piece = sq[:, k * LANES:(k + 1) * LANES]
            part = piece if part is None else part + piece
        return part

    def stages_ab(with_a, with_b):
        part = None
        if with_b:
            part = stage_b_chunk(0, part)
        if with_a:
            n = o * nj + j
            attn = _AttnBlock(n, nb, sink_ref, q_ref, kn_ref, vp_ref, vc_ref, vn_ref, freq_ref,
                              k_ring, bias_scr, cos_scr, sin_scr, ya_scr)
            s_att = [attn.scores(h) for h in range(ATT_KV_HEADS)]
            s_mem = [_mem_scores(h, qm_ref, kv_ref) for h in range(MEM_HEADS)]
        for pair in range(2):
            if with_b:
                part = stage_b_chunk(1 + pair, part)
            if with_a:
                for h in (2 * pair, 2 * pair + 1):
                    attn.values(h, s_att[h])
                    _mem_values(h, s_mem[h], kv_ref, ym_scr)
        if with_b:
            part = stage_b_chunk(3, part)
            slot = lax.rem(o + 1, 2)
            ss_scr[slot] = jnp.where(j == 0, part, ss_scr[slot] + part)
        if with_a:
            rows = pl.ds(pl.multiple_of(j * BLOCK, BLOCK), BLOCK)
            slot = lax.rem(o, 2)
            lhs_scr[slot, rows, 0:ATT_WIDTH] = attn.result(z0_ref, z1_ref, ga_ref).astype(BF16)
            lhs_scr[slot, rows, ATT_WIDTH:ATT_WIDTH + LRU_WIDTH] = _lru_finish(
                hf_ref, hb_ref, zl_ref, gl_ref).astype(BF16)
            lhs_scr[slot, rows, ATT_WIDTH + LRU_WIDTH:] = _norm_gate(
                ym_scr[...], gm_ref[...], zm_ref[...].astype(F32)).astype(BF16)

    def stage_c():
        slot = lax.rem(o, 2)
        ms = jnp.sum(ss_scr[slot], axis=-1, keepdims=True) * (1.0 / D_MODEL)
        o_ref[...] = x_ref[...] + acc_scr[slot, j].astype(F32) * lax.rsqrt(ms + EPS) * gp_ref[...]

    @pl.when(o == 0)
    def _():
        @pl.when(j == 0)
        def _():
            ss_scr[...] = jnp.zeros_like(ss_scr)
            _attn_init(k0_ref, freq_ref, k_ring, bias_scr, cos_scr, sin_scr)

        stages_ab(True, False)

    @pl.when(o == 1)
    def _():
        stages_ab(True, True)

    @pl.when((o >= 2) & (o < n_row_tiles))
    def _():
        stage_c()
        stages_ab(True, True)

    @pl.when(o == n_row_tiles)
    def _():
        stage_c()
        stages_ab(False, True)

    @pl.when(o == n_row_tiles + 1)
    def _():
        stage_c()


def _tail(proj, kv, h_fwd, h_bwd, sink, g_att, g_mem, g_lru, w_out, x, post_g):
    s_len = proj.shape[0]
    tm, tn = TAIL_TM, TAIL_TN
    ni, nj = s_len // tm, D_MODEL // tn
    nb = s_len // BLOCK
    assert tm == nj * BLOCK
    inv_freq = jnp.power(jnp.float32(ROPE_THETA),
                         -jnp.arange(ROT_HALF, dtype=F32) * 2.0 / ROT_DIM)
    freq = jnp.concatenate([inv_freq, inv_freq, jnp.zeros((HEAD_DIM - ROT_DIM,), F32)]).reshape(1, HEAD_DIM)

    def block_of(o, j):
        return jnp.where(o < ni, o * nj + j, nb - 1)

    def rows_spec(width, col, shift=0):
        return pl.BlockSpec((BLOCK, width),
                            lambda o, j, sink: (jnp.clip(block_of(o, j) + shift, 0, nb - 1), col))

    def const_spec(shape):
        return pl.BlockSpec(shape, lambda o, j, sink: (0,) * len(shape))

    slab_spec = pl.BlockSpec((LRU_BLOCKS, BLOCK, LRU_BLOCK_DIM), lambda o, j, sink: (0, block_of(o, j), 0))

    def w_col(o, j, sink):
        return (jnp.where(o == 0, 0, jnp.where(o <= ni, j, nj - 1)), 0, 0)

    def finished_tile(o, j, sink):
        return (jnp.maximum(o - 2, 0), jnp.where(o >= 2, j, 0))

    grid_spec = pltpu.PrefetchScalarGridSpec(
        num_scalar_prefetch=1,
        grid=(ni + 2, nj),
        in_specs=[rows_spec(ATT_WIDTH, 0),
                  pl.BlockSpec((BLOCK, KV_WIDTH), lambda o, j, sink: (0, COL_K512)),
                  rows_spec(KV_WIDTH, COL_K512, 1),
                  rows_spec(KV_WIDTH, COL_V512, -1), rows_spec(KV_WIDTH, COL_V512), rows_spec(KV_WIDTH, COL_V512, 1),
                  rows_spec(1024, COL_ZATT), rows_spec(1024, COL_ZATT + 1),
                  const_spec((1, HEAD_DIM)), const_spec((1, ATT_WIDTH)),
                  rows_spec(MEM_WIDTH, COL_QMEM), const_spec((N_MEM, 2 * MEM_WIDTH)),
                  rows_spec(MEM_WIDTH, COL_ZMEM), const_spec((1, MEM_WIDTH)),
                  slab_spec, slab_spec, rows_spec(LRU_WIDTH, COL_ZLRU), const_spec((1, LRU_WIDTH)),
                  pl.BlockSpec((None, D_MODEL, tn), w_col),
                  pl.BlockSpec((tm, tn), finished_tile),
                  pl.BlockSpec((1, tn), lambda o, j, sink: (0, j))],
        out_specs=pl.BlockSpec((tm, tn), finished_tile),
        scratch_shapes=[pltpu.VMEM((2, tm, D_MODEL), BF16),
                        pltpu.VMEM((2, nj, tm, tn), BF16),
                        pltpu.VMEM((2, tm, LANES), F32),
                        pltpu.VMEM((3, BLOCK, KV_WIDTH), BF16),
                        pltpu.VMEM((3, BLOCK, ATT_GROUP * BLOCK), F32),
                        pltpu.VMEM((BLOCK, HEAD_DIM), F32),
                        pltpu.VMEM((BLOCK, HEAD_DIM), F32),
                        pltpu.VMEM((BLOCK, ATT_WIDTH), F32),
                        pltpu.VMEM((BLOCK, MEM_WIDTH), F32)],
    )
    return pl.pallas_call(
        _tail_kernel,
        grid_spec=grid_spec,
        out_shape=jax.ShapeDtypeStruct((s_len, D_MODEL), F32),
        compiler_params=_params("arbitrary", "arbitrary"),
        name="mixers_out_proj_post_norm",
    )(sink, proj, proj, proj, proj, proj, proj, proj, proj, freq, g_att.reshape(1, ATT_WIDTH),
      proj, kv, proj, g_mem.reshape(1, MEM_WIDTH), h_fwd, h_bwd, proj, g_lru.reshape(1, LRU_WIDTH),
      w_out, x, post_g.reshape(1, D_MODEL))


def _layer(h, mem, pre_g, w_in, sink, conv_w, conv_b, w_a, b_a, w_x, b_x, lam, mem_g, w_mem_kv,
           g_att, g_lru, g_mem, w_out, post_g):
    proj, w_out_bf, w_mem_kv_bf = _in_proj(h, pre_g, w_in, w_out, w_mem_kv)
    kv = _mem_kv(mem, mem_g, w_mem_kv_bf)
    h_fwd, h_bwd = _lru(proj, conv_w, conv_b, w_a, b_a, w_x, b_x, lam)
    return _tail(proj, kv, h_fwd, h_bwd, sink, g_att, g_mem, g_lru, w_out_bf, h, post_g)


def kernel(x, mem, pre_norm_gain, w_in, att_sink, conv_w, conv_b, lru_w_a, lru_b_a, lru_w_x, lru_b_x,
           lru_lambda, mem_norm_gain, w_mem_kv, att_out_gain, lru_out_gain, mem_out_gain, w_out,
           post_norm_gain):
    batch, s_len, d_model = x.shape
    depth = w_in.shape[0]
    outs = []
    for b in range(batch):
        h = x.reshape(s_len, d_model) if batch == 1 else x[b]
        m = mem.reshape(mem.shape[1], d_model) if batch == 1 else mem[b]
        for l in range(depth):
            h = _layer(h, m, pre_norm_gain[l], w_in[l], att_sink[l], conv_w[l], conv_b[l],
                       lru_w_a[l], lru_b_a[l], lru_w_x[l], lru_b_x[l], lru_lambda[l],
                       mem_norm_gain[l], w_mem_kv[l], att_out_gain[l], lru_out_gain[l],
                       mem_out_gain[l], w_out[l], post_norm_gain[l])
        outs.append(h)
    return outs[0].reshape(x.shape) if batch == 1 else jnp.stack(outs, axis=0)
```

```python
import math

import jax
import jax.numpy as jnp
from jax import lax
from jax.experimental import pallas as pl
from jax.experimental.pallas import tpu as pltpu

F32 = jnp.float32
BF16 = jnp.bfloat16

D_MODEL = 4096
N_MEM = 256
HEAD_DIM = 128
ATT_WIDTH = 2048
ATT_HEADS = 16
ATT_KV_HEADS = 4
ATT_GROUP = ATT_HEADS // ATT_KV_HEADS
KV_WIDTH = ATT_KV_HEADS * HEAD_DIM
BLOCK = 128
ROPE_THETA = 500000.0
ROT_DIM = HEAD_DIM // 4
ROT_HALF = ROT_DIM // 2
LRU_WIDTH = 1024
LRU_BLOCKS = 8
LRU_BLOCK_DIM = LRU_WIDTH // LRU_BLOCKS
CONV_WIDTH = 4
CONV_LEFT = 2
LRU_C = 8.0
MEM_HEADS = 4
MEM_WIDTH = 1024
MEM_HEAD_DIM = MEM_WIDTH // MEM_HEADS
EPS = 1e-6
IN_WIDTH = 9216
F32_MIN = float(jnp.finfo(jnp.float32).min)
F32_TINY = float(jnp.finfo(jnp.float32).tiny)

COL_XLRU = 3
COL_QMEM = 4
COL_ZATT = 5
COL_ZLRU = 7
COL_ZMEM = 8
COL_K512 = 4
COL_V512 = 5

VMEM_LIMIT_BYTES = 60000 * 1024
BF16_SUBLANES = 16

MEM_KV_TN = 1024
IN_TM = 1024
IN_TN = 512
IN_KC = 512
IN_KCHUNKS = D_MODEL // IN_KC
TAIL_TM = 512
TAIL_TN = 1024
LANES = 128
LRU_CHUNK = 512
LRU_SUBCHUNK_SEGS = (36, 28)
assert 8 * sum(LRU_SUBCHUNK_SEGS) == LRU_CHUNK and all(s % 8 == 4 for s in LRU_SUBCHUNK_SEGS)


def _params(*sem):
    return pltpu.CompilerParams(dimension_semantics=sem, vmem_limit_bytes=VMEM_LIMIT_BYTES)


def _sigmoid(t):
    return 1.0 / (1.0 + jnp.exp(-t))


def _norm_gate(y, g, z):
    ms = jnp.mean(y * y, axis=-1, keepdims=True)
    return y * lax.rsqrt(ms + EPS) * g * (z * _sigmoid(z))


def _mem_kv_kernel(mem_ref, g_ref, w_ref, o_ref):
    x = mem_ref[...]
    ms = jnp.mean(x * x, axis=-1, keepdims=True)
    u = (x * lax.rsqrt(ms + EPS) * g_ref[...]).astype(BF16)
    o_ref[...] = jnp.dot(u, w_ref[...], preferred_element_type=F32).astype(o_ref.dtype)


def _mem_kv(mem, g, w_mem_kv_bf):
    m, d = mem.shape
    n = w_mem_kv_bf.shape[1]
    return pl.pallas_call(
        _mem_kv_kernel,
        grid=(n // MEM_KV_TN,),
        in_specs=[pl.BlockSpec((m, d), lambda j: (0, 0)),
                  pl.BlockSpec((1, d), lambda j: (0, 0)),
                  pl.BlockSpec((d, MEM_KV_TN), lambda j: (0, j))],
        out_specs=pl.BlockSpec((m, MEM_KV_TN), lambda j: (0, j)),
        out_shape=jax.ShapeDtypeStruct((m, n), BF16),
        compiler_params=_params("parallel"),
        name="mem_kv_proj",
    )(mem, g.reshape(1, d), w_mem_kv_bf)


def _in_proj_kernel(x_ref, g_ref, w_ref, wo_ref, wk_ref, proj_ref, wo_bf_ref, wk_bf_ref, u_scr, ss_scr):
    o = pl.program_id(0)
    j = pl.program_id(1)
    has_chunk = (j < 2 * IN_KCHUNKS) & (o < pl.num_programs(0) - 1)

    def cast_side_streams():
        for c in range(D_MODEL // TAIL_TN):
            wo_bf_ref[c] = wo_ref[:, c * TAIL_TN:(c + 1) * TAIL_TN].astype(BF16)
        wk_bf_ref[...] = wk_ref[...].astype(BF16)

    def normalise_next_tile():
        x = x_ref[...]
        sq = x * x
        part = sq[:, 0:LANES]
        for c in range(1, IN_KC // LANES):
            part = part + sq[:, c * LANES:(c + 1) * LANES]
        ss_old = ss_scr[...]
        ss = jnp.where(j == 0, part, jnp.where(j < IN_KCHUNKS, ss_old + part, ss_old))
        ss_scr[...] = ss
        r = lax.rsqrt(jnp.sum(ss, axis=-1, keepdims=True) * (1.0 / D_MODEL) + EPS)
        chunk = jnp.clip(j - IN_KCHUNKS, 0, IN_KCHUNKS - 1)
        u_scr[lax.rem(o, 2), chunk] = (x * r * g_ref[...]).astype(BF16)

    def multiply_tile():
        slot = lax.rem(o + 1, 2)
        acc = None
        for c in range(IN_KCHUNKS):
            d = jnp.dot(u_scr[slot, c], w_ref[c * IN_KC:(c + 1) * IN_KC, :].astype(BF16),
                        preferred_element_type=F32)
            acc = d if acc is None else acc + d
        proj_ref[...] = acc.astype(proj_ref.dtype)

    @pl.when(o == 0)
    def _():
        @pl.when(j == 0)
        def _():
            ss_scr[...] = jnp.zeros_like(ss_scr)

        cast_side_streams()
        normalise_next_tile()

    @pl.when((o > 0) & has_chunk)
    def _():
        multiply_tile()
        cast_side_streams()
        normalise_next_tile()

    @pl.when((o > 0) & jnp.logical_not(has_chunk))
    def _():
        multiply_tile()
        cast_side_streams()


def _in_proj(x, pre_g, w_in, w_out, w_mem_kv):
    s_len = x.shape[0]
    ni, nj = s_len // IN_TM, IN_WIDTH // IN_TN
    n_steps = (ni + 1) * nj
    side_rows = 32
    n_side = D_MODEL // side_rows
    assert n_side <= n_steps and nj >= 2 * IN_KCHUNKS

    def x_tile(o, j):
        return (jnp.minimum(o, ni - 1), jnp.where(j < 2 * IN_KCHUNKS, lax.rem(j, IN_KCHUNKS), IN_KCHUNKS - 1))

    def side(o, j):
        return (jnp.minimum(o * nj + j, n_side - 1), 0)

    return pl.pallas_call(
        _in_proj_kernel,
        grid=(ni + 1, nj),
        in_specs=[pl.BlockSpec((IN_TM, IN_KC), x_tile),
                  pl.BlockSpec((1, IN_KC), lambda o, j: (0, x_tile(o, j)[1])),
                  pl.BlockSpec((D_MODEL, IN_TN), lambda o, j: (0, jnp.where(o > 0, j, 0))),
                  pl.BlockSpec((side_rows, D_MODEL), side),
                  pl.BlockSpec((side_rows, 2 * MEM_WIDTH), side)],
        out_specs=[pl.BlockSpec((IN_TM, IN_TN), lambda o, j: (jnp.maximum(o - 1, 0), jnp.where(o > 0, j, 0))),
                   pl.BlockSpec((D_MODEL // TAIL_TN, side_rows, TAIL_TN), lambda o, j: (0, side(o, j)[0], 0)),
                   pl.BlockSpec((side_rows, 2 * MEM_WIDTH), side)],
        out_shape=[jax.ShapeDtypeStruct((s_len, IN_WIDTH), BF16),
                   jax.ShapeDtypeStruct((D_MODEL // TAIL_TN, D_MODEL, TAIL_TN), BF16),
                   jax.ShapeDtypeStruct((D_MODEL, 2 * MEM_WIDTH), BF16)],
        scratch_shapes=[pltpu.VMEM((2, IN_KCHUNKS, IN_TM, IN_KC), BF16),
                        pltpu.VMEM((IN_TM, LANES), F32)],
        compiler_params=_params("arbitrary", "arbitrary"),
        name="in_proj_pre_norm",
    )(x, pre_g.reshape(1, D_MODEL), w_in, w_out, w_mem_kv)


def _rope_tables(block_idx, freq, base_cos, base_sin, mult):
    a = (block_idx * BLOCK).astype(F32) * freq
    ca, sa = jnp.cos(a) * mult, jnp.sin(a) * mult
    c = ca * base_cos - sa * base_sin
    s = sa * base_cos + ca * base_sin
    lane = lax.broadcasted_iota(jnp.int32, (BLOCK, HEAD_DIM), 1)
    return c, jnp.where(lane < ROT_HALF, -s, 0.0), jnp.where(lane >= ROT_HALF, s, 0.0)


def _rope(t, tables):
    c, sin_lo, sin_hi = tables
    return (t * c + pltpu.roll(t, HEAD_DIM - ROT_HALF, axis=1) * sin_lo
            + pltpu.roll(t, ROT_HALF, axis=1) * sin_hi)


def _rope_keys(k_ref, tables):
    return jnp.concatenate(
        [_rope(k_ref[:, h * HEAD_DIM:(h + 1) * HEAD_DIM].astype(F32), tables).astype(BF16)
         for h in range(ATT_KV_HEADS)], axis=1)


def _attn_init(k0_ref, freq_ref, k_ring, bias_scr, cos_scr, sin_scr):
    freq = freq_ref[...]
    rows = ATT_GROUP * BLOCK
    ang = lax.broadcasted_iota(jnp.int32, (BLOCK, HEAD_DIM), 0).astype(F32) * freq
    cos_scr[...] = jnp.cos(ang)
    sin_scr[...] = jnp.sin(ang)
    key = lax.broadcasted_iota(jnp.int32, (BLOCK, rows), 0)
    q_local = lax.broadcasted_iota(jnp.int32, (BLOCK, rows), 1) & (BLOCK - 1)
    bias_scr[0] = jnp.where(key >= q_local, jnp.inf, F32_MIN)
    bias_scr[1] = jnp.where(key <= q_local, jnp.inf, F32_MIN)
    bias_scr[2] = jnp.full((BLOCK, rows), F32_MIN, F32)
    k_ring[0] = _rope_keys(k0_ref, _rope_tables(jnp.int32(0), freq, cos_scr[...], sin_scr[...], 1.0))
    k_ring[2] = jnp.zeros((BLOCK, KV_WIDTH), BF16)


class _AttnBlock:
    def __init__(self, n, nb, sink_ref, q_ref, kn_ref, vp_ref, vc_ref, vn_ref, freq_ref,
                 k_ring, bias_scr, cos_scr, sin_scr, y_scr):
        self.sink_ref, self.q_ref, self.v_refs, self.k_ring, self.y_scr = (
            sink_ref, q_ref, (vp_ref, vc_ref, vn_ref), k_ring, y_scr)
        freq = freq_ref[...]
        rows = ATT_GROUP * BLOCK
        self.log2e = math.log2(math.e)
        base_cos = cos_scr[...]
        base_sin = sin_scr[...]
        self.slots = (lax.rem(n + 2, 3), lax.rem(n, 3), lax.rem(n + 1, 3))
        k_ring[self.slots[2]] = _rope_keys(kn_ref, _rope_tables(n + 1, freq, base_cos, base_sin, 1.0))
        self.tab_q = _rope_tables(n, freq, base_cos, base_sin, HEAD_DIM ** -0.5 * self.log2e)
        self.bias_p = bias_scr[jnp.where(n > 0, 0, 2)]
        self.bias_n = bias_scr[jnp.where(n < nb - 1, 1, 2)]
        self.lane_group = lax.broadcasted_iota(jnp.int32, (1, rows), 1) // BLOCK
        self.first_row = lax.broadcasted_iota(jnp.int32, (BF16_SUBLANES, rows), 0) == 0
        self.ones = jnp.ones((3 * BLOCK + BF16_SUBLANES, HEAD_DIM), BF16)

    def scores(self, h):
        ks = slice(h * HEAD_DIM, (h + 1) * HEAD_DIM)
        k_all = jnp.concatenate([self.k_ring[s, :, ks] for s in self.slots], axis=0)
        q_all = jnp.concatenate(
            [_rope(self.q_ref[:, (h * ATT_GROUP + g) * HEAD_DIM:(h * ATT_GROUP + g + 1) * HEAD_DIM]
                   .astype(F32), self.tab_q).astype(BF16) for g in range(ATT_GROUP)], axis=0)
        return lax.dot_general(k_all, q_all, (((1,), (1,)), ((), ())), preferred_element_type=F32)

    def values(self, h, s):
        ks = slice(h * HEAD_DIM, (h + 1) * HEAD_DIM)
        sink = jnp.zeros((1, ATT_GROUP * BLOCK), F32)
        for g in range(ATT_GROUP):
            sink = jnp.where(self.lane_group == g, self.sink_ref[h * ATT_GROUP + g] * self.log2e, sink)
        s_p = jnp.minimum(s[:BLOCK], self.bias_p)
        s_c = s[BLOCK:2 * BLOCK]
        s_n = jnp.minimum(s[2 * BLOCK:], self.bias_n)
        m = jnp.max(jnp.maximum(jnp.maximum(s_p, s_c), s_n), axis=0, keepdims=True)
        m = jnp.maximum(m, sink)
        p_sink = jnp.where(self.first_row, jnp.exp2(sink - m), 0.0)
        p = jnp.concatenate([jnp.exp2(s_p - m), jnp.exp2(s_c - m), jnp.exp2(s_n - m), p_sink],
                            axis=0).astype(BF16)
        v_all = jnp.concatenate([r[:, ks] for r in self.v_refs]
                                + [jnp.zeros((BF16_SUBLANES, HEAD_DIM), BF16)], axis=0)
        v_aug = jnp.concatenate([v_all, self.ones], axis=1)
        o_aug = lax.dot_general(p, v_aug, (((0,), (0,)), ((), ())), preferred_element_type=F32)
        o = o_aug[:, :HEAD_DIM] / o_aug[:, HEAD_DIM:]
        for g in range(ATT_GROUP):
            hq = h * ATT_GROUP + g
            self.y_scr[:, hq * HEAD_DIM:(hq + 1) * HEAD_DIM] = o[g * BLOCK:(g + 1) * BLOCK]

    def result(self, z0_ref, z1_ref, g_ref):
        z = jnp.concatenate([z0_ref[...], z1_ref[...]], axis=1).astype(F32)
        return _norm_gate(self.y_scr[...], g_ref[...], z)


def _lru_direction(d, reverse, chunk, n_chunks, x_ref, xp_ref, xn_ref, cw_ref, cb_ref, wg_ref,
                   bg_ref, lam_ref, out_ref, xe_scr, carry_scr):
    t = LRU_CHUNK
    halo = BF16_SUBLANES
    for blk in range(LRU_BLOCKS):
        cs = slice(blk * LRU_BLOCK_DIM, (blk + 1) * LRU_BLOCK_DIM)
        xe_scr[d, blk, 0:halo, :] = jnp.where(chunk > 0, xp_ref[:, cs].astype(F32), 0.0)
        xe_scr[d, blk, halo:halo + t, :] = x_ref[:, cs].astype(F32)
        xe_scr[d, blk, halo + t:, :] = jnp.where(chunk < n_chunks - 1, xn_ref[:, cs].astype(F32), 0.0)

    neg_lam = -lam_ref[d:d + 1, :]
    softplus = jnp.maximum(neg_lam, 0.0) + jnp.log(1.0 + jnp.exp(-jnp.abs(neg_lam)))
    half_decay = (-0.5 * LRU_C * math.log2(math.e)) * softplus
    row0 = [8 * sum(LRU_SUBCHUNK_SEGS[:k]) for k in range(len(LRU_SUBCHUNK_SEGS))]
    subchunks = list(zip(row0, LRU_SUBCHUNK_SEGS))
    for blk in range(LRU_BLOCKS):
        cs = slice(blk * LRU_BLOCK_DIM, (blk + 1) * LRU_BLOCK_DIM)
        taps = [jnp.broadcast_to(cw_ref[j:j + 1, cs], (8, LRU_BLOCK_DIM)) for j in range(CONV_WIDTH)]
        conv_bias = jnp.broadcast_to(cb_ref[:, cs], (8, LRU_BLOCK_DIM))
        hd = half_decay[:, cs]
        cin = carry_scr[d:d + 1, cs]
        for r0, seg in (reversed(subchunks) if reverse else subchunks):
            order = range(seg - 1, -1, -1) if reverse else range(seg)
            tiles = []
            for m in range(seg):
                acc = conv_bias
                for j in range(CONV_WIDTH):
                    start = halo + r0 + m + j - CONV_LEFT
                    acc = acc + taps[j] * xe_scr[d, blk, pl.ds(start, 8, stride=seg), :]
                tiles.append(acc)
            xc = jnp.concatenate(tiles, axis=0)
            gates = jnp.dot(xc.astype(BF16), wg_ref[d, blk], preferred_element_type=F32)
            t_r = jnp.tanh(gates[:, :LRU_BLOCK_DIM] + bg_ref[d, 0:1, cs])
            t_i = jnp.tanh(gates[:, LRU_BLOCK_DIM:] + bg_ref[d, 1:2, cs])
            a = jnp.exp2(hd * t_r + hd)
            w = 1.0 - a * a
            b = w * lax.rsqrt(jnp.maximum(w, F32_TINY)) * ((t_i + 1.0) * 0.5) * xc
            tile = lambda v, m: v[8 * m:8 * m + 8]
            hh = jnp.zeros((8, LRU_BLOCK_DIM), F32)
            pp = jnp.ones((8, LRU_BLOCK_DIM), F32)
            for m in order:
                hh = tile(a, m) * hh + tile(b, m)
                pp = tile(a, m) * pp
            entering = [None] * 8
            for s in (range(7, -1, -1) if reverse else range(8)):
                entering[s] = cin
                cin = pp[s:s + 1, :] * cin + hh[s:s + 1, :]
            h = jnp.concatenate(entering, axis=0)
            for m in order:
                h = tile(a, m) * h + tile(b, m)
                out_ref[blk, pl.ds(r0 + m, 8, stride=seg), :] = h
        carry_scr[d:d + 1, cs] = cin


def _lru_kernel(xf_ref, xfp_ref, xfn_ref, xb_ref, xbp_ref, xbn_ref, cw_ref, cb_ref, wg_ref, bg_ref,
                lam_ref, hf_ref, hb_ref, xe_scr, carry_scr):
    i = pl.program_id(0)
    nc = pl.num_programs(0)

    @pl.when(i == 0)
    def _():
        carry_scr[...] = jnp.zeros_like(carry_scr)

    shared = (cw_ref, cb_ref, wg_ref, bg_ref, lam_ref)
    scr = (xe_scr, carry_scr)
    _lru_direction(0, False, i, nc, xf_ref, xfp_ref, xfn_ref, *shared, hf_ref, *scr)
    _lru_direction(1, True, nc - 1 - i, nc, xb_ref, xbp_ref, xbn_ref, *shared, hb_ref, *scr)


def _lru(proj, conv_w, conv_b, w_a, b_a, w_x, b_x, lam):
    s_len = proj.shape[0]
    nc = s_len // LRU_CHUNK
    halo_per_chunk = LRU_CHUNK // BF16_SUBLANES
    n_halo = s_len // BF16_SUBLANES
    wg = (0.5 * jnp.concatenate([w_a, w_x], axis=-1)).astype(BF16)
    bg = 0.5 * jnp.stack([b_a, b_x], axis=1)

    def fwd(i):
        return i

    def bwd(i):
        return nc - 1 - i

    def specs(chunk_of):
        return [
            pl.BlockSpec((LRU_CHUNK, LRU_WIDTH), lambda i: (chunk_of(i), COL_XLRU)),
            pl.BlockSpec((BF16_SUBLANES, LRU_WIDTH),
                         lambda i: (jnp.maximum(chunk_of(i) * halo_per_chunk - 1, 0), COL_XLRU)),
            pl.BlockSpec((BF16_SUBLANES, LRU_WIDTH),
                         lambda i: (jnp.minimum((chunk_of(i) + 1) * halo_per_chunk, n_halo - 1), COL_XLRU)),
        ]

    def whole(shape):
        return pl.BlockSpec(shape, lambda i: (0,) * len(shape))

    return pl.pallas_call(
        _lru_kernel,
        grid=(nc,),
        in_specs=specs(fwd) + specs(bwd) + [
            whole((CONV_WIDTH, LRU_WIDTH)), whole((1, LRU_WIDTH)),
            whole((2, LRU_BLOCKS, LRU_BLOCK_DIM, 2 * LRU_BLOCK_DIM)),
            whole((2, 2, LRU_WIDTH)), whole((2, LRU_WIDTH))],
        out_specs=[pl.BlockSpec((LRU_BLOCKS, LRU_CHUNK, LRU_BLOCK_DIM), lambda i: (0, i, 0)),
                   pl.BlockSpec((LRU_BLOCKS, LRU_CHUNK, LRU_BLOCK_DIM), lambda i: (0, nc - 1 - i, 0))],
        out_shape=[jax.ShapeDtypeStruct((LRU_BLOCKS, s_len, LRU_BLOCK_DIM), F32),
                   jax.ShapeDtypeStruct((LRU_BLOCKS, s_len, LRU_BLOCK_DIM), F32)],
        scratch_shapes=[pltpu.VMEM((2, LRU_BLOCKS, LRU_CHUNK + 2 * BF16_SUBLANES, LRU_BLOCK_DIM), F32),
                        pltpu.VMEM((2, LRU_WIDTH), F32)],
        compiler_params=_params("arbitrary"),
        name="rglru",
    )(proj, proj, proj, proj, proj, proj, conv_w, conv_b.reshape(1, LRU_WIDTH), wg, bg, lam)


def _mem_scores(h, q_ref, kv_ref):
    cs = slice(h * MEM_HEAD_DIM, (h + 1) * MEM_HEAD_DIM)
    return lax.dot_general(q_ref[:, cs], kv_ref[:, cs], (((1,), (1,)), ((), ())),
                           preferred_element_type=F32)


def _mem_values(h, s, kv_ref, y_scr):
    cs = slice(h * MEM_HEAD_DIM, (h + 1) * MEM_HEAD_DIM)
    vs = slice(MEM_WIDTH + h * MEM_HEAD_DIM, MEM_WIDTH + (h + 1) * MEM_HEAD_DIM)
    s = s * MEM_HEAD_DIM ** -0.5
    m = jnp.max(s, axis=-1, keepdims=True)
    p = jnp.exp(s - m)
    denom = jnp.sum(p, axis=-1, keepdims=True)
    y_scr[:, cs] = jnp.dot(p.astype(BF16), kv_ref[:, vs], preferred_element_type=F32) / denom


def _lru_finish(hf_ref, hb_ref, zl_ref, gl_ref):
    y_lru = jnp.concatenate([hf_ref[blk] + hb_ref[blk] for blk in range(LRU_BLOCKS)], axis=1)
    return _norm_gate(y_lru, gl_ref[...], zl_ref[...].astype(F32))


def _tail_kernel(sink_ref,
                 q_ref, k0_ref, kn_ref, vp_ref, vc_ref, vn_ref, z0_ref, z1_ref, freq_ref, ga_ref,
                 qm_ref, kv_ref, zm_ref, gm_ref, hf_ref, hb_ref, zl_ref, gl_ref,
                 w_ref, x_ref, gp_ref, o_ref,
                 lhs_scr, acc_scr, ss_scr, k_ring, bias_scr, cos_scr, sin_scr, ya_scr, ym_scr):
    o = pl.program_id(0)
    j = pl.program_id(1)
    nj = pl.num_programs(1)
    n_row_tiles = pl.num_programs(0) - 2
    nb = n_row_tiles * nj

    n_chunks = ATT_KV_HEADS
    chunk = TAIL_TN // n_chunks

    def stage_b_chunk(c, part):
        slot = lax.rem(o + 1, 2)
        cols = slice(c * chunk, (c + 1) * chunk)
        acc = jnp.dot(lhs_scr[slot], w_ref[:, cols], preferred_element_type=F32)
        acc_scr[slot, j, :, cols] = acc.astype(acc_scr.dtype)
        sq = acc * acc
        for k in range(chunk // LANES):
            piece = sq[:, k * LANES:(k + 1) * LANES]
            part = piece if part is None else part + piece
        return part

    def stages_ab(with_a, with_b):
        part = None
        if with_b:
            part = stage_b_chunk(0, part)
        if with_a:
            n = o * nj + j
            attn = _AttnBlock(n, nb, sink_ref, q_ref, kn_ref, vp_ref, vc_ref, vn_ref, freq_ref,
                              k_ring, bias_scr, cos_scr, sin_scr, ya_scr)
            s_att = [attn.scores(h) for h in range(ATT_KV_HEADS)]
            s_mem = [_mem_scores(h, qm_ref, kv_ref) for h in range(MEM_HEADS)]
        for pair in range(2):
            if with_b:
                part = stage_b_chunk(1 + pair, part)
            if with_a:
                for h in (2 * pair, 2 * pair + 1):
                    attn.values(h, s_att[h])
                    _mem_values(h, s_mem[h], kv_ref, ym_scr)
        if with_b:
            part = stage_b_chunk(3, part)
            slot = lax.rem(o + 1, 2)
            ss_scr[slot] = jnp.where(j == 0, part, ss_scr[slot] + part)
        if with_a:
            rows = pl.ds(pl.multiple_of(j * BLOCK, BLOCK), BLOCK)
            slot = lax.rem(o, 2)
            lhs_scr[slot, rows, 0:ATT_WIDTH] = attn.result(z0_ref, z1_ref, ga_ref).astype(BF16)
            lhs_scr[slot, rows, ATT_WIDTH:ATT_WIDTH + LRU_WIDTH] = _lru_finish(
                hf_ref, hb_ref, zl_ref, gl_ref).astype(BF16)
            lhs_scr[slot, rows, ATT_WIDTH + LRU_WIDTH:] = _norm_gate(
                ym_scr[...], gm_ref[...], zm_ref[...].astype(F32)).astype(BF16)

    def stage_c():
        slot = lax.rem(o, 2)
        ms = jnp.sum(ss_scr[slot], axis=-1, keepdims=True) * (1.0 / D_MODEL)
        o_ref[...] = x_ref[...] + acc_scr[slot, j].astype(F32) * lax.rsqrt(ms + EPS) * gp_ref[...]

    @pl.when(o == 0)
    def _():
        @pl.when(j == 0)
        def _():
            ss_scr[...] = jnp.zeros_like(ss_scr)
            _attn_init(k0_ref, freq_ref, k_ring, bias_scr, cos_scr, sin_scr)

        stages_ab(True, False)

    @pl.when(o == 1)
    def _():
        stages_ab(True, True)

    @pl.when((o >= 2) & (o < n_row_tiles))
    def _():
        stage_c()
        stages_ab(True, True)

    @pl.when(o == n_row_tiles)
    def _():
        stage_c()
        stages_ab(False, True)

    @pl.when(o == n_row_tiles + 1)
    def _():
        stage_c()


def _tail(proj, kv, h_fwd, h_bwd, sink, g_att, g_mem, g_lru, w_out, x, post_g):
    s_len = proj.shape[0]
    tm, tn = TAIL_TM, TAIL_TN
    ni, nj = s_len // tm, D_MODEL // tn
    nb = s_len // BLOCK
    assert tm == nj * BLOCK
    inv_freq = jnp.power(jnp.float32(ROPE_THETA),
                         -jnp.arange(ROT_HALF, dtype=F32) * 2.0 / ROT_DIM)
    freq = jnp.concatenate([inv_freq, inv_freq, jnp.zeros((HEAD_DIM - ROT_DIM,), F32)]).reshape(1, HEAD_DIM)

    def block_of(o, j):
        return jnp.where(o < ni, o * nj + j, nb - 1)

    def rows_spec(width, col, shift=0):
        return pl.BlockSpec((BLOCK, width),
                            lambda o, j, sink: (jnp.clip(block_of(o, j) + shift, 0, nb - 1), col))

    def const_spec(shape):
        return pl.BlockSpec(shape, lambda o, j, sink: (0,) * len(shape))

    slab_spec = pl.BlockSpec((LRU_BLOCKS, BLOCK, LRU_BLOCK_DIM), lambda o, j, sink: (0, block_of(o, j), 0))

    def w_col(o, j, sink):
        return (jnp.where(o == 0, 0, jnp.where(o <= ni, j, nj - 1)), 0, 0)

    def finished_tile(o, j, sink):
        return (jnp.maximum(o - 2, 0), jnp.where(o >= 2, j, 0))

    grid_spec = pltpu.PrefetchScalarGridSpec(
        num_scalar_prefetch=1,
        grid=(ni + 2, nj),
        in_specs=[rows_spec(ATT_WIDTH, 0),
                  pl.BlockSpec((BLOCK, KV_WIDTH), lambda o, j, sink: (0, COL_K512)),
                  rows_spec(KV_WIDTH, COL_K512, 1),
                  rows_spec(KV_WIDTH, COL_V512, -1), rows_spec(KV_WIDTH, COL_V512), rows_spec(KV_WIDTH, COL_V512, 1),
                  rows_spec(1024, COL_ZATT), rows_spec(1024, COL_ZATT + 1),
                  const_spec((1, HEAD_DIM)), const_spec((1, ATT_WIDTH)),
                  rows_spec(MEM_WIDTH, COL_QMEM), const_spec((N_MEM, 2 * MEM_WIDTH)),
                  rows_spec(MEM_WIDTH, COL_ZMEM), const_spec((1, MEM_WIDTH)),
                  slab_spec, slab_spec, rows_spec(LRU_WIDTH, COL_ZLRU), const_spec((1, LRU_WIDTH)),
                  pl.BlockSpec((None, D_MODEL, tn), w_col),
                  pl.BlockSpec((tm, tn), finished_tile),
                  pl.BlockSpec((1, tn), lambda o, j, sink: (0, j))],
        out_specs=pl.BlockSpec((tm, tn), finished_tile),
        scratch_shapes=[pltpu.VMEM((2, tm, D_MODEL), BF16),
                        pltpu.VMEM((2, nj, tm, tn), BF16),
                        pltpu.VMEM((2, tm, LANES), F32),
                        pltpu.VMEM((3, BLOCK, KV_WIDTH), BF16),
                        pltpu.VMEM((3, BLOCK, ATT_GROUP * BLOCK), F32),
                        pltpu.VMEM((BLOCK, HEAD_DIM), F32),
                        pltpu.VMEM((BLOCK, HEAD_DIM), F32),
                        pltpu.VMEM((BLOCK, ATT_WIDTH), F32),
                        pltpu.VMEM((BLOCK, MEM_WIDTH), F32)],
    )
    return pl.pallas_call(
        _tail_kernel,
        grid_spec=grid_spec,
        out_shape=jax.ShapeDtypeStruct((s_len, D_MODEL), F32),
        compiler_params=_params("arbitrary", "arbitrary"),
        name="mixers_out_proj_post_norm",
    )(sink, proj, proj, proj, proj, proj, proj, proj, proj, freq, g_att.reshape(1, ATT_WIDTH),
      proj, kv, proj, g_mem.reshape(1, MEM_WIDTH), h_fwd, h_bwd, proj, g_lru.reshape(1, LRU_WIDTH),
      w_out, x, post_g.reshape(1, D_MODEL))


def _layer(h, mem, pre_g, w_in, sink, conv_w, conv_b, w_a, b_a, w_x, b_x, lam, mem_g, w_mem_kv,
           g_att, g_lru, g_mem, w_out, post_g):
    proj, w_out_bf, w_mem_kv_bf = _in_proj(h, pre_g, w_in, w_out, w_mem_kv)
    kv = _mem_kv(mem, mem_g, w_mem_kv_bf)
    h_fwd, h_bwd = _lru(proj, conv_w, conv_b, w_a, b_a, w_x, b_x, lam)
    return _tail(proj, kv, h_fwd, h_bwd, sink, g_att, g_mem, g_lru, w_out_bf, h, post_g)


def kernel(x, mem, pre_norm_gain, w_in, att_sink, conv_w, conv_b, lru_w_a, lru_b_a, lru_w_x, lru_b_x,
           lru_lambda, mem_norm_gain, w_mem_kv, att_out_gain, lru_out_gain, mem_out_gain, w_out,
           post_norm_gain):
    batch, s_len, d_model = x.shape
    depth = w_in.shape[0]
    outs = []
    for b in range(batch):
        h = x.reshape(s_len, d_model) if batch == 1 else x[b]
        m = mem.reshape(mem.shape[1], d_model) if batch == 1 else mem[b]
        for l in range(depth):
            h = _layer(h, m, pre_norm_gain[l], w_in[l], att_sink[l], conv_w[l], conv_b[l],
                       lru_w_a[l], lru_b_a[l], lru_w_x[l], lru_b_x[l], lru_lambda[l],
                       mem_norm_gain[l], w_mem_kv[l], att_out_gain[l], lru_out_gain[l],
                       mem_out_gain[l], w_out[l], post_norm_gain[l])
        outs.append(h)
    return outs[0].reshape(x.shape) if batch == 1 else jnp.stack(outs, axis=0)
```

```python
import math

import jax
import jax.numpy as jnp
from jax import lax
from jax.experimental import pallas as pl
from jax.experimental.pallas import tpu as pltpu

F32 = jnp.float32
BF16 = jnp.bfloat16

D_MODEL = 4096
N_MEM = 256
HEAD_DIM = 128
ATT_WIDTH = 2048
ATT_HEADS = 16
ATT_KV_HEADS = 4
ATT_GROUP = ATT_HEADS // ATT_KV_HEADS
KV_WIDTH = ATT_KV_HEADS * HEAD_DIM
BLOCK = 128
ROPE_THETA = 500000.0
ROT_DIM = HEAD_DIM // 4
ROT_HALF = ROT_DIM // 2
LRU_WIDTH = 1024
LRU_BLOCKS = 8
LRU_BLOCK_DIM = LRU_WIDTH // LRU_BLOCKS
CONV_WIDTH = 4
CONV_LEFT = 2
LRU_C = 8.0
MEM_HEADS = 4
MEM_WIDTH = 1024
MEM_HEAD_DIM = MEM_WIDTH // MEM_HEADS
EPS = 1e-6
IN_WIDTH = 9216
F32_MIN = float(jnp.finfo(jnp.float32).min)
F32_TINY = float(jnp.finfo(jnp.float32).tiny)

COL_XLRU = 3
COL_QMEM = 4
COL_ZATT = 5
COL_ZLRU = 7
COL_ZMEM = 8
COL_K512 = 4
COL_V512 = 5

VMEM_LIMIT_BYTES = 60000 * 1024
BF16_SUBLANES = 16

MEM_KV_TN = 128
IN_TM = 1024
IN_TN = 512
IN_KC = 512
IN_KCHUNKS = D_MODEL // IN_KC
TAIL_TM = 512
TAIL_TN = 1024
LANES = 128
LRU_CHUNK = 512
LRU_SUBCHUNK_SEGS = (36, 28)
assert 8 * sum(LRU_SUBCHUNK_SEGS) == LRU_CHUNK and all(s % 8 == 4 for s in LRU_SUBCHUNK_SEGS)


def _params(*sem):
    return pltpu.CompilerParams(dimension_semantics=sem, vmem_limit_bytes=VMEM_LIMIT_BYTES)


def _sigmoid(t):
    return 1.0 / (1.0 + jnp.exp(-t))


def _norm_gate(y, g, z):
    ms = jnp.mean(y * y, axis=-1, keepdims=True)
    return y * lax.rsqrt(ms + EPS) * g * (z * _sigmoid(z))


def _in_proj_kernel(x_ref, g_ref, w_ref, wo_ref, mem_ref, gm_ref, wkv_ref, proj_ref, wo_bf_ref, kv_ref,
                    u_scr, ss_scr, memu_scr):
    o = pl.program_id(0)
    j = pl.program_id(1)
    has_chunk = (j < 2 * IN_KCHUNKS) & (o < pl.num_programs(0) - 1)

    def cast_side_streams():
        for c in range(D_MODEL // TAIL_TN):
            wo_bf_ref[c] = wo_ref[:, c * TAIL_TN:(c + 1) * TAIL_TN].astype(BF16)

    def normalise_next_tile():
        x = x_ref[...]
        sq = x * x
        part = sq[:, 0:LANES]
        for c in range(1, IN_KC // LANES):
            part = part + sq[:, c * LANES:(c + 1) * LANES]
        ss_old = ss_scr[...]
        ss = jnp.where(j == 0, part, jnp.where(j < IN_KCHUNKS, ss_old + part, ss_old))
        ss_scr[...] = ss
        r = lax.rsqrt(jnp.sum(ss, axis=-1, keepdims=True) * (1.0 / D_MODEL) + EPS)
        chunk = jnp.clip(j - IN_KCHUNKS, 0, IN_KCHUNKS - 1)
        u_scr[lax.rem(o, 2), chunk] = (x * r * g_ref[...]).astype(BF16)

    def multiply_tile():
        slot = lax.rem(o + 1, 2)
        acc = None
        for c in range(IN_KCHUNKS):
            d = jnp.dot(u_scr[slot, c], w_ref[c * IN_KC:(c + 1) * IN_KC, :].astype(BF16),
                        preferred_element_type=F32)
            acc = d if acc is None else acc + d
        proj_ref[...] = acc.astype(proj_ref.dtype)

    @pl.when(o == 0)
    def _():
        @pl.when(j == 0)
        def _():
            ss_scr[...] = jnp.zeros_like(ss_scr)
            m = mem_ref[...]
            ms = jnp.mean(m * m, axis=-1, keepdims=True)
            memu_scr[...] = (m * lax.rsqrt(ms + EPS) * gm_ref[...]).astype(BF16)

        kv_ref[...] = jnp.dot(memu_scr[...], wkv_ref[...].astype(BF16),
                              preferred_element_type=F32).astype(kv_ref.dtype)
        cast_side_streams()
        normalise_next_tile()

    @pl.when((o > 0) & has_chunk)
    def _():
        multiply_tile()
        cast_side_streams()
        normalise_next_tile()

    @pl.when((o > 0) & jnp.logical_not(has_chunk))
    def _():
        multiply_tile()
        cast_side_streams()


def _in_proj(x, pre_g, w_in, w_out, mem, mem_g, w_mem_kv):
    s_len = x.shape[0]
    n_mem = mem.shape[0]
    ni, nj = s_len // IN_TM, IN_WIDTH // IN_TN
    n_steps = (ni + 1) * nj
    side_rows = 32
    n_side = D_MODEL // side_rows
    n_kv_tiles = 2 * MEM_WIDTH // MEM_KV_TN
    assert n_side <= n_steps and nj >= 2 * IN_KCHUNKS and nj >= n_kv_tiles

    def kv_tile(o, j):
        return (0, jnp.where(o == 0, jnp.minimum(j, n_kv_tiles - 1), n_kv_tiles - 1))

    def x_tile(o, j):
        return (jnp.minimum(o, ni - 1), jnp.where(j < 2 * IN_KCHUNKS, lax.rem(j, IN_KCHUNKS), IN_KCHUNKS - 1))

    def side(o, j):
        return (jnp.minimum(o * nj + j, n_side - 1), 0)

    return pl.pallas_call(
        _in_proj_kernel,
        grid=(ni + 1, nj),
        in_specs=[pl.BlockSpec((IN_TM, IN_KC), x_tile),
                  pl.BlockSpec((1, IN_KC), lambda o, j: (0, x_tile(o, j)[1])),
                  pl.BlockSpec((D_MODEL, IN_TN), lambda o, j: (0, jnp.where(o > 0, j, 0))),
                  pl.BlockSpec((side_rows, D_MODEL), side),
                  pl.BlockSpec((n_mem, D_MODEL), lambda o, j: (0, 0)),
                  pl.BlockSpec((1, D_MODEL), lambda o, j: (0, 0)),
                  pl.BlockSpec((D_MODEL, MEM_KV_TN), kv_tile)],
        out_specs=[pl.BlockSpec((IN_TM, IN_TN), lambda o, j: (jnp.maximum(o - 1, 0), jnp.where(o > 0, j, 0))),
                   pl.BlockSpec((D_MODEL // TAIL_TN, side_rows, TAIL_TN), lambda o, j: (0, side(o, j)[0], 0)),
                   pl.BlockSpec((n_mem, MEM_KV_TN), kv_tile)],
        out_shape=[jax.ShapeDtypeStruct((s_len, IN_WIDTH), BF16),
                   jax.ShapeDtypeStruct((D_MODEL // TAIL_TN, D_MODEL, TAIL_TN), BF16),
                   jax.ShapeDtypeStruct((n_mem, 2 * MEM_WIDTH), BF16)],
        scratch_shapes=[pltpu.VMEM((2, IN_KCHUNKS, IN_TM, IN_KC), BF16),
                        pltpu.VMEM((IN_TM, LANES), F32),
                        pltpu.VMEM((n_mem, D_MODEL), BF16)],
        compiler_params=_params("arbitrary", "arbitrary"),
        name="in_proj_pre_norm",
    )(x, pre_g.reshape(1, D_MODEL), w_in, w_out, mem, mem_g.reshape(1, D_MODEL), w_mem_kv)


def _rope_tables(block_idx, freq, base_cos, base_sin, mult):
    a = (block_idx * BLOCK).astype(F32) * freq
    ca, sa = jnp.cos(a) * mult, jnp.sin(a) * mult
    c = ca * base_cos - sa * base_sin
    s = sa * base_cos + ca * base_sin
    lane = lax.broadcasted_iota(jnp.int32, (BLOCK, HEAD_DIM), 1)
    return c, jnp.where(lane < ROT_HALF, -s, 0.0), jnp.where(lane >= ROT_HALF, s, 0.0)


def _rope(t, tables):
    c, sin_lo, sin_hi = tables
    return (t * c + pltpu.roll(t, HEAD_DIM - ROT_HALF, axis=1) * sin_lo
            + pltpu.roll(t, ROT_HALF, axis=1) * sin_hi)


def _rope_keys(k_ref, tables):
    return jnp.concatenate(
        [_rope(k_ref[:, h * HEAD_DIM:(h + 1) * HEAD_DIM].astype(F32), tables).astype(BF16)
         for h in range(ATT_KV_HEADS)], axis=1)


def _attn_init(k0_ref, freq_ref, k_ring, bias_scr, cos_scr, sin_scr):
    freq = freq_ref[...]
    rows = ATT_GROUP * BLOCK
    ang = lax.broadcasted_iota(jnp.int32, (BLOCK, HEAD_DIM), 0).astype(F32) * freq
    cos_scr[...] = jnp.cos(ang)
    sin_scr[...] = jnp.sin(ang)
    key = lax.broadcasted_iota(jnp.int32, (BLOCK, rows), 0)
    q_local = lax.broadcasted_iota(jnp.int32, (BLOCK, rows), 1) & (BLOCK - 1)
    bias_scr[0] = jnp.where(key >= q_local, jnp.inf, F32_MIN)
    bias_scr[1] = jnp.where(key <= q_local, jnp.inf, F32_MIN)
    bias_scr[2] = jnp.full((BLOCK, rows), F32_MIN, F32)
    k_ring[0] = _rope_keys(k0_ref, _rope_tables(jnp.int32(0), freq, cos_scr[...], sin_scr[...], 1.0))
    k_ring[2] = jnp.zeros((BLOCK, KV_WIDTH), BF16)


class _AttnBlock:
    def __init__(self, n, nb, sink_ref, q_ref, kn_ref, vp_ref, vc_ref, vn_ref, freq_ref,
                 k_ring, bias_scr, cos_scr, sin_scr, y_scr):
        self.sink_ref, self.q_ref, self.v_refs, self.k_ring, self.y_scr = (
            sink_ref, q_ref, (vp_ref, vc_ref, vn_ref), k_ring, y_scr)
        freq = freq_ref[...]
        rows = ATT_GROUP * BLOCK
        self.log2e = math.log2(math.e)
        base_cos = cos_scr[...]
        base_sin = sin_scr[...]
        self.slots = (lax.rem(n + 2, 3), lax.rem(n, 3), lax.rem(n + 1, 3))
        k_ring[self.slots[2]] = _rope_keys(kn_ref, _rope_tables(n + 1, freq, base_cos, base_sin, 1.0))
        self.tab_q = _rope_tables(n, freq, base_cos, base_sin, HEAD_DIM ** -0.5 * self.log2e)
        self.bias_p = bias_scr[jnp.where(n > 0, 0, 2)]
        self.bias_n = bias_scr[jnp.where(n < nb - 1, 1, 2)]
        self.lane_group = lax.broadcasted_iota(jnp.int32, (1, rows), 1) // BLOCK
        self.first_row = lax.broadcasted_iota(jnp.int32, (BF16_SUBLANES, rows), 0) == 0
        self.ones = jnp.ones((3 * BLOCK + BF16_SUBLANES, HEAD_DIM), BF16)

    def scores(self, h):
        ks = slice(h * HEAD_DIM, (h + 1) * HEAD_DIM)
        k_all = jnp.concatenate([self.k_ring[s, :, ks] for s in self.slots], axis=0)
        q_all = jnp.concatenate(
            [_rope(self.q_ref[:, (h * ATT_GROUP + g) * HEAD_DIM:(h * ATT_GROUP + g + 1) * HEAD_DIM]
                   .astype(F32), self.tab_q).astype(BF16) for g in range(ATT_GROUP)], axis=0)
        return lax.dot_general(k_all, q_all, (((1,), (1,)), ((), ())), preferred_element_type=F32)

    def values(self, h, s):
        ks = slice(h * HEAD_DIM, (h + 1) * HEAD_DIM)
        sink = jnp.zeros((1, ATT_GROUP * BLOCK), F32)
        for g in range(ATT_GROUP):
            sink = jnp.where(self.lane_group == g, self.sink_ref[h * ATT_GROUP + g] * self.log2e, sink)
        s_p = jnp.minimum(s[:BLOCK], self.bias_p)
        s_c = s[BLOCK:2 * BLOCK]
        s_n = jnp.minimum(s[2 * BLOCK:], self.bias_n)
        m = jnp.max(jnp.maximum(jnp.maximum(s_p, s_c), s_n), axis=0, keepdims=True)
        m = jnp.maximum(m, sink)
        p_sink = jnp.where(self.first_row, jnp.exp2(sink - m), 0.0)
        p = jnp.concatenate([jnp.exp2(s_p - m), jnp.exp2(s_c - m), jnp.exp2(s_n - m), p_sink],
                            axis=0).astype(BF16)
        v_all = jnp.concatenate([r[:, ks] for r in self.v_refs]
                                + [jnp.zeros((BF16_SUBLANES, HEAD_DIM), BF16)], axis=0)
        v_aug = jnp.concatenate([v_all, self.ones], axis=1)
        o_aug = lax.dot_general(p, v_aug, (((0,), (0,)), ((), ())), preferred_element_type=F32)
        o = o_aug[:, :HEAD_DIM] / o_aug[:, HEAD_DIM:]
        for g in range(ATT_GROUP):
            hq = h * ATT_GROUP + g
            self.y_scr[:, hq * HEAD_DIM:(hq + 1) * HEAD_DIM] = o[g * BLOCK:(g + 1) * BLOCK]

    def result(self, z0_ref, z1_ref, g_ref):
        z = jnp.concatenate([z0_ref[...], z1_ref[...]], axis=1).astype(F32)
        return _norm_gate(self.y_scr[...], g_ref[...], z)


def _lru_direction(d, reverse, chunk, n_chunks, x_ref, xp_ref, xn_ref, cw_ref, cb_ref, wg_ref,
                   bg_ref, lam_ref, out_ref, xe_scr, carry_scr):
    t = LRU_CHUNK
    halo = BF16_SUBLANES
    for blk in range(LRU_BLOCKS):
        cs = slice(blk * LRU_BLOCK_DIM, (blk + 1) * LRU_BLOCK_DIM)
        xe_scr[d, blk, 0:halo, :] = jnp.where(chunk > 0, xp_ref[:, cs].astype(F32), 0.0)
        xe_scr[d, blk, halo:halo + t, :] = x_ref[:, cs].astype(F32)
        xe_scr[d, blk, halo + t:, :] = jnp.where(chunk < n_chunks - 1, xn_ref[:, cs].astype(F32), 0.0)

    neg_lam = -lam_ref[d:d + 1, :]
    softplus = jnp.maximum(neg_lam, 0.0) + jnp.log(1.0 + jnp.exp(-jnp.abs(neg_lam)))
    half_decay = (-0.5 * LRU_C * math.log2(math.e)) * softplus
    row0 = [8 * sum(LRU_SUBCHUNK_SEGS[:k]) for k in range(len(LRU_SUBCHUNK_SEGS))]
    subchunks = list(zip(row0, LRU_SUBCHUNK_SEGS))
    for blk in range(LRU_BLOCKS):
        cs = slice(blk * LRU_BLOCK_DIM, (blk + 1) * LRU_BLOCK_DIM)
        taps = [jnp.broadcast_to(cw_ref[j:j + 1, cs], (8, LRU_BLOCK_DIM)) for j in range(CONV_WIDTH)]
        conv_bias = jnp.broadcast_to(cb_ref[:, cs], (8, LRU_BLOCK_DIM))
        hd = half_decay[:, cs]
        cin = carry_scr[d:d + 1, cs]
        for r0, seg in (reversed(subchunks) if reverse else subchunks):
            order = range(seg - 1, -1, -1) if reverse else range(seg)
            tiles = []
            for m in range(seg):
                acc = conv_bias
                for j in range(CONV_WIDTH):
                    start = halo + r0 + m + j - CONV_LEFT
                    acc = acc + taps[j] * xe_scr[d, blk, pl.ds(start, 8, stride=seg), :]
                tiles.append(acc)
            xc = jnp.concatenate(tiles, axis=0)
            gates = jnp.dot(xc.astype(BF16), wg_ref[d, blk], preferred_element_type=F32)
            t_r = jnp.tanh(gates[:, :LRU_BLOCK_DIM] + bg_ref[d, 0:1, cs])
            t_i = jnp.tanh(gates[:, LRU_BLOCK_DIM:] + bg_ref[d, 1:2, cs])
            a = jnp.exp2(hd * t_r + hd)
            w = 1.0 - a * a
            b = w * lax.rsqrt(jnp.maximum(w, F32_TINY)) * ((t_i + 1.0) * 0.5) * xc
            tile = lambda v, m: v[8 * m:8 * m + 8]
            hh = jnp.zeros((8, LRU_BLOCK_DIM), F32)
            pp = jnp.ones((8, LRU_BLOCK_DIM), F32)
            for m in order:
                hh = tile(a, m) * hh + tile(b, m)
                pp = tile(a, m) * pp
            entering = [None] * 8
            for s in (range(7, -1, -1) if reverse else range(8)):
                entering[s] = cin
                cin = pp[s:s + 1, :] * cin + hh[s:s + 1, :]
            h = jnp.concatenate(entering, axis=0)
            for m in order:
                h = tile(a, m) * h + tile(b, m)
                out_ref[blk, pl.ds(r0 + m, 8, stride=seg), :] = h
        carry_scr[d:d + 1, cs] = cin


def _lru_kernel(xf_ref, xfp_ref, xfn_ref, xb_ref, xbp_ref, xbn_ref, cw_ref, cb_ref, wg_ref, bg_ref,
                lam_ref, hf_ref, hb_ref, xe_scr, carry_scr):
    i = pl.program_id(0)
    nc = pl.num_programs(0)

    @pl.when(i == 0)
    def _():
        carry_scr[...] = jnp.zeros_like(carry_scr)

    shared = (cw_ref, cb_ref, wg_ref, bg_ref, lam_ref)
    scr = (xe_scr, carry_scr)
    _lru_direction(0, False, i, nc, xf_ref, xfp_ref, xfn_ref, *shared, hf_ref, *scr)
    _lru_direction(1, True, nc - 1 - i, nc, xb_ref, xbp_ref, xbn_ref, *shared, hb_ref, *scr)


def _lru(proj, conv_w, conv_b, w_a, b_a, w_x, b_x, lam):
    s_len = proj.shape[0]
    nc = s_len // LRU_CHUNK
    halo_per_chunk = LRU_CHUNK // BF16_SUBLANES
    n_halo = s_len // BF16_SUBLANES
    wg = (0.5 * jnp.concatenate([w_a, w_x], axis=-1)).astype(BF16)
    bg = 0.5 * jnp.stack([b_a, b_x], axis=1)

    def fwd(i):
        return i

    def bwd(i):
        return nc - 1 - i

    def specs(chunk_of):
        return [
            pl.BlockSpec((LRU_CHUNK, LRU_WIDTH), lambda i: (chunk_of(i), COL_XLRU)),
            pl.BlockSpec((BF16_SUBLANES, LRU_WIDTH),
                         lambda i: (jnp.maximum(chunk_of(i) * halo_per_chunk - 1, 0), COL_XLRU)),
            pl.BlockSpec((BF16_SUBLANES, LRU_WIDTH),
                         lambda i: (jnp.minimum((chunk_of(i) + 1) * halo_per_chunk, n_halo - 1), COL_XLRU)),
        ]

    def whole(shape):
        return pl.BlockSpec(shape, lambda i: (0,) * len(shape))

    return pl.pallas_call(
        _lru_kernel,
        grid=(nc,),
        in_specs=specs(fwd) + specs(bwd) + [
            whole((CONV_WIDTH, LRU_WIDTH)), whole((1, LRU_WIDTH)),
            whole((2, LRU_BLOCKS, LRU_BLOCK_DIM, 2 * LRU_BLOCK_DIM)),
            whole((2, 2, LRU_WIDTH)), whole((2, LRU_WIDTH))],
        out_specs=[pl.BlockSpec((LRU_BLOCKS, LRU_CHUNK, LRU_BLOCK_DIM), lambda i: (0, i, 0)),
                   pl.BlockSpec((LRU_BLOCKS, LRU_CHUNK, LRU_BLOCK_DIM), lambda i: (0, nc - 1 - i, 0))],
        out_shape=[jax.ShapeDtypeStruct((LRU_BLOCKS, s_len, LRU_BLOCK_DIM), F32),
                   jax.ShapeDtypeStruct((LRU_BLOCKS, s_len, LRU_BLOCK_DIM), F32)],
        scratch_shapes=[pltpu.VMEM((2, LRU_BLOCKS, LRU_CHUNK + 2 * BF16_SUBLANES, LRU_BLOCK_DIM), F32),
                        pltpu.VMEM((2, LRU_WIDTH), F32)],
        compiler_params=_params("arbitrary"),
        name="rglru",
    )(proj, proj, proj, proj, proj, proj, conv_w, conv_b.reshape(1, LRU_WIDTH), wg, bg, lam)


def _mem_scores(h, q_ref, kv_ref):
    cs = slice(h * MEM_HEAD_DIM, (h + 1) * MEM_HEAD_DIM)
    return lax.dot_general(q_ref[:, cs], kv_ref[:, cs], (((1,), (1,)), ((), ())),
                           preferred_element_type=F32)


def _mem_values(h, s, kv_ref, y_scr):
    cs = slice(h * MEM_HEAD_DIM, (h + 1) * MEM_HEAD_DIM)
    vs = slice(MEM_WIDTH + h * MEM_HEAD_DIM, MEM_WIDTH + (h + 1) * MEM_HEAD_DIM)
    s = s * MEM_HEAD_DIM ** -0.5
    m = jnp.max(s, axis=-1, keepdims=True)
    p = jnp.exp(s - m)
    denom = jnp.sum(p, axis=-1, keepdims=True)
    y_scr[:, cs] = jnp.dot(p.astype(BF16), kv_ref[:, vs], preferred_element_type=F32) / denom


def _lru_finish(hf_ref, hb_ref, zl_ref, gl_ref):
    y_lru = jnp.concatenate([hf_ref[blk] + hb_ref[blk] for blk in range(LRU_BLOCKS)], axis=1)
    return _norm_gate(y_lru, gl_ref[...], zl_ref[...].astype(F32))


def _tail_kernel(sink_ref,
                 q_ref, k0_ref, kn_ref, vp_ref, vc_ref, vn_ref, z0_ref, z1_ref, freq_ref, ga_ref,
                 qm_ref, kv_ref, zm_ref, gm_ref, hf_ref, hb_ref, zl_ref, gl_ref,
                 w_ref, x_ref, gp_ref, o_ref,
                 lhs_scr, acc_scr, ss_scr, k_ring, bias_scr, cos_scr, sin_scr, ya_scr, ym_scr):
    o = pl.program_id(0)
    j = pl.program_id(1)
    nj = pl.num_programs(1)
    n_row_tiles = pl.num_programs(0) - 2
    nb = n_row_tiles * nj

    n_chunks = ATT_KV_HEADS
    chunk = TAIL_TN // n_chunks

    def stage_b_chunk(c, part):
        slot = lax.rem(o + 1, 2)
        cols = slice(c * chunk, (c + 1) * chunk)
        acc = jnp.dot(lhs_scr[slot], w_ref[:, cols], preferred_element_type=F32)
        acc_scr[slot, j, :, cols] = acc.astype(acc_scr.dtype)
        sq = acc * acc
        for k in range(chunk // LANES):
            piece = sq[:, k * LANES:(k + 1) * LANES]
            part = piece if part is None else part + piece
        return part

    def stages_ab(with_a, with_b):
        part = None
        if with_b:
            part = stage_b_chunk(0, part)
        if with_a:
            n = o * nj + j
            attn = _AttnBlock(n, nb, sink_ref, q_ref, kn_ref, vp_ref, vc_ref, vn_ref, freq_ref,
                              k_ring, bias_scr, cos_scr, sin_scr, ya_scr)
            s_att = [attn.scores(h) for h in range(ATT_KV_HEADS)]
            s_mem = [_mem_scores(h, qm_ref, kv_ref) for h in range(MEM_HEADS)]
        for pair in range(2):
            if with_b:
                part = stage_b_chunk(1 + pair, part)
            if with_a:
                for h in (2 * pair, 2 * pair + 1):
                    attn.values(h, s_att[h])
                    _mem_values(h, s_mem[h], kv_ref, ym_scr)
        if with_b:
            part = stage_b_chunk(3, part)
            slot = lax.rem(o + 1, 2)
            ss_scr[slot] = jnp.where(j == 0, part, ss_scr[slot] + part)
        if with_a:
            rows = pl.ds(pl.multiple_of(j * BLOCK, BLOCK), BLOCK)
            slot = lax.rem(o, 2)
            lhs_scr[slot, rows, 0:ATT_WIDTH] = attn.result(z0_ref, z1_ref, ga_ref).astype(BF16)
            lhs_scr[slot, rows, ATT_WIDTH:ATT_WIDTH + LRU_WIDTH] = _lru_finish(
                hf_ref, hb_ref, zl_ref, gl_ref).astype(BF16)
            lhs_scr[slot, rows, ATT_WIDTH + LRU_WIDTH:] = _norm_gate(
                ym_scr[...], gm_ref[...], zm_ref[...].astype(F32)).astype(BF16)

    def stage_c():
        slot = lax.rem(o, 2)
        ms = jnp.sum(ss_scr[slot], axis=-1, keepdims=True) * (1.0 / D_MODEL)
        o_ref[...] = x_ref[...] + acc_scr[slot, j].astype(F32) * lax.rsqrt(ms + EPS) * gp_ref[...]

    @pl.when(o == 0)
    def _():
        @pl.when(j == 0)
        def _():
            ss_scr[...] = jnp.zeros_like(ss_scr)
            _attn_init(k0_ref, freq_ref, k_ring, bias_scr, cos_scr, sin_scr)

        stages_ab(True, False)

    @pl.when(o == 1)
    def _():
        stages_ab(True, True)

    @pl.when((o >= 2) & (o < n_row_tiles))
    def _():
        stage_c()
        stages_ab(True, True)

    @pl.when(o == n_row_tiles)
    def _():
        stage_c()
        stages_ab(False, True)

    @pl.when(o == n_row_tiles + 1)
    def _():
        stage_c()


def _tail(proj, kv, h_fwd, h_bwd, sink, g_att, g_mem, g_lru, w_out, x, post_g):
    s_len = proj.shape[0]
    tm, tn = TAIL_TM, TAIL_TN
    ni, nj = s_len // tm, D_MODEL // tn
    nb = s_len // BLOCK
    assert tm == nj * BLOCK
    inv_freq = jnp.power(jnp.float32(ROPE_THETA),
                         -jnp.arange(ROT_HALF, dtype=F32) * 2.0 / ROT_DIM)
    freq = jnp.concatenate([inv_freq, inv_freq, jnp.zeros((HEAD_DIM - ROT_DIM,), F32)]).reshape(1, HEAD_DIM)

    def block_of(o, j):
        return jnp.where(o < ni, o * nj + j, nb - 1)

    def rows_spec(width, col, shift=0):
        return pl.BlockSpec((BLOCK, width),
                            lambda o, j, sink: (jnp.clip(block_of(o, j) + shift, 0, nb - 1), col))

    def const_spec(shape):
        return pl.BlockSpec(shape, lambda o, j, sink: (0,) * len(shape))

    slab_spec = pl.BlockSpec((LRU_BLOCKS, BLOCK, LRU_BLOCK_DIM), lambda o, j, sink: (0, block_of(o, j), 0))

    def w_col(o, j, sink):
        return (jnp.where(o == 0, 0, jnp.where(o <= ni, j, nj - 1)), 0, 0)

    def finished_tile(o, j, sink):
        return (jnp.maximum(o - 2, 0), jnp.where(o >= 2, j, 0))

    grid_spec = pltpu.PrefetchScalarGridSpec(
        num_scalar_prefetch=1,
        grid=(ni + 2, nj),
        in_specs=[rows_spec(ATT_WIDTH, 0),
                  pl.BlockSpec((BLOCK, KV_WIDTH), lambda o, j, sink: (0, COL_K512)),
                  rows_spec(KV_WIDTH, COL_K512, 1),
                  rows_spec(KV_WIDTH, COL_V512, -1), rows_spec(KV_WIDTH, COL_V512), rows_spec(KV_WIDTH, COL_V512, 1),
                  rows_spec(1024, COL_ZATT), rows_spec(1024, COL_ZATT + 1),
                  const_spec((1, HEAD_DIM)), const_spec((1, ATT_WIDTH)),
                  rows_spec(MEM_WIDTH, COL_QMEM), const_spec((N_MEM, 2 * MEM_WIDTH)),
                  rows_spec(MEM_WIDTH, COL_ZMEM), const_spec((1, MEM_WIDTH)),
                  slab_spec, slab_spec, rows_spec(LRU_WIDTH, COL_ZLRU), const_spec((1, LRU_WIDTH)),
                  pl.BlockSpec((None, D_MODEL, tn), w_col),
                  pl.BlockSpec((tm, tn), finished_tile),
                  pl.BlockSpec((1, tn), lambda o, j, sink: (0, j))],
        out_specs=pl.BlockSpec((tm, tn), finished_tile),
        scratch_shapes=[pltpu.VMEM((2, tm, D_MODEL), BF16),
                        pltpu.VMEM((2, nj, tm, tn), BF16),
                        pltpu.VMEM((2, tm, LANES), F32),
                        pltpu.VMEM((3, BLOCK, KV_WIDTH), BF16),
                        pltpu.VMEM((3, BLOCK, ATT_GROUP * BLOCK), F32),
                        pltpu.VMEM((BLOCK, HEAD_DIM), F32),
                        pltpu.VMEM((BLOCK, HEAD_DIM), F32),
                        pltpu.VMEM((BLOCK, ATT_WIDTH), F32),
                        pltpu.VMEM((BLOCK, MEM_WIDTH), F32)],
    )
    return pl.pallas_call(
        _tail_kernel,
        grid_spec=grid_spec,
        out_shape=jax.ShapeDtypeStruct((s_len, D_MODEL), F32),
        compiler_params=_params("arbitrary", "arbitrary"),
        name="mixers_out_proj_post_norm",
    )(sink, proj, proj, proj, proj, proj, proj, proj, proj, freq, g_att.reshape(1, ATT_WIDTH),
      proj, kv, proj, g_mem.reshape(1, MEM_WIDTH), h_fwd, h_bwd, proj, g_lru.reshape(1, LRU_WIDTH),
      w_out, x, post_g.reshape(1, D_MODEL))


def _layer(h, mem, pre_g, w_in, sink, conv_w, conv_b, w_a, b_a, w_x, b_x, lam, mem_g, w_mem_kv,
           g_att, g_lru, g_mem, w_out, post_g):
    proj, w_out_bf, kv = _in_proj(h, pre_g, w_in, w_out, mem, mem_g, w_mem_kv)
    h_fwd, h_bwd = _lru(proj, conv_w, conv_b, w_a, b_a, w_x, b_x, lam)
    return _tail(proj, kv, h_fwd, h_bwd, sink, g_att, g_mem, g_lru, w_out_bf, h, post_g)


def kernel(x, mem, pre_norm_gain, w_in, att_sink, conv_w, conv_b, lru_w_a, lru_b_a, lru_w_x, lru_b_x,
           lru_lambda, mem_norm_gain, w_mem_kv, att_out_gain, lru_out_gain, mem_out_gain, w_out,
           post_norm_gain):
    batch, s_len, d_model = x.shape
    depth = w_in.shape[0]
    outs = []
    for b in range(batch):
        h = x.reshape(s_len, d_model) if batch == 1 else x[b]
        m = mem.reshape(mem.shape[1], d_model) if batch == 1 else mem[b]
        for l in range(depth):
            h = _layer(h, m, pre_norm_gain[l], w_in[l], att_sink[l], conv_w[l], conv_b[l],
                       lru_w_a[l], lru_b_a[l], lru_w_x[l], lru_b_x[l], lru_lambda[l],
                       mem_norm_gain[l], w_mem_kv[l], att_out_gain[l], lru_out_gain[l],
                       mem_out_gain[l], w_out[l], post_norm_gain[l])
        outs.append(h)
    return outs[0].reshape(x.shape) if batch == 1 else jnp.stack(outs, axis=0)
```

```python
import math

import jax
import jax.numpy as jnp
from jax import lax
from jax.experimental import pallas as pl
from jax.experimental.pallas import tpu as pltpu

F32 = jnp.float32
BF16 = jnp.bfloat16

D_MODEL = 4096
N_MEM = 256
HEAD_DIM = 128
ATT_WIDTH = 2048
ATT_HEADS = 16
ATT_KV_HEADS = 4
ATT_GROUP = ATT_HEADS // ATT_KV_HEADS
KV_WIDTH = ATT_KV_HEADS * HEAD_DIM
BLOCK = 128
ROPE_THETA = 500000.0
ROT_DIM = HEAD_DIM // 4
ROT_HALF = ROT_DIM // 2
LRU_WIDTH = 1024
LRU_BLOCKS = 8
LRU_BLOCK_DIM = LRU_WIDTH // LRU_BLOCKS
CONV_WIDTH = 4
CONV_LEFT = 2
LRU_C = 8.0
MEM_HEADS = 4
MEM_WIDTH = 1024
MEM_HEAD_DIM = MEM_WIDTH // MEM_HEADS
EPS = 1e-6
IN_WIDTH = 9216
F32_MIN = float(jnp.finfo(jnp.float32).min)
F32_TINY = float(jnp.finfo(jnp.float32).tiny)

COL_XLRU = 3
COL_QMEM = 4
COL_ZATT = 5
COL_ZLRU = 7
COL_ZMEM = 8
COL_K512 = 4
COL_V512 = 5

VMEM_LIMIT_BYTES = 60000 * 1024
BF16_SUBLANES = 16

MEM_KV_TN = 128
IN_TM = 1024
IN_TN = 512
IN_KC = 512
IN_KCHUNKS = D_MODEL // IN_KC
TAIL_TM = 512
TAIL_TN = 1024
LANES = 128
LRU_CHUNK = 512
LRU_SUBCHUNK_SEGS = (36, 28)
assert 8 * sum(LRU_SUBCHUNK_SEGS) == LRU_CHUNK and all(s % 8 == 4 for s in LRU_SUBCHUNK_SEGS)


def _params(*sem):
    return pltpu.CompilerParams(dimension_semantics=sem, vmem_limit_bytes=VMEM_LIMIT_BYTES)


def _sigmoid(t):
    return 1.0 / (1.0 + jnp.exp(-t))


def _norm_gate(y, g, z):
    ms = jnp.mean(y * y, axis=-1, keepdims=True)
    return y * lax.rsqrt(ms + EPS) * g * (z * _sigmoid(z))


def _in_proj_kernel(x_ref, g_ref, w_ref, wo_ref, mem_ref, gm_ref, wkv_ref, proj_ref, wo_bf_ref, kv_ref,
                    u_scr, ss_scr, rr_scr, memu_scr):
    o = pl.program_id(0)
    j = pl.program_id(1)
    has_chunk = (j < IN_KCHUNKS) & (o < pl.num_programs(0) - 1)

    def cast_side_streams():
        for c in range(D_MODEL // TAIL_TN):
            wo_bf_ref[c] = wo_ref[:, c * TAIL_TN:(c + 1) * TAIL_TN].astype(BF16)

    def take_chunk_of_next_tile():
        slot = lax.rem(o, 2)
        x = x_ref[...]
        u_scr[slot, j] = (x * g_ref[...]).astype(BF16)
        sq = x * x
        part = sq[:, 0:LANES]
        for c in range(1, IN_KC // LANES):
            part = part + sq[:, c * LANES:(c + 1) * LANES]
        ss = jnp.where(j == 0, part, ss_scr[slot] + part)
        ss_scr[slot] = ss
        r = lax.rsqrt(jnp.sum(ss, axis=-1, keepdims=True) * (1.0 / D_MODEL) + EPS)
        rr_scr[slot] = jnp.broadcast_to(r, (IN_TM, LANES))

    def multiply_tile():
        slot = lax.rem(o + 1, 2)
        acc = None
        for c in range(IN_KCHUNKS):
            d = jnp.dot(u_scr[slot, c], w_ref[c * IN_KC:(c + 1) * IN_KC, :].astype(BF16),
                        preferred_element_type=F32)
            acc = d if acc is None else acc + d
        rr = rr_scr[slot]
        for c in range(IN_TN // LANES):
            cols = slice(c * LANES, (c + 1) * LANES)
            proj_ref[:, cols] = (acc[:, cols] * rr).astype(proj_ref.dtype)

    @pl.when(o == 0)
    def _():
        @pl.when(j == 0)
        def _():
            ss_scr[...] = jnp.zeros_like(ss_scr)
            m = mem_ref[...]
            ms = jnp.mean(m * m, axis=-1, keepdims=True)
            memu_scr[...] = (m * lax.rsqrt(ms + EPS) * gm_ref[...]).astype(BF16)

        kv_ref[...] = jnp.dot(memu_scr[...], wkv_ref[...].astype(BF16),
                              preferred_element_type=F32).astype(kv_ref.dtype)
        cast_side_streams()

        @pl.when(has_chunk)
        def _():
            take_chunk_of_next_tile()

    @pl.when((o > 0) & has_chunk)
    def _():
        multiply_tile()
        cast_side_streams()
        take_chunk_of_next_tile()

    @pl.when((o > 0) & jnp.logical_not(has_chunk))
    def _():
        multiply_tile()
        cast_side_streams()


def _in_proj(x, pre_g, w_in, w_out, mem, mem_g, w_mem_kv):
    s_len = x.shape[0]
    n_mem = mem.shape[0]
    ni, nj = s_len // IN_TM, IN_WIDTH // IN_TN
    n_steps = (ni + 1) * nj
    side_rows = 32
    n_side = D_MODEL // side_rows
    n_kv_tiles = 2 * MEM_WIDTH // MEM_KV_TN
    assert n_side <= n_steps and nj >= IN_KCHUNKS and nj >= n_kv_tiles

    def kv_tile(o, j):
        return (0, jnp.where(o == 0, jnp.minimum(j, n_kv_tiles - 1), n_kv_tiles - 1))

    def x_tile(o, j):
        return (jnp.minimum(o, ni - 1), jnp.minimum(j, IN_KCHUNKS - 1))

    def side(o, j):
        return (jnp.minimum(o * nj + j, n_side - 1), 0)

    return pl.pallas_call(
        _in_proj_kernel,
        grid=(ni + 1, nj),
        in_specs=[pl.BlockSpec((IN_TM, IN_KC), x_tile),
                  pl.BlockSpec((1, IN_KC), lambda o, j: (0, x_tile(o, j)[1])),
                  pl.BlockSpec((D_MODEL, IN_TN), lambda o, j: (0, jnp.where(o > 0, j, 0))),
                  pl.BlockSpec((side_rows, D_MODEL), side),
                  pl.BlockSpec((n_mem, D_MODEL), lambda o, j: (0, 0)),
                  pl.BlockSpec((1, D_MODEL), lambda o, j: (0, 0)),
                  pl.BlockSpec((D_MODEL, MEM_KV_TN), kv_tile)],
        out_specs=[pl.BlockSpec((IN_TM, IN_TN), lambda o, j: (jnp.maximum(o - 1, 0), jnp.where(o > 0, j, 0))),
                   pl.BlockSpec((D_MODEL // TAIL_TN, side_rows, TAIL_TN), lambda o, j: (0, side(o, j)[0], 0)),
                   pl.BlockSpec((n_mem, MEM_KV_TN), kv_tile)],
        out_shape=[jax.ShapeDtypeStruct((s_len, IN_WIDTH), BF16),
                   jax.ShapeDtypeStruct((D_MODEL // TAIL_TN, D_MODEL, TAIL_TN), BF16),
                   jax.ShapeDtypeStruct((n_mem, 2 * MEM_WIDTH), BF16)],
        scratch_shapes=[pltpu.VMEM((2, IN_KCHUNKS, IN_TM, IN_KC), BF16),
                        pltpu.VMEM((2, IN_TM, LANES), F32),
                        pltpu.VMEM((2, IN_TM, LANES), F32),
                        pltpu.VMEM((n_mem, D_MODEL), BF16)],
        compiler_params=_params("arbitrary", "arbitrary"),
        name="in_proj_pre_norm",
    )(x, pre_g.reshape(1, D_MODEL), w_in, w_out, mem, mem_g.reshape(1, D_MODEL), w_mem_kv)


def _rope_tables(block_idx, freq, base_cos, base_sin, mult):
    a = (block_idx * BLOCK).astype(F32) * freq
    ca, sa = jnp.cos(a) * mult, jnp.sin(a) * mult
    c = ca * base_cos - sa * base_sin
    s = sa * base_cos + ca * base_sin
    lane = lax.broadcasted_iota(jnp.int32, (BLOCK, HEAD_DIM), 1)
    return c, jnp.where(lane < ROT_HALF, -s, 0.0), jnp.where(lane >= ROT_HALF, s, 0.0)


def _rope(t, tables):
    c, sin_lo, sin_hi = tables
    return (t * c + pltpu.roll(t, HEAD_DIM - ROT_HALF, axis=1) * sin_lo
            + pltpu.roll(t, ROT_HALF, axis=1) * sin_hi)


def _rope_keys(k_ref, tables):
    return jnp.concatenate(
        [_rope(k_ref[:, h * HEAD_DIM:(h + 1) * HEAD_DIM].astype(F32), tables).astype(BF16)
         for h in range(ATT_KV_HEADS)], axis=1)


def _attn_init(k0_ref, freq_ref, k_ring, bias_scr, cos_scr, sin_scr):
    freq = freq_ref[...]
    rows = ATT_GROUP * BLOCK
    ang = lax.broadcasted_iota(jnp.int32, (BLOCK, HEAD_DIM), 0).astype(F32) * freq
    cos_scr[...] = jnp.cos(ang)
    sin_scr[...] = jnp.sin(ang)
    key = lax.broadcasted_iota(jnp.int32, (BLOCK, rows), 0)
    q_local = lax.broadcasted_iota(jnp.int32, (BLOCK, rows), 1) & (BLOCK - 1)
    bias_scr[0] = jnp.where(key >= q_local, jnp.inf, F32_MIN)
    bias_scr[1] = jnp.where(key <= q_local, jnp.inf, F32_MIN)
    bias_scr[2] = jnp.full((BLOCK, rows), F32_MIN, F32)
    k_ring[0] = _rope_keys(k0_ref, _rope_tables(jnp.int32(0), freq, cos_scr[...], sin_scr[...], 1.0))
    k_ring[2] = jnp.zeros((BLOCK, KV_WIDTH), BF16)


class _AttnBlock:
    def __init__(self, n, nb, sink_ref, q_ref, kn_ref, vp_ref, vc_ref, vn_ref, freq_ref,
                 k_ring, bias_scr, cos_scr, sin_scr, y_scr):
        self.sink_ref, self.q_ref, self.v_refs, self.k_ring, self.y_scr = (
            sink_ref, q_ref, (vp_ref, vc_ref, vn_ref), k_ring, y_scr)
        freq = freq_ref[...]
        rows = ATT_GROUP * BLOCK
        self.log2e = math.log2(math.e)
        base_cos = cos_scr[...]
        base_sin = sin_scr[...]
        self.slots = (lax.rem(n + 2, 3), lax.rem(n, 3), lax.rem(n + 1, 3))
        k_ring[self.slots[2]] = _rope_keys(kn_ref, _rope_tables(n + 1, freq, base_cos, base_sin, 1.0))
        self.tab_q = _rope_tables(n, freq, base_cos, base_sin, HEAD_DIM ** -0.5 * self.log2e)
        self.bias_p = bias_scr[jnp.where(n > 0, 0, 2)]
        self.bias_n = bias_scr[jnp.where(n < nb - 1, 1, 2)]
        self.lane_group = lax.broadcasted_iota(jnp.int32, (1, rows), 1) // BLOCK
        self.first_row = lax.broadcasted_iota(jnp.int32, (BF16_SUBLANES, rows), 0) == 0
        self.ones = jnp.ones((3 * BLOCK + BF16_SUBLANES, HEAD_DIM), BF16)

    def scores(self, h):
        ks = slice(h * HEAD_DIM, (h + 1) * HEAD_DIM)
        k_all = jnp.concatenate([self.k_ring[s, :, ks] for s in self.slots], axis=0)
        q_all = jnp.concatenate(
            [_rope(self.q_ref[:, (h * ATT_GROUP + g) * HEAD_DIM:(h * ATT_GROUP + g + 1) * HEAD_DIM]
                   .astype(F32), self.tab_q).astype(BF16) for g in range(ATT_GROUP)], axis=0)
        return lax.dot_general(k_all, q_all, (((1,), (1,)), ((), ())), preferred_element_type=F32)

    def values(self, h, s):
        ks = slice(h * HEAD_DIM, (h + 1) * HEAD_DIM)
        sink = jnp.zeros((1, ATT_GROUP * BLOCK), F32)
        for g in range(ATT_GROUP):
            sink = jnp.where(self.lane_group == g, self.sink_ref[h * ATT_GROUP + g] * self.log2e, sink)
        s_p = jnp.minimum(s[:BLOCK], self.bias_p)
        s_c = s[BLOCK:2 * BLOCK]
        s_n = jnp.minimum(s[2 * BLOCK:], self.bias_n)
        m = jnp.max(jnp.maximum(jnp.maximum(s_p, s_c), s_n), axis=0, keepdims=True)
        m = jnp.maximum(m, sink)
        p_sink = jnp.where(self.first_row, jnp.exp2(sink - m), 0.0)
        p = jnp.concatenate([jnp.exp2(s_p - m), jnp.exp2(s_c - m), jnp.exp2(s_n - m), p_sink],
                            axis=0).astype(BF16)
        v_all = jnp.concatenate([r[:, ks] for r in self.v_refs]
                                + [jnp.zeros((BF16_SUBLANES, HEAD_DIM), BF16)], axis=0)
        v_aug = jnp.concatenate([v_all, self.ones], axis=1)
        o_aug = lax.dot_general(p, v_aug, (((0,), (0,)), ((), ())), preferred_element_type=F32)
        o = o_aug[:, :HEAD_DIM] / o_aug[:, HEAD_DIM:]
        for g in range(ATT_GROUP):
            hq = h * ATT_GROUP + g
            self.y_scr[:, hq * HEAD_DIM:(hq + 1) * HEAD_DIM] = o[g * BLOCK:(g + 1) * BLOCK]

    def result(self, z0_ref, z1_ref, g_ref):
        z = jnp.concatenate([z0_ref[...], z1_ref[...]], axis=1).astype(F32)
        return _norm_gate(self.y_scr[...], g_ref[...], z)


def _lru_direction(d, reverse, chunk, n_chunks, x_ref, xp_ref, xn_ref, cw_ref, cb_ref, wg_ref,
                   bg_ref, lam_ref, out_ref, xe_scr, carry_scr):
    t = LRU_CHUNK
    halo = BF16_SUBLANES
    for blk in range(LRU_BLOCKS):
        cs = slice(blk * LRU_BLOCK_DIM, (blk + 1) * LRU_BLOCK_DIM)
        xe_scr[d, blk, 0:halo, :] = jnp.where(chunk > 0, xp_ref[:, cs].astype(F32), 0.0)
        xe_scr[d, blk, halo:halo + t, :] = x_ref[:, cs].astype(F32)
        xe_scr[d, blk, halo + t:, :] = jnp.where(chunk < n_chunks - 1, xn_ref[:, cs].astype(F32), 0.0)

    neg_lam = -lam_ref[d:d + 1, :]
    softplus = jnp.maximum(neg_lam, 0.0) + jnp.log(1.0 + jnp.exp(-jnp.abs(neg_lam)))
    half_decay = (-0.5 * LRU_C * math.log2(math.e)) * softplus
    row0 = [8 * sum(LRU_SUBCHUNK_SEGS[:k]) for k in range(len(LRU_SUBCHUNK_SEGS))]
    subchunks = list(zip(row0, LRU_SUBCHUNK_SEGS))
    for blk in range(LRU_BLOCKS):
        cs = slice(blk * LRU_BLOCK_DIM, (blk + 1) * LRU_BLOCK_DIM)
        taps = [jnp.broadcast_to(cw_ref[j:j + 1, cs], (8, LRU_BLOCK_DIM)) for j in range(CONV_WIDTH)]
        conv_bias = jnp.broadcast_to(cb_ref[:, cs], (8, LRU_BLOCK_DIM))
        hd = half_decay[:, cs]
        cin = carry_scr[d:d + 1, cs]
        for r0, seg in (reversed(subchunks) if reverse else subchunks):
            order = range(seg - 1, -1, -1) if reverse else range(seg)
            tiles = []
            for m in range(seg):
                acc = conv_bias
                for j in range(CONV_WIDTH):
                    start = halo + r0 + m + j - CONV_LEFT
                    acc = acc + taps[j] * xe_scr[d, blk, pl.ds(start, 8, stride=seg), :]
                tiles.append(acc)
            xc = jnp.concatenate(tiles, axis=0)
            gates = jnp.dot(xc.astype(BF16), wg_ref[d, blk], preferred_element_type=F32)
            t_r = jnp.tanh(gates[:, :LRU_BLOCK_DIM] + bg_ref[d, 0:1, cs])
            t_i = jnp.tanh(gates[:, LRU_BLOCK_DIM:] + bg_ref[d, 1:2, cs])
            a = jnp.exp2(hd * t_r + hd)
            w = 1.0 - a * a
            b = w * lax.rsqrt(jnp.maximum(w, F32_TINY)) * ((t_i + 1.0) * 0.5) * xc
            tile = lambda v, m: v[8 * m:8 * m + 8]
            hh = jnp.zeros((8, LRU_BLOCK_DIM), F32)
            pp = jnp.ones((8, LRU_BLOCK_DIM), F32)
            for m in order:
                hh = tile(a, m) * hh + tile(b, m)
                pp = tile(a, m) * pp
            entering = [None] * 8
            for s in (range(7, -1, -1) if reverse else range(8)):
                entering[s] = cin
                cin = pp[s:s + 1, :] * cin + hh[s:s + 1, :]
            h = jnp.concatenate(entering, axis=0)
            for m in order:
                h = tile(a, m) * h + tile(b, m)
                out_ref[blk, pl.ds(r0 + m, 8, stride=seg), :] = h
        carry_scr[d:d + 1, cs] = cin


def _lru_kernel(xf_ref, xfp_ref, xfn_ref, xb_ref, xbp_ref, xbn_ref, cw_ref, cb_ref, wg_ref, bg_ref,
                lam_ref, hf_ref, hb_ref, xe_scr, carry_scr):
    i = pl.program_id(0)
    nc = pl.num_programs(0)

    @pl.when(i == 0)
    def _():
        carry_scr[...] = jnp.zeros_like(carry_scr)

    shared = (cw_ref, cb_ref, wg_ref, bg_ref, lam_ref)
    scr = (xe_scr, carry_scr)
    _lru_direction(0, False, i, nc, xf_ref, xfp_ref, xfn_ref, *shared, hf_ref, *scr)
    _lru_direction(1, True, nc - 1 - i, nc, xb_ref, xbp_ref, xbn_ref, *shared, hb_ref, *scr)


def _lru(proj, conv_w, conv_b, w_a, b_a, w_x, b_x, lam):
    s_len = proj.shape[0]
    nc = s_len // LRU_CHUNK
    halo_per_chunk = LRU_CHUNK // BF16_SUBLANES
    n_halo = s_len // BF16_SUBLANES
    wg = (0.5 * jnp.concatenate([w_a, w_x], axis=-1)).astype(BF16)
    bg = 0.5 * jnp.stack([b_a, b_x], axis=1)

    def fwd(i):
        return i

    def bwd(i):
        return nc - 1 - i

    def specs(chunk_of):
        return [
            pl.BlockSpec((LRU_CHUNK, LRU_WIDTH), lambda i: (chunk_of(i), COL_XLRU)),
            pl.BlockSpec((BF16_SUBLANES, LRU_WIDTH),
                         lambda i: (jnp.maximum(chunk_of(i) * halo_per_chunk - 1, 0), COL_XLRU)),
            pl.BlockSpec((BF16_SUBLANES, LRU_WIDTH),
                         lambda i: (jnp.minimum((chunk_of(i) + 1) * halo_per_chunk, n_halo - 1), COL_XLRU)),
        ]

    def whole(shape):
        return pl.BlockSpec(shape, lambda i: (0,) * len(shape))

    return pl.pallas_call(
        _lru_kernel,
        grid=(nc,),
        in_specs=specs(fwd) + specs(bwd) + [
            whole((CONV_WIDTH, LRU_WIDTH)), whole((1, LRU_WIDTH)),
            whole((2, LRU_BLOCKS, LRU_BLOCK_DIM, 2 * LRU_BLOCK_DIM)),
            whole((2, 2, LRU_WIDTH)), whole((2, LRU_WIDTH))],
        out_specs=[pl.BlockSpec((LRU_BLOCKS, LRU_CHUNK, LRU_BLOCK_DIM), lambda i: (0, i, 0)),
                   pl.BlockSpec((LRU_BLOCKS, LRU_CHUNK, LRU_BLOCK_DIM), lambda i: (0, nc - 1 - i, 0))],
        out_shape=[jax.ShapeDtypeStruct((LRU_BLOCKS, s_len, LRU_BLOCK_DIM), F32),
                   jax.ShapeDtypeStruct((LRU_BLOCKS, s_len, LRU_BLOCK_DIM), F32)],
        scratch_shapes=[pltpu.VMEM((2, LRU_BLOCKS, LRU_CHUNK + 2 * BF16_SUBLANES, LRU_BLOCK_DIM), F32),
                        pltpu.VMEM((2, LRU_WIDTH), F32)],
        compiler_params=_params("arbitrary"),
        name="rglru",
    )(proj, proj, proj, proj, proj, proj, conv_w, conv_b.reshape(1, LRU_WIDTH), wg, bg, lam)


def _mem_scores(h, q_ref, kv_ref):
    cs = slice(h * MEM_HEAD_DIM, (h + 1) * MEM_HEAD_DIM)
    return lax.dot_general(q_ref[:, cs], kv_ref[:, cs], (((1,), (1,)), ((), ())),
                           preferred_element_type=F32)


def _mem_values(h, s, kv_ref, y_scr):
    cs = slice(h * MEM_HEAD_DIM, (h + 1) * MEM_HEAD_DIM)
    vs = slice(MEM_WIDTH + h * MEM_HEAD_DIM, MEM_WIDTH + (h + 1) * MEM_HEAD_DIM)
    s = s * MEM_HEAD_DIM ** -0.5
    m = jnp.max(s, axis=-1, keepdims=True)
    p = jnp.exp(s - m)
    denom = jnp.sum(p, axis=-1, keepdims=True)
    y_scr[:, cs] = jnp.dot(p.astype(BF16), kv_ref[:, vs], preferred_element_type=F32) / denom


def _lru_finish(hf_ref, hb_ref, zl_ref, gl_ref):
    y_lru = jnp.concatenate([hf_ref[blk] + hb_ref[blk] for blk in range(LRU_BLOCKS)], axis=1)
    return _norm_gate(y_lru, gl_ref[...], zl_ref[...].astype(F32))


def _tail_kernel(sink_ref,
                 q_ref, k0_ref, kn_ref, vp_ref, vc_ref, vn_ref, z0_ref, z1_ref, freq_ref, ga_ref,
                 qm_ref, kv_ref, zm_ref, gm_ref, hf_ref, hb_ref, zl_ref, gl_ref,
                 w_ref, x_ref, gp_ref, o_ref,
                 lhs_scr, acc_scr, ss_scr, k_ring, bias_scr, cos_scr, sin_scr, ya_scr, ym_scr):
    o = pl.program_id(0)
    j = pl.program_id(1)
    nj = pl.num_programs(1)
    n_row_tiles = pl.num_programs(0) - 2
    nb = n_row_tiles * nj

    n_chunks = ATT_KV_HEADS
    chunk = TAIL_TN // n_chunks

    def stage_b_chunk(c, part):
        slot = lax.rem(o + 1, 2)
        cols = slice(c * chunk, (c + 1) * chunk)
        acc = jnp.dot(lhs_scr[slot], w_ref[:, cols], preferred_element_type=F32)
        acc_scr[slot, j, :, cols] = acc.astype(acc_scr.dtype)
        sq = acc * acc
        for k in range(chunk // LANES):
            piece = sq[:, k * LANES:(k + 1) * LANES]
            part = piece if part is None else part + piece
        return part

    def stages_ab(with_a, with_b):
        part = None
        if with_b:
            part = stage_b_chunk(0, part)
        if with_a:
            n = o * nj + j
            attn = _AttnBlock(n, nb, sink_ref, q_ref, kn_ref, vp_ref, vc_ref, vn_ref, freq_ref,
                              k_ring, bias_scr, cos_scr, sin_scr, ya_scr)
            s_att = [attn.scores(h) for h in range(ATT_KV_HEADS)]
            s_mem = [_mem_scores(h, qm_ref, kv_ref) for h in range(MEM_HEADS)]
        for pair in range(2):
            if with_b:
                part = stage_b_chunk(1 + pair, part)
            if with_a:
                for h in (2 * pair, 2 * pair + 1):
                    attn.values(h, s_att[h])
                    _mem_values(h, s_mem[h], kv_ref, ym_scr)
        if with_b:
            part = stage_b_chunk(3, part)
            slot = lax.rem(o + 1, 2)
            ss_scr[slot] = jnp.where(j == 0, part, ss_scr[slot] + part)
        if with_a:
            rows = pl.ds(pl.multiple_of(j * BLOCK, BLOCK), BLOCK)
            slot = lax.rem(o, 2)
            lhs_scr[slot, rows, 0:ATT_WIDTH] = attn.result(z0_ref, z1_ref, ga_ref).astype(BF16)
            lhs_scr[slot, rows, ATT_WIDTH:ATT_WIDTH + LRU_WIDTH] = _lru_finish(
                hf_ref, hb_ref, zl_ref, gl_ref).astype(BF16)
            lhs_scr[slot, rows, ATT_WIDTH + LRU_WIDTH:] = _norm_gate(
                ym_scr[...], gm_ref[...], zm_ref[...].astype(F32)).astype(BF16)

    def stage_c():
        slot = lax.rem(o, 2)
        ms = jnp.sum(ss_scr[slot], axis=-1, keepdims=True) * (1.0 / D_MODEL)
        o_ref[...] = x_ref[...] + acc_scr[slot, j].astype(F32) * lax.rsqrt(ms + EPS) * gp_ref[...]

    @pl.when(o == 0)
    def _():
        @pl.when(j == 0)
        def _():
            ss_scr[...] = jnp.zeros_like(ss_scr)
            _attn_init(k0_ref, freq_ref, k_ring, bias_scr, cos_scr, sin_scr)

        stages_ab(True, False)

    @pl.when(o == 1)
    def _():
        stages_ab(True, True)

    @pl.when((o >= 2) & (o < n_row_tiles))
    def _():
        stage_c()
        stages_ab(True, True)

    @pl.when(o == n_row_tiles)
    def _():
        stage_c()
        stages_ab(False, True)

    @pl.when(o == n_row_tiles + 1)
    def _():
        stage_c()


def _tail(proj, kv, h_fwd, h_bwd, sink, g_att, g_mem, g_lru, w_out, x, post_g):
    s_len = proj.shape[0]
    tm, tn = TAIL_TM, TAIL_TN
    ni, nj = s_len // tm, D_MODEL // tn
    nb = s_len // BLOCK
    assert tm == nj * BLOCK
    inv_freq = jnp.power(jnp.float32(ROPE_THETA),
                         -jnp.arange(ROT_HALF, dtype=F32) * 2.0 / ROT_DIM)
    freq = jnp.concatenate([inv_freq, inv_freq, jnp.zeros((HEAD_DIM - ROT_DIM,), F32)]).reshape(1, HEAD_DIM)

    def block_of(o, j):
        return jnp.where(o < ni, o * nj + j, nb - 1)

    def rows_spec(width, col, shift=0):
        return pl.BlockSpec((BLOCK, width),
                            lambda o, j, sink: (jnp.clip(block_of(o, j) + shift, 0, nb - 1), col))

    def const_spec(shape):
        return pl.BlockSpec(shape, lambda o, j, sink: (0,) * len(shape))

    slab_spec = pl.BlockSpec((LRU_BLOCKS, BLOCK, LRU_BLOCK_DIM), lambda o, j, sink: (0, block_of(o, j), 0))

    def w_col(o, j, sink):
        return (jnp.where(o == 0, 0, jnp.where(o <= ni, j, nj - 1)), 0, 0)

    def finished_tile(o, j, sink):
        return (jnp.maximum(o - 2, 0), jnp.where(o >= 2, j, 0))

    grid_spec = pltpu.PrefetchScalarGridSpec(
        num_scalar_prefetch=1,
        grid=(ni + 2, nj),
        in_specs=[rows_spec(ATT_WIDTH, 0),
                  pl.BlockSpec((BLOCK, KV_WIDTH), lambda o, j, sink: (0, COL_K512)),
                  rows_spec(KV_WIDTH, COL_K512, 1),
                  rows_spec(KV_WIDTH, COL_V512, -1), rows_spec(KV_WIDTH, COL_V512), rows_spec(KV_WIDTH, COL_V512, 1),
                  rows_spec(1024, COL_ZATT), rows_spec(1024, COL_ZATT + 1),
                  const_spec((1, HEAD_DIM)), const_spec((1, ATT_WIDTH)),
                  rows_spec(MEM_WIDTH, COL_QMEM), const_spec((N_MEM, 2 * MEM_WIDTH)),
                  rows_spec(MEM_WIDTH, COL_ZMEM), const_spec((1, MEM_WIDTH)),
                  slab_spec, slab_spec, rows_spec(LRU_WIDTH, COL_ZLRU), const_spec((1, LRU_WIDTH)),
                  pl.BlockSpec((None, D_MODEL, tn), w_col),
                  pl.BlockSpec((tm, tn), finished_tile),
                  pl.BlockSpec((1, tn), lambda o, j, sink: (0, j))],
        out_specs=pl.BlockSpec((tm, tn), finished_tile),
        scratch_shapes=[pltpu.VMEM((2, tm, D_MODEL), BF16),
                        pltpu.VMEM((2, nj, tm, tn), BF16),
                        pltpu.VMEM((2, tm, LANES), F32),
                        pltpu.VMEM((3, BLOCK, KV_WIDTH), BF16),
                        pltpu.VMEM((3, BLOCK, ATT_GROUP * BLOCK), F32),
                        pltpu.VMEM((BLOCK, HEAD_DIM), F32),
                        pltpu.VMEM((BLOCK, HEAD_DIM), F32),
                        pltpu.VMEM((BLOCK, ATT_WIDTH), F32),
                        pltpu.VMEM((BLOCK, MEM_WIDTH), F32)],
    )
    return pl.pallas_call(
        _tail_kernel,
        grid_spec=grid_spec,
        out_shape=jax.ShapeDtypeStruct((s_len, D_MODEL), F32),
        compiler_params=_params("arbitrary", "arbitrary"),
        name="mixers_out_proj_post_norm",
    )(sink, proj, proj, proj, proj, proj, proj, proj, proj, freq, g_att.reshape(1, ATT_WIDTH),
      proj, kv, proj, g_mem.reshape(1, MEM_WIDTH), h_fwd, h_bwd, proj, g_lru.reshape(1, LRU_WIDTH),
      w_out, x, post_g.reshape(1, D_MODEL))


def _layer(h, mem, pre_g, w_in, sink, conv_w, conv_b, w_a, b_a, w_x, b_x, lam, mem_g, w_mem_kv,
           g_att, g_lru, g_mem, w_out, post_g):
    proj, w_out_bf, kv = _in_proj(h, pre_g, w_in, w_out, mem, mem_g, w_mem_kv)
    h_fwd, h_bwd = _lru(proj, conv_w, conv_b, w_a, b_a, w_x, b_x, lam)
    return _tail(proj, kv, h_fwd, h_bwd, sink, g_att, g_mem, g_lru, w_out_bf, h, post_g)


def kernel(x, mem, pre_norm_gain, w_in, att_sink, conv_w, conv_b, lru_w_a, lru_b_a, lru_w_x, lru_b_x,
           lru_lambda, mem_norm_gain, w_mem_kv, att_out_gain, lru_out_gain, mem_out_gain, w_out,
           post_norm_gain):
    batch, s_len, d_model = x.shape
    depth = w_in.shape[0]
    outs = []
    for b in range(batch):
        h = x.reshape(s_len, d_model) if batch == 1 else x[b]
        m = mem.reshape(mem.shape[1], d_model) if batch == 1 else mem[b]
        for l in range(depth):
            h = _layer(h, m, pre_norm_gain[l], w_in[l], att_sink[l], conv_w[l], conv_b[l],
                       lru_w_a[l], lru_b_a[l], lru_w_x[l], lru_b_x[l], lru_lambda[l],
                       mem_norm_gain[l], w_mem_kv[l], att_out_gain[l], lru_out_gain[l],
                       mem_out_gain[l], w_out[l], post_norm_gain[l])
        outs.append(h)
    return outs[0].reshape(x.shape) if batch == 1 else jnp.stack(outs, axis=0)
```

```python
import math

import jax
import jax.numpy as jnp
from jax import lax
from jax.experimental import pallas as pl
from jax.experimental.pallas import tpu as pltpu

F32 = jnp.float32
BF16 = jnp.bfloat16

D_MODEL = 4096
N_MEM = 256
HEAD_DIM = 128
ATT_WIDTH = 2048
ATT_HEADS = 16
ATT_KV_HEADS = 4
ATT_GROUP = ATT_HEADS // ATT_KV_HEADS
KV_WIDTH = ATT_KV_HEADS * HEAD_DIM
BLOCK = 128
ROPE_THETA = 500000.0
ROT_DIM = HEAD_DIM // 4
ROT_HALF = ROT_DIM // 2
LRU_WIDTH = 1024
LRU_BLOCKS = 8
LRU_BLOCK_DIM = LRU_WIDTH // LRU_BLOCKS
CONV_WIDTH = 4
CONV_LEFT = 2
LRU_C = 8.0
MEM_HEADS = 4
MEM_WIDTH = 1024
MEM_HEAD_DIM = MEM_WIDTH // MEM_HEADS
EPS = 1e-6
IN_WIDTH = 9216
F32_MIN = float(jnp.finfo(jnp.float32).min)
F32_TINY = float(jnp.finfo(jnp.float32).tiny)

COL_XLRU = 3
COL_QMEM = 4
COL_ZATT = 5
COL_ZLRU = 7
COL_ZMEM = 8
COL_K512 = 4
COL_V512 = 5

VMEM_LIMIT_BYTES = 62 * 1024 * 1024
BF16_SUBLANES = 16

MEM_KV_TN = 1024
IN_TM = 1024
IN_TN = 768
IN_KC = 512
IN_KCHUNKS = D_MODEL // IN_KC
TAIL_TM = 512
TAIL_TN = 1024
LANES = 128
LRU_CHUNK = 512
LRU_SUBCHUNK_SEGS = (36, 28)
assert 8 * sum(LRU_SUBCHUNK_SEGS) == LRU_CHUNK and all(s % 8 == 4 for s in LRU_SUBCHUNK_SEGS)


def _params(*sem):
    return pltpu.CompilerParams(dimension_semantics=sem, vmem_limit_bytes=VMEM_LIMIT_BYTES)


def _sigmoid(t):
    return 1.0 / (1.0 + jnp.exp(-t))


def _norm_gate(y, g, z):
    ms = jnp.mean(y * y, axis=-1, keepdims=True)
    return y * lax.rsqrt(ms + EPS) * g * (z * _sigmoid(z))


def _mem_kv_kernel(mem_ref, g_ref, w_ref, o_ref):
    x = mem_ref[...]
    ms = jnp.mean(x * x, axis=-1, keepdims=True)
    u = (x * lax.rsqrt(ms + EPS) * g_ref[...]).astype(BF16)
    o_ref[...] = jnp.dot(u, w_ref[...], preferred_element_type=F32).astype(o_ref.dtype)


def _mem_kv(mem, g, w_mem_kv_bf):
    m, d = mem.shape
    n = w_mem_kv_bf.shape[1]
    return pl.pallas_call(
        _mem_kv_kernel,
        grid=(n // MEM_KV_TN,),
        in_specs=[pl.BlockSpec((m, d), lambda j: (0, 0)),
                  pl.BlockSpec((1, d), lambda j: (0, 0)),
                  pl.BlockSpec((d, MEM_KV_TN), lambda j: (0, j))],
        out_specs=pl.BlockSpec((m, MEM_KV_TN), lambda j: (0, j)),
        out_shape=jax.ShapeDtypeStruct((m, n), BF16),
        compiler_params=_params("parallel"),
        name="mem_kv_proj",
    )(mem, g.reshape(1, d), w_mem_kv_bf)


def _in_proj_kernel(x_ref, g_ref, w_ref, wo_ref, wk_ref, proj_ref, wo_bf_ref, wk_bf_ref,
                    u_scr, ss_scr, rr_scr):
    o = pl.program_id(0)
    j = pl.program_id(1)
    has_chunk = (j < IN_KCHUNKS) & (o < pl.num_programs(0) - 1)

    def cast_side_streams():
        for c in range(D_MODEL // TAIL_TN):
            wo_bf_ref[c] = wo_ref[:, c * TAIL_TN:(c + 1) * TAIL_TN].astype(BF16)
        wk_bf_ref[...] = wk_ref[...].astype(BF16)

    def take_chunk_of_next_tile():
        slot = lax.rem(o, 2)
        x = x_ref[...]
        u_scr[slot, j] = (x * g_ref[...]).astype(BF16)
        sq = x * x
        part = sq[:, 0:LANES]
        for c in range(1, IN_KC // LANES):
            part = part + sq[:, c * LANES:(c + 1) * LANES]
        ss = jnp.where(j == 0, part, ss_scr[slot] + part)
        ss_scr[slot] = ss
        r = lax.rsqrt(jnp.sum(ss, axis=-1, keepdims=True) * (1.0 / D_MODEL) + EPS)
        rr_scr[slot] = jnp.broadcast_to(r, (IN_TM, LANES))

    def multiply_tile():
        slot = lax.rem(o + 1, 2)
        acc = None
        for c in range(IN_KCHUNKS):
            d = jnp.dot(u_scr[slot, c], w_ref[c * IN_KC:(c + 1) * IN_KC, :].astype(BF16),
                        preferred_element_type=F32)
            acc = d if acc is None else acc + d
        rr = rr_scr[slot]
        for c in range(IN_TN // LANES):
            cols = slice(c * LANES, (c + 1) * LANES)
            proj_ref[:, cols] = (acc[:, cols] * rr).astype(proj_ref.dtype)

    @pl.when(o == 0)
    def _():
        @pl.when(j == 0)
        def _():
            ss_scr[...] = jnp.zeros_like(ss_scr)

        cast_side_streams()

        @pl.when(has_chunk)
        def _():
            take_chunk_of_next_tile()

    @pl.when((o > 0) & has_chunk)
    def _():
        multiply_tile()
        cast_side_streams()
        take_chunk_of_next_tile()

    @pl.when((o > 0) & jnp.logical_not(has_chunk))
    def _():
        multiply_tile()
        cast_side_streams()


def _in_proj(x, pre_g, w_in, w_out, w_mem_kv):
    s_len = x.shape[0]
    ni, nj = s_len // IN_TM, IN_WIDTH // IN_TN
    n_steps = (ni + 1) * nj
    side_rows = 64
    n_side = D_MODEL // side_rows
    assert n_side <= n_steps and nj >= IN_KCHUNKS

    def x_tile(o, j):
        return (jnp.minimum(o, ni - 1), jnp.minimum(j, IN_KCHUNKS - 1))

    def side(o, j):
        return (jnp.minimum(o * nj + j, n_side - 1), 0)

    return pl.pallas_call(
        _in_proj_kernel,
        grid=(ni + 1, nj),
        in_specs=[pl.BlockSpec((IN_TM, IN_KC), x_tile),
                  pl.BlockSpec((1, IN_KC), lambda o, j: (0, x_tile(o, j)[1])),
                  pl.BlockSpec((D_MODEL, IN_TN), lambda o, j: (0, jnp.where(o > 0, j, 0))),
                  pl.BlockSpec((side_rows, D_MODEL), side),
                  pl.BlockSpec((side_rows, 2 * MEM_WIDTH), side)],
        out_specs=[pl.BlockSpec((IN_TM, IN_TN), lambda o, j: (jnp.maximum(o - 1, 0), jnp.where(o > 0, j, 0))),
                   pl.BlockSpec((D_MODEL // TAIL_TN, side_rows, TAIL_TN), lambda o, j: (0, side(o, j)[0], 0)),
                   pl.BlockSpec((side_rows, 2 * MEM_WIDTH), side)],
        out_shape=[jax.ShapeDtypeStruct((s_len, IN_WIDTH), BF16),
                   jax.ShapeDtypeStruct((D_MODEL // TAIL_TN, D_MODEL, TAIL_TN), BF16),
                   jax.ShapeDtypeStruct((D_MODEL, 2 * MEM_WIDTH), BF16)],
        scratch_shapes=[pltpu.VMEM((2, IN_KCHUNKS, IN_TM, IN_KC), BF16),
                        pltpu.VMEM((2, IN_TM, LANES), F32),
                        pltpu.VMEM((2, IN_TM, LANES), F32)],
        compiler_params=_params("arbitrary", "arbitrary"),
        name="in_proj_pre_norm",
    )(x, pre_g.reshape(1, D_MODEL), w_in, w_out, w_mem_kv)


def _rope_tables(block_idx, freq, base_cos, base_sin, mult):
    a = (block_idx * BLOCK).astype(F32) * freq
    ca, sa = jnp.cos(a) * mult, jnp.sin(a) * mult
    c = ca * base_cos - sa * base_sin
    s = sa * base_cos + ca * base_sin
    lane = lax.broadcasted_iota(jnp.int32, (BLOCK, HEAD_DIM), 1)
    return c, jnp.where(lane < ROT_HALF, -s, 0.0), jnp.where(lane >= ROT_HALF, s, 0.0)


def _rope(t, tables):
    c, sin_lo, sin_hi = tables
    return (t * c + pltpu.roll(t, HEAD_DIM - ROT_HALF, axis=1) * sin_lo
            + pltpu.roll(t, ROT_HALF, axis=1) * sin_hi)


def _rope_keys(k_ref, tables):
    return jnp.concatenate(
        [_rope(k_ref[:, h * HEAD_DIM:(h + 1) * HEAD_DIM].astype(F32), tables).astype(BF16)
         for h in range(ATT_KV_HEADS)], axis=1)


def _attn_init(k0_ref, freq_ref, k_ring, bias_scr, cos_scr, sin_scr):
    freq = freq_ref[...]
    rows = ATT_GROUP * BLOCK
    ang = lax.broadcasted_iota(jnp.int32, (BLOCK, HEAD_DIM), 0).astype(F32) * freq
    cos_scr[...] = jnp.cos(ang)
    sin_scr[...] = jnp.sin(ang)
    key = lax.broadcasted_iota(jnp.int32, (BLOCK, rows), 0)
    q_local = lax.broadcasted_iota(jnp.int32, (BLOCK, rows), 1) & (BLOCK - 1)
    bias_scr[0] = jnp.where(key >= q_local, jnp.inf, F32_MIN)
    bias_scr[1] = jnp.where(key <= q_local, jnp.inf, F32_MIN)
    bias_scr[2] = jnp.full((BLOCK, rows), F32_MIN, F32)
    k_ring[0] = _rope_keys(k0_ref, _rope_tables(jnp.int32(0), freq, cos_scr[...], sin_scr[...], 1.0))
    k_ring[2] = jnp.zeros((BLOCK, KV_WIDTH), BF16)


class _AttnBlock:
    def __init__(self, n, nb, sink_ref, q_ref, kn_ref, vp_ref, vc_ref, vn_ref, freq_ref,
                 k_ring, bias_scr, cos_scr, sin_scr, y_scr):
        self.sink_ref, self.q_ref, self.v_refs, self.k_ring, self.y_scr = (
            sink_ref, q_ref, (vp_ref, vc_ref, vn_ref), k_ring, y_scr)
        freq = freq_ref[...]
        rows = ATT_GROUP * BLOCK
        self.log2e = math.log2(math.e)
        base_cos = cos_scr[...]
        base_sin = sin_scr[...]
        self.slots = (lax.rem(n + 2, 3), lax.rem(n, 3), lax.rem(n + 1, 3))
        k_ring[self.slots[2]] = _rope_keys(kn_ref, _rope_tables(n + 1, freq, base_cos, base_sin, 1.0))
        self.tab_q = _rope_tables(n, freq, base_cos, base_sin, HEAD_DIM ** -0.5 * self.log2e)
        self.bias_p = bias_scr[jnp.where(n > 0, 0, 2)]
        self.bias_n = bias_scr[jnp.where(n < nb - 1, 1, 2)]
        self.lane_group = lax.broadcasted_iota(jnp.int32, (1, rows), 1) // BLOCK
        self.first_row = lax.broadcasted_iota(jnp.int32, (BF16_SUBLANES, rows), 0) == 0
        self.ones = jnp.ones((3 * BLOCK + BF16_SUBLANES, HEAD_DIM), BF16)

    def scores(self, h):
        ks = slice(h * HEAD_DIM, (h + 1) * HEAD_DIM)
        k_all = jnp.concatenate([self.k_ring[s, :, ks] for s in self.slots], axis=0)
        q_all = jnp.concatenate(
            [_rope(self.q_ref[:, (h * ATT_GROUP + g) * HEAD_DIM:(h * ATT_GROUP + g + 1) * HEAD_DIM]
                   .astype(F32), self.tab_q).astype(BF16) for g in range(ATT_GROUP)], axis=0)
        return lax.dot_general(k_all, q_all, (((1,), (1,)), ((), ())), preferred_element_type=F32)

    def values(self, h, s):
        ks = slice(h * HEAD_DIM, (h + 1) * HEAD_DIM)
        sink = jnp.zeros((1, ATT_GROUP * BLOCK), F32)
        for g in range(ATT_GROUP):
            sink = jnp.where(self.lane_group == g, self.sink_ref[h * ATT_GROUP + g] * self.log2e, sink)
        s_p = jnp.minimum(s[:BLOCK], self.bias_p)
        s_c = s[BLOCK:2 * BLOCK]
        s_n = jnp.minimum(s[2 * BLOCK:], self.bias_n)
        m = jnp.max(jnp.maximum(jnp.maximum(s_p, s_c), s_n), axis=0, keepdims=True)
        m = jnp.maximum(m, sink)
        p_sink = jnp.where(self.first_row, jnp.exp2(sink - m), 0.0)
        p = jnp.concatenate([jnp.exp2(s_p - m), jnp.exp2(s_c - m), jnp.exp2(s_n - m), p_sink],
                            axis=0).astype(BF16)
        v_all = jnp.concatenate([r[:, ks] for r in self.v_refs]
                                + [jnp.zeros((BF16_SUBLANES, HEAD_DIM), BF16)], axis=0)
        v_aug = jnp.concatenate([v_all, self.ones], axis=1)
        o_aug = lax.dot_general(p, v_aug, (((0,), (0,)), ((), ())), preferred_element_type=F32)
        o = o_aug[:, :HEAD_DIM] / o_aug[:, HEAD_DIM:]
        for g in range(ATT_GROUP):
            hq = h * ATT_GROUP + g
            self.y_scr[:, hq * HEAD_DIM:(hq + 1) * HEAD_DIM] = o[g * BLOCK:(g + 1) * BLOCK]

    def result(self, z0_ref, z1_ref, g_ref):
        z = jnp.concatenate([z0_ref[...], z1_ref[...]], axis=1).astype(F32)
        return _norm_gate(self.y_scr[...], g_ref[...], z)


def _lru_direction(d, reverse, chunk, n_chunks, x_ref, xp_ref, xn_ref, cw_ref, cb_ref, wg_ref,
                   bg_ref, lam_ref, out_ref, xe_scr, carry_scr):
    t = LRU_CHUNK
    halo = BF16_SUBLANES
    for blk in range(LRU_BLOCKS):
        cs = slice(blk * LRU_BLOCK_DIM, (blk + 1) * LRU_BLOCK_DIM)
        xe_scr[d, blk, 0:halo, :] = jnp.where(chunk > 0, xp_ref[:, cs].astype(F32), 0.0)
        xe_scr[d, blk, halo:halo + t, :] = x_ref[:, cs].astype(F32)
        xe_scr[d, blk, halo + t:, :] = jnp.where(chunk < n_chunks - 1, xn_ref[:, cs].astype(F32), 0.0)

    neg_lam = -lam_ref[d:d + 1, :]
    softplus = jnp.maximum(neg_lam, 0.0) + jnp.log(1.0 + jnp.exp(-jnp.abs(neg_lam)))
    half_decay = (-0.5 * LRU_C * math.log2(math.e)) * softplus
    row0 = [8 * sum(LRU_SUBCHUNK_SEGS[:k]) for k in range(len(LRU_SUBCHUNK_SEGS))]
    subchunks = list(zip(row0, LRU_SUBCHUNK_SEGS))
    for blk in range(LRU_BLOCKS):
        cs = slice(blk * LRU_BLOCK_DIM, (blk + 1) * LRU_BLOCK_DIM)
        taps = [jnp.broadcast_to(cw_ref[j:j + 1, cs], (8, LRU_BLOCK_DIM)) for j in range(CONV_WIDTH)]
        conv_bias = jnp.broadcast_to(cb_ref[:, cs], (8, LRU_BLOCK_DIM))
        hd = half_decay[:, cs]
        cin = carry_scr[d:d + 1, cs]
        for r0, seg in (reversed(subchunks) if reverse else subchunks):
            order = range(seg - 1, -1, -1) if reverse else range(seg)
            tiles = []
            for m in range(seg):
                acc = conv_bias
                for j in range(CONV_WIDTH):
                    start = halo + r0 + m + j - CONV_LEFT
                    acc = acc + taps[j] * xe_scr[d, blk, pl.ds(start, 8, stride=seg), :]
                tiles.append(acc)
            xc = jnp.concatenate(tiles, axis=0)
            gates = jnp.dot(xc.astype(BF16), wg_ref[d, blk], preferred_element_type=F32)
            t_r = jnp.tanh(gates[:, :LRU_BLOCK_DIM] + bg_ref[d, 0:1, cs])
            t_i = jnp.tanh(gates[:, LRU_BLOCK_DIM:] + bg_ref[d, 1:2, cs])
            a = jnp.exp2(hd * t_r + hd)
            w = 1.0 - a * a
            b = w * lax.rsqrt(jnp.maximum(w, F32_TINY)) * ((t_i + 1.0) * 0.5) * xc
            tile = lambda v, m: v[8 * m:8 * m + 8]
            hh = jnp.zeros((8, LRU_BLOCK_DIM), F32)
            pp = jnp.ones((8, LRU_BLOCK_DIM), F32)
            for m in order:
                hh = tile(a, m) * hh + tile(b, m)
                pp = tile(a, m) * pp
            entering = [None] * 8
            for s in (range(7, -1, -1) if reverse else range(8)):
                entering[s] = cin
                cin = pp[s:s + 1, :] * cin + hh[s:s + 1, :]
            h = jnp.concatenate(entering, axis=0)
            for m in order:
                h = tile(a, m) * h + tile(b, m)
                out_ref[blk, pl.ds(r0 + m, 8, stride=seg), :] = h
        carry_scr[d:d + 1, cs] = cin


def _lru_kernel(xf_ref, xfp_ref, xfn_ref, xb_ref, xbp_ref, xbn_ref, cw_ref, cb_ref, wg_ref, bg_ref,
                lam_ref, hf_ref, hb_ref, xe_scr, carry_scr):
    i = pl.program_id(0)
    nc = pl.num_programs(0)

    @pl.when(i == 0)
    def _():
        carry_scr[...] = jnp.zeros_like(carry_scr)

    shared = (cw_ref, cb_ref, wg_ref, bg_ref, lam_ref)
    scr = (xe_scr, carry_scr)
    _lru_direction(0, False, i, nc, xf_ref, xfp_ref, xfn_ref, *shared, hf_ref, *scr)
    _lru_direction(1, True, nc - 1 - i, nc, xb_ref, xbp_ref, xbn_ref, *shared, hb_ref, *scr)


def _lru(proj, conv_w, conv_b, w_a, b_a, w_x, b_x, lam):
    s_len = proj.shape[0]
    nc = s_len // LRU_CHUNK
    halo_per_chunk = LRU_CHUNK // BF16_SUBLANES
    n_halo = s_len // BF16_SUBLANES
    wg = (0.5 * jnp.concatenate([w_a, w_x], axis=-1)).astype(BF16)
    bg = 0.5 * jnp.stack([b_a, b_x], axis=1)

    def fwd(i):
        return i

    def bwd(i):
        return nc - 1 - i

    def specs(chunk_of):
        return [
            pl.BlockSpec((LRU_CHUNK, LRU_WIDTH), lambda i: (chunk_of(i), COL_XLRU)),
            pl.BlockSpec((BF16_SUBLANES, LRU_WIDTH),
                         lambda i: (jnp.maximum(chunk_of(i) * halo_per_chunk - 1, 0), COL_XLRU)),
            pl.BlockSpec((BF16_SUBLANES, LRU_WIDTH),
                         lambda i: (jnp.minimum((chunk_of(i) + 1) * halo_per_chunk, n_halo - 1), COL_XLRU)),
        ]

    def whole(shape):
        return pl.BlockSpec(shape, lambda i: (0,) * len(shape))

    return pl.pallas_call(
        _lru_kernel,
        grid=(nc,),
        in_specs=specs(fwd) + specs(bwd) + [
            whole((CONV_WIDTH, LRU_WIDTH)), whole((1, LRU_WIDTH)),
            whole((2, LRU_BLOCKS, LRU_BLOCK_DIM, 2 * LRU_BLOCK_DIM)),
            whole((2, 2, LRU_WIDTH)), whole((2, LRU_WIDTH))],
        out_specs=[pl.BlockSpec((LRU_BLOCKS, LRU_CHUNK, LRU_BLOCK_DIM), lambda i: (0, i, 0)),
                   pl.BlockSpec((LRU_BLOCKS, LRU_CHUNK, LRU_BLOCK_DIM), lambda i: (0, nc - 1 - i, 0))],
        out_shape=[jax.ShapeDtypeStruct((LRU_BLOCKS, s_len, LRU_BLOCK_DIM), F32),
                   jax.ShapeDtypeStruct((LRU_BLOCKS, s_len, LRU_BLOCK_DIM), F32)],
        scratch_shapes=[pltpu.VMEM((2, LRU_BLOCKS, LRU_CHUNK + 2 * BF16_SUBLANES, LRU_BLOCK_DIM), F32),
                        pltpu.VMEM((2, LRU_WIDTH), F32)],
        compiler_params=_params("arbitrary"),
        name="rglru",
    )(proj, proj, proj, proj, proj, proj, conv_w, conv_b.reshape(1, LRU_WIDTH), wg, bg, lam)


def _mem_scores(h, q_ref, kv_ref):
    cs = slice(h * MEM_HEAD_DIM, (h + 1) * MEM_HEAD_DIM)
    return lax.dot_general(q_ref[:, cs], kv_ref[:, cs], (((1,), (1,)), ((), ())),
                           preferred_element_type=F32)


def _mem_values(h, s, kv_ref, y_scr):
    cs = slice(h * MEM_HEAD_DIM, (h + 1) * MEM_HEAD_DIM)
    vs = slice(MEM_WIDTH + h * MEM_HEAD_DIM, MEM_WIDTH + (h + 1) * MEM_HEAD_DIM)
    s = s * MEM_HEAD_DIM ** -0.5
    m = jnp.max(s, axis=-1, keepdims=True)
    p = jnp.exp(s - m)
    denom = jnp.sum(p, axis=-1, keepdims=True)
    y_scr[:, cs] = jnp.dot(p.astype(BF16), kv_ref[:, vs], preferred_element_type=F32) / denom


def _lru_finish(hf_ref, hb_ref, zl_ref, gl_ref):
    y_lru = jnp.concatenate([hf_ref[blk] + hb_ref[blk] for blk in range(LRU_BLOCKS)], axis=1)
    return _norm_gate(y_lru, gl_ref[...], zl_ref[...].astype(F32))


def _tail_kernel(sink_ref,
                 q_ref, k0_ref, kn_ref, vp_ref, vc_ref, vn_ref, z0_ref, z1_ref, freq_ref, ga_ref,
                 qm_ref, kv_ref, zm_ref, gm_ref, hf_ref, hb_ref, zl_ref, gl_ref,
                 w_ref, x_ref, gp_ref, o_ref,
                 lhs_scr, acc_scr, ss_scr, k_ring, bias_scr, cos_scr, sin_scr, ya_scr, ym_scr):
    o = pl.program_id(0)
    j = pl.program_id(1)
    nj = pl.num_programs(1)
    n_row_tiles = pl.num_programs(0) - 2
    nb = n_row_tiles * nj

    n_chunks = ATT_KV_HEADS
    chunk = TAIL_TN // n_chunks

    def stage_b_chunk(c, part):
        slot = lax.rem(o + 1, 2)
        cols = slice(c * chunk, (c + 1) * chunk)
        acc = jnp.dot(lhs_scr[slot], w_ref[:, cols], preferred_element_type=F32)
        acc_scr[slot, j, :, cols] = acc.astype(acc_scr.dtype)
        sq = acc * acc
        for k in range(chunk // LANES):
            piece = sq[:, k * LANES:(k + 1) * LANES]
            part = piece if part is None else part + piece
        return part

    def stages_ab(with_a, with_b):
        part = None
        if with_b:
            part = stage_b_chunk(0, part)
        if with_a:
            n = o * nj + j
            attn = _AttnBlock(n, nb, sink_ref, q_ref, kn_ref, vp_ref, vc_ref, vn_ref, freq_ref,
                              k_ring, bias_scr, cos_scr, sin_scr, ya_scr)
            s_att = [attn.scores(h) for h in range(ATT_KV_HEADS)]
            s_mem = [_mem_scores(h, qm_ref, kv_ref) for h in range(MEM_HEADS)]
        for pair in range(2):
            if with_b:
                part = stage_b_chunk(1 + pair, part)
            if with_a:
                for h in (2 * pair, 2 * pair + 1):
                    attn.values(h, s_att[h])
                    _mem_values(h, s_mem[h], kv_ref, ym_scr)
        if with_b:
            part = stage_b_chunk(3, part)
            slot = lax.rem(o + 1, 2)
            ss_scr[slot] = jnp.where(j == 0, part, ss_scr[slot] + part)
        if with_a:
            rows = pl.ds(pl.multiple_of(j * BLOCK, BLOCK), BLOCK)
            slot = lax.rem(o, 2)
            lhs_scr[slot, rows, 0:ATT_WIDTH] = attn.result(z0_ref, z1_ref, ga_ref).astype(BF16)
            lhs_scr[slot, rows, ATT_WIDTH:ATT_WIDTH + LRU_WIDTH] = _lru_finish(
                hf_ref, hb_ref, zl_ref, gl_ref).astype(BF16)
            lhs_scr[slot, rows, ATT_WIDTH + LRU_WIDTH:] = _norm_gate(
                ym_scr[...], gm_ref[...], zm_ref[...].astype(F32)).astype(BF16)

    def stage_c():
        slot = lax.rem(o, 2)
        ms = jnp.sum(ss_scr[slot], axis=-1, keepdims=True) * (1.0 / D_MODEL)
        o_ref[...] = x_ref[...] + acc_scr[slot, j].astype(F32) * lax.rsqrt(ms + EPS) * gp_ref[...]

    @pl.when(o == 0)
    def _():
        @pl.when(j == 0)
        def _():
            ss_scr[...] = jnp.zeros_like(ss_scr)
            _attn_init(k0_ref, freq_ref, k_ring, bias_scr, cos_scr, sin_scr)

        stages_ab(True, False)

    @pl.when(o == 1)
    def _():
        stages_ab(True, True)

    @pl.when((o >= 2) & (o < n_row_tiles))
    def _():
        stage_c()
        stages_ab(True, True)

    @pl.when(o == n_row_tiles)
    def _():
        stage_c()
        stages_ab(False, True)

    @pl.when(o == n_row_tiles + 1)
    def _():
        stage_c()


def _tail(proj, kv, h_fwd, h_bwd, sink, g_att, g_mem, g_lru, w_out, x, post_g):
    s_len = proj.shape[0]
    tm, tn = TAIL_TM, TAIL_TN
    ni, nj = s_len // tm, D_MODEL // tn
    nb = s_len // BLOCK
    assert tm == nj * BLOCK
    inv_freq = jnp.power(jnp.float32(ROPE_THETA),
                         -jnp.arange(ROT_HALF, dtype=F32) * 2.0 / ROT_DIM)
    freq = jnp.concatenate([inv_freq, inv_freq, jnp.zeros((HEAD_DIM - ROT_DIM,), F32)]).reshape(1, HEAD_DIM)

    def block_of(o, j):
        return jnp.where(o < ni, o * nj + j, nb - 1)

    def rows_spec(width, col, shift=0):
        return pl.BlockSpec((BLOCK, width),
                            lambda o, j, sink: (jnp.clip(block_of(o, j) + shift, 0, nb - 1), col))

    def const_spec(shape):
        return pl.BlockSpec(shape, lambda o, j, sink: (0,) * len(shape))

    slab_spec = pl.BlockSpec((LRU_BLOCKS, BLOCK, LRU_BLOCK_DIM), lambda o, j, sink: (0, block_of(o, j), 0))

    def w_col(o, j, sink):
        return (jnp.where(o == 0, 0, jnp.where(o <= ni, j, nj - 1)), 0, 0)

    def finished_tile(o, j, sink):
        return (jnp.maximum(o - 2, 0), jnp.where(o >= 2, j, 0))

    grid_spec = pltpu.PrefetchScalarGridSpec(
        num_scalar_prefetch=1,
        grid=(ni + 2, nj),
        in_specs=[rows_spec(ATT_WIDTH, 0),
                  pl.BlockSpec((BLOCK, KV_WIDTH), lambda o, j, sink: (0, COL_K512)),
                  rows_spec(KV_WIDTH, COL_K512, 1),
                  rows_spec(KV_WIDTH, COL_V512, -1), rows_spec(KV_WIDTH, COL_V512), rows_spec(KV_WIDTH, COL_V512, 1),
                  rows_spec(1024, COL_ZATT), rows_spec(1024, COL_ZATT + 1),
                  const_spec((1, HEAD_DIM)), const_spec((1, ATT_WIDTH)),
                  rows_spec(MEM_WIDTH, COL_QMEM), const_spec((N_MEM, 2 * MEM_WIDTH)),
                  rows_spec(MEM_WIDTH, COL_ZMEM), const_spec((1, MEM_WIDTH)),
                  slab_spec, slab_spec, rows_spec(LRU_WIDTH, COL_ZLRU), const_spec((1, LRU_WIDTH)),
                  pl.BlockSpec((None, D_MODEL, tn), w_col),
                  pl.BlockSpec((tm, tn), finished_tile),
                  pl.BlockSpec((1, tn), lambda o, j, sink: (0, j))],
        out_specs=pl.BlockSpec((tm, tn), finished_tile),
        scratch_shapes=[pltpu.VMEM((2, tm, D_MODEL), BF16),
                        pltpu.VMEM((2, nj, tm, tn), BF16),
                        pltpu.VMEM((2, tm, LANES), F32),
                        pltpu.VMEM((3, BLOCK, KV_WIDTH), BF16),
                        pltpu.VMEM((3, BLOCK, ATT_GROUP * BLOCK), F32),
                        pltpu.VMEM((BLOCK, HEAD_DIM), F32),
                        pltpu.VMEM((BLOCK, HEAD_DIM), F32),
                        pltpu.VMEM((BLOCK, ATT_WIDTH), F32),
                        pltpu.VMEM((BLOCK, MEM_WIDTH), F32)],
    )
    return pl.pallas_call(
        _tail_kernel,
        grid_spec=grid_spec,
        out_shape=jax.ShapeDtypeStruct((s_len, D_MODEL), F32),
        compiler_params=_params("arbitrary", "arbitrary"),
        name="mixers_out_proj_post_norm",
    )(sink, proj, proj, proj, proj, proj, proj, proj, proj, freq, g_att.reshape(1, ATT_WIDTH),
      proj, kv, proj, g_mem.reshape(1, MEM_WIDTH), h_fwd, h_bwd, proj, g_lru.reshape(1, LRU_WIDTH),
      w_out, x, post_g.reshape(1, D_MODEL))


def _layer(h, mem, pre_g, w_in, sink, conv_w, conv_b, w_a, b_a, w_x, b_x, lam, mem_g, w_mem_kv,
           g_att, g_lru, g_mem, w_out, post_g):
    proj, w_out_bf, w_mem_kv_bf = _in_proj(h, pre_g, w_in, w_out, w_mem_kv)
    kv = _mem_kv(mem, mem_g, w_mem_kv_bf)
    h_fwd, h_bwd = _lru(proj, conv_w, conv_b, w_a, b_a, w_x, b_x, lam)
    return _tail(proj, kv, h_fwd, h_bwd, sink, g_att, g_mem, g_lru, w_out_bf, h, post_g)


def kernel(x, mem, pre_norm_gain, w_in, att_sink, conv_w, conv_b, lru_w_a, lru_b_a, lru_w_x, lru_b_x,
           lru_lambda, mem_norm_gain, w_mem_kv, att_out_gain, lru_out_gain, mem_out_gain, w_out,
           post_norm_gain):
    batch, s_len, d_model = x.shape
    depth = w_in.shape[0]
    outs = []
    for b in range(batch):
        h = x.reshape(s_len, d_model) if batch == 1 else x[b]
        m = mem.reshape(mem.shape[1], d_model) if batch == 1 else mem[b]
        for l in range(depth):
            h = _layer(h, m, pre_norm_gain[l], w_in[l], att_sink[l], conv_w[l], conv_b[l],
                       lru_w_a[l], lru_b_a[l], lru_w_x[l], lru_b_x[l], lru_lambda[l],
                       mem_norm_gain[l], w_mem_kv[l], att_out_gain[l], lru_out_gain[l],
                       mem_out_gain[l], w_out[l], post_norm_gain[l])
        outs.append(h)
    return outs[0].reshape(x.shape) if batch == 1 else jnp.stack(outs, axis=0)
```

```python
import math

import jax
import jax.numpy as jnp
from jax import lax
from jax.experimental import pallas as pl
from jax.experimental.pallas import tpu as pltpu

F32 = jnp.float32
BF16 = jnp.bfloat16

D_MODEL = 4096
N_MEM = 256
HEAD_DIM = 128
ATT_WIDTH = 2048
ATT_HEADS = 16
ATT_KV_HEADS = 4
ATT_GROUP = ATT_HEADS // ATT_KV_HEADS
KV_WIDTH = ATT_KV_HEADS * HEAD_DIM
BLOCK = 128
ROPE_THETA = 500000.0
ROT_DIM = HEAD_DIM // 4
ROT_HALF = ROT_DIM // 2
LRU_WIDTH = 1024
LRU_BLOCKS = 8
LRU_BLOCK_DIM = LRU_WIDTH // LRU_BLOCKS
CONV_WIDTH = 4
CONV_LEFT = 2
LRU_C = 8.0
MEM_HEADS = 4
MEM_WIDTH = 1024
MEM_HEAD_DIM = MEM_WIDTH // MEM_HEADS
EPS = 1e-6
IN_WIDTH = 9216
F32_MIN = float(jnp.finfo(jnp.float32).min)
F32_TINY = float(jnp.finfo(jnp.float32).tiny)

COL_XLRU = 3
COL_QMEM = 4
COL_ZATT = 5
COL_ZLRU = 7
COL_ZMEM = 8
COL_K512 = 4
COL_V512 = 5

VMEM_LIMIT_BYTES = 60000 * 1024
IN_PROJ_VMEM_LIMIT_BYTES = 62 * 1024 * 1024
BF16_SUBLANES = 16

MEM_KV_TN = 1024
IN_TM = 1024
IN_TN = 768
IN_KC = 512
IN_KCHUNKS = D_MODEL // IN_KC
TAIL_TM = 512
TAIL_TN = 1024
LANES = 128
LRU_CHUNK = 512
LRU_SUBCHUNK_SEGS = (36, 28)
assert 8 * sum(LRU_SUBCHUNK_SEGS) == LRU_CHUNK and all(s % 8 == 4 for s in LRU_SUBCHUNK_SEGS)


def _params(*sem, vmem_limit_bytes=VMEM_LIMIT_BYTES):
    return pltpu.CompilerParams(dimension_semantics=sem, vmem_limit_bytes=vmem_limit_bytes)


def _sigmoid(t):
    return 1.0 / (1.0 + jnp.exp(-t))


def _norm_gate(y, g, z):
    ms = jnp.mean(y * y, axis=-1, keepdims=True)
    return y * lax.rsqrt(ms + EPS) * g * (z * _sigmoid(z))


def _mem_kv_kernel(mem_ref, g_ref, w_ref, o_ref):
    x = mem_ref[...]
    ms = jnp.mean(x * x, axis=-1, keepdims=True)
    u = (x * lax.rsqrt(ms + EPS) * g_ref[...]).astype(BF16)
    o_ref[...] = jnp.dot(u, w_ref[...], preferred_element_type=F32).astype(o_ref.dtype)


def _mem_kv(mem, g, w_mem_kv_bf):
    m, d = mem.shape
    n = w_mem_kv_bf.shape[1]
    return pl.pallas_call(
        _mem_kv_kernel,
        grid=(n // MEM_KV_TN,),
        in_specs=[pl.BlockSpec((m, d), lambda j: (0, 0)),
                  pl.BlockSpec((1, d), lambda j: (0, 0)),
                  pl.BlockSpec((d, MEM_KV_TN), lambda j: (0, j))],
        out_specs=pl.BlockSpec((m, MEM_KV_TN), lambda j: (0, j)),
        out_shape=jax.ShapeDtypeStruct((m, n), BF16),
        compiler_params=_params("parallel"),
        name="mem_kv_proj",
    )(mem, g.reshape(1, d), w_mem_kv_bf)


def _in_proj_kernel(x_ref, g_ref, w_ref, wo_ref, wk_ref, proj_ref, wo_bf_ref, wk_bf_ref,
                    u_scr, ss_scr, rr_scr):
    o = pl.program_id(0)
    j = pl.program_id(1)
    has_chunk = (j < IN_KCHUNKS) & (o < pl.num_programs(0) - 1)

    def cast_side_streams():
        for c in range(D_MODEL // TAIL_TN):
            wo_bf_ref[c] = wo_ref[:, c * TAIL_TN:(c + 1) * TAIL_TN].astype(BF16)
        wk_bf_ref[...] = wk_ref[...].astype(BF16)

    def take_chunk_of_next_tile():
        slot = lax.rem(o, 2)
        x = x_ref[...]
        u_scr[slot, j] = (x * g_ref[...]).astype(BF16)
        sq = x * x
        part = sq[:, 0:LANES]
        for c in range(1, IN_KC // LANES):
            part = part + sq[:, c * LANES:(c + 1) * LANES]
        ss = jnp.where(j == 0, part, ss_scr[slot] + part)
        ss_scr[slot] = ss
        r = lax.rsqrt(jnp.sum(ss, axis=-1, keepdims=True) * (1.0 / D_MODEL) + EPS)
        rr_scr[slot] = jnp.broadcast_to(r, (IN_TM, LANES))

    def multiply_tile():
        slot = lax.rem(o + 1, 2)
        acc = None
        for c in range(IN_KCHUNKS):
            d = jnp.dot(u_scr[slot, c], w_ref[c * IN_KC:(c + 1) * IN_KC, :].astype(BF16),
                        preferred_element_type=F32)
            acc = d if acc is None else acc + d
        rr = rr_scr[slot]
        for c in range(IN_TN // LANES):
            cols = slice(c * LANES, (c + 1) * LANES)
            proj_ref[:, cols] = (acc[:, cols] * rr).astype(proj_ref.dtype)

    @pl.when(o == 0)
    def _():
        @pl.when(j == 0)
        def _():
            ss_scr[...] = jnp.zeros_like(ss_scr)

        cast_side_streams()

        @pl.when(has_chunk)
        def _():
            take_chunk_of_next_tile()

    @pl.when((o > 0) & has_chunk)
    def _():
        multiply_tile()
        cast_side_streams()
        take_chunk_of_next_tile()

    @pl.when((o > 0) & jnp.logical_not(has_chunk))
    def _():
        multiply_tile()
        cast_side_streams()


def _in_proj(x, pre_g, w_in, w_out, w_mem_kv):
    s_len = x.shape[0]
    ni, nj = s_len // IN_TM, IN_WIDTH // IN_TN
    n_steps = (ni + 1) * nj
    side_rows = 64
    n_side = D_MODEL // side_rows
    assert n_side <= n_steps and nj >= IN_KCHUNKS

    def x_tile(o, j):
        return (jnp.minimum(o, ni - 1), jnp.minimum(j, IN_KCHUNKS - 1))

    def side(o, j):
        return (jnp.minimum(o * nj + j, n_side - 1), 0)

    return pl.pallas_call(
        _in_proj_kernel,
        grid=(ni + 1, nj),
        in_specs=[pl.BlockSpec((IN_TM, IN_KC), x_tile),
                  pl.BlockSpec((1, IN_KC), lambda o, j: (0, x_tile(o, j)[1])),
                  pl.BlockSpec((D_MODEL, IN_TN), lambda o, j: (0, jnp.where(o > 0, j, 0))),
                  pl.BlockSpec((side_rows, D_MODEL), side),
                  pl.BlockSpec((side_rows, 2 * MEM_WIDTH), side)],
        out_specs=[pl.BlockSpec((IN_TM, IN_TN), lambda o, j: (jnp.maximum(o - 1, 0), jnp.where(o > 0, j, 0))),
                   pl.BlockSpec((D_MODEL // TAIL_TN, side_rows, TAIL_TN), lambda o, j: (0, side(o, j)[0], 0)),
                   pl.BlockSpec((side_rows, 2 * MEM_WIDTH), side)],
        out_shape=[jax.ShapeDtypeStruct((s_len, IN_WIDTH), BF16),
                   jax.ShapeDtypeStruct((D_MODEL // TAIL_TN, D_MODEL, TAIL_TN), BF16),
                   jax.ShapeDtypeStruct((D_MODEL, 2 * MEM_WIDTH), BF16)],
        scratch_shapes=[pltpu.VMEM((2, IN_KCHUNKS, IN_TM, IN_KC), BF16),
                        pltpu.VMEM((2, IN_TM, LANES), F32),
                        pltpu.VMEM((2, IN_TM, LANES), F32)],
        compiler_params=_params("arbitrary", "arbitrary", vmem_limit_bytes=IN_PROJ_VMEM_LIMIT_BYTES),
        name="in_proj_pre_norm",
    )(x, pre_g.reshape(1, D_MODEL), w_in, w_out, w_mem_kv)


def _rope_tables(block_idx, freq, base_cos, base_sin, mult):
    a = (block_idx * BLOCK).astype(F32) * freq
    ca, sa = jnp.cos(a) * mult, jnp.sin(a) * mult
    c = ca * base_cos - sa * base_sin
    s = sa * base_cos + ca * base_sin
    lane = lax.broadcasted_iota(jnp.int32, (BLOCK, HEAD_DIM), 1)
    return c, jnp.where(lane < ROT_HALF, -s, 0.0), jnp.where(lane >= ROT_HALF, s, 0.0)


def _rope(t, tables):
    c, sin_lo, sin_hi = tables
    return (t * c + pltpu.roll(t, HEAD_DIM - ROT_HALF, axis=1) * sin_lo
            + pltpu.roll(t, ROT_HALF, axis=1) * sin_hi)


def _rope_keys(k_ref, tables):
    return jnp.concatenate(
        [_rope(k_ref[:, h * HEAD_DIM:(h + 1) * HEAD_DIM].astype(F32), tables).astype(BF16)
         for h in range(ATT_KV_HEADS)], axis=1)


def _attn_init(k0_ref, freq_ref, k_ring, bias_scr, cos_scr, sin_scr):
    freq = freq_ref[...]
    rows = ATT_GROUP * BLOCK
    ang = lax.broadcasted_iota(jnp.int32, (BLOCK, HEAD_DIM), 0).astype(F32) * freq
    cos_scr[...] = jnp.cos(ang)
    sin_scr[...] = jnp.sin(ang)
    key = lax.broadcasted_iota(jnp.int32, (BLOCK, rows), 0)
    q_local = lax.broadcasted_iota(jnp.int32, (BLOCK, rows), 1) & (BLOCK - 1)
    bias_scr[0] = jnp.where(key >= q_local, jnp.inf, F32_MIN)
    bias_scr[1] = jnp.where(key <= q_local, jnp.inf, F32_MIN)
    bias_scr[2] = jnp.full((BLOCK, rows), F32_MIN, F32)
    k_ring[0] = _rope_keys(k0_ref, _rope_tables(jnp.int32(0), freq, cos_scr[...], sin_scr[...], 1.0))
    k_ring[2] = jnp.zeros((BLOCK, KV_WIDTH), BF16)


class _AttnBlock:
    def __init__(self, n, nb, sink_ref, q_ref, kn_ref, vp_ref, vc_ref, vn_ref, freq_ref,
                 k_ring, bias_scr, cos_scr, sin_scr, y_scr):
        self.sink_ref, self.q_ref, self.v_refs, self.k_ring, self.y_scr = (
            sink_ref, q_ref, (vp_ref, vc_ref, vn_ref), k_ring, y_scr)
        freq = freq_ref[...]
        rows = ATT_GROUP * BLOCK
        self.log2e = math.log2(math.e)
        base_cos = cos_scr[...]
        base_sin = sin_scr[...]
        self.slots = (lax.rem(n + 2, 3), lax.rem(n, 3), lax.rem(n + 1, 3))
        k_ring[self.slots[2]] = _rope_keys(kn_ref, _rope_tables(n + 1, freq, base_cos, base_sin, 1.0))
        self.tab_q = _rope_tables(n, freq, base_cos, base_sin, HEAD_DIM ** -0.5 * self.log2e)
        self.bias_p = bias_scr[jnp.where(n > 0, 0, 2)]
        self.bias_n = bias_scr[jnp.where(n < nb - 1, 1, 2)]
        self.lane_group = lax.broadcasted_iota(jnp.int32, (1, rows), 1) // BLOCK
        self.first_row = lax.broadcasted_iota(jnp.int32, (BF16_SUBLANES, rows), 0) == 0
        self.ones = jnp.ones((3 * BLOCK + BF16_SUBLANES, HEAD_DIM), BF16)

    def scores(self, h):
        ks = slice(h * HEAD_DIM, (h + 1) * HEAD_DIM)
        k_all = jnp.concatenate([self.k_ring[s, :, ks] for s in self.slots], axis=0)
        q_all = jnp.concatenate(
            [_rope(self.q_ref[:, (h * ATT_GROUP + g) * HEAD_DIM:(h * ATT_GROUP + g + 1) * HEAD_DIM]
                   .astype(F32), self.tab_q).astype(BF16) for g in range(ATT_GROUP)], axis=0)
        return lax.dot_general(k_all, q_all, (((1,), (1,)), ((), ())), preferred_element_type=F32)

    def values(self, h, s):
        ks = slice(h * HEAD_DIM, (h + 1) * HEAD_DIM)
        sink = jnp.zeros((1, ATT_GROUP * BLOCK), F32)
        for g in range(ATT_GROUP):
            sink = jnp.where(self.lane_group == g, self.sink_ref[h * ATT_GROUP + g] * self.log2e, sink)
        s_p = jnp.minimum(s[:BLOCK], self.bias_p)
        s_c = s[BLOCK:2 * BLOCK]
        s_n = jnp.minimum(s[2 * BLOCK:], self.bias_n)
        m = jnp.max(jnp.maximum(jnp.maximum(s_p, s_c), s_n), axis=0, keepdims=True)
        m = jnp.maximum(m, sink)
        p_sink = jnp.where(self.first_row, jnp.exp2(sink - m), 0.0)
        p = jnp.concatenate([jnp.exp2(s_p - m), jnp.exp2(s_c - m), jnp.exp2(s_n - m), p_sink],
                            axis=0).astype(BF16)
        v_all = jnp.concatenate([r[:, ks] for r in self.v_refs]
                                + [jnp.zeros((BF16_SUBLANES, HEAD_DIM), BF16)], axis=0)
        v_aug = jnp.concatenate([v_all, self.ones], axis=1)
        o_aug = lax.dot_general(p, v_aug, (((0,), (0,)), ((), ())), preferred_element_type=F32)
        o = o_aug[:, :HEAD_DIM] / o_aug[:, HEAD_DIM:]
        for g in range(ATT_GROUP):
            hq = h * ATT_GROUP + g
            self.y_scr[:, hq * HEAD_DIM:(hq + 1) * HEAD_DIM] = o[g * BLOCK:(g + 1) * BLOCK]

    def result(self, z0_ref, z1_ref, g_ref):
        z = jnp.concatenate([z0_ref[...], z1_ref[...]], axis=1).astype(F32)
        return _norm_gate(self.y_scr[...], g_ref[...], z)


def _lru_direction(d, reverse, chunk, n_chunks, x_ref, xp_ref, xn_ref, cw_ref, cb_ref, wg_ref,
                   bg_ref, lam_ref, out_ref, xe_scr, carry_scr):
    t = LRU_CHUNK
    halo = BF16_SUBLANES
    for blk in range(LRU_BLOCKS):
        cs = slice(blk * LRU_BLOCK_DIM, (blk + 1) * LRU_BLOCK_DIM)
        xe_scr[d, blk, 0:halo, :] = jnp.where(chunk > 0, xp_ref[:, cs].astype(F32), 0.0)
        xe_scr[d, blk, halo:halo + t, :] = x_ref[:, cs].astype(F32)
        xe_scr[d, blk, halo + t:, :] = jnp.where(chunk < n_chunks - 1, xn_ref[:, cs].astype(F32), 0.0)

    neg_lam = -lam_ref[d:d + 1, :]
    softplus = jnp.maximum(neg_lam, 0.0) + jnp.log(1.0 + jnp.exp(-jnp.abs(neg_lam)))
    half_decay = (-0.5 * LRU_C * math.log2(math.e)) * softplus
    row0 = [8 * sum(LRU_SUBCHUNK_SEGS[:k]) for k in range(len(LRU_SUBCHUNK_SEGS))]
    subchunks = list(zip(row0, LRU_SUBCHUNK_SEGS))
    for blk in range(LRU_BLOCKS):
        cs = slice(blk * LRU_BLOCK_DIM, (blk + 1) * LRU_BLOCK_DIM)
        taps = [jnp.broadcast_to(cw_ref[j:j + 1, cs], (8, LRU_BLOCK_DIM)) for j in range(CONV_WIDTH)]
        conv_bias = jnp.broadcast_to(cb_ref[:, cs], (8, LRU_BLOCK_DIM))
        hd = half_decay[:, cs]
        cin = carry_scr[d:d + 1, cs]
        for r0, seg in (reversed(subchunks) if reverse else subchunks):
            order = range(seg - 1, -1, -1) if reverse else range(seg)
            tiles = []
            for m in range(seg):
                acc = conv_bias
                for j in range(CONV_WIDTH):
                    start = halo + r0 + m + j - CONV_LEFT
                    acc = acc + taps[j] * xe_scr[d, blk, pl.ds(start, 8, stride=seg), :]
                tiles.append(acc)
            xc = jnp.concatenate(tiles, axis=0)
            gates = jnp.dot(xc.astype(BF16), wg_ref[d, blk], preferred_element_type=F32)
            t_r = jnp.tanh(gates[:, :LRU_BLOCK_DIM] + bg_ref[d, 0:1, cs])
            t_i = jnp.tanh(gates[:, LRU_BLOCK_DIM:] + bg_ref[d, 1:2, cs])
            a = jnp.exp2(hd * t_r + hd)
            w = 1.0 - a * a
            b = w * lax.rsqrt(jnp.maximum(w, F32_TINY)) * ((t_i + 1.0) * 0.5) * xc
            tile = lambda v, m: v[8 * m:8 * m + 8]
            hh = jnp.zeros((8, LRU_BLOCK_DIM), F32)
            pp = jnp.ones((8, LRU_BLOCK_DIM), F32)
            for m in order:
                hh = tile(a, m) * hh + tile(b, m)
                pp = tile(a, m) * pp
            entering = [None] * 8
            for s in (range(7, -1, -1) if reverse else range(8)):
                entering[s] = cin
                cin = pp[s:s + 1, :] * cin + hh[s:s + 1, :]
            h = jnp.concatenate(entering, axis=0)
            for m in order:
                h = tile(a, m) * h + tile(b, m)
                out_ref[blk, pl.ds(r0 + m, 8, stride=seg), :] = h
        carry_scr[d:d + 1, cs] = cin


def _lru_kernel(xf_ref, xfp_ref, xfn_ref, xb_ref, xbp_ref, xbn_ref, cw_ref, cb_ref, wg_ref, bg_ref,
                lam_ref, hf_ref, hb_ref, xe_scr, carry_scr):
    i = pl.program_id(0)
    nc = pl.num_programs(0)

    @pl.when(i == 0)
    def _():
        carry_scr[...] = jnp.zeros_like(carry_scr)

    shared = (cw_ref, cb_ref, wg_ref, bg_ref, lam_ref)
    scr = (xe_scr, carry_scr)
    _lru_direction(0, False, i, nc, xf_ref, xfp_ref, xfn_ref, *shared, hf_ref, *scr)
    _lru_direction(1, True, nc - 1 - i, nc, xb_ref, xbp_ref, xbn_ref, *shared, hb_ref, *scr)


def _lru(proj, conv_w, conv_b, w_a, b_a, w_x, b_x, lam):
    s_len = proj.shape[0]
    nc = s_len // LRU_CHUNK
    halo_per_chunk = LRU_CHUNK // BF16_SUBLANES
    n_halo = s_len // BF16_SUBLANES
    wg = (0.5 * jnp.concatenate([w_a, w_x], axis=-1)).astype(BF16)
    bg = 0.5 * jnp.stack([b_a, b_x], axis=1)

    def fwd(i):
        return i

    def bwd(i):
        return nc - 1 - i

    def specs(chunk_of):
        return [
            pl.BlockSpec((LRU_CHUNK, LRU_WIDTH), lambda i: (chunk_of(i), COL_XLRU)),
            pl.BlockSpec((BF16_SUBLANES, LRU_WIDTH),
                         lambda i: (jnp.maximum(chunk_of(i) * halo_per_chunk - 1, 0), COL_XLRU)),
            pl.BlockSpec((BF16_SUBLANES, LRU_WIDTH),
                         lambda i: (jnp.minimum((chunk_of(i) + 1) * halo_per_chunk, n_halo - 1), COL_XLRU)),
        ]

    def whole(shape):
        return pl.BlockSpec(shape, lambda i: (0,) * len(shape))

    return pl.pallas_call(
        _lru_kernel,
        grid=(nc,),
        in_specs=specs(fwd) + specs(bwd) + [
            whole((CONV_WIDTH, LRU_WIDTH)), whole((1, LRU_WIDTH)),
            whole((2, LRU_BLOCKS, LRU_BLOCK_DIM, 2 * LRU_BLOCK_DIM)),
            whole((2, 2, LRU_WIDTH)), whole((2, LRU_WIDTH))],
        out_specs=[pl.BlockSpec((LRU_BLOCKS, LRU_CHUNK, LRU_BLOCK_DIM), lambda i: (0, i, 0)),
                   pl.BlockSpec((LRU_BLOCKS, LRU_CHUNK, LRU_BLOCK_DIM), lambda i: (0, nc - 1 - i, 0))],
        out_shape=[jax.ShapeDtypeStruct((LRU_BLOCKS, s_len, LRU_BLOCK_DIM), F32),
                   jax.ShapeDtypeStruct((LRU_BLOCKS, s_len, LRU_BLOCK_DIM), F32)],
        scratch_shapes=[pltpu.VMEM((2, LRU_BLOCKS, LRU_CHUNK + 2 * BF16_SUBLANES, LRU_BLOCK_DIM), F32),
                        pltpu.VMEM((2, LRU_WIDTH), F32)],
        compiler_params=_params("arbitrary"),
        name="rglru",
    )(proj, proj, proj, proj, proj, proj, conv_w, conv_b.reshape(1, LRU_WIDTH), wg, bg, lam)


def _mem_scores(h, q_ref, kv_ref):
    cs = slice(h * MEM_HEAD_DIM, (h + 1) * MEM_HEAD_DIM)
    return lax.dot_general(q_ref[:, cs], kv_ref[:, cs], (((1,), (1,)), ((), ())),
                           preferred_element_type=F32)


def _mem_values(h, s, kv_ref, y_scr):
    cs = slice(h * MEM_HEAD_DIM, (h + 1) * MEM_HEAD_DIM)
    vs = slice(MEM_WIDTH + h * MEM_HEAD_DIM, MEM_WIDTH + (h + 1) * MEM_HEAD_DIM)
    s = s * MEM_HEAD_DIM ** -0.5
    m = jnp.max(s, axis=-1, keepdims=True)
    p = jnp.exp(s - m)
    denom = jnp.sum(p, axis=-1, keepdims=True)
    y_scr[:, cs] = jnp.dot(p.astype(BF16), kv_ref[:, vs], preferred_element_type=F32) / denom


def _lru_finish(hf_ref, hb_ref, zl_ref, gl_ref):
    y_lru = jnp.concatenate([hf_ref[blk] + hb_ref[blk] for blk in range(LRU_BLOCKS)], axis=1)
    return _norm_gate(y_lru, gl_ref[...], zl_ref[...].astype(F32))


def _tail_kernel(sink_ref,
                 q_ref, k0_ref, kn_ref, vp_ref, vc_ref, vn_ref, z0_ref, z1_ref, freq_ref, ga_ref,
                 qm_ref, kv_ref, zm_ref, gm_ref, hf_ref, hb_ref, zl_ref, gl_ref,
                 w_ref, x_ref, gp_ref, o_ref,
                 lhs_scr, acc_scr, ss_scr, k_ring, bias_scr, cos_scr, sin_scr, ya_scr, ym_scr):
    o = pl.program_id(0)
    j = pl.program_id(1)
    nj = pl.num_programs(1)
    n_row_tiles = pl.num_programs(0) - 2
    nb = n_row_tiles * nj

    n_chunks = ATT_KV_HEADS
    chunk = TAIL_TN // n_chunks

    def stage_b_chunk(c, part):
        slot = lax.rem(o + 1, 2)
        cols = slice(c * chunk, (c + 1) * chunk)
        acc = jnp.dot(lhs_scr[slot], w_ref[:, cols], preferred_element_type=F32)
        acc_scr[slot, j, :, cols] = acc.astype(acc_scr.dtype)
        sq = acc * acc
        for k in range(chunk // LANES):
            piece = sq[:, k * LANES:(k + 1) * LANES]
            part = piece if part is None else part + piece
        return part

    def stages_ab(with_a, with_b):
        part = None
        if with_b:
            part = stage_b_chunk(0, part)
        if with_a:
            n = o * nj + j
            attn = _AttnBlock(n, nb, sink_ref, q_ref, kn_ref, vp_ref, vc_ref, vn_ref, freq_ref,
                              k_ring, bias_scr, cos_scr, sin_scr, ya_scr)
            s_att = [attn.scores(h) for h in range(ATT_KV_HEADS)]
            s_mem = [_mem_scores(h, qm_ref, kv_ref) for h in range(MEM_HEADS)]
        for pair in range(2):
            if with_b:
                part = stage_b_chunk(1 + pair, part)
            if with_a:
                for h in (2 * pair, 2 * pair + 1):
                    attn.values(h, s_att[h])
                    _mem_values(h, s_mem[h], kv_ref, ym_scr)
        if with_b:
            part = stage_b_chunk(3, part)
            slot = lax.rem(o + 1, 2)
            ss_scr[slot] = jnp.where(j == 0, part, ss_scr[slot] + part)
        if with_a:
            rows = pl.ds(pl.multiple_of(j * BLOCK, BLOCK), BLOCK)
            slot = lax.rem(o, 2)
            lhs_scr[slot, rows, 0:ATT_WIDTH] = attn.result(z0_ref, z1_ref, ga_ref).astype(BF16)
            lhs_scr[slot, rows, ATT_WIDTH:ATT_WIDTH + LRU_WIDTH] = _lru_finish(
                hf_ref, hb_ref, zl_ref, gl_ref).astype(BF16)
            lhs_scr[slot, rows, ATT_WIDTH + LRU_WIDTH:] = _norm_gate(
                ym_scr[...], gm_ref[...], zm_ref[...].astype(F32)).astype(BF16)

    def stage_c():
        slot = lax.rem(o, 2)
        ms = jnp.sum(ss_scr[slot], axis=-1, keepdims=True) * (1.0 / D_MODEL)
        o_ref[...] = x_ref[...] + acc_scr[slot, j].astype(F32) * lax.rsqrt(ms + EPS) * gp_ref[...]

    @pl.when(o == 0)
    def _():
        @pl.when(j == 0)
        def _():
            ss_scr[...] = jnp.zeros_like(ss_scr)
            _attn_init(k0_ref, freq_ref, k_ring, bias_scr, cos_scr, sin_scr)

        stages_ab(True, False)

    @pl.when(o == 1)
    def _():
        stages_ab(True, True)

    @pl.when((o >= 2) & (o < n_row_tiles))
    def _():
        stage_c()
        stages_ab(True, True)

    @pl.when(o == n_row_tiles)
    def _():
        stage_c()
        stages_ab(False, True)

    @pl.when(o == n_row_tiles + 1)
    def _():
        stage_c()


def _tail(proj, kv, h_fwd, h_bwd, sink, g_att, g_mem, g_lru, w_out, x, post_g):
    s_len = proj.shape[0]
    tm, tn = TAIL_TM, TAIL_TN
    ni, nj = s_len // tm, D_MODEL // tn
    nb = s_len // BLOCK
    assert tm == nj * BLOCK
    inv_freq = jnp.power(jnp.float32(ROPE_THETA),
                         -jnp.arange(ROT_HALF, dtype=F32) * 2.0 / ROT_DIM)
    freq = jnp.concatenate([inv_freq, inv_freq, jnp.zeros((HEAD_DIM - ROT_DIM,), F32)]).reshape(1, HEAD_DIM)

    def block_of(o, j):
        return jnp.where(o < ni, o * nj + j, nb - 1)

    def rows_spec(width, col, shift=0):
        return pl.BlockSpec((BLOCK, width),
                            lambda o, j, sink: (jnp.clip(block_of(o, j) + shift, 0, nb - 1), col))

    def const_spec(shape):
        return pl.BlockSpec(shape, lambda o, j, sink: (0,) * len(shape))

    slab_spec = pl.BlockSpec((LRU_BLOCKS, BLOCK, LRU_BLOCK_DIM), lambda o, j, sink: (0, block_of(o, j), 0))

    def w_col(o, j, sink):
        return (jnp.where(o == 0, 0, jnp.where(o <= ni, j, nj - 1)), 0, 0)

    def finished_tile(o, j, sink):
        return (jnp.maximum(o - 2, 0), jnp.where(o >= 2, j, 0))

    grid_spec = pltpu.PrefetchScalarGridSpec(
        num_scalar_prefetch=1,
        grid=(ni + 2, nj),
        in_specs=[rows_spec(ATT_WIDTH, 0),
                  pl.BlockSpec((BLOCK, KV_WIDTH), lambda o, j, sink: (0, COL_K512)),
                  rows_spec(KV_WIDTH, COL_K512, 1),
                  rows_spec(KV_WIDTH, COL_V512, -1), rows_spec(KV_WIDTH, COL_V512), rows_spec(KV_WIDTH, COL_V512, 1),
                  rows_spec(1024, COL_ZATT), rows_spec(1024, COL_ZATT + 1),
                  const_spec((1, HEAD_DIM)), const_spec((1, ATT_WIDTH)),
                  rows_spec(MEM_WIDTH, COL_QMEM), const_spec((N_MEM, 2 * MEM_WIDTH)),
                  rows_spec(MEM_WIDTH, COL_ZMEM), const_spec((1, MEM_WIDTH)),
                  slab_spec, slab_spec, rows_spec(LRU_WIDTH, COL_ZLRU), const_spec((1, LRU_WIDTH)),
                  pl.BlockSpec((None, D_MODEL, tn), w_col),
                  pl.BlockSpec((tm, tn), finished_tile),
                  pl.BlockSpec((1, tn), lambda o, j, sink: (0, j))],
        out_specs=pl.BlockSpec((tm, tn), finished_tile),
        scratch_shapes=[pltpu.VMEM((2, tm, D_MODEL), BF16),
                        pltpu.VMEM((2, nj, tm, tn), BF16),
                        pltpu.VMEM((2, tm, LANES), F32),
                        pltpu.VMEM((3, BLOCK, KV_WIDTH), BF16),
                        pltpu.VMEM((3, BLOCK, ATT_GROUP * BLOCK), F32),
                        pltpu.VMEM((BLOCK, HEAD_DIM), F32),
                        pltpu.VMEM((BLOCK, HEAD_DIM), F32),
                        pltpu.VMEM((BLOCK, ATT_WIDTH), F32),
                        pltpu.VMEM((BLOCK, MEM_WIDTH), F32)],
    )
    return pl.pallas_call(
        _tail_kernel,
        grid_spec=grid_spec,
        out_shape=jax.ShapeDtypeStruct((s_len, D_MODEL), F32),
        compiler_params=_params("arbitrary", "arbitrary"),
        name="mixers_out_proj_post_norm",
    )(sink, proj, proj, proj, proj, proj, proj, proj, proj, freq, g_att.reshape(1, ATT_WIDTH),
      proj, kv, proj, g_mem.reshape(1, MEM_WIDTH), h_fwd, h_bwd, proj, g_lru.reshape(1, LRU_WIDTH),
      w_out, x, post_g.reshape(1, D_MODEL))


def _layer(h, mem, pre_g, w_in, sink, conv_w, conv_b, w_a, b_a, w_x, b_x, lam, mem_g, w_mem_kv,
           g_att, g_lru, g_mem, w_out, post_g):
    proj, w_out_bf, w_mem_kv_bf = _in_proj(h, pre_g, w_in, w_out, w_mem_kv)
    kv = _mem_kv(mem, mem_g, w_mem_kv_bf)
    h_fwd, h_bwd = _lru(proj, conv_w, conv_b, w_a, b_a, w_x, b_x, lam)
    return _tail(proj, kv, h_fwd, h_bwd, sink, g_att, g_mem, g_lru, w_out_bf, h, post_g)


def kernel(x, mem, pre_norm_gain, w_in, att_sink, conv_w, conv_b, lru_w_a, lru_b_a, lru_w_x, lru_b_x,
           lru_lambda, mem_norm_gain, w_mem_kv, att_out_gain, lru_out_gain, mem_out_gain, w_out,
           post_norm_gain):
    batch, s_len, d_model = x.shape
    depth = w_in.shape[0]
    outs = []
    for b in range(batch):
        h = x.reshape(s_len, d_model) if batch == 1 else x[b]
        m = mem.reshape(mem.shape[1], d_model) if batch == 1 else mem[b]
        for l in range(depth):
            h = _layer(h, m, pre_norm_gain[l], w_in[l], att_sink[l], conv_w[l], conv_b[l],
                       lru_w_a[l], lru_b_a[l], lru_w_x[l], lru_b_x[l], lru_lambda[l],
                       mem_norm_gain[l], w_mem_kv[l], att_out_gain[l], lru_out_gain[l],
                       mem_out_gain[l], w_out[l], post_norm_gain[l])
        outs.append(h)
    return outs[0].reshape(x.shape) if batch == 1 else jnp.stack(outs, axis=0)
```

```python
import math

import jax
import jax.numpy as jnp
from jax import lax
from jax.experimental import pallas as pl
from jax.experimental.pallas import tpu as pltpu

F32 = jnp.float32
BF16 = jnp.bfloat16

D_MODEL = 4096
N_MEM = 256
HEAD_DIM = 128
ATT_WIDTH = 2048
ATT_HEADS = 16
ATT_KV_HEADS = 4
ATT_GROUP = ATT_HEADS // ATT_KV_HEADS
KV_WIDTH = ATT_KV_HEADS * HEAD_DIM
BLOCK = 128
ROPE_THETA = 500000.0
ROT_DIM = HEAD_DIM // 4
ROT_HALF = ROT_DIM // 2
LRU_WIDTH = 1024
LRU_BLOCKS = 8
LRU_BLOCK_DIM = LRU_WIDTH // LRU_BLOCKS
CONV_WIDTH = 4
CONV_LEFT = 2
LRU_C = 8.0
MEM_HEADS = 4
MEM_WIDTH = 1024
MEM_HEAD_DIM = MEM_WIDTH // MEM_HEADS
EPS = 1e-6
IN_WIDTH = 9216
F32_MIN = float(jnp.finfo(jnp.float32).min)
F32_TINY = float(jnp.finfo(jnp.float32).tiny)

COL_XLRU = 3
COL_QMEM = 4
COL_ZATT = 5
COL_ZLRU = 7
COL_ZMEM = 8
COL_K512 = 4
COL_V512 = 5

VMEM_LIMIT_BYTES = 60000 * 1024
IN_PROJ_VMEM_LIMIT_BYTES = 62 * 1024 * 1024
BF16_SUBLANES = 16

MEM_KV_TN = 1024
IN_TM = 1024
IN_TN = 768
IN_KC = 512
IN_KCHUNKS = D_MODEL // IN_KC
TAIL_TM = 512
TAIL_TN = 1024
NORM_ROW_CHUNK = 32
LANES = 128
LRU_CHUNK = 512
LRU_SUBCHUNK_SEGS = (36, 28)
assert 8 * sum(LRU_SUBCHUNK_SEGS) == LRU_CHUNK and all(s % 8 == 4 for s in LRU_SUBCHUNK_SEGS)


def _params(*sem, vmem_limit_bytes=VMEM_LIMIT_BYTES):
    return pltpu.CompilerParams(dimension_semantics=sem, vmem_limit_bytes=vmem_limit_bytes)


def _sigmoid(t):
    return 1.0 / (1.0 + jnp.exp(-t))


def _norm_gate(y, g, z):
    ms = jnp.mean(y * y, axis=-1, keepdims=True)
    return y * lax.rsqrt(ms + EPS) * g * (z * _sigmoid(z))


def _mem_kv_kernel(mem_ref, g_ref, w_ref, o_ref):
    x = mem_ref[...]
    ms = jnp.mean(x * x, axis=-1, keepdims=True)
    u = (x * lax.rsqrt(ms + EPS) * g_ref[...]).astype(BF16)
    o_ref[...] = jnp.dot(u, w_ref[...], preferred_element_type=F32).astype(o_ref.dtype)


def _mem_kv(mem, g, w_mem_kv_bf):
    m, d = mem.shape
    n = w_mem_kv_bf.shape[1]
    return pl.pallas_call(
        _mem_kv_kernel,
        grid=(n // MEM_KV_TN,),
        in_specs=[pl.BlockSpec((m, d), lambda j: (0, 0)),
                  pl.BlockSpec((1, d), lambda j: (0, 0)),
                  pl.BlockSpec((d, MEM_KV_TN), lambda j: (0, j))],
        out_specs=pl.BlockSpec((m, MEM_KV_TN), lambda j: (0, j)),
        out_shape=jax.ShapeDtypeStruct((m, n), BF16),
        compiler_params=_params("parallel"),
        name="mem_kv_proj",
    )(mem, g.reshape(1, d), w_mem_kv_bf)


def _in_proj_kernel(x_ref, g_ref, w_ref, wo_ref, wk_ref, proj_ref, wo_bf_ref, wk_bf_ref,
                    u_scr, ss_scr, rr_scr):
    o = pl.program_id(0)
    j = pl.program_id(1)
    has_chunk = (j < IN_KCHUNKS) & (o < pl.num_programs(0) - 1)

    def cast_side_streams():
        for c in range(D_MODEL // TAIL_TN):
            wo_bf_ref[c] = wo_ref[:, c * TAIL_TN:(c + 1) * TAIL_TN].astype(BF16)
        wk_bf_ref[...] = wk_ref[...].astype(BF16)

    def take_chunk_of_next_tile():
        slot = lax.rem(o, 2)
        x = x_ref[...]
        u_scr[slot, j] = (x * g_ref[...]).astype(BF16)
        sq = x * x
        part = sq[:, 0:LANES]
        for c in range(1, IN_KC // LANES):
            part = part + sq[:, c * LANES:(c + 1) * LANES]
        ss = jnp.where(j == 0, part, ss_scr[slot] + part)
        ss_scr[slot] = ss
        r = lax.rsqrt(jnp.sum(ss, axis=-1, keepdims=True) * (1.0 / D_MODEL) + EPS)
        rr_scr[slot] = jnp.broadcast_to(r, (IN_TM, LANES))

    def multiply_tile():
        slot = lax.rem(o + 1, 2)
        acc = None
        for c in range(IN_KCHUNKS):
            d = jnp.dot(u_scr[slot, c], w_ref[c * IN_KC:(c + 1) * IN_KC, :].astype(BF16),
                        preferred_element_type=F32)
            acc = d if acc is None else acc + d
        rr = rr_scr[slot]
        for c in range(IN_TN // LANES):
            cols = slice(c * LANES, (c + 1) * LANES)
            proj_ref[:, cols] = (acc[:, cols] * rr).astype(proj_ref.dtype)

    @pl.when(o == 0)
    def _():
        @pl.when(j == 0)
        def _():
            ss_scr[...] = jnp.zeros_like(ss_scr)

        cast_side_streams()

        @pl.when(has_chunk)
        def _():
            take_chunk_of_next_tile()

    @pl.when((o > 0) & has_chunk)
    def _():
        multiply_tile()
        cast_side_streams()
        take_chunk_of_next_tile()

    @pl.when((o > 0) & jnp.logical_not(has_chunk))
    def _():
        multiply_tile()
        cast_side_streams()


def _in_proj(x, pre_g, w_in, w_out, w_mem_kv):
    s_len = x.shape[0]
    ni, nj = s_len // IN_TM, IN_WIDTH // IN_TN
    n_steps = (ni + 1) * nj
    side_rows = 64
    n_side = D_MODEL // side_rows
    assert n_side <= n_steps and nj >= IN_KCHUNKS

    def x_tile(o, j):
        return (jnp.minimum(o, ni - 1), jnp.minimum(j, IN_KCHUNKS - 1))

    def side(o, j):
        return (jnp.minimum(o * nj + j, n_side - 1), 0)

    return pl.pallas_call(
        _in_proj_kernel,
        grid=(ni + 1, nj),
        in_specs=[pl.BlockSpec((IN_TM, IN_KC), x_tile),
                  pl.BlockSpec((1, IN_KC), lambda o, j: (0, x_tile(o, j)[1])),
                  pl.BlockSpec((D_MODEL, IN_TN), lambda o, j: (0, jnp.where(o > 0, j, 0))),
                  pl.BlockSpec((side_rows, D_MODEL), side),
                  pl.BlockSpec((side_rows, 2 * MEM_WIDTH), side)],
        out_specs=[pl.BlockSpec((IN_TM, IN_TN), lambda o, j: (jnp.maximum(o - 1, 0), jnp.where(o > 0, j, 0))),
                   pl.BlockSpec((D_MODEL // TAIL_TN, side_rows, TAIL_TN), lambda o, j: (0, side(o, j)[0], 0)),
                   pl.BlockSpec((side_rows, 2 * MEM_WIDTH), side)],
        out_shape=[jax.ShapeDtypeStruct((s_len, IN_WIDTH), BF16),
                   jax.ShapeDtypeStruct((D_MODEL // TAIL_TN, D_MODEL, TAIL_TN), BF16),
                   jax.ShapeDtypeStruct((D_MODEL, 2 * MEM_WIDTH), BF16)],
        scratch_shapes=[pltpu.VMEM((2, IN_KCHUNKS, IN_TM, IN_KC), BF16),
                        pltpu.VMEM((2, IN_TM, LANES), F32),
                        pltpu.VMEM((2, IN_TM, LANES), F32)],
        compiler_params=_params("arbitrary", "arbitrary", vmem_limit_bytes=IN_PROJ_VMEM_LIMIT_BYTES),
        name="in_proj_pre_norm",
    )(x, pre_g.reshape(1, D_MODEL), w_in, w_out, w_mem_kv)


def _rope_tables(block_idx, freq, base_cos, base_sin, mult):
    a = (block_idx * BLOCK).astype(F32) * freq
    ca, sa = jnp.cos(a) * mult, jnp.sin(a) * mult
    c = ca * base_cos - sa * base_sin
    s = sa * base_cos + ca * base_sin
    lane = lax.broadcasted_iota(jnp.int32, (BLOCK, HEAD_DIM), 1)
    return c, jnp.where(lane < ROT_HALF, -s, 0.0), jnp.where(lane >= ROT_HALF, s, 0.0)


def _rope(t, tables):
    c, sin_lo, sin_hi = tables
    return (t * c + pltpu.roll(t, HEAD_DIM - ROT_HALF, axis=1) * sin_lo
            + pltpu.roll(t, ROT_HALF, axis=1) * sin_hi)


def _rope_keys(k_ref, tables):
    return jnp.concatenate(
        [_rope(k_ref[:, h * HEAD_DIM:(h + 1) * HEAD_DIM].astype(F32), tables).astype(BF16)
         for h in range(ATT_KV_HEADS)], axis=1)


def _attn_init(k0_ref, freq_ref, k_ring, bias_scr, cos_scr, sin_scr):
    freq = freq_ref[...]
    rows = ATT_GROUP * BLOCK
    ang = lax.broadcasted_iota(jnp.int32, (BLOCK, HEAD_DIM), 0).astype(F32) * freq
    cos_scr[...] = jnp.cos(ang)
    sin_scr[...] = jnp.sin(ang)
    key = lax.broadcasted_iota(jnp.int32, (BLOCK, rows), 0)
    q_local = lax.broadcasted_iota(jnp.int32, (BLOCK, rows), 1) & (BLOCK - 1)
    bias_scr[0] = jnp.where(key >= q_local, jnp.inf, F32_MIN)
    bias_scr[1] = jnp.where(key <= q_local, jnp.inf, F32_MIN)
    bias_scr[2] = jnp.full((BLOCK, rows), F32_MIN, F32)
    k_ring[0] = _rope_keys(k0_ref, _rope_tables(jnp.int32(0), freq, cos_scr[...], sin_scr[...], 1.0))
    k_ring[2] = jnp.zeros((BLOCK, KV_WIDTH), BF16)


class _AttnBlock:
    def __init__(self, n, nb, sink_ref, q_ref, kn_ref, vp_ref, vc_ref, vn_ref, freq_ref,
                 k_ring, bias_scr, cos_scr, sin_scr, y_scr):
        self.sink_ref, self.q_ref, self.v_refs, self.k_ring, self.y_scr = (
            sink_ref, q_ref, (vp_ref, vc_ref, vn_ref), k_ring, y_scr)
        freq = freq_ref[...]
        rows = ATT_GROUP * BLOCK
        self.log2e = math.log2(math.e)
        base_cos = cos_scr[...]
        base_sin = sin_scr[...]
        self.slots = (lax.rem(n + 2, 3), lax.rem(n, 3), lax.rem(n + 1, 3))
        k_ring[self.slots[2]] = _rope_keys(kn_ref, _rope_tables(n + 1, freq, base_cos, base_sin, 1.0))
        self.tab_q = _rope_tables(n, freq, base_cos, base_sin, HEAD_DIM ** -0.5 * self.log2e)
        self.bias_p = bias_scr[jnp.where(n > 0, 0, 2)]
        self.bias_n = bias_scr[jnp.where(n < nb - 1, 1, 2)]
        self.lane_group = lax.broadcasted_iota(jnp.int32, (1, rows), 1) // BLOCK
        self.first_row = lax.broadcasted_iota(jnp.int32, (BF16_SUBLANES, rows), 0) == 0
        self.ones = jnp.ones((3 * BLOCK + BF16_SUBLANES, HEAD_DIM), BF16)

    def scores(self, h):
        ks = slice(h * HEAD_DIM, (h + 1) * HEAD_DIM)
        k_all = jnp.concatenate([self.k_ring[s, :, ks] for s in self.slots], axis=0)
        q_all = jnp.concatenate(
            [_rope(self.q_ref[:, (h * ATT_GROUP + g) * HEAD_DIM:(h * ATT_GROUP + g + 1) * HEAD_DIM]
                   .astype(F32), self.tab_q).astype(BF16) for g in range(ATT_GROUP)], axis=0)
        return lax.dot_general(k_all, q_all, (((1,), (1,)), ((), ())), preferred_element_type=F32)

    def values(self, h, s):
        ks = slice(h * HEAD_DIM, (h + 1) * HEAD_DIM)
        sink = jnp.zeros((1, ATT_GROUP * BLOCK), F32)
        for g in range(ATT_GROUP):
            sink = jnp.where(self.lane_group == g, self.sink_ref[h * ATT_GROUP + g] * self.log2e, sink)
        s_p = jnp.minimum(s[:BLOCK], self.bias_p)
        s_c = s[BLOCK:2 * BLOCK]
        s_n = jnp.minimum(s[2 * BLOCK:], self.bias_n)
        m = jnp.max(jnp.maximum(jnp.maximum(s_p, s_c), s_n), axis=0, keepdims=True)
        m = jnp.maximum(m, sink)
        p_sink = jnp.where(self.first_row, jnp.exp2(sink - m), 0.0)
        p = jnp.concatenate([jnp.exp2(s_p - m), jnp.exp2(s_c - m), jnp.exp2(s_n - m), p_sink],
                            axis=0).astype(BF16)
        v_all = jnp.concatenate([r[:, ks] for r in self.v_refs]
                                + [jnp.zeros((BF16_SUBLANES, HEAD_DIM), BF16)], axis=0)
        v_aug = jnp.concatenate([v_all, self.ones], axis=1)
        o_aug = lax.dot_general(p, v_aug, (((0,), (0,)), ((), ())), preferred_element_type=F32)
        o = o_aug[:, :HEAD_DIM] / o_aug[:, HEAD_DIM:]
        for g in range(ATT_GROUP):
            hq = h * ATT_GROUP + g
            self.y_scr[:, hq * HEAD_DIM:(hq + 1) * HEAD_DIM] = o[g * BLOCK:(g + 1) * BLOCK]

    def result(self, z0_ref, z1_ref, g_ref):
        z = jnp.concatenate([z0_ref[...], z1_ref[...]], axis=1).astype(F32)
        return _norm_gate(self.y_scr[...], g_ref[...], z)


def _lru_direction(d, reverse, chunk, n_chunks, x_ref, xp_ref, xn_ref, cw_ref, cb_ref, wg_ref,
                   bg_ref, lam_ref, out_ref, xe_scr, carry_scr):
    t = LRU_CHUNK
    halo = BF16_SUBLANES
    for blk in range(LRU_BLOCKS):
        cs = slice(blk * LRU_BLOCK_DIM, (blk + 1) * LRU_BLOCK_DIM)
        xe_scr[d, blk, 0:halo, :] = jnp.where(chunk > 0, xp_ref[:, cs].astype(F32), 0.0)
        xe_scr[d, blk, halo:halo + t, :] = x_ref[:, cs].astype(F32)
        xe_scr[d, blk, halo + t:, :] = jnp.where(chunk < n_chunks - 1, xn_ref[:, cs].astype(F32), 0.0)

    neg_lam = -lam_ref[d:d + 1, :]
    softplus = jnp.maximum(neg_lam, 0.0) + jnp.log(1.0 + jnp.exp(-jnp.abs(neg_lam)))
    half_decay = (-0.5 * LRU_C * math.log2(math.e)) * softplus
    row0 = [8 * sum(LRU_SUBCHUNK_SEGS[:k]) for k in range(len(LRU_SUBCHUNK_SEGS))]
    subchunks = list(zip(row0, LRU_SUBCHUNK_SEGS))
    for blk in range(LRU_BLOCKS):
        cs = slice(blk * LRU_BLOCK_DIM, (blk + 1) * LRU_BLOCK_DIM)
        taps = [jnp.broadcast_to(cw_ref[j:j + 1, cs], (8, LRU_BLOCK_DIM)) for j in range(CONV_WIDTH)]
        conv_bias = jnp.broadcast_to(cb_ref[:, cs], (8, LRU_BLOCK_DIM))
        hd = half_decay[:, cs]
        cin = carry_scr[d:d + 1, cs]
        for r0, seg in (reversed(subchunks) if reverse else subchunks):
            order = range(seg - 1, -1, -1) if reverse else range(seg)
            tiles = []
            for m in range(seg):
                acc = conv_bias
                for j in range(CONV_WIDTH):
                    start = halo + r0 + m + j - CONV_LEFT
                    acc = acc + taps[j] * xe_scr[d, blk, pl.ds(start, 8, stride=seg), :]
                tiles.append(acc)
            xc = jnp.concatenate(tiles, axis=0)
            gates = jnp.dot(xc.astype(BF16), wg_ref[d, blk], preferred_element_type=F32)
            t_r = jnp.tanh(gates[:, :LRU_BLOCK_DIM] + bg_ref[d, 0:1, cs])
            t_i = jnp.tanh(gates[:, LRU_BLOCK_DIM:] + bg_ref[d, 1:2, cs])
            a = jnp.exp2(hd * t_r + hd)
            w = 1.0 - a * a
            b = w * lax.rsqrt(jnp.maximum(w, F32_TINY)) * ((t_i + 1.0) * 0.5) * xc
            tile = lambda v, m: v[8 * m:8 * m + 8]
            hh = jnp.zeros((8, LRU_BLOCK_DIM), F32)
            pp = jnp.ones((8, LRU_BLOCK_DIM), F32)
            for m in order:
                hh = tile(a, m) * hh + tile(b, m)
                pp = tile(a, m) * pp
            entering = [None] * 8
            for s in (range(7, -1, -1) if reverse else range(8)):
                entering[s] = cin
                cin = pp[s:s + 1, :] * cin + hh[s:s + 1, :]
            h = jnp.concatenate(entering, axis=0)
            for m in order:
                h = tile(a, m) * h + tile(b, m)
                out_ref[blk, pl.ds(r0 + m, 8, stride=seg), :] = h
        carry_scr[d:d + 1, cs] = cin


def _lru_kernel(xf_ref, xfp_ref, xfn_ref, xb_ref, xbp_ref, xbn_ref, cw_ref, cb_ref, wg_ref, bg_ref,
                lam_ref, hf_ref, hb_ref, xe_scr, carry_scr):
    i = pl.program_id(0)
    nc = pl.num_programs(0)

    @pl.when(i == 0)
    def _():
        carry_scr[...] = jnp.zeros_like(carry_scr)

    shared = (cw_ref, cb_ref, wg_ref, bg_ref, lam_ref)
    scr = (xe_scr, carry_scr)
    _lru_direction(0, False, i, nc, xf_ref, xfp_ref, xfn_ref, *shared, hf_ref, *scr)
    _lru_direction(1, True, nc - 1 - i, nc, xb_ref, xbp_ref, xbn_ref, *shared, hb_ref, *scr)


def _lru(proj, conv_w, conv_b, w_a, b_a, w_x, b_x, lam):
    s_len = proj.shape[0]
    nc = s_len // LRU_CHUNK
    halo_per_chunk = LRU_CHUNK // BF16_SUBLANES
    n_halo = s_len // BF16_SUBLANES
    wg = (0.5 * jnp.concatenate([w_a, w_x], axis=-1)).astype(BF16)
    bg = 0.5 * jnp.stack([b_a, b_x], axis=1)

    def fwd(i):
        return i

    def bwd(i):
        return nc - 1 - i

    def specs(chunk_of):
        return [
            pl.BlockSpec((LRU_CHUNK, LRU_WIDTH), lambda i: (chunk_of(i), COL_XLRU)),
            pl.BlockSpec((BF16_SUBLANES, LRU_WIDTH),
                         lambda i: (jnp.maximum(chunk_of(i) * halo_per_chunk - 1, 0), COL_XLRU)),
            pl.BlockSpec((BF16_SUBLANES, LRU_WIDTH),
                         lambda i: (jnp.minimum((chunk_of(i) + 1) * halo_per_chunk, n_halo - 1), COL_XLRU)),
        ]

    def whole(shape):
        return pl.BlockSpec(shape, lambda i: (0,) * len(shape))

    return pl.pallas_call(
        _lru_kernel,
        grid=(nc,),
        in_specs=specs(fwd) + specs(bwd) + [
            whole((CONV_WIDTH, LRU_WIDTH)), whole((1, LRU_WIDTH)),
            whole((2, LRU_BLOCKS, LRU_BLOCK_DIM, 2 * LRU_BLOCK_DIM)),
            whole((2, 2, LRU_WIDTH)), whole((2, LRU_WIDTH))],
        out_specs=[pl.BlockSpec((LRU_BLOCKS, LRU_CHUNK, LRU_BLOCK_DIM), lambda i: (0, i, 0)),
                   pl.BlockSpec((LRU_BLOCKS, LRU_CHUNK, LRU_BLOCK_DIM), lambda i: (0, nc - 1 - i, 0))],
        out_shape=[jax.ShapeDtypeStruct((LRU_BLOCKS, s_len, LRU_BLOCK_DIM), F32),
                   jax.ShapeDtypeStruct((LRU_BLOCKS, s_len, LRU_BLOCK_DIM), F32)],
        scratch_shapes=[pltpu.VMEM((2, LRU_BLOCKS, LRU_CHUNK + 2 * BF16_SUBLANES, LRU_BLOCK_DIM), F32),
                        pltpu.VMEM((2, LRU_WIDTH), F32)],
        compiler_params=_params("arbitrary"),
        name="rglru",
    )(proj, proj, proj, proj, proj, proj, conv_w, conv_b.reshape(1, LRU_WIDTH), wg, bg, lam)


def _mem_scores(h, q_ref, kv_ref):
    cs = slice(h * MEM_HEAD_DIM, (h + 1) * MEM_HEAD_DIM)
    return lax.dot_general(q_ref[:, cs], kv_ref[:, cs], (((1,), (1,)), ((), ())),
                           preferred_element_type=F32)


def _mem_values(h, s, kv_ref, y_scr):
    cs = slice(h * MEM_HEAD_DIM, (h + 1) * MEM_HEAD_DIM)
    vs = slice(MEM_WIDTH + h * MEM_HEAD_DIM, MEM_WIDTH + (h + 1) * MEM_HEAD_DIM)
    s = s * MEM_HEAD_DIM ** -0.5
    m = jnp.max(s, axis=-1, keepdims=True)
    p = jnp.exp(s - m)
    denom = jnp.sum(p, axis=-1, keepdims=True)
    y_scr[:, cs] = jnp.dot(p.astype(BF16), kv_ref[:, vs], preferred_element_type=F32) / denom


def _lru_finish(hf_ref, hb_ref, zl_ref, gl_ref):
    y_lru = jnp.concatenate([hf_ref[blk] + hb_ref[blk] for blk in range(LRU_BLOCKS)], axis=1)
    return _norm_gate(y_lru, gl_ref[...], zl_ref[...].astype(F32))


def _tail_kernel(sink_ref,
                 q_ref, k0_ref, kn_ref, vp_ref, vc_ref, vn_ref, z0_ref, z1_ref, freq_ref, ga_ref,
                 qm_ref, kv_ref, zm_ref, gm_ref, hf_ref, hb_ref, zl_ref, gl_ref,
                 w_ref, x_ref, gp_ref, o_ref,
                 lhs_scr, acc_scr, ss_scr, k_ring, bias_scr, cos_scr, sin_scr, ya_scr, ym_scr):
    o = pl.program_id(0)
    j = pl.program_id(1)
    nj = pl.num_programs(1)
    n_row_tiles = pl.num_programs(0) - 2
    nb = n_row_tiles * nj

    n_chunks = ATT_KV_HEADS
    chunk = TAIL_TN // n_chunks

    def stage_b_chunk(c, part):
        slot = lax.rem(o + 1, 2)
        cols = slice(c * chunk, (c + 1) * chunk)
        acc = jnp.dot(lhs_scr[slot], w_ref[:, cols], preferred_element_type=F32)
        acc_scr[slot, j, :, cols] = acc.astype(acc_scr.dtype)
        sq = acc * acc
        for k in range(chunk // LANES):
            piece = sq[:, k * LANES:(k + 1) * LANES]
            part = piece if part is None else part + piece
        return part

    def stages_ab(with_a, with_b):
        part = None
        if with_b:
            part = stage_b_chunk(0, part)
        if with_a:
            n = o * nj + j
            attn = _AttnBlock(n, nb, sink_ref, q_ref, kn_ref, vp_ref, vc_ref, vn_ref, freq_ref,
                              k_ring, bias_scr, cos_scr, sin_scr, ya_scr)
            s_att = [attn.scores(h) for h in range(ATT_KV_HEADS)]
            s_mem = [_mem_scores(h, qm_ref, kv_ref) for h in range(MEM_HEADS)]
        for pair in range(2):
            if with_b:
                part = stage_b_chunk(1 + pair, part)
            if with_a:
                for h in (2 * pair, 2 * pair + 1):
                    attn.values(h, s_att[h])
                    _mem_values(h, s_mem[h], kv_ref, ym_scr)
        if with_b:
            part = stage_b_chunk(3, part)
            slot = lax.rem(o + 1, 2)
            ss_scr[slot] = jnp.where(j == 0, part, ss_scr[slot] + part)
        if with_a:
            slot = lax.rem(o, 2)
            for rc in range(BLOCK // NORM_ROW_CHUNK):
                rs = slice(rc * NORM_ROW_CHUNK, (rc + 1) * NORM_ROW_CHUNK)
                rows = pl.ds(pl.multiple_of(j * BLOCK + rc * NORM_ROW_CHUNK, NORM_ROW_CHUNK), NORM_ROW_CHUNK)
                z_att = jnp.concatenate([z0_ref[rs, :], z1_ref[rs, :]], axis=1).astype(F32)
                lhs_scr[slot, rows, 0:ATT_WIDTH] = _norm_gate(ya_scr[rs, :], ga_ref[...], z_att).astype(BF16)
                y_lru = jnp.concatenate([hf_ref[blk, rs, :] + hb_ref[blk, rs, :] for blk in range(LRU_BLOCKS)],
                                        axis=1)
                lhs_scr[slot, rows, ATT_WIDTH:ATT_WIDTH + LRU_WIDTH] = _norm_gate(
                    y_lru, gl_ref[...], zl_ref[rs, :].astype(F32)).astype(BF16)
                lhs_scr[slot, rows, ATT_WIDTH + LRU_WIDTH:] = _norm_gate(
                    ym_scr[rs, :], gm_ref[...], zm_ref[rs, :].astype(F32)).astype(BF16)

    def stage_c():
        slot = lax.rem(o, 2)
        ms = jnp.sum(ss_scr[slot], axis=-1, keepdims=True) * (1.0 / D_MODEL)
        o_ref[...] = x_ref[...] + acc_scr[slot, j].astype(F32) * lax.rsqrt(ms + EPS) * gp_ref[...]

    @pl.when(o == 0)
    def _():
        @pl.when(j == 0)
        def _():
            ss_scr[...] = jnp.zeros_like(ss_scr)
            _attn_init(k0_ref, freq_ref, k_ring, bias_scr, cos_scr, sin_scr)

        stages_ab(True, False)

    @pl.when(o == 1)
    def _():
        stages_ab(True, True)

    @pl.when((o >= 2) & (o < n_row_tiles))
    def _():
        stage_c()
        stages_ab(True, True)

    @pl.when(o == n_row_tiles)
    def _():
        stage_c()
        stages_ab(False, True)

    @pl.when(o == n_row_tiles + 1)
    def _():
        stage_c()


def _tail(proj, kv, h_fwd, h_bwd, sink, g_att, g_mem, g_lru, w_out, x, post_g):
    s_len = proj.shape[0]
    tm, tn = TAIL_TM, TAIL_TN
    ni, nj = s_len // tm, D_MODEL // tn
    nb = s_len // BLOCK
    assert tm == nj * BLOCK
    inv_freq = jnp.power(jnp.float32(ROPE_THETA),
                         -jnp.arange(ROT_HALF, dtype=F32) * 2.0 / ROT_DIM)
    freq = jnp.concatenate([inv_freq, inv_freq, jnp.zeros((HEAD_DIM - ROT_DIM,), F32)]).reshape(1, HEAD_DIM)

    def block_of(o, j):
        return jnp.where(o < ni, o * nj + j, nb - 1)

    def rows_spec(width, col, shift=0):
        return pl.BlockSpec((BLOCK, width),
                            lambda o, j, sink: (jnp.clip(block_of(o, j) + shift, 0, nb - 1), col))

    def const_spec(shape):
        return pl.BlockSpec(shape, lambda o, j, sink: (0,) * len(shape))

    slab_spec = pl.BlockSpec((LRU_BLOCKS, BLOCK, LRU_BLOCK_DIM), lambda o, j, sink: (0, block_of(o, j), 0))

    def w_col(o, j, sink):
        return (jnp.where(o == 0, 0, jnp.where(o <= ni, j, nj - 1)), 0, 0)

    def finished_tile(o, j, sink):
        return (jnp.maximum(o - 2, 0), jnp.where(o >= 2, j, 0))

    grid_spec = pltpu.PrefetchScalarGridSpec(
        num_scalar_prefetch=1,
        grid=(ni + 2, nj),
        in_specs=[rows_spec(ATT_WIDTH, 0),
                  pl.BlockSpec((BLOCK, KV_WIDTH), lambda o, j, sink: (0, COL_K512)),
                  rows_spec(KV_WIDTH, COL_K512, 1),
                  rows_spec(KV_WIDTH, COL_V512, -1), rows_spec(KV_WIDTH, COL_V512), rows_spec(KV_WIDTH, COL_V512, 1),
                  rows_spec(1024, COL_ZATT), rows_spec(1024, COL_ZATT + 1),
                  const_spec((1, HEAD_DIM)), const_spec((1, ATT_WIDTH)),
                  rows_spec(MEM_WIDTH, COL_QMEM), const_spec((N_MEM, 2 * MEM_WIDTH)),
                  rows_spec(MEM_WIDTH, COL_ZMEM), const_spec((1, MEM_WIDTH)),
                  slab_spec, slab_spec, rows_spec(LRU_WIDTH, COL_ZLRU), const_spec((1, LRU_WIDTH)),
                  pl.BlockSpec((None, D_MODEL, tn), w_col),
                  pl.BlockSpec((tm, tn), finished_tile),
                  pl.BlockSpec((1, tn), lambda o, j, sink: (0, j))],
        out_specs=pl.BlockSpec((tm, tn), finished_tile),
        scratch_shapes=[pltpu.VMEM((2, tm, D_MODEL), BF16),
                        pltpu.VMEM((2, nj, tm, tn), BF16),
                        pltpu.VMEM((2, tm, LANES), F32),
                        pltpu.VMEM((3, BLOCK, KV_WIDTH), BF16),
                        pltpu.VMEM((3, BLOCK, ATT_GROUP * BLOCK), F32),
                        pltpu.VMEM((BLOCK, HEAD_DIM), F32),
                        pltpu.VMEM((BLOCK, HEAD_DIM), F32),
                        pltpu.VMEM((BLOCK, ATT_WIDTH), F32),
                        pltpu.VMEM((BLOCK, MEM_WIDTH), F32)],
    )
    return pl.pallas_call(
        _tail_kernel,
        grid_spec=grid_spec,
        out_shape=jax.ShapeDtypeStruct((s_len, D_MODEL), F32),
        compiler_params=_params("arbitrary", "arbitrary"),
        name="mixers_out_proj_post_norm",
    )(sink, proj, proj, proj, proj, proj, proj, proj, proj, freq, g_att.reshape(1, ATT_WIDTH),
      proj, kv, proj, g_mem.reshape(1, MEM_WIDTH), h_fwd, h_bwd, proj, g_lru.reshape(1, LRU_WIDTH),
      w_out, x, post_g.reshape(1, D_MODEL))


def _layer(h, mem, pre_g, w_in, sink, conv_w, conv_b, w_a, b_a, w_x, b_x, lam, mem_g, w_mem_kv,
           g_att, g_lru, g_mem, w_out, post_g):
    proj, w_out_bf, w_mem_kv_bf = _in_proj(h, pre_g, w_in, w_out, w_mem_kv)
    kv = _mem_kv(mem, mem_g, w_mem_kv_bf)
    h_fwd, h_bwd = _lru(proj, conv_w, conv_b, w_a, b_a, w_x, b_x, lam)
    return _tail(proj, kv, h_fwd, h_bwd, sink, g_att, g_mem, g_lru, w_out_bf, h, post_g)


def kernel(x, mem, pre_norm_gain, w_in, att_sink, conv_w, conv_b, lru_w_a, lru_b_a, lru_w_x, lru_b_x,
           lru_lambda, mem_norm_gain, w_mem_kv, att_out_gain, lru_out_gain, mem_out_gain, w_out,
           post_norm_gain):
    batch, s_len, d_model = x.shape
    depth = w_in.shape[0]
    outs = []
    for b in range(batch):
        h = x.reshape(s_len, d_model) if batch == 1 else x[b]
        m = mem.reshape(mem.shape[1], d_model) if batch == 1 else mem[b]
        for l in range(depth):
            h = _layer(h, m, pre_norm_gain[l], w_in[l], att_sink[l], conv_w[l], conv_b[l],
                       lru_w_a[l], lru_b_a[l], lru_w_x[l], lru_b_x[l], lru_lambda[l],
                       mem_norm_gain[l], w_mem_kv[l], att_out_gain[l], lru_out_gain[l],
                       mem_out_gain[l], w_out[l], post_norm_gain[l])
        outs.append(h)
    return outs[0].reshape(x.shape) if batch == 1 else jnp.stack(outs, axis=0)
```

```python
import math

import jax
import jax.numpy as jnp
from jax import lax
from jax.experimental import pallas as pl
from jax.experimental.pallas import tpu as pltpu

F32 = jnp.float32
BF16 = jnp.bfloat16

D_MODEL = 4096
N_MEM = 256
HEAD_DIM = 128
ATT_WIDTH = 2048
ATT_HEADS = 16
ATT_KV_HEADS = 4
ATT_GROUP = ATT_HEADS // ATT_KV_HEADS
KV_WIDTH = ATT_KV_HEADS * HEAD_DIM
BLOCK = 128
ROPE_THETA = 500000.0
ROT_DIM = HEAD_DIM // 4
ROT_HALF = ROT_DIM // 2
LRU_WIDTH = 1024
LRU_BLOCKS = 8
LRU_BLOCK_DIM = LRU_WIDTH // LRU_BLOCKS
CONV_WIDTH = 4
CONV_LEFT = 2
LRU_C = 8.0
MEM_HEADS = 4
MEM_WIDTH = 1024
MEM_HEAD_DIM = MEM_WIDTH // MEM_HEADS
EPS = 1e-6
IN_WIDTH = 9216
F32_MIN = float(jnp.finfo(jnp.float32).min)
F32_TINY = float(jnp.finfo(jnp.float32).tiny)

COL_XLRU = 3
COL_QMEM = 4
COL_ZATT = 5
COL_ZLRU = 7
COL_ZMEM = 8
COL_K512 = 4
COL_V512 = 5

VMEM_LIMIT_BYTES = 60000 * 1024
IN_PROJ_VMEM_LIMIT_BYTES = 62 * 1024 * 1024
BF16_SUBLANES = 16

MEM_KV_TN = 1024
IN_TM = 1024
IN_TN = 768
IN_KC = 512
IN_KCHUNKS = D_MODEL // IN_KC
TAIL_TM = 512
TAIL_TN = 1024
LANES = 128
LRU_CHUNK = 512
LRU_SUBCHUNK_SEGS = (36, 28)
assert 8 * sum(LRU_SUBCHUNK_SEGS) == LRU_CHUNK and all(s % 8 == 4 for s in LRU_SUBCHUNK_SEGS)


def _params(*sem, vmem_limit_bytes=VMEM_LIMIT_BYTES):
    return pltpu.CompilerParams(dimension_semantics=sem, vmem_limit_bytes=vmem_limit_bytes)


def _sigmoid(t):
    return 1.0 / (1.0 + jnp.exp(-t))


def _norm_gate(y, g, z):
    ms = jnp.mean(y * y, axis=-1, keepdims=True)
    return y * lax.rsqrt(ms + EPS) * g * (z * _sigmoid(z))


def _mem_kv_kernel(mem_ref, g_ref, w_ref, o_ref):
    x = mem_ref[...]
    ms = jnp.mean(x * x, axis=-1, keepdims=True)
    u = (x * lax.rsqrt(ms + EPS) * g_ref[...]).astype(BF16)
    o_ref[...] = jnp.dot(u, w_ref[...], preferred_element_type=F32).astype(o_ref.dtype)


def _mem_kv(mem, g, w_mem_kv_bf):
    m, d = mem.shape
    n = w_mem_kv_bf.shape[1]
    return pl.pallas_call(
        _mem_kv_kernel,
        grid=(n // MEM_KV_TN,),
        in_specs=[pl.BlockSpec((m, d), lambda j: (0, 0)),
                  pl.BlockSpec((1, d), lambda j: (0, 0)),
                  pl.BlockSpec((d, MEM_KV_TN), lambda j: (0, j))],
        out_specs=pl.BlockSpec((m, MEM_KV_TN), lambda j: (0, j)),
        out_shape=jax.ShapeDtypeStruct((m, n), BF16),
        compiler_params=_params("parallel"),
        name="mem_kv_proj",
    )(mem, g.reshape(1, d), w_mem_kv_bf)


def _in_proj_kernel(x_ref, g_ref, w_ref, wo_ref, wk_ref, proj_ref, wo_bf_ref, wk_bf_ref,
                    u_scr, ss_scr, rr_scr):
    o = pl.program_id(0)
    j = pl.program_id(1)
    has_chunk = (j < IN_KCHUNKS) & (o < pl.num_programs(0) - 1)

    def cast_side_streams():
        for c in range(D_MODEL // TAIL_TN):
            wo_bf_ref[c] = wo_ref[:, c * TAIL_TN:(c + 1) * TAIL_TN].astype(BF16)
        wk_bf_ref[...] = wk_ref[...].astype(BF16)

    def take_chunk_of_next_tile():
        slot = lax.rem(o, 2)
        x = x_ref[...]
        u_scr[slot, j] = (x * g_ref[...]).astype(BF16)
        sq = x * x
        part = sq[:, 0:LANES]
        for c in range(1, IN_KC // LANES):
            part = part + sq[:, c * LANES:(c + 1) * LANES]
        ss = jnp.where(j == 0, part, ss_scr[slot] + part)
        ss_scr[slot] = ss
        r = lax.rsqrt(jnp.sum(ss, axis=-1, keepdims=True) * (1.0 / D_MODEL) + EPS)
        rr_scr[slot] = jnp.broadcast_to(r, (IN_TM, LANES))

    def multiply_tile():
        slot = lax.rem(o + 1, 2)
        acc = None
        for c in range(IN_KCHUNKS):
            d = jnp.dot(u_scr[slot, c], w_ref[c * IN_KC:(c + 1) * IN_KC, :].astype(BF16),
                        preferred_element_type=F32)
            acc = d if acc is None else acc + d
        rr = rr_scr[slot]
        for c in range(IN_TN // LANES):
            cols = slice(c * LANES, (c + 1) * LANES)
            proj_ref[:, cols] = (acc[:, cols] * rr).astype(proj_ref.dtype)

    @pl.when(o == 0)
    def _():
        @pl.when(j == 0)
        def _():
            ss_scr[...] = jnp.zeros_like(ss_scr)

        cast_side_streams()

        @pl.when(has_chunk)
        def _():
            take_chunk_of_next_tile()

    @pl.when((o > 0) & has_chunk)
    def _():
        multiply_tile()
        cast_side_streams()
        take_chunk_of_next_tile()

    @pl.when((o > 0) & jnp.logical_not(has_chunk))
    def _():
        multiply_tile()
        cast_side_streams()


def _in_proj(x, pre_g, w_in, w_out, w_mem_kv):
    s_len = x.shape[0]
    ni, nj = s_len // IN_TM, IN_WIDTH // IN_TN
    n_steps = (ni + 1) * nj
    side_rows = 64
    n_side = D_MODEL // side_rows
    assert n_side <= n_steps and nj >= IN_KCHUNKS

    def x_tile(o, j):
        return (jnp.minimum(o, ni - 1), jnp.minimum(j, IN_KCHUNKS - 1))

    def side(o, j):
        return (jnp.minimum(o * nj + j, n_side - 1), 0)

    return pl.pallas_call(
        _in_proj_kernel,
        grid=(ni + 1, nj),
        in_specs=[pl.BlockSpec((IN_TM, IN_KC), x_tile),
                  pl.BlockSpec((1, IN_KC), lambda o, j: (0, x_tile(o, j)[1])),
                  pl.BlockSpec((D_MODEL, IN_TN), lambda o, j: (0, jnp.where(o > 0, j, 0))),
                  pl.BlockSpec((side_rows, D_MODEL), side),
                  pl.BlockSpec((side_rows, 2 * MEM_WIDTH), side)],
        out_specs=[pl.BlockSpec((IN_TM, IN_TN), lambda o, j: (jnp.maximum(o - 1, 0), jnp.where(o > 0, j, 0))),
                   pl.BlockSpec((D_MODEL // TAIL_TN, side_rows, TAIL_TN), lambda o, j: (0, side(o, j)[0], 0)),
                   pl.BlockSpec((side_rows, 2 * MEM_WIDTH), side)],
        out_shape=[jax.ShapeDtypeStruct((s_len, IN_WIDTH), BF16),
                   jax.ShapeDtypeStruct((D_MODEL // TAIL_TN, D_MODEL, TAIL_TN), BF16),
                   jax.ShapeDtypeStruct((D_MODEL, 2 * MEM_WIDTH), BF16)],
        scratch_shapes=[pltpu.VMEM((2, IN_KCHUNKS, IN_TM, IN_KC), BF16),
                        pltpu.VMEM((2, IN_TM, LANES), F32),
                        pltpu.VMEM((2, IN_TM, LANES), F32)],
        compiler_params=_params("arbitrary", "arbitrary", vmem_limit_bytes=IN_PROJ_VMEM_LIMIT_BYTES),
        name="in_proj_pre_norm",
    )(x, pre_g.reshape(1, D_MODEL), w_in, w_out, w_mem_kv)


def _rope_tables(block_idx, freq, base_cos, base_sin, mult):
    a = (block_idx * BLOCK).astype(F32) * freq
    ca, sa = jnp.cos(a) * mult, jnp.sin(a) * mult
    c = ca * base_cos - sa * base_sin
    s = sa * base_cos + ca * base_sin
    lane = lax.broadcasted_iota(jnp.int32, (BLOCK, HEAD_DIM), 1)
    return c, jnp.where(lane < ROT_HALF, -s, 0.0), jnp.where(lane >= ROT_HALF, s, 0.0)


def _rope(t, tables):
    c, sin_lo, sin_hi = tables
    return (t * c + pltpu.roll(t, HEAD_DIM - ROT_HALF, axis=1) * sin_lo
            + pltpu.roll(t, ROT_HALF, axis=1) * sin_hi)


def _rope_keys(k_ref, tables):
    return jnp.concatenate(
        [_rope(k_ref[:, h * HEAD_DIM:(h + 1) * HEAD_DIM].astype(F32), tables).astype(BF16)
         for h in range(ATT_KV_HEADS)], axis=1)


def _attn_init(k0_ref, freq_ref, k_ring, bias_scr, cos_scr, sin_scr):
    freq = freq_ref[...]
    rows = ATT_GROUP * BLOCK
    ang = lax.broadcasted_iota(jnp.int32, (BLOCK, HEAD_DIM), 0).astype(F32) * freq
    cos_scr[...] = jnp.cos(ang)
    sin_scr[...] = jnp.sin(ang)
    key = lax.broadcasted_iota(jnp.int32, (BLOCK, rows), 0)
    q_local = lax.broadcasted_iota(jnp.int32, (BLOCK, rows), 1) & (BLOCK - 1)
    bias_scr[0] = jnp.where(key >= q_local, jnp.inf, F32_MIN)
    bias_scr[1] = jnp.where(key <= q_local, jnp.inf, F32_MIN)
    bias_scr[2] = jnp.full((BLOCK, rows), F32_MIN, F32)
    k_ring[0] = _rope_keys(k0_ref, _rope_tables(jnp.int32(0), freq, cos_scr[...], sin_scr[...], 1.0))
    k_ring[2] = jnp.zeros((BLOCK, KV_WIDTH), BF16)


class _AttnBlock:
    def __init__(self, n, nb, sink_ref, q_ref, kn_ref, vp_ref, vc_ref, vn_ref, freq_ref,
                 k_ring, bias_scr, cos_scr, sin_scr, y_scr):
        self.sink_ref, self.q_ref, self.v_refs, self.k_ring, self.y_scr = (
            sink_ref, q_ref, (vp_ref, vc_ref, vn_ref), k_ring, y_scr)
        freq = freq_ref[...]
        rows = ATT_GROUP * BLOCK
        self.log2e = math.log2(math.e)
        base_cos = cos_scr[...]
        base_sin = sin_scr[...]
        self.slots = (lax.rem(n + 2, 3), lax.rem(n, 3), lax.rem(n + 1, 3))
        k_ring[self.slots[2]] = _rope_keys(kn_ref, _rope_tables(n + 1, freq, base_cos, base_sin, 1.0))
        self.tab_q = _rope_tables(n, freq, base_cos, base_sin, HEAD_DIM ** -0.5 * self.log2e)
        self.bias_p = bias_scr[jnp.where(n > 0, 0, 2)]
        self.bias_n = bias_scr[jnp.where(n < nb - 1, 1, 2)]
        self.lane_group = lax.broadcasted_iota(jnp.int32, (1, rows), 1) // BLOCK
        self.first_row = lax.broadcasted_iota(jnp.int32, (BF16_SUBLANES, rows), 0) == 0
        self.ones = jnp.ones((3 * BLOCK + BF16_SUBLANES, HEAD_DIM), BF16)

    def scores(self, h):
        ks = slice(h * HEAD_DIM, (h + 1) * HEAD_DIM)
        k_all = jnp.concatenate([self.k_ring[s, :, ks] for s in self.slots], axis=0)
        q_all = jnp.concatenate(
            [_rope(self.q_ref[:, (h * ATT_GROUP + g) * HEAD_DIM:(h * ATT_GROUP + g + 1) * HEAD_DIM]
                   .astype(F32), self.tab_q).astype(BF16) for g in range(ATT_GROUP)], axis=0)
        return lax.dot_general(k_all, q_all, (((1,), (1,)), ((), ())), preferred_element_type=F32)

    def values(self, h, s):
        ks = slice(h * HEAD_DIM, (h + 1) * HEAD_DIM)
        sink = jnp.zeros((1, ATT_GROUP * BLOCK), F32)
        for g in range(ATT_GROUP):
            sink = jnp.where(self.lane_group == g, self.sink_ref[h * ATT_GROUP + g] * self.log2e, sink)
        s_p = jnp.minimum(s[:BLOCK], self.bias_p)
        s_c = s[BLOCK:2 * BLOCK]
        s_n = jnp.minimum(s[2 * BLOCK:], self.bias_n)
        m = jnp.max(jnp.maximum(jnp.maximum(s_p, s_c), s_n), axis=0, keepdims=True)
        m = jnp.maximum(m, sink)
        p_sink = jnp.where(self.first_row, jnp.exp2(sink - m), 0.0)
        p = jnp.concatenate([jnp.exp2(s_p - m), jnp.exp2(s_c - m), jnp.exp2(s_n - m), p_sink],
                            axis=0).astype(BF16)
        v_all = jnp.concatenate([r[:, ks] for r in self.v_refs]
                                + [jnp.zeros((BF16_SUBLANES, HEAD_DIM), BF16)], axis=0)
        v_aug = jnp.concatenate([v_all, self.ones], axis=1)
        o_aug = lax.dot_general(p, v_aug, (((0,), (0,)), ((), ())), preferred_element_type=F32)
        o = o_aug[:, :HEAD_DIM] / o_aug[:, HEAD_DIM:]
        for g in range(ATT_GROUP):
            hq = h * ATT_GROUP + g
            self.y_scr[:, hq * HEAD_DIM:(hq + 1) * HEAD_DIM] = o[g * BLOCK:(g + 1) * BLOCK]

    def result(self, z0_ref, z1_ref, g_ref):
        z = jnp.concatenate([z0_ref[...], z1_ref[...]], axis=1).astype(F32)
        return _norm_gate(self.y_scr[...], g_ref[...], z)


def _lru_direction(d, reverse, chunk, n_chunks, x_ref, xp_ref, xn_ref, cw_ref, cb_ref, wg_ref,
                   bg_ref, lam_ref, out_ref, xe_scr, carry_scr):
    t = LRU_CHUNK
    halo = BF16_SUBLANES
    for blk in range(LRU_BLOCKS):
        cs = slice(blk * LRU_BLOCK_DIM, (blk + 1) * LRU_BLOCK_DIM)
        xe_scr[d, blk, 0:halo, :] = jnp.where(chunk > 0, xp_ref[:, cs].astype(F32), 0.0)
        xe_scr[d, blk, halo:halo + t, :] = x_ref[:, cs].astype(F32)
        xe_scr[d, blk, halo + t:, :] = jnp.where(chunk < n_chunks - 1, xn_ref[:, cs].astype(F32), 0.0)

    neg_lam = -lam_ref[d:d + 1, :]
    softplus = jnp.maximum(neg_lam, 0.0) + jnp.log(1.0 + jnp.exp(-jnp.abs(neg_lam)))
    half_decay = (-0.5 * LRU_C * math.log2(math.e)) * softplus
    row0 = [8 * sum(LRU_SUBCHUNK_SEGS[:k]) for k in range(len(LRU_SUBCHUNK_SEGS))]
    subchunks = list(zip(row0, LRU_SUBCHUNK_SEGS))
    for blk in range(LRU_BLOCKS):
        cs = slice(blk * LRU_BLOCK_DIM, (blk + 1) * LRU_BLOCK_DIM)
        taps = [jnp.broadcast_to(cw_ref[j:j + 1, cs], (8, LRU_BLOCK_DIM)) for j in range(CONV_WIDTH)]
        conv_bias = jnp.broadcast_to(cb_ref[:, cs], (8, LRU_BLOCK_DIM))
        hd = half_decay[:, cs]
        cin = carry_scr[d:d + 1, cs]
        for r0, seg in (reversed(subchunks) if reverse else subchunks):
            order = range(seg - 1, -1, -1) if reverse else range(seg)
            tiles = []
            for m in range(seg):
                acc = conv_bias
                for j in range(CONV_WIDTH):
                    start = halo + r0 + m + j - CONV_LEFT
                    acc = acc + taps[j] * xe_scr[d, blk, pl.ds(start, 8, stride=seg), :]
                tiles.append(acc)
            xc = jnp.concatenate(tiles, axis=0)
            gates = jnp.dot(xc.astype(BF16), wg_ref[d, blk], preferred_element_type=F32)
            t_r = jnp.tanh(gates[:, :LRU_BLOCK_DIM] + bg_ref[d, 0:1, cs])
            t_i = jnp.tanh(gates[:, LRU_BLOCK_DIM:] + bg_ref[d, 1:2, cs])
            a = jnp.exp2(hd * t_r + hd)
            w = 1.0 - a * a
            b = w * lax.rsqrt(jnp.maximum(w, F32_TINY)) * ((t_i + 1.0) * 0.5) * xc
            tile = lambda v, m: v[8 * m:8 * m + 8]
            hh = jnp.zeros((8, LRU_BLOCK_DIM), F32)
            pp = jnp.ones((8, LRU_BLOCK_DIM), F32)
            for m in order:
                hh = tile(a, m) * hh + tile(b, m)
                pp = tile(a, m) * pp
            entering = [None] * 8
            for s in (range(7, -1, -1) if reverse else range(8)):
                entering[s] = cin
                cin = pp[s:s + 1, :] * cin + hh[s:s + 1, :]
            h = jnp.concatenate(entering, axis=0)
            for m in order:
                h = tile(a, m) * h + tile(b, m)
                out_ref[blk, pl.ds(r0 + m, 8, stride=seg), :] = h
        carry_scr[d:d + 1, cs] = cin


def _lru_kernel(xf_ref, xfp_ref, xfn_ref, xb_ref, xbp_ref, xbn_ref, cw_ref, cb_ref, wg_ref, bg_ref,
                lam_ref, hf_ref, hb_ref, xe_scr, carry_scr):
    i = pl.program_id(0)
    nc = pl.num_programs(0)

    @pl.when(i == 0)
    def _():
        carry_scr[...] = jnp.zeros_like(carry_scr)

    shared = (cw_ref, cb_ref, wg_ref, bg_ref, lam_ref)
    scr = (xe_scr, carry_scr)
    _lru_direction(0, False, i, nc, xf_ref, xfp_ref, xfn_ref, *shared, hf_ref, *scr)
    _lru_direction(1, True, nc - 1 - i, nc, xb_ref, xbp_ref, xbn_ref, *shared, hb_ref, *scr)


def _lru(proj, conv_w, conv_b, w_a, b_a, w_x, b_x, lam):
    s_len = proj.shape[0]
    nc = s_len // LRU_CHUNK
    halo_per_chunk = LRU_CHUNK // BF16_SUBLANES
    n_halo = s_len // BF16_SUBLANES
    wg = (0.5 * jnp.concatenate([w_a, w_x], axis=-1)).astype(BF16)
    bg = 0.5 * jnp.stack([b_a, b_x], axis=1)

    def fwd(i):
        return i

    def bwd(i):
        return nc - 1 - i

    def specs(chunk_of):
        return [
            pl.BlockSpec((LRU_CHUNK, LRU_WIDTH), lambda i: (chunk_of(i), COL_XLRU)),
            pl.BlockSpec((BF16_SUBLANES, LRU_WIDTH),
                         lambda i: (jnp.maximum(chunk_of(i) * halo_per_chunk - 1, 0), COL_XLRU)),
            pl.BlockSpec((BF16_SUBLANES, LRU_WIDTH),
                         lambda i: (jnp.minimum((chunk_of(i) + 1) * halo_per_chunk, n_halo - 1), COL_XLRU)),
        ]

    def whole(shape):
        return pl.BlockSpec(shape, lambda i: (0,) * len(shape))

    return pl.pallas_call(
        _lru_kernel,
        grid=(nc,),
        in_specs=specs(fwd) + specs(bwd) + [
            whole((CONV_WIDTH, LRU_WIDTH)), whole((1, LRU_WIDTH)),
            whole((2, LRU_BLOCKS, LRU_BLOCK_DIM, 2 * LRU_BLOCK_DIM)),
            whole((2, 2, LRU_WIDTH)), whole((2, LRU_WIDTH))],
        out_specs=[pl.BlockSpec((LRU_BLOCKS, LRU_CHUNK, LRU_BLOCK_DIM), lambda i: (0, i, 0)),
                   pl.BlockSpec((LRU_BLOCKS, LRU_CHUNK, LRU_BLOCK_DIM), lambda i: (0, nc - 1 - i, 0))],
        out_shape=[jax.ShapeDtypeStruct((LRU_BLOCKS, s_len, LRU_BLOCK_DIM), F32),
                   jax.ShapeDtypeStruct((LRU_BLOCKS, s_len, LRU_BLOCK_DIM), F32)],
        scratch_shapes=[pltpu.VMEM((2, LRU_BLOCKS, LRU_CHUNK + 2 * BF16_SUBLANES, LRU_BLOCK_DIM), F32),
                        pltpu.VMEM((2, LRU_WIDTH), F32)],
        compiler_params=_params("arbitrary"),
        name="rglru",
    )(proj, proj, proj, proj, proj, proj, conv_w, conv_b.reshape(1, LRU_WIDTH), wg, bg, lam)


def _mem_scores(h, q_ref, kv_ref):
    cs = slice(h * MEM_HEAD_DIM, (h + 1) * MEM_HEAD_DIM)
    return lax.dot_general(q_ref[:, cs], kv_ref[:, cs], (((1,), (1,)), ((), ())),
                           preferred_element_type=F32)


def _mem_values(h, s, kv_ref, y_scr):
    cs = slice(h * MEM_HEAD_DIM, (h + 1) * MEM_HEAD_DIM)
    vs = slice(MEM_WIDTH + h * MEM_HEAD_DIM, MEM_WIDTH + (h + 1) * MEM_HEAD_DIM)
    s = s * MEM_HEAD_DIM ** -0.5
    m = jnp.max(s, axis=-1, keepdims=True)
    p = jnp.exp(s - m)
    denom = jnp.sum(p, axis=-1, keepdims=True)
    y_scr[:, cs] = jnp.dot(p.astype(BF16), kv_ref[:, vs], preferred_element_type=F32) / denom


def _lru_finish(hf_ref, hb_ref, zl_ref, gl_ref):
    y_lru = jnp.concatenate([hf_ref[blk] + hb_ref[blk] for blk in range(LRU_BLOCKS)], axis=1)
    return _norm_gate(y_lru, gl_ref[...], zl_ref[...].astype(F32))


def _tail_kernel(sink_ref,
                 q_ref, k0_ref, kn_ref, vp_ref, vc_ref, vn_ref, z0_ref, z1_ref, freq_ref, ga_ref,
                 qm_ref, kv_ref, zm_ref, gm_ref, hf_ref, hb_ref, zl_ref, gl_ref,
                 w_ref, x_ref, gp_ref, o_ref,
                 lhs_scr, acc_scr, ss_scr, k_ring, bias_scr, cos_scr, sin_scr, ya_scr, ym_scr):
    o = pl.program_id(0)
    j = pl.program_id(1)
    nj = pl.num_programs(1)
    n_row_tiles = pl.num_programs(0) - 2
    nb = n_row_tiles * nj

    n_chunks = ATT_KV_HEADS
    chunk = TAIL_TN // n_chunks

    def stage_b_chunk(c, part):
        slot = lax.rem(o + 1, 2)
        cols = slice(c * chunk, (c + 1) * chunk)
        acc = jnp.dot(lhs_scr[slot], w_ref[:, cols], preferred_element_type=F32)
        acc_scr[slot, j, :, cols] = acc.astype(acc_scr.dtype)
        sq = acc * acc
        for k in range(chunk // LANES):
            piece = sq[:, k * LANES:(k + 1) * LANES]
            part = piece if part is None else part + piece
        return part

    def stages_ab(with_a, with_b):
        part = None
        if with_b:
            part = stage_b_chunk(0, part)
        if with_a:
            n = o * nj + j
            attn = _AttnBlock(n, nb, sink_ref, q_ref, kn_ref, vp_ref, vc_ref, vn_ref, freq_ref,
                              k_ring, bias_scr, cos_scr, sin_scr, ya_scr)
            s_att = [attn.scores(h) for h in range(ATT_KV_HEADS)]
            s_mem = [_mem_scores(h, qm_ref, kv_ref) for h in range(MEM_HEADS)]
        for pair in range(2):
            if with_b:
                part = stage_b_chunk(1 + pair, part)
            if with_a:
                for h in (2 * pair, 2 * pair + 1):
                    attn.values(h, s_att[h])
                    _mem_values(h, s_mem[h], kv_ref, ym_scr)
        if with_b:
            part = stage_b_chunk(3, part)
            slot = lax.rem(o + 1, 2)
            ss_scr[slot] = jnp.where(j == 0, part, ss_scr[slot] + part)
        if with_a:
            rows = pl.ds(pl.multiple_of(j * BLOCK, BLOCK), BLOCK)
            slot = lax.rem(o, 2)
            lhs_scr[slot, rows, 0:ATT_WIDTH] = attn.result(z0_ref, z1_ref, ga_ref).astype(BF16)
            lhs_scr[slot, rows, ATT_WIDTH:ATT_WIDTH + LRU_WIDTH] = _lru_finish(
                hf_ref, hb_ref, zl_ref, gl_ref).astype(BF16)
            lhs_scr[slot, rows, ATT_WIDTH + LRU_WIDTH:] = _norm_gate(
                ym_scr[...], gm_ref[...], zm_ref[...].astype(F32)).astype(BF16)

    def stage_c():
        slot = lax.rem(o, 2)
        ms = jnp.sum(ss_scr[slot], axis=-1, keepdims=True) * (1.0 / D_MODEL)
        o_ref[...] = x_ref[...] + acc_scr[slot, j].astype(F32) * lax.rsqrt(ms + EPS) * gp_ref[...]

    @pl.when(o == 0)
    def _():
        @pl.when(j == 0)
        def _():
            ss_scr[...] = jnp.zeros_like(ss_scr)
            _attn_init(k0_ref, freq_ref, k_ring, bias_scr, cos_scr, sin_scr)

        stages_ab(True, False)

    @pl.when(o == 1)
    def _():
        stages_ab(True, True)

    @pl.when((o >= 2) & (o < n_row_tiles))
    def _():
        stage_c()
        stages_ab(True, True)

    @pl.when(o == n_row_tiles)
    def _():
        stage_c()
        stages_ab(False, True)

    @pl.when(o == n_row_tiles + 1)
    def _():
        stage_c()


def _tail(proj, kv, h_fwd, h_bwd, sink, g_att, g_mem, g_lru, w_out, x, post_g):
    s_len = proj.shape[0]
    tm, tn = TAIL_TM, TAIL_TN
    ni, nj = s_len // tm, D_MODEL // tn
    nb = s_len // BLOCK
    assert tm == nj * BLOCK
    inv_freq = jnp.power(jnp.float32(ROPE_THETA),
                         -jnp.arange(ROT_HALF, dtype=F32) * 2.0 / ROT_DIM)
    freq = jnp.concatenate([inv_freq, inv_freq, jnp.zeros((HEAD_DIM - ROT_DIM,), F32)]).reshape(1, HEAD_DIM)

    def block_of(o, j):
        return jnp.where(o < ni, o * nj + j, nb - 1)

    def rows_spec(width, col, shift=0):
        return pl.BlockSpec((BLOCK, width),
                            lambda o, j, sink: (jnp.clip(block_of(o, j) + shift, 0, nb - 1), col))

    def const_spec(shape):
        return pl.BlockSpec(shape, lambda o, j, sink: (0,) * len(shape), pipeline_mode=pl.Buffered(1))

    slab_spec = pl.BlockSpec((LRU_BLOCKS, BLOCK, LRU_BLOCK_DIM), lambda o, j, sink: (0, block_of(o, j), 0))

    def w_col(o, j, sink):
        return (jnp.where(o == 0, 0, jnp.where(o <= ni, j, nj - 1)), 0, 0)

    def finished_tile(o, j, sink):
        return (jnp.maximum(o - 2, 0), jnp.where(o >= 2, j, 0))

    grid_spec = pltpu.PrefetchScalarGridSpec(
        num_scalar_prefetch=1,
        grid=(ni + 2, nj),
        in_specs=[rows_spec(ATT_WIDTH, 0),
                  pl.BlockSpec((BLOCK, KV_WIDTH), lambda o, j, sink: (0, COL_K512), pipeline_mode=pl.Buffered(1)),
                  rows_spec(KV_WIDTH, COL_K512, 1),
                  rows_spec(KV_WIDTH, COL_V512, -1), rows_spec(KV_WIDTH, COL_V512), rows_spec(KV_WIDTH, COL_V512, 1),
                  rows_spec(1024, COL_ZATT), rows_spec(1024, COL_ZATT + 1),
                  const_spec((1, HEAD_DIM)), const_spec((1, ATT_WIDTH)),
                  rows_spec(MEM_WIDTH, COL_QMEM), const_spec((N_MEM, 2 * MEM_WIDTH)),
                  rows_spec(MEM_WIDTH, COL_ZMEM), const_spec((1, MEM_WIDTH)),
                  slab_spec, slab_spec, rows_spec(LRU_WIDTH, COL_ZLRU), const_spec((1, LRU_WIDTH)),
                  pl.BlockSpec((None, D_MODEL, tn), w_col),
                  pl.BlockSpec((tm, tn), finished_tile),
                  pl.BlockSpec((1, tn), lambda o, j, sink: (0, j))],
        out_specs=pl.BlockSpec((tm, tn), finished_tile),
        scratch_shapes=[pltpu.VMEM((2, tm, D_MODEL), BF16),
                        pltpu.VMEM((2, nj, tm, tn), BF16),
                        pltpu.VMEM((2, tm, LANES), F32),
                        pltpu.VMEM((3, BLOCK, KV_WIDTH), BF16),
                        pltpu.VMEM((3, BLOCK, ATT_GROUP * BLOCK), F32),
                        pltpu.VMEM((BLOCK, HEAD_DIM), F32),
                        pltpu.VMEM((BLOCK, HEAD_DIM), F32),
                        pltpu.VMEM((BLOCK, ATT_WIDTH), F32),
                        pltpu.VMEM((BLOCK, MEM_WIDTH), F32)],
    )
    return pl.pallas_call(
        _tail_kernel,
        grid_spec=grid_spec,
        out_shape=jax.ShapeDtypeStruct((s_len, D_MODEL), F32),
        compiler_params=_params("arbitrary", "arbitrary"),
        name="mixers_out_proj_post_norm",
    )(sink, proj, proj, proj, proj, proj, proj, proj, proj, freq, g_att.reshape(1, ATT_WIDTH),
      proj, kv, proj, g_mem.reshape(1, MEM_WIDTH), h_fwd, h_bwd, proj, g_lru.reshape(1, LRU_WIDTH),
      w_out, x, post_g.reshape(1, D_MODEL))


def _layer(h, mem, pre_g, w_in, sink, conv_w, conv_b, w_a, b_a, w_x, b_x, lam, mem_g, w_mem_kv,
           g_att, g_lru, g_mem, w_out, post_g):
    proj, w_out_bf, w_mem_kv_bf = _in_proj(h, pre_g, w_in, w_out, w_mem_kv)
    kv = _mem_kv(mem, mem_g, w_mem_kv_bf)
    h_fwd, h_bwd = _lru(proj, conv_w, conv_b, w_a, b_a, w_x, b_x, lam)
    return _tail(proj, kv, h_fwd, h_bwd, sink, g_att, g_mem, g_lru, w_out_bf, h, post_g)


def kernel(x, mem, pre_norm_gain, w_in, att_sink, conv_w, conv_b, lru_w_a, lru_b_a, lru_w_x, lru_b_x,
           lru_lambda, mem_norm_gain, w_mem_kv, att_out_gain, lru_out_gain, mem_out_gain, w_out,
           post_norm_gain):
    batch, s_len, d_model = x.shape
    depth = w_in.shape[0]
    outs = []
    for b in range(batch):
        h = x.reshape(s_len, d_model) if batch == 1 else x[b]
        m = mem.reshape(mem.shape[1], d_model) if batch == 1 else mem[b]
        for l in range(depth):
            h = _layer(h, m, pre_norm_gain[l], w_in[l], att_sink[l], conv_w[l], conv_b[l],
                       lru_w_a[l], lru_b_a[l], lru_w_x[l], lru_b_x[l], lru_lambda[l],
                       mem_norm_gain[l], w_mem_kv[l], att_out_gain[l], lru_out_gain[l],
                       mem_out_gain[l], w_out[l], post_norm_gain[l])
        outs.append(h)
    return outs[0].reshape(x.shape) if batch == 1 else jnp.stack(outs, axis=0)
```
